```python
import math
import numpy as np
import jax
import jax.numpy as jnp
from jax import lax

D_MODEL = 1024
BATCH = 2
SEQ = 8192
DEPTH = 2

HEAD_DIM = 64
N_HEADS = D_MODEL // HEAD_DIM
HEADS_PER_MIXER = N_HEADS // 4
GROUP_W = HEADS_PER_MIXER * HEAD_DIM
D_MIX = 4 * GROUP_W
D_FF = 2816
Q_BLOCK = 128
ROPE_THETA = 10000.0
LN_EPS = 1e-5
NORM_EPS = 1e-6
NEG = -1e30
DIFF_QK_DIM = HEAD_DIM // 2
GDN_CONV = 4
GDN_CHUNK = 64
CMP_BLOCK = 32
CMP_STRIDE = 16
CMP_HIDDEN = 256
SLC_BLOCK = 64
SLC_TOPN = 16
WINDOW = 512
FORCE = 1e4
PROJ_SIZES = ((HEADS_PER_MIXER * DIFF_QK_DIM,) * 4 + (GROUP_W,)
              + (GROUP_W,) * 4 + (HEADS_PER_MIXER,) * 2
              + (GROUP_W,) + (HEAD_DIM,) * 6 + (3 * HEADS_PER_MIXER,)
              + (GROUP_W,) * 3)
P_TOTAL = sum(PROJ_SIZES)

kernel_name = 'hybrid_parallel_head_groups_block'


def _layer_norm(x, g, b):
    x32 = x.astype(jnp.float32)
    mu = jnp.mean(x32, axis=-1, keepdims=True)
    var = jnp.mean(jnp.square(x32 - mu), axis=-1, keepdims=True)
    return ((x32 - mu) * lax.rsqrt(var + LN_EPS) * g + b).astype(x.dtype)


def _rms_norm(x, g):
    x32 = x.astype(jnp.float32)
    y = x32 * lax.rsqrt(jnp.mean(jnp.square(x32), axis=-1, keepdims=True) + NORM_EPS) * g
    return y.astype(x.dtype)


def _l2norm(x):
    return x * lax.rsqrt(jnp.sum(x * x, axis=-1, keepdims=True) + NORM_EPS)


def _swiglu(x, w_gu, w_down):
    g, u = jnp.split(x @ w_gu, 2, axis=-1)
    return (jax.nn.silu(g) * u) @ w_down


def _to_heads(t, n):
    B, S, _ = t.shape
    return t.reshape(B, S, n, -1).transpose(0, 2, 1, 3)


def _from_heads(t):
    B, H, S, d = t.shape
    return t.transpose(0, 2, 1, 3).reshape(B, S, H * d)


def _unblock(o):
    n, B, H, Q, d = o.shape
    return o.transpose(1, 2, 0, 3, 4).reshape(B, H, n * Q, d)


def _rope_tables(seq, dim):
    inv = ROPE_THETA ** (-jnp.arange(0, dim, 2, dtype=jnp.float32) / dim)
    ang = jnp.arange(seq, dtype=jnp.float32)[:, None] * inv[None, :]
    return jnp.cos(ang), jnp.sin(ang)


def _rope(t, cs):
    cos, sin = cs[0].astype(t.dtype), cs[1].astype(t.dtype)
    t1, t2 = jnp.split(t, 2, axis=-1)
    return jnp.concatenate([t1 * cos - t2 * sin, t2 * cos + t1 * sin], axis=-1)


def _causal_dwconv(x, w):
    K = w.shape[0]
    return lax.conv_general_dilated(x, w[:, None, :], window_strides=(1,), padding=((K - 1, 0),),
                                    dimension_numbers=('NWC', 'WIO', 'NWC'),
                                    feature_group_count=x.shape[-1])


def diff_attention(q1, q2, k1, k2, v, lam_q1, lam_k1, lam_q2, lam_k2, subln_g, lam_init):
    f32 = jnp.float32
    B, H, S, dq = q1.shape
    scale = dq ** -0.5
    lam = (jnp.exp(jnp.sum(lam_q1.astype(f32) * lam_k1.astype(f32)))
           - jnp.exp(jnp.sum(lam_q2.astype(f32) * lam_k2.astype(f32))) + lam_init)
    kpos = jnp.arange(S)

    def block(i):
        start = i * Q_BLOCK
        qpos = start + jnp.arange(Q_BLOCK)
        causal = kpos[None, :] <= qpos[:, None]

        def probs(q, k):
            qb = lax.dynamic_slice_in_dim(q, start, Q_BLOCK, axis=2)
            s = jnp.einsum('bhqd,bhkd->bhqk', qb, k).astype(f32) * scale
            return jax.nn.softmax(jnp.where(causal, s, NEG), axis=-1)

        w = probs(q1, k1) - lam * probs(q2, k2)
        return jnp.einsum('bhqk,bhkd->bhqd', w.astype(v.dtype), v)

    o = _unblock(lax.map(block, jnp.arange(S // Q_BLOCK)))
    return _from_heads(_rms_norm(o, subln_g) * (1.0 - lam_init))


def gated_delta_net(q, k, v, z, a, b, conv_w, a_log, dt_bias, norm_g):
    f32 = jnp.float32
    B, S, _ = q.shape
    H, d, C = HEADS_PER_MIXER, HEAD_DIM, GDN_CHUNK
    N = S // C
    qkv = jax.nn.silu(_causal_dwconv(jnp.concatenate([q, k, v], axis=-1), conv_w))
    q, k, v = [_to_heads(t, H).astype(f32) for t in jnp.split(qkv, 3, axis=-1)]
    q = _l2norm(q) * d ** -0.5
    k = _l2norm(k)
    beta = jax.nn.sigmoid(b.astype(f32)).transpose(0, 2, 1)
    g = (-jnp.exp(a_log.astype(f32))[None, :, None]
         * jax.nn.softplus(a.astype(f32).transpose(0, 2, 1) + dt_bias.astype(f32)[None, :, None]))

    def chunk(t):
        return t.reshape((B, H, N, C) + t.shape[3:])

    q, k, v, beta, g = chunk(q), chunk(k), chunk(v), chunk(beta), chunk(g)
    G = jnp.cumsum(g, axis=-1)
    tri = jnp.tril(jnp.ones((C, C), bool))
    strict = jnp.tril(jnp.ones((C, C), bool), -1)
    decay = jnp.exp(jnp.where(tri, G[..., :, None] - G[..., None, :], -jnp.inf))
    kb = k * beta[..., None]
    A = jnp.where(strict, jnp.einsum('bhncd,bhnsd->bhncs', kb, k) * decay, 0.0)
    lhs = A + jnp.eye(C, dtype=f32)
    rhs = jnp.concatenate([v * beta[..., None], kb * jnp.exp(G)[..., None]], axis=-1)
    sol = lax.linalg.triangular_solve(lhs, rhs, left_side=True, lower=True, unit_diagonal=True)
    u, w = jnp.split(sol, 2, axis=-1)
    intra = jnp.einsum('bhncd,bhnsd->bhncs', q, k) * decay
    q_dec = q * jnp.exp(G)[..., None]
    k_dec = k * jnp.exp(G[..., -1:] - G)[..., None]
    g_last = jnp.exp(G[..., -1])

    def step(state, inp):
        u_n, w_n, qd_n, kd_n, in_n, gl_n = inp
        v_new = u_n - jnp.einsum('bhcd,bhde->bhce', w_n, state)
        o_n = (jnp.einsum('bhcd,bhde->bhce', qd_n, state)
               + jnp.einsum('bhcs,bhse->bhce', in_n, v_new))
        state = state * gl_n[..., None, None] + jnp.einsum('bhcd,bhce->bhde', kd_n, v_new)
        return state, o_n

    xs = tuple(jnp.moveaxis(t, 2, 0) for t in (u, w, q_dec, k_dec, intra, g_last))
    _, o = lax.scan(step, jnp.zeros((B, H, d, d), f32), xs)
    o = _rms_norm(_unblock(o), norm_g) * jax.nn.silu(_to_heads(z, H).astype(f32))
    return _from_heads(o).astype(z.dtype)


def native_sparse_attention(q, kc, vc, ks, vs, kw, vw, gate, pe_k, pe_v,
                            ck_w1, ck_w2, cv_w1, cv_w2, rope_n):
    f32 = jnp.float32
    B, S, _ = q.shape
    H, d = HEADS_PER_MIXER, HEAD_DIM
    scale = d ** -0.5
    q = _rope(_to_heads(q, H), rope_n)
    kc, ks, kw = _rope(kc, rope_n), _rope(ks, rope_n), _rope(kw, rope_n)
    tpos = jnp.arange(S)

    n_cmp = (S - CMP_BLOCK) // CMP_STRIDE + 1
    cidx = np.arange(n_cmp)[:, None] * CMP_STRIDE + np.arange(CMP_BLOCK)[None, :]

    def compress(t, pe, w1, w2):
        blk = (t[:, cidx, :] + pe).reshape(B, n_cmp, CMP_BLOCK * d)
        return jax.nn.silu(blk @ w1) @ w2

    k_cmp = compress(kc, pe_k, ck_w1, ck_w2)
    v_cmp = compress(vc, pe_v, cv_w1, cv_w2)
    cmask = jnp.asarray(cidx[:, -1])[None, :] <= tpos[:, None]
    s_cmp = jnp.einsum('bhtd,bcd->bhtc', q, k_cmp).astype(f32) * scale
    p_cmp = jax.nn.softmax(jnp.where(cmask, s_cmp, NEG), axis=-1) * cmask
    o_cmp = jnp.einsum('bhtc,bcd->bhtd', p_cmp.astype(v_cmp.dtype), v_cmp)

    n_slc = S // SLC_BLOCK
    sstart = np.arange(n_slc) * SLC_BLOCK
    overlap = (cidx[:, :1] < sstart[None, :] + SLC_BLOCK) & (cidx[:, -1:] >= sstart[None, :])
    imp = jnp.einsum('btc,cj->btj', p_cmp.sum(axis=1), jnp.asarray(overlap, f32))
    blk = jnp.arange(n_slc)[None, :]
    cur = (tpos // SLC_BLOCK)[:, None]
    valid = blk <= cur
    forced = valid & ((blk == 0) | (blk == cur) | (blk == cur - 1))
    imp = jnp.where(forced, FORCE, jnp.where(valid, imp, -FORCE))
    _, sel = lax.top_k(imp, min(SLC_TOPN, n_slc))

    ks_blk = ks.reshape(B, n_slc, SLC_BLOCK, d)
    vs_blk = vs.reshape(B, n_slc, SLC_BLOCK, d)
    kw_pad = jnp.pad(kw, ((0, 0), (WINDOW, 0), (0, 0)))
    vw_pad = jnp.pad(vw, ((0, 0), (WINDOW, 0), (0, 0)))
    gather = jax.vmap(lambda tb, ib: tb[ib])

    def block(i):
        start = i * Q_BLOCK
        qb = lax.dynamic_slice_in_dim(q, start, Q_BLOCK, axis=2)
        qpos = start + jnp.arange(Q_BLOCK)
        sel_b = lax.dynamic_slice_in_dim(sel, start, Q_BLOCK, axis=1)
        kg = gather(ks_blk, sel_b).reshape(B, Q_BLOCK, -1, d)
        vg = gather(vs_blk, sel_b).reshape(B, Q_BLOCK, -1, d)
        kpos = (sel_b[..., None] * SLC_BLOCK + jnp.arange(SLC_BLOCK)).reshape(B, Q_BLOCK, -1)
        smask = (kpos <= qpos[None, :, None])[:, None]
        ss = jnp.einsum('bhqd,bqkd->bhqk', qb, kg).astype(f32) * scale
        ps = jax.nn.softmax(jnp.where(smask, ss, NEG), axis=-1)
        o_s = jnp.einsum('bhqk,bqkd->bhqd', ps.astype(vg.dtype), vg)
        kwin = lax.dynamic_slice_in_dim(kw_pad, start, Q_BLOCK + WINDOW, axis=1)
        vwin = lax.dynamic_slice_in_dim(vw_pad, start, Q_BLOCK + WINDOW, axis=1)
        wpos = start - WINDOW + jnp.arange(Q_BLOCK + WINDOW)
        rel = qpos[:, None] - wpos[None, :]
        wmask = (rel >= 0) & (rel < WINDOW) & (wpos[None, :] >= 0)
        sw = jnp.einsum('bhqd,bkd->bhqk', qb, kwin).astype(f32) * scale
        pw = jax.nn.softmax(jnp.where(wmask, sw, NEG), axis=-1)
        o_w = jnp.einsum('bhqk,bkd->bhqd', pw.astype(vwin.dtype), vwin)
        return o_s, o_w

    o_sel, o_win = lax.map(block, jnp.arange(S // Q_BLOCK))
    o_sel, o_win = _unblock(o_sel), _unblock(o_win)
    gts = jax.nn.sigmoid(gate).reshape(B, S, H, 3).transpose(0, 2, 1, 3)
    o = gts[..., 0:1] * o_cmp + gts[..., 1:2] * o_sel + gts[..., 2:3] * o_win
    return _from_heads(o)


def stick_breaking(q, k, v):
    f32 = jnp.float32
    H = HEADS_PER_MIXER
    q, k, v = _to_heads(q, H), _to_heads(k, H), _to_heads(v, H)
    B, _, S, d = q.shape
    scale = d ** -0.5
    kpos = jnp.arange(S)

    def block(i):
        start = i * Q_BLOCK
        qb = lax.dynamic_slice_in_dim(q, start, Q_BLOCK, axis=2)
        qpos = start + jnp.arange(Q_BLOCK)
        before = kpos[None, :] < qpos[:, None]
        z = jnp.einsum('bhqd,bhkd->bhqk', qb, k).astype(f32) * scale
        log_keep = jnp.where(before, jax.nn.log_sigmoid(-z), 0.0)
        between = lax.cumsum(log_keep, axis=3, reverse=True) - log_keep
        att = jnp.where(before, jnp.exp(jax.nn.log_sigmoid(z) + between), 0.0)
        return jnp.einsum('bhqk,bhkd->bhqd', att.astype(v.dtype), v)

    return _from_heads(_unblock(lax.map(block, jnp.arange(S // Q_BLOCK))))


def hybrid_mixer(x, w_in, w_out, lam_q1, lam_k1, lam_q2, lam_k2, subln_g,
                 conv_w, a_log, dt_bias, gdn_g, pe_k, pe_v, ck_w1, ck_w2, cv_w1, cv_w2,
                 lam_init, rope_d, rope_n):
    H = HEADS_PER_MIXER
    (dq1, dq2, dk1, dk2, dv, gq, gk, gv, gz, ga, gb,
     nq, nkc, nvc, nks, nvs, nkw, nvw, ngate, sq, sk, sv) = jnp.split(
        x @ w_in, np.cumsum(PROJ_SIZES)[:-1].tolist(), axis=-1)
    o_diff = diff_attention(_rope(_to_heads(dq1, H), rope_d), _rope(_to_heads(dq2, H), rope_d),
                            _rope(_to_heads(dk1, H), rope_d), _rope(_to_heads(dk2, H), rope_d),
                            _to_heads(dv, H), lam_q1, lam_k1, lam_q2, lam_k2, subln_g, lam_init)
    o_gdn = gated_delta_net(gq, gk, gv, gz, ga, gb, conv_w, a_log, dt_bias, gdn_g)
    o_nsa = native_sparse_attention(nq, nkc, nvc, nks, nvs, nkw, nvw, ngate, pe_k, pe_v,
                                    ck_w1, ck_w2, cv_w1, cv_w2, rope_n)
    o_sb = stick_breaking(sq, sk, sv)
    return jnp.concatenate([o_diff, o_gdn, o_nsa, o_sb], axis=-1) @ w_out


def setup_inputs(seed: int = 0) -> dict:
    key = jax.random.key(seed)
    ks = iter(jax.random.split(key, 48))
    L, H = DEPTH, HEADS_PER_MIXER
    beta_dn = (8 * DEPTH) ** -0.25

    def nrm(shape, scale):
        return jax.random.normal(next(ks), shape, jnp.float32) * scale

    def gain(shape):
        return 1.0 + nrm(shape, 0.02)

    x = nrm((BATCH, SEQ, D_MODEL), 1.0)
    w_in = nrm((L, D_MODEL, P_TOTAL), D_MODEL ** -0.5)
    w_out = nrm((L, D_MIX, D_MODEL), beta_dn * D_MIX ** -0.5)
    ffn1_w_gu = nrm((L, D_MODEL, 2 * D_FF), D_MODEL ** -0.5)
    ffn1_w_down = nrm((L, D_FF, D_MODEL), beta_dn * D_FF ** -0.5)
    ffn2_w_gu = nrm((L, D_MODEL, 2 * D_FF), D_MODEL ** -0.5)
    ffn2_w_down = nrm((L, D_FF, D_MODEL), beta_dn * D_FF ** -0.5)
    ln1_g, ln1_b = gain((L, D_MODEL)), nrm((L, D_MODEL), 0.02)
    ln2_g, ln2_b = gain((L, D_MODEL)), nrm((L, D_MODEL), 0.02)
    ln3_g, ln3_b = gain((L, D_MODEL)), nrm((L, D_MODEL), 0.02)
    diff_lam_q1 = nrm((L, DIFF_QK_DIM), 0.1)
    diff_lam_k1 = nrm((L, DIFF_QK_DIM), 0.1)
    diff_lam_q2 = nrm((L, DIFF_QK_DIM), 0.1)
    diff_lam_k2 = nrm((L, DIFF_QK_DIM), 0.1)
    diff_subln_g = gain((L, HEAD_DIM))
    gdn_conv_w = nrm((L, GDN_CONV, 3 * GROUP_W), GDN_CONV ** -0.5)
    gdn_a_log = jnp.log(jax.random.uniform(next(ks), (L, H), jnp.float32, 1.0, 16.0))
    dt = jnp.exp(jax.random.uniform(next(ks), (L, H), jnp.float32, math.log(1e-3), math.log(1e-1)))
    gdn_dt_bias = dt + jnp.log(-jnp.expm1(-dt))
    gdn_norm_g = gain((L, HEAD_DIM))
    nsa_pe_k = nrm((L, CMP_BLOCK, HEAD_DIM), 0.02)
    nsa_pe_v = nrm((L, CMP_BLOCK, HEAD_DIM), 0.02)
    nsa_cmp_k_w1 = nrm((L, CMP_BLOCK * HEAD_DIM, CMP_HIDDEN), (CMP_BLOCK * HEAD_DIM) ** -0.5)
    nsa_cmp_k_w2 = nrm((L, CMP_HIDDEN, HEAD_DIM), CMP_HIDDEN ** -0.5)
    nsa_cmp_v_w1 = nrm((L, CMP_BLOCK * HEAD_DIM, CMP_HIDDEN), (CMP_BLOCK * HEAD_DIM) ** -0.5)
    nsa_cmp_v_w2 = nrm((L, CMP_HIDDEN, HEAD_DIM), CMP_HIDDEN ** -0.5)
    return {'x': x, 'w_in': w_in, 'w_out': w_out,
            'ffn1_w_gu': ffn1_w_gu, 'ffn1_w_down': ffn1_w_down,
            'ffn2_w_gu': ffn2_w_gu, 'ffn2_w_down': ffn2_w_down,
            'ln1_g': ln1_g, 'ln1_b': ln1_b, 'ln2_g': ln2_g, 'ln2_b': ln2_b,
            'ln3_g': ln3_g, 'ln3_b': ln3_b,
            'diff_lam_q1': diff_lam_q1, 'diff_lam_k1': diff_lam_k1,
            'diff_lam_q2': diff_lam_q2, 'diff_lam_k2': diff_lam_k2, 'diff_subln_g': diff_subln_g,
            'gdn_conv_w': gdn_conv_w, 'gdn_a_log': gdn_a_log, 'gdn_dt_bias': gdn_dt_bias,
            'gdn_norm_g': gdn_norm_g,
            'nsa_pe_k': nsa_pe_k, 'nsa_pe_v': nsa_pe_v,
            'nsa_cmp_k_w1': nsa_cmp_k_w1, 'nsa_cmp_k_w2': nsa_cmp_k_w2,
            'nsa_cmp_v_w1': nsa_cmp_v_w1, 'nsa_cmp_v_w2': nsa_cmp_v_w2}


def reference(x, w_in, w_out, ffn1_w_gu, ffn1_w_down, ffn2_w_gu, ffn2_w_down,
              ln1_g, ln1_b, ln2_g, ln2_b, ln3_g, ln3_b,
              diff_lam_q1, diff_lam_k1, diff_lam_q2, diff_lam_k2, diff_subln_g,
              gdn_conv_w, gdn_a_log, gdn_dt_bias, gdn_norm_g,
              nsa_pe_k, nsa_pe_v, nsa_cmp_k_w1, nsa_cmp_k_w2, nsa_cmp_v_w1, nsa_cmp_v_w2):
    S = x.shape[1]
    alpha = (2 * DEPTH) ** 0.25
    rope_d = _rope_tables(S, DIFF_QK_DIM)
    rope_n = _rope_tables(S, HEAD_DIM)
    for l in range(DEPTH):
        lam_init = 0.8 - 0.6 * math.exp(-0.3 * l)
        x = _layer_norm(alpha * x + 0.5 * _swiglu(x, ffn1_w_gu[l], ffn1_w_down[l]), ln1_g[l], ln1_b[l])
        mix = hybrid_mixer(x, w_in[l], w_out[l], diff_lam_q1[l], diff_lam_k1[l], diff_lam_q2[l],
                           diff_lam_k2[l], diff_subln_g[l], gdn_conv_w[l], gdn_a_log[l],
                           gdn_dt_bias[l], gdn_norm_g[l], nsa_pe_k[l], nsa_pe_v[l],
                           nsa_cmp_k_w1[l], nsa_cmp_k_w2[l], nsa_cmp_v_w1[l], nsa_cmp_v_w2[l],
                           lam_init, rope_d, rope_n)
        x = _layer_norm(alpha * x + mix, ln2_g[l], ln2_b[l])
        x = _layer_norm(alpha * x + 0.5 * _swiglu(x, ffn2_w_gu[l], ffn2_w_down[l]), ln3_g[l], ln3_b[l])
    return x
```

```python
import functools
import math

import numpy as np
import jax
import jax.numpy as jnp
from jax import lax
from jax.experimental import pallas as pl
from jax.experimental.pallas import tpu as pltpu

F32 = jnp.float32
BF16 = jnp.bfloat16

DEPTH = 2
HEAD_DIM = 64
HEADS = 4
GROUP_W = HEADS * HEAD_DIM
DIFF_QK = HEAD_DIM // 2
GDN_CHUNK = 64
CMP_BLOCK, CMP_STRIDE = 32, 16
SLC_BLOCK, SLC_TOPN = 64, 16
WINDOW = 512
FORCE = 1e4
ROPE_THETA = 10000.0
LN_EPS = 1e-5
NORM_EPS = 1e-6
NEG = -1e30
LOG2E = 1.4426950408889634

LANE = 128
V7X_VMEM_BYTES = 64 * 1024 * 1024
VMEM_LIMIT = V7X_VMEM_BYTES - 8 * 1024 * 1024

TM_FFN = 512
FF_CHUNK = 1408
TM_PROJ = 256
T_ATT = 512
TQ_CMP = 256
TQ_WIN = 128
TQ_SEL, TK_SEL = 256, 512


def _cp(sem):
    return pltpu.CompilerParams(dimension_semantics=sem, vmem_limit_bytes=VMEM_LIMIT)


def _iota(shape, dim):
    return lax.broadcasted_iota(jnp.int32, shape, dim)


def _dot(a, b):
    return jnp.dot(a, b, preferred_element_type=F32)


def _dot_nt(a, b):
    return lax.dot_general(a, b, (((1,), (1,)), ((), ())), preferred_element_type=F32)


def _split2(x):
    hi = x.astype(BF16)
    lo = (x - hi.astype(F32)).astype(BF16)
    return hi, lo


def _dot_x2(x, w):
    hi, lo = _split2(x)
    return _dot(hi, w) + _dot(lo, w)


def _dot_x3(x, w):
    hi = x.astype(BF16)
    r = x - hi.astype(F32)
    mid = r.astype(BF16)
    lo = (r - mid.astype(F32)).astype(BF16)
    return _dot(hi, w) + _dot(mid, w) + _dot(lo, w)


def _mm3(x, wh, wl):
    xh, xl = _split2(x)
    return _dot(xh, wh) + _dot(xl, wh) + _dot(xh, wl)


def _layer_norm(y, g, b):
    mu = jnp.mean(y, axis=-1, keepdims=True)
    d = y - mu
    var = jnp.mean(d * d, axis=-1, keepdims=True)
    return d * lax.rsqrt(var + LN_EPS) * g + b


def _const_spec(shape):
    nd = len(shape)
    return pl.BlockSpec(shape, lambda *_: (0,) * nd, pipeline_mode=pl.Buffered(1))


def _ffn_ln_kernel(x_ref, wgu_ref, wd_ref, g_ref, b_ref, o_ref, *, alpha, d_ff, ff_chunk):
    x = x_ref[...]
    xb = x.astype(BF16)
    acc = None
    for c in range(d_ff // ff_chunk):
        lo = c * ff_chunk
        g = _dot(xb, wgu_ref[:, lo:lo + ff_chunk])
        u = _dot(xb, wgu_ref[:, d_ff + lo:d_ff + lo + ff_chunk])
        a = (g * jax.nn.sigmoid(g) * u).astype(BF16)
        part = _dot(a, wd_ref[lo:lo + ff_chunk, :])
        acc = part if acc is None else acc + part
    o_ref[...] = _layer_norm(alpha * x + 0.5 * acc, g_ref[...], b_ref[...])


def _ffn_ln(x, w_gu, w_down, g, b, alpha):
    T, D = x.shape
    d_ff = w_down.shape[0]
    tm = min(TM_FFN, T)
    ff_chunk = FF_CHUNK if d_ff % FF_CHUNK == 0 else d_ff
    return pl.pallas_call(
        functools.partial(_ffn_ln_kernel, alpha=alpha, d_ff=d_ff, ff_chunk=ff_chunk),
        grid=(T // tm,),
        in_specs=[pl.BlockSpec((tm, D), lambda i: (i, 0)),
                  _const_spec((D, 2 * d_ff)), _const_spec((d_ff, D)),
                  _const_spec((1, D)), _const_spec((1, D))],
        out_specs=pl.BlockSpec((tm, D), lambda i: (i, 0)),
        out_shape=jax.ShapeDtypeStruct((T, D), F32),
        compiler_params=_cp(("parallel",)),
        name="ffn_ln",
    )(x, w_gu, w_down, g.reshape(1, D), b.reshape(1, D))


def _outproj_ln_kernel(x_ref, o0_ref, o1_ref, o2_ref, o3_ref, w_ref, g_ref, b_ref, out_ref, *, alpha):
    gw = o0_ref.shape[1]
    mix = None
    for k, o_ref in enumerate((o0_ref, o1_ref, o2_ref, o3_ref)):
        part = _dot(o_ref[...], w_ref[k * gw:(k + 1) * gw, :])
        mix = part if mix is None else mix + part
    out_ref[...] = _layer_norm(alpha * x_ref[...] + mix, g_ref[...], b_ref[...])


def _outproj_ln(x, outs, w_out, g, b, alpha):
    T, D = x.shape
    tm = min(TM_FFN, T)
    gw = outs[0].shape[1]
    return pl.pallas_call(
        functools.partial(_outproj_ln_kernel, alpha=alpha),
        grid=(T // tm,),
        in_specs=[pl.BlockSpec((tm, D), lambda i: (i, 0))]
        + [pl.BlockSpec((tm, gw), lambda i: (i, 0))] * 4
        + [_const_spec(w_out.shape), _const_spec((1, D)), _const_spec((1, D))],
        out_specs=pl.BlockSpec((tm, D), lambda i: (i, 0)),
        out_shape=jax.ShapeDtypeStruct((T, D), F32),
        compiler_params=_cp(("parallel",)),
        name="outproj_ln",
    )(x, *outs, w_out, g.reshape(1, D), b.reshape(1, D))


N_ROPE_BLK = 9
N_MAIN_BLK = 27


def _proj_kernel(x_ref, wm_ref, wr_ref, tab_ref, hb_ref, hg_ref, kc_ref, gt_ref):
    xb = x_ref[...].astype(BF16)
    nr = N_ROPE_BLK * LANE
    h = _dot(xb, wm_ref[:, :nr])
    hr = _dot(xb, wr_ref[...])
    tab = tab_ref[...]
    for c in range(N_ROPE_BLK):
        t0 = 0 if c < 4 else (2 if c < 6 else 4)
        cs = tab[:, t0 * LANE:(t0 + 1) * LANE]
        sn = tab[:, (t0 + 1) * LANE:(t0 + 2) * LANE]
        val = h[:, c * LANE:(c + 1) * LANE] * cs + hr[:, c * LANE:(c + 1) * LANE] * sn
        if c < 8:
            hb_ref[:, c * LANE:(c + 1) * LANE] = val.astype(BF16)
        else:
            kc_ref[...] = val
    hp = _dot(xb, wm_ref[:, 9 * LANE:17 * LANE])
    hb_ref[:, 8 * LANE:16 * LANE] = hp.astype(BF16)
    hf = _dot(xb, wm_ref[:, 17 * LANE:27 * LANE])
    hg_ref[...] = hf[:, :9 * LANE]
    gt_ref[...] = hf[:, 9 * LANE:10 * LANE]


def _proj(x, wm, wr, tab, seq_len):
    T, D = x.shape
    tm = min(TM_PROJ, seq_len)
    nst = seq_len // tm
    return pl.pallas_call(
        _proj_kernel,
        grid=(T // tm,),
        in_specs=[pl.BlockSpec((tm, D), lambda i: (i, 0)),
                  _const_spec(wm.shape), _const_spec(wr.shape),
                  pl.BlockSpec((tm, 6 * LANE), lambda i: (i % nst, 0))],
        out_specs=[pl.BlockSpec((tm, 16 * LANE), lambda i: (i, 0)),
                   pl.BlockSpec((tm, 9 * LANE), lambda i: (i, 0)),
                   pl.BlockSpec((tm, LANE), lambda i: (i, 0)),
                   pl.BlockSpec((tm, LANE), lambda i: (i, 0))],
        out_shape=[jax.ShapeDtypeStruct((T, 16 * LANE), BF16),
                   jax.ShapeDtypeStruct((T, 9 * LANE), F32),
                   jax.ShapeDtypeStruct((T, LANE), F32),
                   jax.ShapeDtypeStruct((T, LANE), F32)],
        compiler_params=_cp(("parallel",)),
        name="in_proj",
    )(x, wm, wr, tab)


def _rot_cols(w, d):
    k, n = w.shape
    w4 = w.reshape(k, n // d, 2, d // 2)
    return jnp.concatenate([-w4[:, :, 1], w4[:, :, 0]], axis=2).reshape(k, n)


def _prep_w_in(w):
    k = w.shape[0]
    sizes = ((HEADS * DIFF_QK,) * 4 + (GROUP_W,) + (GROUP_W,) * 4 + (HEADS,) * 2
             + (GROUP_W,) + (HEAD_DIM,) * 6 + (3 * HEADS,) + (GROUP_W,) * 3)
    offs = np.concatenate([[0], np.cumsum(sizes)])
    (dq1, dq2, dk1, dk2, dv, gq, gk, gv, gz, ga, gb,
     nq, nkc, nvc, nks, nvs, nkw, nvw, ngate, sq, sk, sv) = [w[:, offs[i]:offs[i + 1]] for i in range(len(sizes))]
    scale = HEAD_DIM ** -0.5
    z64 = jnp.zeros((k, HEAD_DIM), w.dtype)
    pad = lambda a: jnp.concatenate([a, jnp.zeros((k, LANE - a.shape[1]), w.dtype)], axis=1)
    main = jnp.concatenate(
        [dq1, dq2, dk1, dk2, nq * scale, nks, nvs, nkw, nvw, nkc, nvc,
         dv, sq * scale, sk, sv, gq, gk, gv, gz, pad(jnp.concatenate([ga, gb], axis=1)), pad(ngate)], axis=1)
    r32 = lambda a: _rot_cols(a, DIFF_QK)
    r64 = lambda a: _rot_cols(a, HEAD_DIM)
    rot = jnp.concatenate(
        [r32(dq1), r32(dq2), r32(dk1), r32(dk2), r64(nq * scale),
         r64(nks), z64, r64(nkw), z64, r64(nkc), z64], axis=1)
    return main.astype(BF16), rot.astype(BF16)


def _rope_table(seq_len):
    def cs(dim):
        inv = ROPE_THETA ** (-jnp.arange(0, dim, 2, dtype=F32) / dim)
        ang = jnp.arange(seq_len, dtype=F32)[:, None] * inv[None, :]
        return jnp.cos(ang), jnp.sin(ang)
    cd, sd = cs(DIFF_QK)
    cn, sn = cs(HEAD_DIM)
    one = jnp.ones((seq_len, HEAD_DIM), F32)
    zero = jnp.zeros((seq_len, HEAD_DIM), F32)
    return jnp.concatenate(
        [jnp.tile(cd, (1, 8)), jnp.tile(sd, (1, 8)), jnp.tile(cn, (1, 4)), jnp.tile(sn, (1, 4)),
         jnp.tile(cn, (1, 2)), one, jnp.tile(sn, (1, 2)), zero], axis=1)


def _tri_pairs(n, descending=False):
    qi, kj = [], []
    for i in range(n):
        js = range(i, -1, -1) if descending else range(i + 1)
        for j in js:
            qi.append(i)
            kj.append(j)
    return jnp.asarray(qi, jnp.int32), jnp.asarray(kj, jnp.int32)


def _diff_kernel(qi_ref, kj_ref, q1_ref, q2_ref, k1_ref, k2_ref, v_ref, lam_ref, g_ref, o_ref,
                 m_scr, l_scr, acc_scr, *, lam_init):
    p = pl.program_id(1)
    i = qi_ref[p]
    j = kj_ref[p]
    tq, tk = q1_ref.shape[0], k1_ref.shape[0]
    c = (DIFF_QK ** -0.5) * LOG2E
    head32 = _iota((1, LANE), 1) // DIFF_QK
    head64 = _iota((1, GROUP_W), 1) // HEAD_DIM

    @pl.when(j == 0)
    def _init():
        m_scr[...] = jnp.full(m_scr.shape, NEG, F32)
        l_scr[...] = jnp.zeros(l_scr.shape, F32)
        acc_scr[...] = jnp.zeros(acc_scr.shape, F32)

    def step(masked):
        v = v_ref[...]
        qs = (q1_ref[...], q2_ref[...])
        ks = (k1_ref[...], k2_ref[...])
        if masked:
            causal = _iota((tq, tk), 1) <= _iota((tq, tk), 0)
        for t in range(2):
            pv = None
            alpha_full = None
            for h in range(HEADS):
                km = jnp.where(head32 == h, ks[t], jnp.zeros_like(ks[t]))
                s = _dot_nt(qs[t], km)
                if masked:
                    s = jnp.where(causal, s, NEG)
                idx = t * HEADS + h
                m_prev = m_scr[idx]
                m_new = jnp.maximum(m_prev, jnp.max(s, axis=-1, keepdims=True))
                pe = jnp.exp2((s - m_new) * c)
                alpha = jnp.exp2((m_prev - m_new) * c)
                l_scr[idx] = alpha * l_scr[idx] + jnp.sum(pe, axis=-1, keepdims=True)
                m_scr[idx] = m_new
                vm = jnp.where(head64 == h, v, jnp.zeros_like(v))
                d = _dot(pe.astype(BF16), vm)
                pv = d if pv is None else pv + d
                alpha_full = alpha if h == 0 else jnp.where(head64 == h, alpha, alpha_full)
            acc_scr[t] = acc_scr[t] * alpha_full + pv

    @pl.when(j < i)
    def _off():
        step(False)

    @pl.when(j == i)
    def _diag():
        step(True)
        lp = lam_ref[...]
        lam = (jnp.exp(jnp.sum(lp[0:1] * lp[1:2], axis=-1, keepdims=True))
               - jnp.exp(jnp.sum(lp[2:3] * lp[3:4], axis=-1, keepdims=True)) + lam_init)
        lf = []
        for t in range(2):
            full = l_scr[t * HEADS]
            for h in range(1, HEADS):
                full = jnp.where(head64 == h, l_scr[t * HEADS + h], full)
            lf.append(full)
        o = acc_scr[0] / lf[0] - lam * (acc_scr[1] / lf[1])
        o2 = o * o
        ssq = None
        for h in range(HEADS):
            sh = jnp.sum(jnp.where(head64 == h, o2, 0.0), axis=-1, keepdims=True)
            ssq = sh if h == 0 else jnp.where(head64 == h, sh, ssq)
        y = o * lax.rsqrt(ssq * (1.0 / HEAD_DIM) + NORM_EPS) * g_ref[...] * (1.0 - lam_init)
        o_ref[...] = y.astype(o_ref.dtype)


def _diff_attention(hb, lam_p, subln_g, lam_init, batch, seq_len):
    t = min(T_ATT, seq_len)
    nq = seq_len // t
    qi, kj = _tri_pairs(nq)
    g_full = jnp.tile(subln_g, HEADS).reshape(1, GROUP_W)
    qmap = lambda col: (lambda b, p, qi, kj: (b * nq + qi[p], col))
    kmap = lambda col: (lambda b, p, qi, kj: (b * nq + kj[p], col))
    grid_spec = pltpu.PrefetchScalarGridSpec(
        num_scalar_prefetch=2,
        grid=(batch, qi.shape[0]),
        in_specs=[pl.BlockSpec((t, LANE), qmap(0)), pl.BlockSpec((t, LANE), qmap(1)),
                  pl.BlockSpec((t, LANE), kmap(2)), pl.BlockSpec((t, LANE), kmap(3)),
                  pl.BlockSpec((t, GROUP_W), kmap(4)),
                  pl.BlockSpec((4, DIFF_QK), lambda b, p, qi, kj: (0, 0)),
                  pl.BlockSpec((1, GROUP_W), lambda b, p, qi, kj: (0, 0))],
        out_specs=pl.BlockSpec((t, GROUP_W), lambda b, p, qi, kj: (b * nq + qi[p], 0)),
        scratch_shapes=[pltpu.VMEM((2 * HEADS, t, 1), F32), pltpu.VMEM((2 * HEADS, t, 1), F32),
                        pltpu.VMEM((2, t, GROUP_W), F32)])
    return pl.pallas_call(
        functools.partial(_diff_kernel, lam_init=lam_init),
        grid_spec=grid_spec,
        out_shape=jax.ShapeDtypeStruct((batch * seq_len, GROUP_W), BF16),
        compiler_params=_cp(("parallel", "arbitrary")),
        name="diff_attn",
    )(qi, kj, hb, hb, hb, hb, hb, lam_p, g_full)


def _sb_kernel(qi_ref, kj_ref, q_ref, k_ref, v_ref, o_ref, carry_scr, acc_scr):
    p = pl.program_id(1)
    i = qi_ref[p]
    j = kj_ref[p]
    tq, tk = q_ref.shape[0], k_ref.shape[0]
    head64 = _iota((1, GROUP_W), 1) // HEAD_DIM

    @pl.when(j == i)
    def _init():
        carry_scr[...] = jnp.zeros(carry_scr.shape, F32)
        acc_scr[...] = jnp.zeros(acc_scr.shape, F32)

    def step(masked):
        q, k, v = q_ref[...], k_ref[...], v_ref[...]
        m_incl = jnp.where(_iota((tk, tk), 0) >= _iota((tk, tk), 1), 1.0, 0.0).astype(BF16)
        if masked:
            before = _iota((tq, tk), 1) < _iota((tq, tk), 0)
        pv = None
        for h in range(HEADS):
            km = jnp.where(head64 == h, k, jnp.zeros_like(k))
            z = _dot_nt(q, km)
            lk = -(jnp.maximum(z, 0.0) + jnp.log1p(jnp.exp(-jnp.abs(z))))
            if masked:
                lk = jnp.where(before, lk, 0.0)
            cum = _dot_x2(lk, m_incl)
            att = jnp.exp(z + cum + carry_scr[h])
            if masked:
                att = jnp.where(before, att, 0.0)
            carry_scr[h] = carry_scr[h] + cum[:, 0:1]
            vm = jnp.where(head64 == h, v, jnp.zeros_like(v))
            d = _dot(att.astype(BF16), vm)
            pv = d if pv is None else pv + d
        acc_scr[...] = acc_scr[...] + pv

    @pl.when(j == i)
    def _diag():
        step(True)

    @pl.when(j < i)
    def _off():
        step(False)

    @pl.when(j == 0)
    def _fin():
        o_ref[...] = acc_scr[...].astype(o_ref.dtype)


def _sb_attention(hb, batch, seq_len):
    t = min(T_ATT, seq_len)
    nq = seq_len // t
    qi, kj = _tri_pairs(nq, descending=True)
    grid_spec = pltpu.PrefetchScalarGridSpec(
        num_scalar_prefetch=2,
        grid=(batch, qi.shape[0]),
        in_specs=[pl.BlockSpec((t, GROUP_W), lambda b, p, qi, kj: (b * nq + qi[p], 5)),
                  pl.BlockSpec((t, GROUP_W), lambda b, p, qi, kj: (b * nq + kj[p], 6)),
                  pl.BlockSpec((t, GROUP_W), lambda b, p, qi, kj: (b * nq + kj[p], 7))],
        out_specs=pl.BlockSpec((t, GROUP_W), lambda b, p, qi, kj: (b * nq + qi[p], 0)),
        scratch_shapes=[pltpu.VMEM((HEADS, t, 1), F32), pltpu.VMEM((t, GROUP_W), F32)])
    return pl.pallas_call(
        _sb_kernel,
        grid_spec=grid_spec,
        out_shape=jax.ShapeDtypeStruct((batch * seq_len, GROUP_W), BF16),
        compiler_params=_cp(("parallel", "arbitrary")),
        name="sb_attn",
    )(qi, kj, hb, hb, hb)


def _stack_heads(q):
    qf = q.astype(F32)
    lo = _iota((1, LANE), 1) < HEAD_DIM
    parts = []
    for blk in (qf[:, :LANE], qf[:, LANE:]):
        parts.append(jnp.where(lo, blk, 0.0))
        parts.append(jnp.where(lo, pltpu.roll(blk, HEAD_DIM, 1), 0.0))
    return jnp.concatenate(parts, axis=0).astype(BF16)


def _unstack_heads(r, tq):
    lo = _iota((1, LANE), 1) < HEAD_DIM
    blk0 = jnp.where(lo, pltpu.roll(r[0:tq], HEAD_DIM, 1), r[tq:2 * tq])
    blk1 = jnp.where(lo, pltpu.roll(r[2 * tq:3 * tq], HEAD_DIM, 1), r[3 * tq:4 * tq])
    return jnp.concatenate([blk0, blk1], axis=1)


def _nsa_compress_kernel(r_ref, pe_ref, w1lo_ref, w1hi_ref, w2_ref, o_ref):
    r = r_ref[...]
    n = r.shape[0]
    y1 = _dot((r + pe_ref[0:1, :]).astype(BF16), w1lo_ref[...])
    y2 = _dot((r + pe_ref[1:2, :]).astype(BF16), w1hi_ref[...])
    hid = y1 + pltpu.roll(y2, n - 1, 0)
    act = (hid * jax.nn.sigmoid(hid)).astype(BF16)
    o_ref[...] = _dot(act, w2_ref[...]).astype(o_ref.dtype)


def _nsa_compress(kcvc, pe_k, pe_v, ck_w1, ck_w2, cv_w1, cv_w2, batch, seq_len):
    nrow = seq_len // CMP_STRIDE
    width = CMP_STRIDE * LANE
    r = kcvc.reshape(batch * nrow, width)
    hid = ck_w1.shape[1]
    pe = jnp.concatenate([pe_k, pe_v], axis=1).reshape(2, width)
    zk = jnp.zeros((CMP_BLOCK, HEAD_DIM, hid), F32)
    w1 = jnp.concatenate(
        [jnp.concatenate([ck_w1.reshape(CMP_BLOCK, HEAD_DIM, hid), zk], axis=2),
         jnp.concatenate([zk, cv_w1.reshape(CMP_BLOCK, HEAD_DIM, hid)], axis=2)], axis=1)
    w1 = w1.reshape(2, width, 2 * hid).astype(BF16)
    zo = jnp.zeros((hid, HEAD_DIM), F32)
    w2 = jnp.concatenate([jnp.concatenate([ck_w2, zo], axis=1),
                          jnp.concatenate([zo, cv_w2], axis=1)], axis=0).astype(BF16)
    return pl.pallas_call(
        _nsa_compress_kernel,
        grid=(batch,),
        in_specs=[pl.BlockSpec((nrow, width), lambda b: (b, 0)),
                  _const_spec((2, width)), _const_spec((width, 2 * hid)), _const_spec((width, 2 * hid)),
                  _const_spec((2 * hid, LANE))],
        out_specs=pl.BlockSpec((nrow, LANE), lambda b: (b, 0)),
        out_shape=jax.ShapeDtypeStruct((batch * nrow, LANE), BF16),
        compiler_params=_cp(("parallel",)),
        name="nsa_compress",
    )(r, pe, w1[0], w1[1], w2)


def _nsa_cmp_kernel(q_ref, kv_ref, ov_ref, ocmp_ref, sel_ref):
    i = pl.program_id(1)
    tq = q_ref.shape[0]
    ncmp = kv_ref.shape[0]
    nslc = ov_ref.shape[1]
    qs = _stack_heads(q_ref[...])
    kv = kv_ref[...]
    s = _dot_nt(qs, kv)
    tpos = i * tq + (_iota((HEADS * tq, 1), 0) & (tq - 1))
    cend = _iota((1, ncmp), 1) * CMP_STRIDE + (CMP_BLOCK - 1)
    cm = cend <= tpos
    s = jnp.where(cm, s, NEG)
    e = jnp.exp(s - jnp.max(s, axis=-1, keepdims=True))
    pr = jnp.where(cm, e / jnp.sum(e, axis=-1, keepdims=True), 0.0)
    ocmp_ref[...] = _unstack_heads(_dot(pr.astype(BF16), kv), tq)
    psum = pr[0:tq] + pr[tq:2 * tq] + pr[2 * tq:3 * tq] + pr[3 * tq:4 * tq]
    imp = _dot_x2(psum, ov_ref[...])
    blk = _iota((1, nslc), 1)
    cur = (i * tq + _iota((tq, 1), 0)) // SLC_BLOCK
    work = jnp.where(blk == 0, FORCE, jnp.where(blk == cur, FORCE, jnp.where(blk == cur - 1, FORCE, imp)))
    work = jnp.where(blk <= cur, work, -FORCE)
    sel = jnp.zeros((tq, nslc), F32)
    for _ in range(min(SLC_TOPN, nslc)):
        mx = jnp.max(work, axis=-1, keepdims=True)
        first = jnp.min(jnp.where(work == mx, blk, nslc), axis=-1, keepdims=True)
        hit = blk == first
        sel = jnp.where(hit, 1.0, sel)
        work = jnp.where(hit, -jnp.inf, work)
    sel_ref[...] = sel.astype(sel_ref.dtype)


def _nsa_cmp(hb, kvcmp, batch, seq_len):
    tq = min(TQ_CMP, seq_len)
    nq = seq_len // tq
    ncmp = seq_len // CMP_STRIDE
    nslc = seq_len // SLC_BLOCK
    cstart = np.arange(ncmp)[:, None] * CMP_STRIDE
    sstart = np.arange(nslc)[None, :] * SLC_BLOCK
    ov = (cstart < sstart + SLC_BLOCK) & (cstart + CMP_BLOCK - 1 >= sstart)
    ov &= (np.arange(ncmp)[:, None] < (seq_len - CMP_BLOCK) // CMP_STRIDE + 1)
    ov = jnp.asarray(ov, BF16)
    return pl.pallas_call(
        _nsa_cmp_kernel,
        grid=(batch, nq),
        in_specs=[pl.BlockSpec((tq, GROUP_W), lambda b, i: (b * nq + i, 2)),
                  pl.BlockSpec((ncmp, LANE), lambda b, i: (b, 0)),
                  pl.BlockSpec((ncmp, nslc), lambda b, i: (0, 0))],
        out_specs=[pl.BlockSpec((tq, GROUP_W), lambda b, i: (b * nq + i, 0)),
                   pl.BlockSpec((tq, nslc), lambda b, i: (b * nq + i, 0))],
        out_shape=[jax.ShapeDtypeStruct((batch * seq_len, GROUP_W), F32),
                   jax.ShapeDtypeStruct((batch * seq_len, nslc), BF16)],
        compiler_params=_cp(("parallel", "parallel")),
        name="nsa_cmp_select",
    )(hb, kvcmp, ov)


def _nsa_win_kernel(q_ref, kv_ref, o_ref, *, window):
    i = pl.program_id(1)
    tq = q_ref.shape[0]
    span = tq + window
    base = pl.multiple_of(jnp.maximum(i * tq - window, 0), LANE)
    kv = kv_ref[pl.ds(base, span), :]
    s = _dot_nt(_stack_heads(q_ref[...]), kv)
    qpos = i * tq + (_iota((HEADS * tq, 1), 0) & (tq - 1))
    rel = qpos - (base + _iota((1, span), 1))
    s = jnp.where(rel >= 0, jnp.where(rel < window, s, NEG), NEG)
    e = jnp.exp(s - jnp.max(s, axis=-1, keepdims=True))
    pr = e / jnp.sum(e, axis=-1, keepdims=True)
    o_ref[...] = _unstack_heads(_dot(pr.astype(BF16), kv), tq)


def _nsa_window(hb, batch, seq_len):
    tq = min(TQ_WIN, seq_len)
    nq = seq_len // tq
    window = min(WINDOW, seq_len - tq)
    return pl.pallas_call(
        functools.partial(_nsa_win_kernel, window=window),
        grid=(batch, nq),
        in_specs=[pl.BlockSpec((tq, GROUP_W), lambda b, i: (b * nq + i, 2)),
                  pl.BlockSpec((seq_len, LANE), lambda b, i: (b, 7))],
        out_specs=pl.BlockSpec((tq, GROUP_W), lambda b, i: (b * nq + i, 0)),
        out_shape=jax.ShapeDtypeStruct((batch * seq_len, GROUP_W), F32),
        compiler_params=_cp(("parallel", "parallel")),
        name="nsa_window",
    )(hb, hb)


def _nsa_sel_kernel(qi_ref, kj_ref, q_ref, kv_ref, sel_ref, ocmp_ref, owin_ref, gate_ref, o_ref,
                    qs_scr, m_scr, l_scr, acc_scr):
    p = pl.program_id(1)
    i = qi_ref[p]
    j = kj_ref[p]
    tq, tk = q_ref.shape[0], kv_ref.shape[0]
    nslc = sel_ref.shape[1]

    @pl.when(j == 0)
    def _init():
        qs_scr[...] = _stack_heads(q_ref[...])
        m_scr[...] = jnp.full(m_scr.shape, NEG, F32)
        l_scr[...] = jnp.zeros(l_scr.shape, F32)
        acc_scr[...] = jnp.zeros(acc_scr.shape, F32)

    kv = kv_ref[...]
    kpos = j * tk + _iota((1, tk), 1)
    expand = jnp.where(kpos // SLC_BLOCK == _iota((nslc, 1), 0), 1.0, 0.0).astype(BF16)
    picked = _dot(sel_ref[...], expand)
    qpos = i * tq + _iota((tq, 1), 0)
    ok = jnp.where(kpos <= qpos, picked, 0.0)
    ok4 = jnp.concatenate([ok] * HEADS, axis=0) > 0.5
    s = jnp.where(ok4, _dot_nt(qs_scr[...], kv), NEG)
    m_prev = m_scr[...]
    m_new = jnp.maximum(m_prev, jnp.max(s, axis=-1, keepdims=True))
    pe = jnp.exp(s - m_new)
    alpha = jnp.exp(m_prev - m_new)
    l_scr[...] = alpha * l_scr[...] + jnp.sum(pe, axis=-1, keepdims=True)
    m_scr[...] = m_new
    acc_scr[...] = alpha * acc_scr[...] + _dot(pe.astype(BF16), kv)

    @pl.when(j == (i * tq) // tk)
    def _fin():
        osel = _unstack_heads(acc_scr[...] / l_scr[...], tq)
        sig = jax.nn.sigmoid(gate_ref[...])
        grow = _iota((LANE, 1), 0)
        ghead = _iota((1, GROUP_W), 1) // HEAD_DIM
        out = None
        for br, o_br in enumerate((ocmp_ref[...], osel, owin_ref[...])):
            e_br = jnp.where(grow == 3 * ghead + br, 1.0, 0.0).astype(BF16)
            term = _dot_x3(sig, e_br) * o_br
            out = term if out is None else out + term
        o_ref[...] = out.astype(o_ref.dtype)


def _nsa_select(hb, sel, ocmp, owin, gates, batch, seq_len):
    tq = min(TQ_SEL, seq_len)
    tk = min(TK_SEL, seq_len)
    nq, nk = seq_len // tq, seq_len // tk
    nslc = seq_len // SLC_BLOCK
    qi, kj = [], []
    for i in range(nq):
        for j in range((i * tq) // tk + 1):
            qi.append(i)
            kj.append(j)
    qi, kj = jnp.asarray(qi, jnp.int32), jnp.asarray(kj, jnp.int32)
    qrow = lambda b, p, qi, kj: (b * nq + qi[p], 0)
    grid_spec = pltpu.PrefetchScalarGridSpec(
        num_scalar_prefetch=2,
        grid=(batch, qi.shape[0]),
        in_specs=[pl.BlockSpec((tq, GROUP_W), lambda b, p, qi, kj: (b * nq + qi[p], 2)),
                  pl.BlockSpec((tk, LANE), lambda b, p, qi, kj: (b * nk + kj[p], 6)),
                  pl.BlockSpec((tq, nslc), qrow),
                  pl.BlockSpec((tq, GROUP_W), qrow), pl.BlockSpec((tq, GROUP_W), qrow),
                  pl.BlockSpec((tq, LANE), qrow)],
        out_specs=pl.BlockSpec((tq, GROUP_W), qrow),
        scratch_shapes=[pltpu.VMEM((HEADS * tq, LANE), BF16), pltpu.VMEM((HEADS * tq, 1), F32),
                        pltpu.VMEM((HEADS * tq, 1), F32), pltpu.VMEM((HEADS * tq, LANE), F32)])
    return pl.pallas_call(
        _nsa_sel_kernel,
        grid_spec=grid_spec,
        out_shape=jax.ShapeDtypeStruct((batch * seq_len, GROUP_W), BF16),
        compiler_params=_cp(("parallel", "arbitrary")),
        name="nsa_select_gate",
    )(qi, kj, hb, hb, sel, ocmp, owin, gates)


def _nsa(hb, kcvc, gates, pe_k, pe_v, ck_w1, ck_w2, cv_w1, cv_w2, batch, seq_len):
    kvcmp = _nsa_compress(kcvc, pe_k, pe_v, ck_w1, ck_w2, cv_w1, cv_w2, batch, seq_len)
    ocmp, sel = _nsa_cmp(hb, kvcmp, batch, seq_len)
    owin = _nsa_window(hb, batch, seq_len)
    return _nsa_select(hb, sel, ocmp, owin, gates, batch, seq_len)


def _bd(mc, mask_bd):
    return jnp.where(mask_bd, jnp.concatenate([mc] * HEADS, axis=0), jnp.zeros((), mc.dtype))


def _gdn_prep_kernel(x_ref, ab_ref, cw_ref, alog_ref, dtb_ref,
                     u_ref, w_ref, qd_ref, in_ref, kdt_ref, gl_ref, xpad_scr):
    n = pl.program_id(1)
    C = GDN_CHUNK
    W = GROUP_W

    @pl.when(n == 0)
    def _():
        xpad_scr[0:8, :] = jnp.zeros((8, xpad_scr.shape[1]), F32)

    x = x_ref[...]
    xpad_scr[8:8 + C, :] = x
    cw = cw_ref[...]
    conv = (cw[0:1] * xpad_scr[5:5 + C, :] + cw[1:2] * xpad_scr[6:6 + C, :]
            + cw[2:3] * xpad_scr[7:7 + C, :] + cw[3:4] * x)
    xpad_scr[0:8, :] = x[C - 8:C, :]
    qkv = conv * jax.nn.sigmoid(conv)
    q, k, v = qkv[:, 0:W], qkv[:, W:2 * W], qkv[:, 2 * W:3 * W]

    r256 = _iota((W, W), 0)
    c256 = _iota((W, W), 1)
    mask_bd = (r256 // HEAD_DIM) == (c256 // HEAD_DIM)
    ones_bd = jnp.where(mask_bd, 1.0, 0.0).astype(BF16)
    eye256 = jnp.where(r256 == c256, 1.0, 0.0).astype(BF16)
    row = _iota((C, W), 0)
    jl = _iota((C, W), 1) % HEAD_DIM
    ltri = jnp.where(_iota((C, C), 1) <= _iota((C, C), 0), 1.0, 0.0).astype(BF16)

    qn = q * lax.rsqrt(_dot_x2(q * q, ones_bd) + NORM_EPS) * (HEAD_DIM ** -0.5)
    kn = k * lax.rsqrt(_dot_x2(k * k, ones_bd) + NORM_EPS)

    ab = ab_ref[...]
    z = ab + dtb_ref[...]
    g128 = -jnp.exp(alog_ref[...]) * (jnp.maximum(z, 0.0) + jnp.log1p(jnp.exp(-jnp.abs(z))))
    beta128 = jax.nn.sigmoid(ab)
    erow = _iota((LANE, W), 0)
    ehead = _iota((LANE, W), 1) // HEAD_DIM
    g_hl = _dot_x3(g128, jnp.where(erow == ehead, 1.0, 0.0).astype(BF16))
    beta = _dot_x3(beta128, jnp.where(erow == ehead + HEADS, 1.0, 0.0).astype(BF16))

    gc = _dot_x3_left(ltri, g_hl)
    glast = gc[C - 1:C, :]
    exp_g = jnp.exp(gc)
    gl_ref[0] = jnp.exp(glast)
    dmat = _dot_x3_left(ltri, jnp.where(row > jl, g_hl, 0.0))
    decay = jnp.where(jl <= row, jnp.exp(dmat), 0.0)

    kn_b = kn.astype(BF16)
    kt4 = _dot_nt(eye256, jnp.concatenate([kn_b] * HEADS, axis=0))
    kb_mat = jnp.where(mask_bd, kt4, 0.0).astype(BF16)
    kbeta = kn * beta
    a_c = jnp.where(jl < row, _dot(kbeta.astype(BF16), kb_mat) * decay, 0.0)
    in_ref[...] = (_dot(qn.astype(BF16), kb_mat) * decay).astype(in_ref.dtype)

    t_c = jnp.where(jl == row, 1.0, 0.0) - a_c
    p_c = a_c
    for _ in range(5):
        ph, pl_ = _split2(_bd(p_c, mask_bd))
        p_c = _mm3(p_c, ph, pl_)
        ph, pl_ = _split2(_bd(p_c, mask_bd))
        t_c = t_c + _mm3(t_c, ph, pl_)

    vh, vl = _split2(_bd(v * beta, mask_bd))
    u_ref[...] = _mm3(t_c, vh, vl)
    wh, wl = _split2(_bd(kbeta * exp_g, mask_bd))
    w_ref[...] = _mm3(t_c, wh, wl).astype(w_ref.dtype)
    qd_ref[...] = (qn * exp_g).astype(qd_ref.dtype)
    kd = (kn * jnp.exp(glast - gc)).astype(BF16)
    kdt_ref[0] = _dot_nt(eye256, kd).astype(kdt_ref.dtype)


def _dot_x3_left(w, x):
    hi = x.astype(BF16)
    r = x - hi.astype(F32)
    mid = r.astype(BF16)
    lo = (r - mid.astype(F32)).astype(BF16)
    return _dot(w, hi) + _dot(w, mid) + _dot(w, lo)


def _gdn_scan_kernel(u_ref, w_ref, qd_ref, in_ref, kdt_ref, gl_ref, z_ref, g_ref, o_ref, s_scr):
    n = pl.program_id(0)
    W = GROUP_W

    @pl.when(n == 0)
    def _():
        s_scr[...] = jnp.zeros(s_scr.shape, F32)

    mask_bd = (_iota((W, W), 0) // HEAD_DIM) == (_iota((W, W), 1) // HEAD_DIM)
    ones_bd = jnp.where(mask_bd, 1.0, 0.0).astype(BF16)
    for b in range(u_ref.shape[0]):
        s = s_scr[b]
        sb = s.astype(BF16)
        v_new = u_ref[b] - _dot(w_ref[b], sb)
        vb = v_new.astype(BF16)
        o = _dot(qd_ref[b], sb) + _dot(in_ref[b], _bd(vb, mask_bd))
        s_scr[b] = s * gl_ref[b, 0] + jnp.where(mask_bd, _dot(kdt_ref[b, 0], vb), 0.0)
        ms = _dot_x2(o * o, ones_bd) * (1.0 / HEAD_DIM)
        zz = z_ref[b]
        y = o * lax.rsqrt(ms + NORM_EPS) * g_ref[...] * (zz * jax.nn.sigmoid(zz))
        o_ref[b] = y.astype(o_ref.dtype)


def _gdn(hg, conv_w, a_log, dt_bias, norm_g, batch, seq_len):
    C = GDN_CHUNK
    nc = seq_len // C
    T = batch * seq_len
    W = GROUP_W
    padl = lambda a: jnp.concatenate([a, jnp.zeros((LANE - a.shape[0],), F32)]).reshape(1, LANE)
    row = lambda b, n: (b * nc + n, 0)
    u, w, qd, intra, kdt, gl = pl.pallas_call(
        _gdn_prep_kernel,
        grid=(batch, nc),
        in_specs=[pl.BlockSpec((C, 3 * W), row),
                  pl.BlockSpec((C, LANE), lambda b, n: (b * nc + n, 8)),
                  pl.BlockSpec((4, 3 * W), lambda b, n: (0, 0)),
                  pl.BlockSpec((1, LANE), lambda b, n: (0, 0)),
                  pl.BlockSpec((1, LANE), lambda b, n: (0, 0))],
        out_specs=[pl.BlockSpec((C, W), row)] * 4
        + [pl.BlockSpec((1, W, C), lambda b, n: (b * nc + n, 0, 0)),
           pl.BlockSpec((1, 1, W), lambda b, n: (b * nc + n, 0, 0))],
        out_shape=[jax.ShapeDtypeStruct((T, W), F32), jax.ShapeDtypeStruct((T, W), BF16),
                   jax.ShapeDtypeStruct((T, W), BF16), jax.ShapeDtypeStruct((T, W), BF16),
                   jax.ShapeDtypeStruct((batch * nc, W, C), BF16),
                   jax.ShapeDtypeStruct((batch * nc, 1, W), F32)],
        scratch_shapes=[pltpu.VMEM((8 + C, 3 * W), F32)],
        compiler_params=_cp(("parallel", "arbitrary")),
        name="gdn_prep",
    )(hg, hg, conv_w, padl(a_log), padl(dt_bias))

    r3 = lambda a: a.reshape(batch, seq_len, a.shape[-1])
    blk = pl.BlockSpec((batch, C, W), lambda n: (0, n, 0))
    out = pl.pallas_call(
        _gdn_scan_kernel,
        grid=(nc,),
        in_specs=[blk, blk, blk, blk,
                  pl.BlockSpec((batch, 1, W, C), lambda n: (0, n, 0, 0)),
                  pl.BlockSpec((batch, 1, 1, W), lambda n: (0, n, 0, 0)),
                  pl.BlockSpec((batch, C, W), lambda n: (0, n, 3)),
                  pl.BlockSpec((1, W), lambda n: (0, 0))],
        out_specs=blk,
        out_shape=jax.ShapeDtypeStruct((batch, seq_len, W), BF16),
        scratch_shapes=[pltpu.VMEM((batch, W, W), F32)],
        compiler_params=_cp(("arbitrary",)),
        name="gdn_scan",
    )(r3(u), r3(w), r3(qd), r3(intra), kdt.reshape(batch, nc, W, C), gl.reshape(batch, nc, 1, W),
      r3(hg), jnp.tile(norm_g, HEADS).reshape(1, W))
    return out.reshape(T, W)


def kernel(x, w_in, w_out, ffn1_w_gu, ffn1_w_down, ffn2_w_gu, ffn2_w_down, ln1_g, ln1_b, ln2_g, ln2_b, ln3_g, ln3_b, diff_lam_q1, diff_lam_k1, diff_lam_q2, diff_lam_k2, diff_subln_g, gdn_conv_w, gdn_a_log, gdn_dt_bias, gdn_norm_g, nsa_pe_k, nsa_pe_v, nsa_cmp_k_w1, nsa_cmp_k_w2, nsa_cmp_v_w1, nsa_cmp_v_w2):
    B, S, D = x.shape
    depth = w_in.shape[0]
    alpha = (2 * depth) ** 0.25
    tab = _rope_table(S)
    xf = x.reshape(B * S, D)
    for l in range(depth):
        lam_init = 0.8 - 0.6 * math.exp(-0.3 * l)
        xf = _ffn_ln(xf, ffn1_w_gu[l].astype(BF16), ffn1_w_down[l].astype(BF16), ln1_g[l], ln1_b[l], alpha)
        wm, wr = _prep_w_in(w_in[l])
        hb, hg, kcvc, gates = _proj(xf, wm, wr, tab, S)
        lam_p = jnp.stack([diff_lam_q1[l], diff_lam_k1[l], diff_lam_q2[l], diff_lam_k2[l]])
        o_diff = _diff_attention(hb, lam_p, diff_subln_g[l], lam_init, B, S)
        o_gdn = _gdn(hg, gdn_conv_w[l], gdn_a_log[l], gdn_dt_bias[l], gdn_norm_g[l], B, S)
        o_nsa = _nsa(hb, kcvc, gates, nsa_pe_k[l], nsa_pe_v[l], nsa_cmp_k_w1[l], nsa_cmp_k_w2[l],
                     nsa_cmp_v_w1[l], nsa_cmp_v_w2[l], B, S)
        o_sb = _sb_attention(hb, B, S)
        xf = _outproj_ln(xf, (o_diff, o_gdn, o_nsa, o_sb), w_out[l].astype(BF16), ln2_g[l], ln2_b[l], alpha)
        xf = _ffn_ln(xf, ffn2_w_gu[l].astype(BF16), ffn2_w_down[l].astype(BF16), ln3_g[l], ln3_b[l], alpha)
    return xf.reshape(B, S, D)
```

```python
import functools
import math

import numpy as np
import jax
import jax.numpy as jnp
from jax import lax
from jax.experimental import pallas as pl
from jax.experimental.pallas import tpu as pltpu

F32 = jnp.float32
BF16 = jnp.bfloat16

DEPTH = 2
HEAD_DIM = 64
HEADS = 4
GROUP_W = HEADS * HEAD_DIM
DIFF_QK = HEAD_DIM // 2
GDN_CHUNK = 64
CMP_BLOCK, CMP_STRIDE = 32, 16
SLC_BLOCK, SLC_TOPN = 64, 16
WINDOW = 512
FORCE = 1e4
ROPE_THETA = 10000.0
LN_EPS = 1e-5
NORM_EPS = 1e-6
NEG = -1e30
LOG2E = 1.4426950408889634

LANE = 128
V7X_VMEM_BYTES = 64 * 1024 * 1024
VMEM_LIMIT = V7X_VMEM_BYTES - 8 * 1024 * 1024

TM_FFN = 512
FF_CHUNK = 1408
TM_PROJ = 256
T_ATT = 512
TQ_CMP = 256
TQ_WIN = 128
TQ_SEL, TK_SEL = 512, 512


def _cp(sem):
    return pltpu.CompilerParams(dimension_semantics=sem, vmem_limit_bytes=VMEM_LIMIT)


def _iota(shape, dim):
    return lax.broadcasted_iota(jnp.int32, shape, dim)


def _dot(a, b):
    return jnp.dot(a, b, preferred_element_type=F32)


def _dot_nt(a, b):
    return lax.dot_general(a, b, (((1,), (1,)), ((), ())), preferred_element_type=F32)


def _split2(x):
    hi = x.astype(BF16)
    lo = (x - hi.astype(F32)).astype(BF16)
    return hi, lo


def _dot_x2(x, w):
    hi, lo = _split2(x)
    return _dot(hi, w) + _dot(lo, w)


def _dot_x3(x, w):
    hi = x.astype(BF16)
    r = x - hi.astype(F32)
    mid = r.astype(BF16)
    lo = (r - mid.astype(F32)).astype(BF16)
    return _dot(hi, w) + _dot(mid, w) + _dot(lo, w)


def _layer_norm(y, g, b):
    mu = jnp.mean(y, axis=-1, keepdims=True)
    d = y - mu
    var = jnp.mean(d * d, axis=-1, keepdims=True)
    return d * lax.rsqrt(var + LN_EPS) * g + b


def _const_spec(shape):
    nd = len(shape)
    return pl.BlockSpec(shape, lambda *_: (0,) * nd, pipeline_mode=pl.Buffered(1))


def _ffn_ln_kernel(x_ref, wgu_ref, wd_ref, g_ref, b_ref, o_ref, *, alpha, d_ff, ff_chunk):
    x = x_ref[...]
    xb = x.astype(BF16)
    acc = None
    for c in range(d_ff // ff_chunk):
        lo = c * ff_chunk
        g = _dot(xb, wgu_ref[:, lo:lo + ff_chunk])
        u = _dot(xb, wgu_ref[:, d_ff + lo:d_ff + lo + ff_chunk])
        a = (g * jax.nn.sigmoid(g) * u).astype(BF16)
        part = _dot(a, wd_ref[lo:lo + ff_chunk, :])
        acc = part if acc is None else acc + part
    o_ref[...] = _layer_norm(alpha * x + 0.5 * acc, g_ref[...], b_ref[...])


def _ffn_ln(x, w_gu, w_down, g, b, alpha):
    T, D = x.shape
    d_ff = w_down.shape[0]
    tm = min(TM_FFN, T)
    ff_chunk = FF_CHUNK if d_ff % FF_CHUNK == 0 else d_ff
    return pl.pallas_call(
        functools.partial(_ffn_ln_kernel, alpha=alpha, d_ff=d_ff, ff_chunk=ff_chunk),
        grid=(T // tm,),
        in_specs=[pl.BlockSpec((tm, D), lambda i: (i, 0)),
                  _const_spec((D, 2 * d_ff)), _const_spec((d_ff, D)),
                  _const_spec((1, D)), _const_spec((1, D))],
        out_specs=pl.BlockSpec((tm, D), lambda i: (i, 0)),
        out_shape=jax.ShapeDtypeStruct((T, D), F32),
        compiler_params=_cp(("parallel",)),
        name="ffn_ln",
    )(x, w_gu, w_down, g.reshape(1, D), b.reshape(1, D))


def _outproj_ln_kernel(x_ref, o0_ref, o1_ref, o2_ref, o3_ref, w_ref, g_ref, b_ref, out_ref, *, alpha):
    gw = o0_ref.shape[1]
    mix = None
    for k, o_ref in enumerate((o0_ref, o1_ref, o2_ref, o3_ref)):
        part = _dot(o_ref[...], w_ref[k * gw:(k + 1) * gw, :])
        mix = part if mix is None else mix + part
    out_ref[...] = _layer_norm(alpha * x_ref[...] + mix, g_ref[...], b_ref[...])


def _outproj_ln(x, outs, w_out, g, b, alpha):
    T, D = x.shape
    tm = min(TM_FFN, T)
    gw = outs[0].shape[1]
    return pl.pallas_call(
        functools.partial(_outproj_ln_kernel, alpha=alpha),
        grid=(T // tm,),
        in_specs=[pl.BlockSpec((tm, D), lambda i: (i, 0))]
        + [pl.BlockSpec((tm, gw), lambda i: (i, 0))] * 4
        + [_const_spec(w_out.shape), _const_spec((1, D)), _const_spec((1, D))],
        out_specs=pl.BlockSpec((tm, D), lambda i: (i, 0)),
        out_shape=jax.ShapeDtypeStruct((T, D), F32),
        compiler_params=_cp(("parallel",)),
        name="outproj_ln",
    )(x, *outs, w_out, g.reshape(1, D), b.reshape(1, D))


N_ROPE_BLK = 9
N_MAIN_BLK = 27


def _proj_kernel(x_ref, wm_ref, wr_ref, wt_ref, tab_ref, hb_ref, hg_ref, kc_ref, gt_ref, ht_ref):
    xb = x_ref[...].astype(BF16)
    ht_ref[...] = _dot_nt(wt_ref[...], xb).astype(BF16)
    nr = N_ROPE_BLK * LANE
    h = _dot(xb, wm_ref[:, :nr])
    hr = _dot(xb, wr_ref[...])
    tab = tab_ref[...]
    for c in range(N_ROPE_BLK):
        t0 = 0 if c < 4 else (2 if c < 6 else 4)
        cs = tab[:, t0 * LANE:(t0 + 1) * LANE]
        sn = tab[:, (t0 + 1) * LANE:(t0 + 2) * LANE]
        val = h[:, c * LANE:(c + 1) * LANE] * cs + hr[:, c * LANE:(c + 1) * LANE] * sn
        if c < 8:
            hb_ref[:, c * LANE:(c + 1) * LANE] = val.astype(BF16)
        else:
            kc_ref[...] = val
    hp = _dot(xb, wm_ref[:, 9 * LANE:17 * LANE])
    hb_ref[:, 8 * LANE:16 * LANE] = hp.astype(BF16)
    hf = _dot(xb, wm_ref[:, 17 * LANE:27 * LANE])
    hg_ref[...] = hf[:, :9 * LANE]
    gt_ref[...] = hf[:, 9 * LANE:10 * LANE]


def _proj(x, wm, wr, wt, tab, seq_len):
    T, D = x.shape
    tm = min(TM_PROJ, seq_len)
    nst = seq_len // tm
    return pl.pallas_call(
        _proj_kernel,
        grid=(T // tm,),
        in_specs=[pl.BlockSpec((tm, D), lambda i: (i, 0)),
                  _const_spec(wm.shape), _const_spec(wr.shape), _const_spec(wt.shape),
                  pl.BlockSpec((tm, 6 * LANE), lambda i: (i % nst, 0))],
        out_specs=[pl.BlockSpec((tm, 16 * LANE), lambda i: (i, 0)),
                   pl.BlockSpec((tm, 9 * LANE), lambda i: (i, 0)),
                   pl.BlockSpec((tm, LANE), lambda i: (i, 0)),
                   pl.BlockSpec((tm, LANE), lambda i: (i, 0)),
                   pl.BlockSpec((wt.shape[0], tm), lambda i: (0, i))],
        out_shape=[jax.ShapeDtypeStruct((T, 16 * LANE), BF16),
                   jax.ShapeDtypeStruct((T, 9 * LANE), F32),
                   jax.ShapeDtypeStruct((T, LANE), F32),
                   jax.ShapeDtypeStruct((T, LANE), F32),
                   jax.ShapeDtypeStruct((wt.shape[0], T), BF16)],
        compiler_params=_cp(("parallel",)),
        name="in_proj",
    )(x, wm, wr, wt, tab)


def _rot_cols(w, d):
    k, n = w.shape
    w4 = w.reshape(k, n // d, 2, d // 2)
    return jnp.concatenate([-w4[:, :, 1], w4[:, :, 0]], axis=2).reshape(k, n)


def _prep_w_in(w):
    k = w.shape[0]
    sizes = ((HEADS * DIFF_QK,) * 4 + (GROUP_W,) + (GROUP_W,) * 4 + (HEADS,) * 2
             + (GROUP_W,) + (HEAD_DIM,) * 6 + (3 * HEADS,) + (GROUP_W,) * 3)
    offs = np.concatenate([[0], np.cumsum(sizes)])
    (dq1, dq2, dk1, dk2, dv, gq, gk, gv, gz, ga, gb,
     nq, nkc, nvc, nks, nvs, nkw, nvw, ngate, sq, sk, sv) = [w[:, offs[i]:offs[i + 1]] for i in range(len(sizes))]
    scale = HEAD_DIM ** -0.5
    z64 = jnp.zeros((k, HEAD_DIM), w.dtype)
    pad = lambda a: jnp.concatenate([a, jnp.zeros((k, LANE - a.shape[1]), w.dtype)], axis=1)
    main = jnp.concatenate(
        [dq1, dq2, dk1, dk2, nq * scale, nks, nvs, nkw, nvw, nkc, nvc,
         dv, sq * scale, sk, sv, gq, gk, gv, gz, pad(jnp.concatenate([ga, gb], axis=1)), pad(ngate)], axis=1)
    r32 = lambda a: _rot_cols(a, DIFF_QK)
    r64 = lambda a: _rot_cols(a, HEAD_DIM)
    rot = jnp.concatenate(
        [r32(dq1), r32(dq2), r32(dk1), r32(dk2), r64(nq * scale),
         r64(nks), z64, r64(nkw), z64, r64(nkc), z64], axis=1)
    wt = jnp.concatenate([dv, sv, nvs, nvw], axis=1).T
    return main.astype(BF16), rot.astype(BF16), wt.astype(BF16)


def _rope_table(seq_len):
    def cs(dim):
        inv = ROPE_THETA ** (-jnp.arange(0, dim, 2, dtype=F32) / dim)
        ang = jnp.arange(seq_len, dtype=F32)[:, None] * inv[None, :]
        return jnp.cos(ang), jnp.sin(ang)
    cd, sd = cs(DIFF_QK)
    cn, sn = cs(HEAD_DIM)
    one = jnp.ones((seq_len, HEAD_DIM), F32)
    zero = jnp.zeros((seq_len, HEAD_DIM), F32)
    return jnp.concatenate(
        [jnp.tile(cd, (1, 8)), jnp.tile(sd, (1, 8)), jnp.tile(cn, (1, 4)), jnp.tile(sn, (1, 4)),
         jnp.tile(cn, (1, 2)), one, jnp.tile(sn, (1, 2)), zero], axis=1)


def _tri_pairs(n, descending=False):
    qi, kj = [], []
    for i in range(n):
        js = range(i, -1, -1) if descending else range(i + 1)
        for j in js:
            qi.append(i)
            kj.append(j)
    return jnp.asarray(qi, jnp.int32), jnp.asarray(kj, jnp.int32)


ONES_ROWS = 16
SKEW = 2


def _diff_kernel(qi_ref, kj_ref, q1_ref, q2_ref, k1_ref, k2_ref, vt_ref, lam_ref, g_ref, o_ref,
                 m_scr, acc_scr, *, lam_init):
    p = pl.program_id(1)
    i = qi_ref[p]
    j = kj_ref[p]
    tq, tk = q1_ref.shape[0], k1_ref.shape[0]
    c = (DIFF_QK ** -0.5) * LOG2E
    head32 = _iota((1, LANE), 1) // DIFF_QK

    @pl.when(j == 0)
    def _init():
        m_scr[...] = jnp.full(m_scr.shape, NEG, F32)
        acc_scr[...] = jnp.zeros(acc_scr.shape, F32)

    def step(masked):
        ones = jnp.ones((ONES_ROWS, tk), BF16)
        qs = (q1_ref[...], q2_ref[...])
        ks = (k1_ref[...], k2_ref[...])
        chains = [(t, h) for t in range(2) for h in range(HEADS)]

        def scores(t, h):
            km = jnp.where(head32 == h, ks[t], jnp.zeros_like(ks[t]))
            st = _dot_nt(km, qs[t])
            if masked:
                st = jnp.where(_iota((tk, tq), 0) <= _iota((tk, tq), 1), st, NEG)
            return st

        pend = [scores(*chains[n]) for n in range(SKEW)]
        for n, (t, h) in enumerate(chains):
            st = pend.pop(0)
            if n + SKEW < len(chains):
                pend.append(scores(*chains[n + SKEW]))
            idx = t * HEADS + h
            m_prev = m_scr[idx:idx + 1, :]
            m_new = jnp.maximum(m_prev, jnp.max(st, axis=0, keepdims=True))
            pt = jnp.exp2((st - m_new) * c).astype(BF16)
            alpha = jnp.exp2((m_prev - m_new) * c)
            m_scr[idx:idx + 1, :] = m_new
            vh = jnp.concatenate([vt_ref[h * HEAD_DIM:(h + 1) * HEAD_DIM, :], ones], axis=0)
            acc_scr[idx] = acc_scr[idx] * alpha + _dot(vh, pt)

    @pl.when(j < i)
    def _off():
        step(False)

    @pl.when(j == i)
    def _diag():
        step(True)
        lp = lam_ref[...]
        lam = (jnp.exp(jnp.sum(lp[0:1] * lp[1:2], axis=-1, keepdims=True))
               - jnp.exp(jnp.sum(lp[2:3] * lp[3:4], axis=-1, keepdims=True)) + lam_init)
        parts = []
        for h in range(HEADS):
            a0, a1 = acc_scr[h], acc_scr[HEADS + h]
            oh = (a0[:HEAD_DIM] / a0[HEAD_DIM:HEAD_DIM + 1] - lam * (a1[:HEAD_DIM] / a1[HEAD_DIM:HEAD_DIM + 1]))
            ms = jnp.sum(oh * oh, axis=0, keepdims=True) * (1.0 / HEAD_DIM)
            parts.append(oh * lax.rsqrt(ms + NORM_EPS))
        y = jnp.concatenate(parts, axis=0).T * g_ref[...] * (1.0 - lam_init)
        o_ref[...] = y.astype(o_ref.dtype)


def _diff_attention(hb, ht, lam_p, subln_g, lam_init, batch, seq_len):
    t = min(T_ATT, seq_len)
    nq = seq_len // t
    qi, kj = _tri_pairs(nq)
    g_full = jnp.tile(subln_g, HEADS).reshape(1, GROUP_W)
    qmap = lambda col: (lambda b, p, qi, kj: (b * nq + qi[p], col))
    kmap = lambda col: (lambda b, p, qi, kj: (b * nq + kj[p], col))
    grid_spec = pltpu.PrefetchScalarGridSpec(
        num_scalar_prefetch=2,
        grid=(batch, qi.shape[0]),
        in_specs=[pl.BlockSpec((t, LANE), qmap(0)), pl.BlockSpec((t, LANE), qmap(1)),
                  pl.BlockSpec((t, LANE), kmap(2)), pl.BlockSpec((t, LANE), kmap(3)),
                  pl.BlockSpec((GROUP_W, t), lambda b, p, qi, kj: (0, b * nq + kj[p])),
                  pl.BlockSpec((4, DIFF_QK), lambda b, p, qi, kj: (0, 0)),
                  pl.BlockSpec((1, GROUP_W), lambda b, p, qi, kj: (0, 0))],
        out_specs=pl.BlockSpec((t, GROUP_W), lambda b, p, qi, kj: (b * nq + qi[p], 0)),
        scratch_shapes=[pltpu.VMEM((2 * HEADS, t), F32),
                        pltpu.VMEM((2 * HEADS, HEAD_DIM + ONES_ROWS, t), F32)])
    return pl.pallas_call(
        functools.partial(_diff_kernel, lam_init=lam_init),
        grid_spec=grid_spec,
        out_shape=jax.ShapeDtypeStruct((batch * seq_len, GROUP_W), BF16),
        compiler_params=_cp(("parallel", "arbitrary")),
        name="diff_attn",
    )(qi, kj, hb, hb, hb, hb, ht, lam_p, g_full)


SB_CUM = 256


def _sb_kernel(qi_ref, kj_ref, q_ref, k_ref, v_ref, o_ref, carry_scr, acc_scr):
    p = pl.program_id(1)
    i = qi_ref[p]
    j = kj_ref[p]
    tq, tk = q_ref.shape[0], k_ref.shape[0]
    head64 = _iota((1, GROUP_W), 1) // HEAD_DIM
    cw = min(SB_CUM, tk)

    @pl.when(j == i)
    def _init():
        carry_scr[...] = jnp.zeros(carry_scr.shape, F32)
        acc_scr[...] = jnp.zeros(acc_scr.shape, F32)

    def step(masked):
        q, k, v = q_ref[...], k_ref[...], v_ref[...]
        m_incl = jnp.where(_iota((cw, cw), 0) >= _iota((cw, cw), 1), 1.0, 0.0).astype(BF16)
        m2 = jnp.concatenate([m_incl, m_incl], axis=0)
        if masked:
            before = _iota((tq, tk), 1) < _iota((tq, tk), 0)

        def scores(h):
            return _dot_nt(q, jnp.where(head64 == h, k, jnp.zeros_like(k)))

        pend = [scores(h) for h in range(SKEW)]
        pv = None
        for h in range(HEADS):
            z = pend.pop(0)
            if h + SKEW < HEADS:
                pend.append(scores(h + SKEW))
            sp = jnp.maximum(z, 0.0) + jnp.log(1.0 + jnp.exp(-jnp.abs(z)))
            if masked:
                sp = jnp.where(before, sp, 0.0)
            hi, lo = _split2(sp)
            blocks = []
            suffix = None
            for blk in reversed(range(tk // cw)):
                sl = slice(blk * cw, (blk + 1) * cw)
                cb = _dot(jnp.concatenate([hi[:, sl], lo[:, sl]], axis=1), m2)
                if suffix is not None:
                    cb = cb + suffix
                blocks.insert(0, cb)
                suffix = cb[:, 0:1]
            cum = jnp.concatenate(blocks, axis=1)
            att = jnp.exp(z - cum - carry_scr[h])
            if masked:
                att = jnp.where(before, att, 0.0)
            carry_scr[h] = carry_scr[h] + suffix
            vm = jnp.where(head64 == h, v, jnp.zeros_like(v))
            d = _dot(att.astype(BF16), vm)
            pv = d if pv is None else pv + d
        acc_scr[...] = acc_scr[...] + pv

    @pl.when(j == i)
    def _diag():
        step(True)

    @pl.when(j < i)
    def _off():
        step(False)

    @pl.when(j == 0)
    def _fin():
        o_ref[...] = acc_scr[...].astype(o_ref.dtype)


def _sb_attention(hb, batch, seq_len):
    t = min(T_ATT, seq_len)
    nq = seq_len // t
    qi, kj = _tri_pairs(nq, descending=True)
    grid_spec = pltpu.PrefetchScalarGridSpec(
        num_scalar_prefetch=2,
        grid=(batch, qi.shape[0]),
        in_specs=[pl.BlockSpec((t, GROUP_W), lambda b, p, qi, kj: (b * nq + qi[p], 5)),
                  pl.BlockSpec((t, GROUP_W), lambda b, p, qi, kj: (b * nq + kj[p], 6)),
                  pl.BlockSpec((t, GROUP_W), lambda b, p, qi, kj: (b * nq + kj[p], 7))],
        out_specs=pl.BlockSpec((t, GROUP_W), lambda b, p, qi, kj: (b * nq + qi[p], 0)),
        scratch_shapes=[pltpu.VMEM((HEADS, t, 1), F32), pltpu.VMEM((t, GROUP_W), F32)])
    return pl.pallas_call(
        _sb_kernel,
        grid_spec=grid_spec,
        out_shape=jax.ShapeDtypeStruct((batch * seq_len, GROUP_W), BF16),
        compiler_params=_cp(("parallel", "arbitrary")),
        name="sb_attn",
    )(qi, kj, hb, hb, hb)


def _stack_heads(q):
    qf = q.astype(F32)
    lo = _iota((1, LANE), 1) < HEAD_DIM
    parts = []
    for blk in (qf[:, :LANE], qf[:, LANE:]):
        parts.append(jnp.where(lo, blk, 0.0))
        parts.append(jnp.where(lo, pltpu.roll(blk, HEAD_DIM, 1), 0.0))
    return jnp.concatenate(parts, axis=0).astype(BF16)


def _unstack_heads(r, tq):
    lo = _iota((1, LANE), 1) < HEAD_DIM
    blk0 = jnp.where(lo, pltpu.roll(r[0:tq], HEAD_DIM, 1), r[tq:2 * tq])
    blk1 = jnp.where(lo, pltpu.roll(r[2 * tq:3 * tq], HEAD_DIM, 1), r[3 * tq:4 * tq])
    return jnp.concatenate([blk0, blk1], axis=1)


def _nsa_compress_kernel(r_ref, pe_ref, w1lo_ref, w1hi_ref, w2_ref, o_ref):
    r = r_ref[...]
    n = r.shape[0]
    y1 = _dot((r + pe_ref[0:1, :]).astype(BF16), w1lo_ref[...])
    y2 = _dot((r + pe_ref[1:2, :]).astype(BF16), w1hi_ref[...])
    hid = y1 + pltpu.roll(y2, n - 1, 0)
    act = (hid * jax.nn.sigmoid(hid)).astype(BF16)
    o_ref[...] = _dot(act, w2_ref[...]).astype(o_ref.dtype)


def _nsa_compress(kcvc, pe_k, pe_v, ck_w1, ck_w2, cv_w1, cv_w2, batch, seq_len):
    nrow = seq_len // CMP_STRIDE
    width = CMP_STRIDE * LANE
    r = kcvc.reshape(batch * nrow, width)
    hid = ck_w1.shape[1]
    pe = jnp.concatenate([pe_k, pe_v], axis=1).reshape(2, width)
    zk = jnp.zeros((CMP_BLOCK, HEAD_DIM, hid), F32)
    w1 = jnp.concatenate(
        [jnp.concatenate([ck_w1.reshape(CMP_BLOCK, HEAD_DIM, hid), zk], axis=2),
         jnp.concatenate([zk, cv_w1.reshape(CMP_BLOCK, HEAD_DIM, hid)], axis=2)], axis=1)
    w1 = w1.reshape(2, width, 2 * hid).astype(BF16)
    zo = jnp.zeros((hid, HEAD_DIM), F32)
    w2 = jnp.concatenate([jnp.concatenate([ck_w2, zo], axis=1),
                          jnp.concatenate([zo, cv_w2], axis=1)], axis=0).astype(BF16)
    return pl.pallas_call(
        _nsa_compress_kernel,
        grid=(batch,),
        in_specs=[pl.BlockSpec((nrow, width), lambda b: (b, 0)),
                  _const_spec((2, width)), _const_spec((width, 2 * hid)), _const_spec((width, 2 * hid)),
                  _const_spec((2 * hid, LANE))],
        out_specs=pl.BlockSpec((nrow, LANE), lambda b: (b, 0)),
        out_shape=jax.ShapeDtypeStruct((batch * nrow, LANE), BF16),
        compiler_params=_cp(("parallel",)),
        name="nsa_compress",
    )(r, pe, w1[0], w1[1], w2)


def _nsa_cmp_kernel(q_ref, kv_ref, ov_ref, ocmp_ref, sel_ref):
    i = pl.program_id(1)
    tq = q_ref.shape[0]
    ncmp = kv_ref.shape[0]
    nslc = ov_ref.shape[1]
    qs = _stack_heads(q_ref[...])
    kv = kv_ref[...]
    s = _dot_nt(qs, kv)
    tpos = i * tq + (_iota((HEADS * tq, 1), 0) & (tq - 1))
    cend = _iota((1, ncmp), 1) * CMP_STRIDE + (CMP_BLOCK - 1)
    cm = cend <= tpos
    s = jnp.where(cm, s, NEG)
    e = jnp.exp(s - jnp.max(s, axis=-1, keepdims=True))
    pr = jnp.where(cm, e / jnp.sum(e, axis=-1, keepdims=True), 0.0)
    ocmp_ref[...] = _unstack_heads(_dot(pr.astype(BF16), kv), tq)
    psum = pr[0:tq] + pr[tq:2 * tq] + pr[2 * tq:3 * tq] + pr[3 * tq:4 * tq]
    imp = _dot_x2(psum, ov_ref[...])
    blk = _iota((1, nslc), 1)
    cur = (i * tq + _iota((tq, 1), 0)) // SLC_BLOCK
    work = jnp.where(blk == 0, FORCE, jnp.where(blk == cur, FORCE, jnp.where(blk == cur - 1, FORCE, imp)))
    work = jnp.where(blk <= cur, work, -FORCE)
    sel = jnp.zeros((tq, nslc), F32)
    for _ in range(min(SLC_TOPN, nslc)):
        mx = jnp.max(work, axis=-1, keepdims=True)
        first = jnp.min(jnp.where(work == mx, blk, nslc), axis=-1, keepdims=True)
        hit = blk == first
        sel = jnp.where(hit, 1.0, sel)
        work = jnp.where(hit, -jnp.inf, work)
    sel_ref[...] = jnp.where(sel > 0.5, 0.0, NEG).T


def _nsa_cmp(hb, kvcmp, batch, seq_len):
    tq = min(TQ_CMP, seq_len)
    nq = seq_len // tq
    ncmp = seq_len // CMP_STRIDE
    nslc = seq_len // SLC_BLOCK
    cstart = np.arange(ncmp)[:, None] * CMP_STRIDE
    sstart = np.arange(nslc)[None, :] * SLC_BLOCK
    ov = (cstart < sstart + SLC_BLOCK) & (cstart + CMP_BLOCK - 1 >= sstart)
    ov &= (np.arange(ncmp)[:, None] < (seq_len - CMP_BLOCK) // CMP_STRIDE + 1)
    ov = jnp.asarray(ov, BF16)
    return pl.pallas_call(
        _nsa_cmp_kernel,
        grid=(batch, nq),
        in_specs=[pl.BlockSpec((tq, GROUP_W), lambda b, i: (b * nq + i, 2)),
                  pl.BlockSpec((ncmp, LANE), lambda b, i: (b, 0)),
                  pl.BlockSpec((ncmp, nslc), lambda b, i: (0, 0))],
        out_specs=[pl.BlockSpec((tq, GROUP_W), lambda b, i: (b * nq + i, 0)),
                   pl.BlockSpec((nslc, tq), lambda b, i: (0, b * nq + i))],
        out_shape=[jax.ShapeDtypeStruct((batch * seq_len, GROUP_W), F32),
                   jax.ShapeDtypeStruct((nslc, batch * seq_len), F32)],
        compiler_params=_cp(("parallel", "parallel")),
        name="nsa_cmp_select",
    )(hb, kvcmp, ov)


def _nsa_win_kernel(q_ref, kv_ref, o_ref, *, window):
    i = pl.program_id(1)
    tq = q_ref.shape[0]
    span = tq + window
    base = pl.multiple_of(jnp.maximum(i * tq - window, 0), LANE)
    kv = kv_ref[pl.ds(base, span), :]
    s = _dot_nt(_stack_heads(q_ref[...]), kv)
    qpos = i * tq + (_iota((HEADS * tq, 1), 0) & (tq - 1))
    rel = qpos - (base + _iota((1, span), 1))
    s = jnp.where(rel >= 0, jnp.where(rel < window, s, NEG), NEG)
    e = jnp.exp(s - jnp.max(s, axis=-1, keepdims=True))
    pr = e / jnp.sum(e, axis=-1, keepdims=True)
    o_ref[...] = _unstack_heads(_dot(pr.astype(BF16), kv), tq)


def _nsa_window(hb, batch, seq_len):
    tq = min(TQ_WIN, seq_len)
    nq = seq_len // tq
    window = min(WINDOW, seq_len - tq)
    return pl.pallas_call(
        functools.partial(_nsa_win_kernel, window=window),
        grid=(batch, nq),
        in_specs=[pl.BlockSpec((tq, GROUP_W), lambda b, i: (b * nq + i, 2)),
                  pl.BlockSpec((seq_len, LANE), lambda b, i: (b, 7))],
        out_specs=pl.BlockSpec((tq, GROUP_W), lambda b, i: (b * nq + i, 0)),
        out_shape=jax.ShapeDtypeStruct((batch * seq_len, GROUP_W), F32),
        compiler_params=_cp(("parallel", "parallel")),
        name="nsa_window",
    )(hb, hb)


def _nsa_sel_kernel(qi_ref, kj_ref, q_ref, k_ref, vt_ref, bias_ref, ocmp_ref, owin_ref, gate_ref, o_ref,
                    qs_scr, m_scr, acc_scr):
    p = pl.program_id(1)
    i = qi_ref[p]
    j = kj_ref[p]
    tq, tk = q_ref.shape[0], k_ref.shape[0]
    nblk = tk // SLC_BLOCK

    @pl.when(j == 0)
    def _init():
        qs = _stack_heads(q_ref[...])
        for h in range(HEADS):
            qs_scr[h] = qs[h * tq:(h + 1) * tq]
        m_scr[...] = jnp.full(m_scr.shape, NEG, F32)
        acc_scr[...] = jnp.zeros(acc_scr.shape, F32)

    def step(masked):
        kv = k_ref[...]
        bias = bias_ref[...][:, None, :]
        vh = jnp.concatenate([vt_ref[...], jnp.ones((ONES_ROWS, tk), BF16)], axis=0)

        def scores(h):
            st = _dot_nt(kv, qs_scr[h])
            st = (st.reshape(nblk, SLC_BLOCK, tq) + bias).reshape(tk, tq)
            if masked:
                st = jnp.where(j * tk + _iota((tk, tq), 0) <= i * tq + _iota((tk, tq), 1), st, NEG)
            return st

        pend = [scores(h) for h in range(SKEW)]
        for h in range(HEADS):
            st = pend.pop(0)
            if h + SKEW < HEADS:
                pend.append(scores(h + SKEW))
            m_prev = m_scr[h:h + 1, :]
            m_new = jnp.maximum(m_prev, jnp.max(st, axis=0, keepdims=True))
            pt = jnp.exp(st - m_new).astype(BF16)
            alpha = jnp.exp(m_prev - m_new)
            m_scr[h:h + 1, :] = m_new
            acc_scr[h] = acc_scr[h] * alpha + _dot(vh, pt)

    last = (i * tq + tq - 1) // tk

    @pl.when((j + 1) * tk <= i * tq)
    def _past():
        step(False)

    @pl.when((j + 1) * tk > i * tq)
    def _diag():
        step(True)

    @pl.when(j == last)
    def _fin():
        parts = []
        for h in range(HEADS):
            a = acc_scr[h]
            parts.append(a[:HEAD_DIM] / a[HEAD_DIM:HEAD_DIM + 1])
        osel = jnp.concatenate(parts, axis=0).T
        sig = jax.nn.sigmoid(gate_ref[...])
        grow = _iota((LANE, 1), 0)
        ghead = _iota((1, GROUP_W), 1) // HEAD_DIM
        out = None
        for br, o_br in enumerate((ocmp_ref[...], osel, owin_ref[...])):
            e_br = jnp.where(grow == 3 * ghead + br, 1.0, 0.0).astype(BF16)
            term = _dot_x3(sig, e_br) * o_br
            out = term if out is None else out + term
        o_ref[...] = out.astype(o_ref.dtype)


def _nsa_select(hb, ht, sel, ocmp, owin, gates, batch, seq_len):
    tq = min(TQ_SEL, seq_len)
    tk = min(TK_SEL, seq_len)
    nq, nk = seq_len // tq, seq_len // tk
    qi, kj = [], []
    for i in range(nq):
        for j in range((i * tq + tq - 1) // tk + 1):
            qi.append(i)
            kj.append(j)
    qi, kj = jnp.asarray(qi, jnp.int32), jnp.asarray(kj, jnp.int32)
    qrow = lambda b, p, qi, kj: (b * nq + qi[p], 0)
    vs_row_blk = 2 * GROUP_W // HEAD_DIM
    grid_spec = pltpu.PrefetchScalarGridSpec(
        num_scalar_prefetch=2,
        grid=(batch, qi.shape[0]),
        in_specs=[pl.BlockSpec((tq, GROUP_W), lambda b, p, qi, kj: (b * nq + qi[p], 2)),
                  pl.BlockSpec((tk, LANE), lambda b, p, qi, kj: (b * nk + kj[p], 6)),
                  pl.BlockSpec((HEAD_DIM, tk), lambda b, p, qi, kj: (vs_row_blk, b * nk + kj[p])),
                  pl.BlockSpec((tk // SLC_BLOCK, tq), lambda b, p, qi, kj: (kj[p], b * nq + qi[p])),
                  pl.BlockSpec((tq, GROUP_W), qrow), pl.BlockSpec((tq, GROUP_W), qrow),
                  pl.BlockSpec((tq, LANE), qrow)],
        out_specs=pl.BlockSpec((tq, GROUP_W), qrow),
        scratch_shapes=[pltpu.VMEM((HEADS, tq, LANE), BF16), pltpu.VMEM((HEADS, tq), F32),
                        pltpu.VMEM((HEADS, HEAD_DIM + ONES_ROWS, tq), F32)])
    return pl.pallas_call(
        _nsa_sel_kernel,
        grid_spec=grid_spec,
        out_shape=jax.ShapeDtypeStruct((batch * seq_len, GROUP_W), BF16),
        compiler_params=_cp(("parallel", "arbitrary")),
        name="nsa_select_gate",
    )(qi, kj, hb, hb, ht, sel, ocmp, owin, gates)


def _nsa(hb, ht, kcvc, gates, pe_k, pe_v, ck_w1, ck_w2, cv_w1, cv_w2, batch, seq_len):
    kvcmp = _nsa_compress(kcvc, pe_k, pe_v, ck_w1, ck_w2, cv_w1, cv_w2, batch, seq_len)
    ocmp, sel = _nsa_cmp(hb, kvcmp, batch, seq_len)
    owin = _nsa_window(hb, batch, seq_len)
    return _nsa_select(hb, ht, sel, ocmp, owin, gates, batch, seq_len)


def _bd(mc, mask_bd):
    return jnp.where(mask_bd, jnp.concatenate([mc] * HEADS, axis=0), jnp.zeros((), mc.dtype))


def _mm_bd(x, mc, mask_bd):
    return _dot(x.astype(BF16), _bd(mc.astype(BF16), mask_bd))


def _gdn_prep_kernel(x_ref, ab_ref, cw_ref, alog_ref, dtb_ref,
                     u_ref, w_ref, qd_ref, in_ref, kdt_ref, gl_ref, xpad_scr):
    n = pl.program_id(0)
    C = GDN_CHUNK
    W = GROUP_W
    nb, rows = x_ref.shape[0], x_ref.shape[1]
    inst = [(b, c) for b in range(nb) for c in range(rows // C)]

    @pl.when(n == 0)
    def _():
        xpad_scr[:, 0:8, :] = jnp.zeros((nb, 8, xpad_scr.shape[2]), F32)

    cw = cw_ref[...]
    qkv_b = []
    for b in range(nb):
        x = x_ref[b]
        xpad_scr[b, 8:8 + rows, :] = x
        conv = (cw[0:1] * xpad_scr[b, 5:5 + rows, :] + cw[1:2] * xpad_scr[b, 6:6 + rows, :]
                + cw[2:3] * xpad_scr[b, 7:7 + rows, :] + cw[3:4] * x)
        xpad_scr[b, 0:8, :] = x[rows - 8:rows, :]
        qkv_b.append(conv * jax.nn.sigmoid(conv))

    r256 = _iota((W, W), 0)
    c256 = _iota((W, W), 1)
    mask_bd = (r256 // HEAD_DIM) == (c256 // HEAD_DIM)
    ones_bd = jnp.where(mask_bd, 1.0, 0.0).astype(BF16)
    eye256 = jnp.where(r256 == c256, 1.0, 0.0).astype(BF16)
    row = _iota((C, W), 0)
    jl = _iota((C, W), 1) % HEAD_DIM
    ltri = jnp.where(_iota((C, C), 1) <= _iota((C, C), 0), 1.0, 0.0).astype(BF16)
    erow = _iota((LANE, W), 0)
    ehead = _iota((LANE, W), 1) // HEAD_DIM
    e_g = jnp.where(erow == ehead, 1.0, 0.0).astype(BF16)
    e_b = jnp.where(erow == ehead + HEADS, 1.0, 0.0).astype(BF16)

    def each(f, *lists):
        return [f(*args) for args in zip(*lists)]

    sl = [slice(c * C, (c + 1) * C) for _, c in inst]
    q = [qkv_b[b][sl[k], 0:W] for k, (b, _) in enumerate(inst)]
    kk = [qkv_b[b][sl[k], W:2 * W] for k, (b, _) in enumerate(inst)]
    v = [qkv_b[b][sl[k], 2 * W:3 * W] for k, (b, _) in enumerate(inst)]
    ab = [ab_ref[b, sl[k], :] for k, (b, _) in enumerate(inst)]

    qn = each(lambda t: t * lax.rsqrt(_dot_x2(t * t, ones_bd) + NORM_EPS) * (HEAD_DIM ** -0.5), q)
    kn = each(lambda t: t * lax.rsqrt(_dot_x2(t * t, ones_bd) + NORM_EPS), kk)

    def gate(a):
        z = a + dtb_ref[...]
        return -jnp.exp(alog_ref[...]) * (jnp.maximum(z, 0.0) + jnp.log1p(jnp.exp(-jnp.abs(z))))

    g_hl = each(lambda a: _dot_x3(gate(a), e_g), ab)
    beta = each(lambda a: _dot_x3(jax.nn.sigmoid(a), e_b), ab)
    gc = each(lambda g: _dot_x3_left(ltri, g), g_hl)
    glast = each(lambda g: g[C - 1:C, :], gc)
    exp_g = each(jnp.exp, gc)
    dmat = each(lambda g: _dot_x3_left(ltri, jnp.where(row > jl, g, 0.0)), g_hl)
    decay = each(lambda d: jnp.where(jl <= row, jnp.exp(d), 0.0), dmat)

    kt4 = each(lambda t: _dot_nt(eye256, jnp.concatenate([t.astype(BF16)] * HEADS, axis=0)), kn)
    kb_mat = each(lambda t: jnp.where(mask_bd, t, 0.0).astype(BF16), kt4)
    kbeta = each(lambda t, bb: t * bb, kn, beta)
    a_c = each(lambda t, m, d: jnp.where(jl < row, _dot(t.astype(BF16), m) * d, 0.0), kbeta, kb_mat, decay)
    intra = each(lambda t, m, d: _dot(t.astype(BF16), m) * d, qn, kb_mat, decay)

    t_c = each(lambda a: jnp.where(jl == row, 1.0, 0.0) - a, a_c)
    p_c = a_c
    for _ in range(5):
        p_c = each(lambda pc: _mm_bd(pc, pc, mask_bd), p_c)
        t_c = each(lambda tc, pc: tc + _mm_bd(tc, pc, mask_bd), t_c, p_c)

    u = each(lambda tc, t, bb: _mm_bd(tc, t * bb, mask_bd), t_c, v, beta)
    w = each(lambda tc, t, e: _mm_bd(tc, t * e, mask_bd), t_c, kbeta, exp_g)
    kdt = each(lambda t, gl, g: _dot_nt(eye256, (t * jnp.exp(gl - g)).astype(BF16)), kn, glast, gc)

    for k, (b, c) in enumerate(inst):
        u_ref[b, sl[k], :] = u[k]
        w_ref[b, sl[k], :] = w[k].astype(w_ref.dtype)
        qd_ref[b, sl[k], :] = (qn[k] * exp_g[k]).astype(qd_ref.dtype)
        in_ref[b, sl[k], :] = intra[k].astype(in_ref.dtype)
        kdt_ref[b, c] = kdt[k].astype(kdt_ref.dtype)
        gl_ref[b, c] = jnp.exp(glast[k])


def _dot_x3_left(w, x):
    hi = x.astype(BF16)
    r = x - hi.astype(F32)
    mid = r.astype(BF16)
    lo = (r - mid.astype(F32)).astype(BF16)
    return _dot(w, hi) + _dot(w, mid) + _dot(w, lo)


def _gdn_scan_kernel(u_ref, w_ref, qd_ref, in_ref, kdt_ref, gl_ref, z_ref, g_ref, o_ref, s_scr):
    n = pl.program_id(0)
    C = GDN_CHUNK
    W = GROUP_W
    nb, rows = u_ref.shape[0], u_ref.shape[1]

    @pl.when(n == 0)
    def _():
        s_scr[...] = jnp.zeros(s_scr.shape, F32)

    mask_bd = (_iota((W, W), 0) // HEAD_DIM) == (_iota((W, W), 1) // HEAD_DIM)
    ones_bd = jnp.where(mask_bd, 1.0, 0.0).astype(BF16)
    s = [s_scr[b] for b in range(nb)]
    for c in range(rows // C):
        sl = slice(c * C, (c + 1) * C)
        sb = [t.astype(BF16) for t in s]
        v_new = [u_ref[b, sl, :] - _dot(w_ref[b, sl, :], sb[b]) for b in range(nb)]
        vb = [t.astype(BF16) for t in v_new]
        s = [s[b] * gl_ref[b, c] + jnp.where(mask_bd, _dot(kdt_ref[b, c], vb[b]), 0.0) for b in range(nb)]
        o = [_dot(qd_ref[b, sl, :], sb[b]) + _dot(in_ref[b, sl, :], _bd(vb[b], mask_bd)) for b in range(nb)]
        for b in range(nb):
            ms = _dot_x2(o[b] * o[b], ones_bd) * (1.0 / HEAD_DIM)
            zz = z_ref[b, sl, :]
            y = o[b] * lax.rsqrt(ms + NORM_EPS) * g_ref[...] * (zz * jax.nn.sigmoid(zz))
            o_ref[b, sl, :] = y.astype(o_ref.dtype)
    for b in range(nb):
        s_scr[b] = s[b]


GDN_PREP_CHUNKS = 4
GDN_SCAN_CHUNKS = 4


def _gdn(hg, conv_w, a_log, dt_bias, norm_g, batch, seq_len):
    C = GDN_CHUNK
    nc = seq_len // C
    W = GROUP_W
    padl = lambda a: jnp.concatenate([a, jnp.zeros((LANE - a.shape[0],), F32)]).reshape(1, LANE)
    hg3 = hg.reshape(batch, seq_len, hg.shape[-1])
    cp = math.gcd(GDN_PREP_CHUNKS, nc)
    rp = cp * C
    blkp = pl.BlockSpec((batch, rp, W), lambda n: (0, n, 0))
    u, w, qd, intra, kdt, gl = pl.pallas_call(
        _gdn_prep_kernel,
        grid=(nc // cp,),
        in_specs=[pl.BlockSpec((batch, rp, 3 * W), lambda n: (0, n, 0)),
                  pl.BlockSpec((batch, rp, LANE), lambda n: (0, n, 8)),
                  pl.BlockSpec((4, 3 * W), lambda n: (0, 0)),
                  pl.BlockSpec((1, LANE), lambda n: (0, 0)),
                  pl.BlockSpec((1, LANE), lambda n: (0, 0))],
        out_specs=[blkp] * 4
        + [pl.BlockSpec((batch, cp, W, C), lambda n: (0, n, 0, 0)),
           pl.BlockSpec((batch, cp, 1, W), lambda n: (0, n, 0, 0))],
        out_shape=[jax.ShapeDtypeStruct((batch, seq_len, W), F32), jax.ShapeDtypeStruct((batch, seq_len, W), BF16),
                   jax.ShapeDtypeStruct((batch, seq_len, W), BF16), jax.ShapeDtypeStruct((batch, seq_len, W), BF16),
                   jax.ShapeDtypeStruct((batch, nc, W, C), BF16),
                   jax.ShapeDtypeStruct((batch, nc, 1, W), F32)],
        scratch_shapes=[pltpu.VMEM((batch, 8 + rp, 3 * W), F32)],
        compiler_params=_cp(("arbitrary",)),
        name="gdn_prep",
    )(hg3, hg3, conv_w, padl(a_log), padl(dt_bias))

    cs = math.gcd(GDN_SCAN_CHUNKS, nc)
    blk = pl.BlockSpec((batch, cs * C, W), lambda n: (0, n, 0))
    out = pl.pallas_call(
        _gdn_scan_kernel,
        grid=(nc // cs,),
        in_specs=[blk, blk, blk, blk,
                  pl.BlockSpec((batch, cs, W, C), lambda n: (0, n, 0, 0)),
                  pl.BlockSpec((batch, cs, 1, W), lambda n: (0, n, 0, 0)),
                  pl.BlockSpec((batch, cs * C, W), lambda n: (0, n, 3)),
                  pl.BlockSpec((1, W), lambda n: (0, 0))],
        out_specs=blk,
        out_shape=jax.ShapeDtypeStruct((batch, seq_len, W), BF16),
        scratch_shapes=[pltpu.VMEM((batch, W, W), F32)],
        compiler_params=_cp(("arbitrary",)),
        name="gdn_scan",
    )(u, w, qd, intra, kdt, gl, hg3, jnp.tile(norm_g, HEADS).reshape(1, W))
    return out.reshape(batch * seq_len, W)


def kernel(x, w_in, w_out, ffn1_w_gu, ffn1_w_down, ffn2_w_gu, ffn2_w_down, ln1_g, ln1_b, ln2_g, ln2_b, ln3_g, ln3_b, diff_lam_q1, diff_lam_k1, diff_lam_q2, diff_lam_k2, diff_subln_g, gdn_conv_w, gdn_a_log, gdn_dt_bias, gdn_norm_g, nsa_pe_k, nsa_pe_v, nsa_cmp_k_w1, nsa_cmp_k_w2, nsa_cmp_v_w1, nsa_cmp_v_w2):
    B, S, D = x.shape
    depth = w_in.shape[0]
    alpha = (2 * depth) ** 0.25
    tab = _rope_table(S)
    xf = x.reshape(B * S, D)
    for l in range(depth):
        lam_init = 0.8 - 0.6 * math.exp(-0.3 * l)
        xf = _ffn_ln(xf, ffn1_w_gu[l].astype(BF16), ffn1_w_down[l].astype(BF16), ln1_g[l], ln1_b[l], alpha)
        wm, wr, wt = _prep_w_in(w_in[l])
        hb, hg, kcvc, gates, ht = _proj(xf, wm, wr, wt, tab, S)
        lam_p = jnp.stack([diff_lam_q1[l], diff_lam_k1[l], diff_lam_q2[l], diff_lam_k2[l]])
        o_diff = _diff_attention(hb, ht, lam_p, diff_subln_g[l], lam_init, B, S)
        o_gdn = _gdn(hg, gdn_conv_w[l], gdn_a_log[l], gdn_dt_bias[l], gdn_norm_g[l], B, S)
        o_nsa = _nsa(hb, ht, kcvc, gates, nsa_pe_k[l], nsa_pe_v[l], nsa_cmp_k_w1[l], nsa_cmp_k_w2[l],
                     nsa_cmp_v_w1[l], nsa_cmp_v_w2[l], B, S)
        o_sb = _sb_attention(hb, B, S)
        xf = _outproj_ln(xf, (o_diff, o_gdn, o_nsa, o_sb), w_out[l].astype(BF16), ln2_g[l], ln2_b[l], alpha)
        xf = _ffn_ln(xf, ffn2_w_gu[l].astype(BF16), ffn2_w_down[l].astype(BF16), ln3_g[l], ln3_b[l], alpha)
    return xf.reshape(B, S, D)
```

```python
import functools
import math

import numpy as np
import jax
import jax.numpy as jnp
from jax import lax
from jax.experimental import pallas as pl
from jax.experimental.pallas import tpu as pltpu

F32 = jnp.float32
BF16 = jnp.bfloat16

DEPTH = 2
HEAD_DIM = 64
HEADS = 4
GROUP_W = HEADS * HEAD_DIM
DIFF_QK = HEAD_DIM // 2
GDN_CHUNK = 64
CMP_BLOCK, CMP_STRIDE = 32, 16
SLC_BLOCK, SLC_TOPN = 64, 16
WINDOW = 512
FORCE = 1e4
ROPE_THETA = 10000.0
LN_EPS = 1e-5
NORM_EPS = 1e-6
NEG = -1e30
LOG2E = 1.4426950408889634

LANE = 128
V7X_VMEM_BYTES = 64 * 1024 * 1024
VMEM_LIMIT = V7X_VMEM_BYTES - 8 * 1024 * 1024

TM_FFN = 512
FF_CHUNK = 1408
TM_PROJ = 256
T_ATT = 512
TQ_CMP = 256
TQ_WIN = 512
TQ_SEL, TK_SEL = 512, 512


def _cp(sem):
    return pltpu.CompilerParams(dimension_semantics=sem, vmem_limit_bytes=VMEM_LIMIT)


def _iota(shape, dim):
    return lax.broadcasted_iota(jnp.int32, shape, dim)


def _dot(a, b):
    return jnp.dot(a, b, preferred_element_type=F32)


def _dot_nt(a, b):
    return lax.dot_general(a, b, (((1,), (1,)), ((), ())), preferred_element_type=F32)


def _split2(x):
    hi = x.astype(BF16)
    lo = (x - hi.astype(F32)).astype(BF16)
    return hi, lo


def _dot_x2(x, w):
    hi, lo = _split2(x)
    return _dot(hi, w) + _dot(lo, w)


def _dot_x3(x, w):
    hi = x.astype(BF16)
    r = x - hi.astype(F32)
    mid = r.astype(BF16)
    lo = (r - mid.astype(F32)).astype(BF16)
    return _dot(hi, w) + _dot(mid, w) + _dot(lo, w)


def _layer_norm(y, g, b):
    mu = jnp.mean(y, axis=-1, keepdims=True)
    d = y - mu
    var = jnp.mean(d * d, axis=-1, keepdims=True)
    return d * lax.rsqrt(var + LN_EPS) * g + b


def _const_spec(shape):
    nd = len(shape)
    return pl.BlockSpec(shape, lambda *_: (0,) * nd, pipeline_mode=pl.Buffered(1))


def _ffn_ln_kernel(x_ref, wgu_ref, wd_ref, g_ref, b_ref, o_ref, *, alpha, d_ff, ff_chunk):
    x = x_ref[...]
    xb = x.astype(BF16)
    acc = None
    for c in range(d_ff // ff_chunk):
        lo = c * ff_chunk
        g = _dot(xb, wgu_ref[:, lo:lo + ff_chunk])
        u = _dot(xb, wgu_ref[:, d_ff + lo:d_ff + lo + ff_chunk])
        a = (g * jax.nn.sigmoid(g) * u).astype(BF16)
        part = _dot(a, wd_ref[lo:lo + ff_chunk, :])
        acc = part if acc is None else acc + part
    o_ref[...] = _layer_norm(alpha * x + 0.5 * acc, g_ref[...], b_ref[...])


def _ffn_ln(x, w_gu, w_down, g, b, alpha):
    T, D = x.shape
    d_ff = w_down.shape[0]
    tm = min(TM_FFN, T)
    ff_chunk = FF_CHUNK if d_ff % FF_CHUNK == 0 else d_ff
    return pl.pallas_call(
        functools.partial(_ffn_ln_kernel, alpha=alpha, d_ff=d_ff, ff_chunk=ff_chunk),
        grid=(T // tm,),
        in_specs=[pl.BlockSpec((tm, D), lambda i: (i, 0)),
                  _const_spec((D, 2 * d_ff)), _const_spec((d_ff, D)),
                  _const_spec((1, D)), _const_spec((1, D))],
        out_specs=pl.BlockSpec((tm, D), lambda i: (i, 0)),
        out_shape=jax.ShapeDtypeStruct((T, D), F32),
        compiler_params=_cp(("parallel",)),
        name="ffn_ln",
    )(x, w_gu, w_down, g.reshape(1, D), b.reshape(1, D))


def _outproj_ln_kernel(x_ref, o0_ref, o1_ref, o2_ref, o3_ref, w_ref, g_ref, b_ref, out_ref, *, alpha):
    gw = o0_ref.shape[1]
    mix = None
    for k, o_ref in enumerate((o0_ref, o1_ref, o2_ref, o3_ref)):
        part = _dot(o_ref[...], w_ref[k * gw:(k + 1) * gw, :])
        mix = part if mix is None else mix + part
    out_ref[...] = _layer_norm(alpha * x_ref[...] + mix, g_ref[...], b_ref[...])


def _outproj_ln(x, outs, w_out, g, b, alpha):
    T, D = x.shape
    tm = min(TM_FFN, T)
    gw = outs[0].shape[1]
    return pl.pallas_call(
        functools.partial(_outproj_ln_kernel, alpha=alpha),
        grid=(T // tm,),
        in_specs=[pl.BlockSpec((tm, D), lambda i: (i, 0))]
        + [pl.BlockSpec((tm, gw), lambda i: (i, 0))] * 4
        + [_const_spec(w_out.shape), _const_spec((1, D)), _const_spec((1, D))],
        out_specs=pl.BlockSpec((tm, D), lambda i: (i, 0)),
        out_shape=jax.ShapeDtypeStruct((T, D), F32),
        compiler_params=_cp(("parallel",)),
        name="outproj_ln",
    )(x, *outs, w_out, g.reshape(1, D), b.reshape(1, D))


N_ROPE_BLK = 9
N_MAIN_BLK = 25


def _proj_kernel(x_ref, wm_ref, wr_ref, wt_ref, tab_ref, hb_ref, hg_ref, kc_ref, gt_ref, ht_ref):
    xb = x_ref[...].astype(BF16)
    ht_ref[...] = _dot_nt(wt_ref[...], xb).astype(BF16)
    nr = N_ROPE_BLK * LANE
    h = _dot(xb, wm_ref[:, :nr])
    hr = _dot(xb, wr_ref[...])
    tab = tab_ref[...]
    for c in range(N_ROPE_BLK):
        t0 = 0 if c < 4 else (2 if c < 6 else 4)
        cs = tab[:, t0 * LANE:(t0 + 1) * LANE]
        sn = tab[:, (t0 + 1) * LANE:(t0 + 2) * LANE]
        val = h[:, c * LANE:(c + 1) * LANE] * cs + hr[:, c * LANE:(c + 1) * LANE] * sn
        if c < 8:
            hb_ref[:, c * LANE:(c + 1) * LANE] = val.astype(BF16)
        else:
            kc_ref[...] = val
    hp = _dot(xb, wm_ref[:, 9 * LANE:15 * LANE])
    hb_ref[:, 8 * LANE:14 * LANE] = hp.astype(BF16)
    hf = _dot(xb, wm_ref[:, 15 * LANE:25 * LANE])
    hg_ref[...] = hf[:, :9 * LANE]
    gt_ref[...] = hf[:, 9 * LANE:10 * LANE]


def _proj(x, wm, wr, wt, tab, seq_len):
    T, D = x.shape
    tm = min(TM_PROJ, seq_len)
    nst = seq_len // tm
    return pl.pallas_call(
        _proj_kernel,
        grid=(T // tm,),
        in_specs=[pl.BlockSpec((tm, D), lambda i: (i, 0)),
                  _const_spec(wm.shape), _const_spec(wr.shape), _const_spec(wt.shape),
                  pl.BlockSpec((tm, 6 * LANE), lambda i: (i % nst, 0))],
        out_specs=[pl.BlockSpec((tm, 14 * LANE), lambda i: (i, 0)),
                   pl.BlockSpec((tm, 9 * LANE), lambda i: (i, 0)),
                   pl.BlockSpec((tm, LANE), lambda i: (i, 0)),
                   pl.BlockSpec((tm, LANE), lambda i: (i, 0)),
                   pl.BlockSpec((wt.shape[0], tm), lambda i: (0, i))],
        out_shape=[jax.ShapeDtypeStruct((T, 14 * LANE), BF16),
                   jax.ShapeDtypeStruct((T, 9 * LANE), F32),
                   jax.ShapeDtypeStruct((T, LANE), F32),
                   jax.ShapeDtypeStruct((T, LANE), F32),
                   jax.ShapeDtypeStruct((wt.shape[0], T), BF16)],
        compiler_params=_cp(("parallel",)),
        name="in_proj",
    )(x, wm, wr, wt, tab)


def _rot_cols(w, d):
    k, n = w.shape
    w4 = w.reshape(k, n // d, 2, d // 2)
    return jnp.concatenate([-w4[:, :, 1], w4[:, :, 0]], axis=2).reshape(k, n)


def _prep_w_in(w):
    k = w.shape[0]
    w = w.astype(BF16)
    sizes = ((HEADS * DIFF_QK,) * 4 + (GROUP_W,) + (GROUP_W,) * 4 + (HEADS,) * 2
             + (GROUP_W,) + (HEAD_DIM,) * 6 + (3 * HEADS,) + (GROUP_W,) * 3)
    offs = np.concatenate([[0], np.cumsum(sizes)])
    (dq1, dq2, dk1, dk2, dv, gq, gk, gv, gz, ga, gb,
     nq, nkc, nvc, nks, nvs, nkw, nvw, ngate, sq, sk, sv) = [w[:, offs[i]:offs[i + 1]] for i in range(len(sizes))]
    scale = HEAD_DIM ** -0.5
    z64 = jnp.zeros((k, HEAD_DIM), w.dtype)
    pad = lambda a: jnp.concatenate([a, jnp.zeros((k, LANE - a.shape[1]), w.dtype)], axis=1)
    main = jnp.concatenate(
        [dq1, dq2, dk1, dk2, nq * scale, nks, nvs, nkw, nvw, nkc, nvc,
         sq * scale, sk, sv, gq, gk, gv, gz, pad(jnp.concatenate([ga, gb], axis=1)), pad(ngate)], axis=1)
    r32 = lambda a: _rot_cols(a, DIFF_QK)
    r64 = lambda a: _rot_cols(a, HEAD_DIM)
    rot = jnp.concatenate(
        [r32(dq1), r32(dq2), r32(dk1), r32(dk2), r64(nq * scale),
         r64(nks), z64, r64(nkw), z64, r64(nkc), z64], axis=1)
    wt = jnp.concatenate([dv, sv, nvs, nvw], axis=1).T
    return main, rot, wt


def _rope_table(seq_len):
    def cs(dim):
        inv = ROPE_THETA ** (-jnp.arange(0, dim, 2, dtype=F32) / dim)
        ang = jnp.arange(seq_len, dtype=F32)[:, None] * inv[None, :]
        return jnp.cos(ang), jnp.sin(ang)
    cd, sd = cs(DIFF_QK)
    cn, sn = cs(HEAD_DIM)
    one = jnp.ones((seq_len, HEAD_DIM), F32)
    zero = jnp.zeros((seq_len, HEAD_DIM), F32)
    return jnp.concatenate(
        [jnp.tile(cd, (1, 8)), jnp.tile(sd, (1, 8)), jnp.tile(cn, (1, 4)), jnp.tile(sn, (1, 4)),
         jnp.tile(cn, (1, 2)), one, jnp.tile(sn, (1, 2)), zero], axis=1)


def _tri_pairs(n, descending=False):
    qi, kj = [], []
    for i in range(n):
        js = range(i, -1, -1) if descending else range(i + 1)
        for j in js:
            qi.append(i)
            kj.append(j)
    return jnp.asarray(qi, jnp.int32), jnp.asarray(kj, jnp.int32)


ONES_ROWS = 16
SKEW = 3
SKEW_SB = 1


def _diff_kernel(qi_ref, kj_ref, q1_ref, q2_ref, k1_ref, k2_ref, vt_ref, lam_ref, g_ref, o_ref,
                 m_scr, acc_scr, *, lam_init):
    p = pl.program_id(1)
    i = qi_ref[p]
    j = kj_ref[p]
    tq, tk = q1_ref.shape[0], k1_ref.shape[0]
    c = (DIFF_QK ** -0.5) * LOG2E
    head32 = _iota((1, LANE), 1) // DIFF_QK

    @pl.when(j == 0)
    def _init():
        m_scr[...] = jnp.full(m_scr.shape, NEG, F32)
        acc_scr[...] = jnp.zeros(acc_scr.shape, F32)

    def step(masked):
        ones = jnp.ones((ONES_ROWS, tk), BF16)
        qs = (q1_ref[...], q2_ref[...])
        ks = (k1_ref[...], k2_ref[...])
        chains = [(t, h) for t in range(2) for h in range(HEADS)]

        def scores(t, h):
            km = jnp.where(head32 == h, ks[t], jnp.zeros_like(ks[t]))
            st = _dot_nt(km, qs[t])
            if masked:
                st = jnp.where(_iota((tk, tq), 0) <= _iota((tk, tq), 1), st, NEG)
            return st

        pend = [scores(*chains[n]) for n in range(SKEW)]
        for n, (t, h) in enumerate(chains):
            st = pend.pop(0)
            if n + SKEW < len(chains):
                pend.append(scores(*chains[n + SKEW]))
            idx = t * HEADS + h
            m_prev = m_scr[idx:idx + 1, :]
            m_new = jnp.maximum(m_prev, jnp.max(st, axis=0, keepdims=True))
            pt = jnp.exp2((st - m_new) * c).astype(BF16)
            alpha = jnp.exp2((m_prev - m_new) * c)
            m_scr[idx:idx + 1, :] = m_new
            vh = jnp.concatenate([vt_ref[h * HEAD_DIM:(h + 1) * HEAD_DIM, :], ones], axis=0)
            acc_scr[idx] = acc_scr[idx] * alpha + _dot(vh, pt)

    @pl.when(j < i)
    def _off():
        step(False)

    @pl.when(j == i)
    def _diag():
        step(True)
        lp = lam_ref[...]
        lam = (jnp.exp(jnp.sum(lp[0:1] * lp[1:2], axis=-1, keepdims=True))
               - jnp.exp(jnp.sum(lp[2:3] * lp[3:4], axis=-1, keepdims=True)) + lam_init)
        parts = []
        for h in range(HEADS):
            a0, a1 = acc_scr[h], acc_scr[HEADS + h]
            oh = (a0[:HEAD_DIM] / a0[HEAD_DIM:HEAD_DIM + 1] - lam * (a1[:HEAD_DIM] / a1[HEAD_DIM:HEAD_DIM + 1]))
            ms = jnp.sum(oh * oh, axis=0, keepdims=True) * (1.0 / HEAD_DIM)
            parts.append(oh * lax.rsqrt(ms + NORM_EPS))
        y = jnp.concatenate(parts, axis=0).T * g_ref[...] * (1.0 - lam_init)
        o_ref[...] = y.astype(o_ref.dtype)


def _diff_attention(hb, ht, lam_p, subln_g, lam_init, batch, seq_len):
    t = min(T_ATT, seq_len)
    nq = seq_len // t
    qi, kj = _tri_pairs(nq)
    g_full = jnp.tile(subln_g, HEADS).reshape(1, GROUP_W)
    qmap = lambda col: (lambda b, p, qi, kj: (b * nq + qi[p], col))
    kmap = lambda col: (lambda b, p, qi, kj: (b * nq + kj[p], col))
    grid_spec = pltpu.PrefetchScalarGridSpec(
        num_scalar_prefetch=2,
        grid=(batch, qi.shape[0]),
        in_specs=[pl.BlockSpec((t, LANE), qmap(0)), pl.BlockSpec((t, LANE), qmap(1)),
                  pl.BlockSpec((t, LANE), kmap(2)), pl.BlockSpec((t, LANE), kmap(3)),
                  pl.BlockSpec((GROUP_W, t), lambda b, p, qi, kj: (0, b * nq + kj[p])),
                  pl.BlockSpec((4, DIFF_QK), lambda b, p, qi, kj: (0, 0)),
                  pl.BlockSpec((1, GROUP_W), lambda b, p, qi, kj: (0, 0))],
        out_specs=pl.BlockSpec((t, GROUP_W), lambda b, p, qi, kj: (b * nq + qi[p], 0)),
        scratch_shapes=[pltpu.VMEM((2 * HEADS, t), F32),
                        pltpu.VMEM((2 * HEADS, HEAD_DIM + ONES_ROWS, t), F32)])
    return pl.pallas_call(
        functools.partial(_diff_kernel, lam_init=lam_init),
        grid_spec=grid_spec,
        out_shape=jax.ShapeDtypeStruct((batch * seq_len, GROUP_W), BF16),
        compiler_params=_cp(("parallel", "arbitrary")),
        name="diff_attn",
    )(qi, kj, hb, hb, hb, hb, ht, lam_p, g_full)


SB_CUM = 256


def _sb_kernel(qi_ref, kj_ref, q_ref, k_ref, v_ref, o_ref, carry_scr, acc_scr):
    p = pl.program_id(1)
    i = qi_ref[p]
    j = kj_ref[p]
    tq, tk = q_ref.shape[0], k_ref.shape[0]
    head64 = _iota((1, GROUP_W), 1) // HEAD_DIM
    cw = min(SB_CUM, tk)

    @pl.when(j == i)
    def _init():
        carry_scr[...] = jnp.zeros(carry_scr.shape, F32)
        acc_scr[...] = jnp.zeros(acc_scr.shape, F32)

    def step(masked):
        q, k, v = q_ref[...], k_ref[...], v_ref[...]
        m_incl = jnp.where(_iota((cw, cw), 0) >= _iota((cw, cw), 1), 1.0, 0.0).astype(BF16)
        m2 = jnp.concatenate([m_incl, m_incl], axis=0)
        if masked:
            before = _iota((tq, tk), 1) < _iota((tq, tk), 0)

        def scores(h):
            return _dot_nt(q, jnp.where(head64 == h, k, jnp.zeros_like(k)))

        pend = [scores(h) for h in range(SKEW_SB)]
        pv = None
        for h in range(HEADS):
            zl = pend.pop(0) * LOG2E
            if h + SKEW_SB < HEADS:
                pend.append(scores(h + SKEW_SB))
            sp = jnp.maximum(zl, 0.0) + jnp.log2(1.0 + jnp.exp2(-jnp.abs(zl)))
            if masked:
                sp = jnp.where(before, sp, 0.0)
            hi, lo = _split2(sp)
            blocks = []
            suffix = carry_scr[h]
            for blk in reversed(range(tk // cw)):
                sl = slice(blk * cw, (blk + 1) * cw)
                cb = _dot(jnp.concatenate([hi[:, sl], lo[:, sl]], axis=1), m2) + suffix
                blocks.insert(0, cb)
                suffix = cb[:, 0:1]
            carry_scr[h] = suffix
            att = jnp.exp2(zl - jnp.concatenate(blocks, axis=1))
            if masked:
                att = jnp.where(before, att, 0.0)
            vm = jnp.where(head64 == h, v, jnp.zeros_like(v))
            d = _dot(att.astype(BF16), vm)
            pv = d if pv is None else pv + d
        acc_scr[...] = acc_scr[...] + pv

    @pl.when(j == i)
    def _diag():
        step(True)

    @pl.when(j < i)
    def _off():
        step(False)

    @pl.when(j == 0)
    def _fin():
        o_ref[...] = acc_scr[...].astype(o_ref.dtype)


def _sb_attention(hb, batch, seq_len):
    t = min(T_ATT, seq_len)
    nq = seq_len // t
    qi, kj = _tri_pairs(nq, descending=True)
    grid_spec = pltpu.PrefetchScalarGridSpec(
        num_scalar_prefetch=2,
        grid=(batch, qi.shape[0]),
        in_specs=[pl.BlockSpec((t, GROUP_W), lambda b, p, qi, kj: (b * nq + qi[p], 4)),
                  pl.BlockSpec((t, GROUP_W), lambda b, p, qi, kj: (b * nq + kj[p], 5)),
                  pl.BlockSpec((t, GROUP_W), lambda b, p, qi, kj: (b * nq + kj[p], 6))],
        out_specs=pl.BlockSpec((t, GROUP_W), lambda b, p, qi, kj: (b * nq + qi[p], 0)),
        scratch_shapes=[pltpu.VMEM((HEADS, t, 1), F32), pltpu.VMEM((t, GROUP_W), F32)])
    return pl.pallas_call(
        _sb_kernel,
        grid_spec=grid_spec,
        out_shape=jax.ShapeDtypeStruct((batch * seq_len, GROUP_W), BF16),
        compiler_params=_cp(("parallel", "arbitrary")),
        name="sb_attn",
    )(qi, kj, hb, hb, hb)


def _stack_heads(q):
    qf = q.astype(F32)
    lo = _iota((1, LANE), 1) < HEAD_DIM
    parts = []
    for blk in (qf[:, :LANE], qf[:, LANE:]):
        parts.append(jnp.where(lo, blk, 0.0))
        parts.append(jnp.where(lo, pltpu.roll(blk, HEAD_DIM, 1), 0.0))
    return jnp.concatenate(parts, axis=0).astype(BF16)


def _nsa_compress_kernel(r_ref, pe_ref, w1lo_ref, w1hi_ref, w2_ref, w2vt_ref, o_ref, vt_ref):
    r = r_ref[...]
    n = r.shape[0]
    y1 = _dot((r + pe_ref[0:1, :]).astype(BF16), w1lo_ref[...])
    y2 = _dot((r + pe_ref[1:2, :]).astype(BF16), w1hi_ref[...])
    hid = y1 + pltpu.roll(y2, n - 1, 0)
    act = (hid * jax.nn.sigmoid(hid)).astype(BF16)
    o_ref[...] = _dot(act, w2_ref[...]).astype(o_ref.dtype)
    vt_ref[...] = _dot_nt(w2vt_ref[...], act).astype(vt_ref.dtype)


def _nsa_compress(kcvc, pe_k, pe_v, ck_w1, ck_w2, cv_w1, cv_w2, batch, seq_len):
    nrow = seq_len // CMP_STRIDE
    width = CMP_STRIDE * LANE
    r = kcvc.reshape(batch * nrow, width)
    hid = ck_w1.shape[1]
    pe = jnp.concatenate([pe_k, pe_v], axis=1).reshape(2, width)
    zk = jnp.zeros((CMP_BLOCK, HEAD_DIM, hid), BF16)
    w1 = jnp.concatenate(
        [jnp.concatenate([ck_w1.astype(BF16).reshape(CMP_BLOCK, HEAD_DIM, hid), zk], axis=2),
         jnp.concatenate([zk, cv_w1.astype(BF16).reshape(CMP_BLOCK, HEAD_DIM, hid)], axis=2)], axis=1)
    w1 = w1.reshape(2, width, 2 * hid)
    zo = jnp.zeros((hid, HEAD_DIM), BF16)
    w2 = jnp.concatenate([jnp.concatenate([ck_w2.astype(BF16), zo], axis=1),
                          jnp.concatenate([zo, cv_w2.astype(BF16)], axis=1)], axis=0)
    return pl.pallas_call(
        _nsa_compress_kernel,
        grid=(batch,),
        in_specs=[pl.BlockSpec((nrow, width), lambda b: (b, 0)),
                  _const_spec((2, width)), _const_spec((width, 2 * hid)), _const_spec((width, 2 * hid)),
                  _const_spec((2 * hid, LANE)), _const_spec((HEAD_DIM, 2 * hid))],
        out_specs=[pl.BlockSpec((nrow, LANE), lambda b: (b, 0)),
                   pl.BlockSpec((HEAD_DIM, nrow), lambda b: (b, 0))],
        out_shape=[jax.ShapeDtypeStruct((batch * nrow, LANE), BF16),
                   jax.ShapeDtypeStruct((batch * HEAD_DIM, nrow), BF16)],
        compiler_params=_cp(("parallel",)),
        name="nsa_compress",
    )(r, pe, w1[0], w1[1], w2, w2[:, HEAD_DIM:].T)


def _nsa_cmp_kernel(q_ref, kv_ref, vt_ref, ovt_ref, ocmp_ref, bias_ref):
    i = pl.program_id(1)
    tq = q_ref.shape[0]
    ncmp = kv_ref.shape[0]
    nslc = ovt_ref.shape[0]
    qs = _stack_heads(q_ref[...])
    kv = kv_ref[...]
    vt = vt_ref[...]
    tpos = i * tq + _iota((1, tq), 1)
    cm = _iota((ncmp, 1), 0) * CMP_STRIDE + (CMP_BLOCK - 1) <= tpos
    sts = [_dot_nt(kv, qs[h * tq:(h + 1) * tq]) for h in range(HEADS)]
    psum = None
    parts = []
    for h in range(HEADS):
        st = jnp.where(cm, sts[h], NEG)
        e = jnp.exp(st - jnp.max(st, axis=0, keepdims=True))
        pr = jnp.where(cm, e * (1.0 / jnp.sum(e, axis=0, keepdims=True)), 0.0)
        parts.append(_dot(vt, pr.astype(BF16)))
        psum = pr if psum is None else psum + pr
    ocmp_ref[...] = jnp.concatenate(parts, axis=0).T
    hi, lo = _split2(psum)
    imp = _dot(ovt_ref[...], hi) + _dot(ovt_ref[...], lo)
    blk = _iota((nslc, 1), 0)
    cur = tpos // SLC_BLOCK
    work = jnp.where(blk == 0, FORCE, jnp.where(blk == cur, FORCE, jnp.where(blk == cur - 1, FORCE, imp)))
    work = jnp.where(blk <= cur, work, -FORCE)
    sel = jnp.zeros((nslc, tq), F32)
    for _ in range(min(SLC_TOPN, nslc)):
        mx = jnp.max(work, axis=0, keepdims=True)
        first = jnp.min(jnp.where(work == mx, blk, nslc), axis=0, keepdims=True)
        hit = blk == first
        sel = jnp.where(hit, 1.0, sel)
        work = jnp.where(hit, -jnp.inf, work)
    bias_ref[...] = jnp.where(sel > 0.5, 0.0, NEG)


def _nsa_cmp(hb, kvcmp, vtcmp, batch, seq_len):
    tq = min(TQ_CMP, seq_len)
    nq = seq_len // tq
    ncmp = seq_len // CMP_STRIDE
    nslc = seq_len // SLC_BLOCK
    cstart = np.arange(ncmp)[None, :] * CMP_STRIDE
    sstart = np.arange(nslc)[:, None] * SLC_BLOCK
    ovt = (cstart < sstart + SLC_BLOCK) & (cstart + CMP_BLOCK - 1 >= sstart)
    ovt &= (np.arange(ncmp)[None, :] < (seq_len - CMP_BLOCK) // CMP_STRIDE + 1)
    ovt = jnp.asarray(ovt, BF16)
    return pl.pallas_call(
        _nsa_cmp_kernel,
        grid=(batch, nq),
        in_specs=[pl.BlockSpec((tq, GROUP_W), lambda b, i: (b * nq + i, 2)),
                  pl.BlockSpec((ncmp, LANE), lambda b, i: (b, 0)),
                  pl.BlockSpec((HEAD_DIM, ncmp), lambda b, i: (b, 0)),
                  pl.BlockSpec((nslc, ncmp), lambda b, i: (0, 0))],
        out_specs=[pl.BlockSpec((tq, GROUP_W), lambda b, i: (b * nq + i, 0)),
                   pl.BlockSpec((nslc, tq), lambda b, i: (0, b * nq + i))],
        out_shape=[jax.ShapeDtypeStruct((batch * seq_len, GROUP_W), F32),
                   jax.ShapeDtypeStruct((nslc, batch * seq_len), F32)],
        compiler_params=_cp(("parallel", "parallel")),
        name="nsa_cmp_select",
    )(hb, kvcmp, vtcmp, ovt)


def _nsa_win_kernel(q_ref, k_ref, vt_ref, o_ref, *, window):
    i = pl.program_id(1)
    tq = q_ref.shape[0]
    span = tq + window
    base = pl.multiple_of(jnp.maximum(i * tq - window, 0), LANE)
    kv = k_ref[pl.ds(base, span), :]
    vh = jnp.concatenate([vt_ref[:, pl.ds(base, span)], jnp.ones((ONES_ROWS, span), BF16)], axis=0)
    qs = _stack_heads(q_ref[...])
    rel = (i * tq + _iota((1, tq), 1)) - (base + _iota((span, 1), 0))
    bias = jnp.where(rel >= 0, jnp.where(rel < window, 0.0, NEG), NEG)
    sts = [_dot_nt(kv, qs[h * tq:(h + 1) * tq]) for h in range(HEADS)]
    parts = []
    for h in range(HEADS):
        st = sts[h] + bias
        e = jnp.exp(st - jnp.max(st, axis=0, keepdims=True)).astype(BF16)
        r = _dot(vh, e)
        parts.append(r[:HEAD_DIM] / r[HEAD_DIM:HEAD_DIM + 1])
    o_ref[...] = jnp.concatenate(parts, axis=0).T


def _nsa_window(hb, ht, batch, seq_len):
    tq = min(TQ_WIN, seq_len)
    nq = seq_len // tq
    window = min(WINDOW, seq_len - tq)
    vw_row_blk = 2 * GROUP_W // HEAD_DIM + 1
    return pl.pallas_call(
        functools.partial(_nsa_win_kernel, window=window),
        grid=(batch, nq),
        in_specs=[pl.BlockSpec((tq, GROUP_W), lambda b, i: (b * nq + i, 2)),
                  pl.BlockSpec((seq_len, LANE), lambda b, i: (b, 7)),
                  pl.BlockSpec((HEAD_DIM, seq_len), lambda b, i: (vw_row_blk, b))],
        out_specs=pl.BlockSpec((tq, GROUP_W), lambda b, i: (b * nq + i, 0)),
        out_shape=jax.ShapeDtypeStruct((batch * seq_len, GROUP_W), F32),
        compiler_params=_cp(("parallel", "parallel")),
        name="nsa_window",
    )(hb, hb, ht)


def _nsa_sel_kernel(qi_ref, kj_ref, q_ref, k_ref, vt_ref, bias_ref, ocmp_ref, owin_ref, gate_ref, o_ref,
                    qs_scr, m_scr, acc_scr):
    p = pl.program_id(1)
    i = qi_ref[p]
    j = kj_ref[p]
    tq, tk = q_ref.shape[0], k_ref.shape[0]
    nblk = tk // SLC_BLOCK

    @pl.when(j == 0)
    def _init():
        qs = _stack_heads(q_ref[...])
        for h in range(HEADS):
            qs_scr[h] = qs[h * tq:(h + 1) * tq]
        m_scr[...] = jnp.full(m_scr.shape, NEG, F32)
        acc_scr[...] = jnp.zeros(acc_scr.shape, F32)

    def step(masked):
        kv = k_ref[...]
        bias = bias_ref[...][:, None, :]
        vh = jnp.concatenate([vt_ref[...], jnp.ones((ONES_ROWS, tk), BF16)], axis=0)

        def scores(h):
            st = _dot_nt(kv, qs_scr[h])
            st = (st.reshape(nblk, SLC_BLOCK, tq) + bias).reshape(tk, tq)
            if masked:
                st = jnp.where(j * tk + _iota((tk, tq), 0) <= i * tq + _iota((tk, tq), 1), st, NEG)
            return st

        pend = [scores(h) for h in range(SKEW)]
        for h in range(HEADS):
            st = pend.pop(0)
            if h + SKEW < HEADS:
                pend.append(scores(h + SKEW))
            m_prev = m_scr[h:h + 1, :]
            m_new = jnp.maximum(m_prev, jnp.max(st, axis=0, keepdims=True))
            pt = jnp.exp(st - m_new).astype(BF16)
            alpha = jnp.exp(m_prev - m_new)
            m_scr[h:h + 1, :] = m_new
            acc_scr[h] = acc_scr[h] * alpha + _dot(vh, pt)

    last = (i * tq + tq - 1) // tk

    @pl.when((j + 1) * tk <= i * tq)
    def _past():
        step(False)

    @pl.when((j + 1) * tk > i * tq)
    def _diag():
        step(True)

    @pl.when(j == last)
    def _fin():
        parts = []
        for h in range(HEADS):
            a = acc_scr[h]
            parts.append(a[:HEAD_DIM] / a[HEAD_DIM:HEAD_DIM + 1])
        osel = jnp.concatenate(parts, axis=0).T
        sig = jax.nn.sigmoid(gate_ref[...])
        grow = _iota((LANE, 1), 0)
        ghead = _iota((1, GROUP_W), 1) // HEAD_DIM
        out = None
        for br, o_br in enumerate((ocmp_ref[...], osel, owin_ref[...])):
            e_br = jnp.where(grow == 3 * ghead + br, 1.0, 0.0).astype(BF16)
            term = _dot_x3(sig, e_br) * o_br
            out = term if out is None else out + term
        o_ref[...] = out.astype(o_ref.dtype)


def _nsa_select(hb, ht, sel, ocmp, owin, gates, batch, seq_len):
    tq = min(TQ_SEL, seq_len)
    tk = min(TK_SEL, seq_len)
    nq, nk = seq_len // tq, seq_len // tk
    qi, kj = [], []
    for i in range(nq):
        for j in range((i * tq + tq - 1) // tk + 1):
            qi.append(i)
            kj.append(j)
    qi, kj = jnp.asarray(qi, jnp.int32), jnp.asarray(kj, jnp.int32)
    qrow = lambda b, p, qi, kj: (b * nq + qi[p], 0)
    vs_row_blk = 2 * GROUP_W // HEAD_DIM
    grid_spec = pltpu.PrefetchScalarGridSpec(
        num_scalar_prefetch=2,
        grid=(batch, qi.shape[0]),
        in_specs=[pl.BlockSpec((tq, GROUP_W), lambda b, p, qi, kj: (b * nq + qi[p], 2)),
                  pl.BlockSpec((tk, LANE), lambda b, p, qi, kj: (b * nk + kj[p], 6)),
                  pl.BlockSpec((HEAD_DIM, tk), lambda b, p, qi, kj: (vs_row_blk, b * nk + kj[p])),
                  pl.BlockSpec((tk // SLC_BLOCK, tq), lambda b, p, qi, kj: (kj[p], b * nq + qi[p])),
                  pl.BlockSpec((tq, GROUP_W), qrow), pl.BlockSpec((tq, GROUP_W), qrow),
                  pl.BlockSpec((tq, LANE), qrow)],
        out_specs=pl.BlockSpec((tq, GROUP_W), qrow),
        scratch_shapes=[pltpu.VMEM((HEADS, tq, LANE), BF16), pltpu.VMEM((HEADS, tq), F32),
                        pltpu.VMEM((HEADS, HEAD_DIM + ONES_ROWS, tq), F32)])
    return pl.pallas_call(
        _nsa_sel_kernel,
        grid_spec=grid_spec,
        out_shape=jax.ShapeDtypeStruct((batch * seq_len, GROUP_W), BF16),
        compiler_params=_cp(("parallel", "arbitrary")),
        name="nsa_select_gate",
    )(qi, kj, hb, hb, ht, sel, ocmp, owin, gates)


def _nsa(hb, ht, kcvc, gates, pe_k, pe_v, ck_w1, ck_w2, cv_w1, cv_w2, batch, seq_len):
    kvcmp, vtcmp = _nsa_compress(kcvc, pe_k, pe_v, ck_w1, ck_w2, cv_w1, cv_w2, batch, seq_len)
    ocmp, sel = _nsa_cmp(hb, kvcmp, vtcmp, batch, seq_len)
    owin = _nsa_window(hb, ht, batch, seq_len)
    return _nsa_select(hb, ht, sel, ocmp, owin, gates, batch, seq_len)


def _bd(mc, mask_bd):
    return jnp.where(mask_bd, jnp.concatenate([mc] * HEADS, axis=0), jnp.zeros((), mc.dtype))


def _mm_bd(x, mc, mask_bd):
    return _dot(x.astype(BF16), _bd(mc.astype(BF16), mask_bd))


def _gdn_prep_kernel(x_ref, ab_ref, cw_ref, alog_ref, dtb_ref,
                     u_ref, w_ref, qd_ref, in_ref, kdt_ref, gl_ref, xpad_scr):
    n = pl.program_id(0)
    C = GDN_CHUNK
    W = GROUP_W
    nb, rows = x_ref.shape[0], x_ref.shape[1]
    inst = [(b, c) for b in range(nb) for c in range(rows // C)]

    @pl.when(n == 0)
    def _():
        xpad_scr[:, 0:8, :] = jnp.zeros((nb, 8, xpad_scr.shape[2]), F32)

    cw = cw_ref[...]
    qkv_b = []
    for b in range(nb):
        x = x_ref[b]
        xpad_scr[b, 8:8 + rows, :] = x
        conv = (cw[0:1] * xpad_scr[b, 5:5 + rows, :] + cw[1:2] * xpad_scr[b, 6:6 + rows, :]
                + cw[2:3] * xpad_scr[b, 7:7 + rows, :] + cw[3:4] * x)
        xpad_scr[b, 0:8, :] = x[rows - 8:rows, :]
        qkv_b.append(conv * jax.nn.sigmoid(conv))

    r256 = _iota((W, W), 0)
    c256 = _iota((W, W), 1)
    mask_bd = (r256 // HEAD_DIM) == (c256 // HEAD_DIM)
    ones_bd = jnp.where(mask_bd, 1.0, 0.0).astype(BF16)
    eye256 = jnp.where(r256 == c256, 1.0, 0.0).astype(BF16)
    row = _iota((C, W), 0)
    jl = _iota((C, W), 1) % HEAD_DIM
    ltri = jnp.where(_iota((C, C), 1) <= _iota((C, C), 0), 1.0, 0.0).astype(BF16)
    erow = _iota((LANE, W), 0)
    ehead = _iota((LANE, W), 1) // HEAD_DIM
    e_g = jnp.where(erow == ehead, 1.0, 0.0).astype(BF16)
    e_b = jnp.where(erow == ehead + HEADS, 1.0, 0.0).astype(BF16)

    def each(f, *lists):
        return [f(*args) for args in zip(*lists)]

    sl = [slice(c * C, (c + 1) * C) for _, c in inst]
    q = [qkv_b[b][sl[k], 0:W] for k, (b, _) in enumerate(inst)]
    kk = [qkv_b[b][sl[k], W:2 * W] for k, (b, _) in enumerate(inst)]
    v = [qkv_b[b][sl[k], 2 * W:3 * W] for k, (b, _) in enumerate(inst)]
    ab = [ab_ref[b, sl[k], :] for k, (b, _) in enumerate(inst)]

    qn = each(lambda t: t * lax.rsqrt(_dot_x2(t * t, ones_bd) + NORM_EPS) * (HEAD_DIM ** -0.5), q)
    kn = each(lambda t: t * lax.rsqrt(_dot_x2(t * t, ones_bd) + NORM_EPS), kk)

    def gate(a):
        z = a + dtb_ref[...]
        return -jnp.exp(alog_ref[...]) * (jnp.maximum(z, 0.0) + jnp.log1p(jnp.exp(-jnp.abs(z))))

    g_hl = each(lambda a: _dot_x3(gate(a), e_g), ab)
    beta = each(lambda a: _dot_x3(jax.nn.sigmoid(a), e_b), ab)
    gc = each(lambda g: _dot_x3_left(ltri, g), g_hl)
    glast = each(lambda g: g[C - 1:C, :], gc)
    exp_g = each(jnp.exp, gc)
    dmat = each(lambda g: _dot_x3_left(ltri, jnp.where(row > jl, g, 0.0)), g_hl)
    decay = each(lambda d: jnp.where(jl <= row, jnp.exp(d), 0.0), dmat)

    kt4 = each(lambda t: _dot_nt(eye256, jnp.concatenate([t.astype(BF16)] * HEADS, axis=0)), kn)
    kb_mat = each(lambda t: jnp.where(mask_bd, t, 0.0).astype(BF16), kt4)
    kbeta = each(lambda t, bb: t * bb, kn, beta)
    a_c = each(lambda t, m, d: jnp.where(jl < row, _dot(t.astype(BF16), m) * d, 0.0), kbeta, kb_mat, decay)
    intra = each(lambda t, m, d: _dot(t.astype(BF16), m) * d, qn, kb_mat, decay)

    t_c = each(lambda a: jnp.where(jl == row, 1.0, 0.0) - a, a_c)
    p_c = a_c
    for _ in range(5):
        p_c = each(lambda pc: _mm_bd(pc, pc, mask_bd), p_c)
        t_c = each(lambda tc, pc: tc + _mm_bd(tc, pc, mask_bd), t_c, p_c)

    u = each(lambda tc, t, bb: _mm_bd(tc, t * bb, mask_bd), t_c, v, beta)
    w = each(lambda tc, t, e: _mm_bd(tc, t * e, mask_bd), t_c, kbeta, exp_g)
    kdt = each(lambda t, gl, g: _dot_nt(eye256, (t * jnp.exp(gl - g)).astype(BF16)), kn, glast, gc)

    for k, (b, c) in enumerate(inst):
        u_ref[b, sl[k], :] = u[k]
        w_ref[b, sl[k], :] = w[k].astype(w_ref.dtype)
        qd_ref[b, sl[k], :] = (qn[k] * exp_g[k]).astype(qd_ref.dtype)
        in_ref[b, sl[k], :] = intra[k].astype(in_ref.dtype)
        kdt_ref[b, c] = kdt[k].astype(kdt_ref.dtype)
        gl_ref[b, c] = jnp.exp(glast[k])


def _dot_x3_left(w, x):
    hi = x.astype(BF16)
    r = x - hi.astype(F32)
    mid = r.astype(BF16)
    lo = (r - mid.astype(F32)).astype(BF16)
    return _dot(w, hi) + _dot(w, mid) + _dot(w, lo)


def _gdn_scan_kernel(u_ref, w_ref, qd_ref, in_ref, kdt_ref, gl_ref, z_ref, g_ref, o_ref, s_scr):
    n = pl.program_id(0)
    C = GDN_CHUNK
    W = GROUP_W
    nb, rows = u_ref.shape[0], u_ref.shape[1]

    @pl.when(n == 0)
    def _():
        s_scr[...] = jnp.zeros(s_scr.shape, F32)

    mask_bd = (_iota((W, W), 0) // HEAD_DIM) == (_iota((W, W), 1) // HEAD_DIM)
    ones_bd = jnp.where(mask_bd, 1.0, 0.0).astype(BF16)
    s = [s_scr[b] for b in range(nb)]
    for c in range(rows // C):
        sl = slice(c * C, (c + 1) * C)
        sb = [t.astype(BF16) for t in s]
        v_new = [u_ref[b, sl, :] - _dot(w_ref[b, sl, :], sb[b]) for b in range(nb)]
        vb = [t.astype(BF16) for t in v_new]
        s = [s[b] * gl_ref[b, c] + jnp.where(mask_bd, _dot(kdt_ref[b, c], vb[b]), 0.0) for b in range(nb)]
        o = [_dot(qd_ref[b, sl, :], sb[b]) + _dot(in_ref[b, sl, :], _bd(vb[b], mask_bd)) for b in range(nb)]
        for b in range(nb):
            ms = _dot_x2(o[b] * o[b], ones_bd) * (1.0 / HEAD_DIM)
            zz = z_ref[b, sl, :]
            y = o[b] * lax.rsqrt(ms + NORM_EPS) * g_ref[...] * (zz * jax.nn.sigmoid(zz))
            o_ref[b, sl, :] = y.astype(o_ref.dtype)
    for b in range(nb):
        s_scr[b] = s[b]


GDN_PREP_CHUNKS = 4
GDN_SCAN_CHUNKS = 4


def _gdn(hg, conv_w, a_log, dt_bias, norm_g, batch, seq_len):
    C = GDN_CHUNK
    nc = seq_len // C
    W = GROUP_W
    padl = lambda a: jnp.concatenate([a, jnp.zeros((LANE - a.shape[0],), F32)]).reshape(1, LANE)
    hg3 = hg.reshape(batch, seq_len, hg.shape[-1])
    cp = math.gcd(GDN_PREP_CHUNKS, nc)
    rp = cp * C
    blkp = pl.BlockSpec((batch, rp, W), lambda n: (0, n, 0))
    u, w, qd, intra, kdt, gl = pl.pallas_call(
        _gdn_prep_kernel,
        grid=(nc // cp,),
        in_specs=[pl.BlockSpec((batch, rp, 3 * W), lambda n: (0, n, 0)),
                  pl.BlockSpec((batch, rp, LANE), lambda n: (0, n, 8)),
                  pl.BlockSpec((4, 3 * W), lambda n: (0, 0)),
                  pl.BlockSpec((1, LANE), lambda n: (0, 0)),
                  pl.BlockSpec((1, LANE), lambda n: (0, 0))],
        out_specs=[blkp] * 4
        + [pl.BlockSpec((batch, cp, W, C), lambda n: (0, n, 0, 0)),
           pl.BlockSpec((batch, cp, 1, W), lambda n: (0, n, 0, 0))],
        out_shape=[jax.ShapeDtypeStruct((batch, seq_len, W), F32), jax.ShapeDtypeStruct((batch, seq_len, W), BF16),
                   jax.ShapeDtypeStruct((batch, seq_len, W), BF16), jax.ShapeDtypeStruct((batch, seq_len, W), BF16),
                   jax.ShapeDtypeStruct((batch, nc, W, C), BF16),
                   jax.ShapeDtypeStruct((batch, nc, 1, W), F32)],
        scratch_shapes=[pltpu.VMEM((batch, 8 + rp, 3 * W), F32)],
        compiler_params=_cp(("arbitrary",)),
        name="gdn_prep",
    )(hg3, hg3, conv_w, padl(a_log), padl(dt_bias))

    cs = math.gcd(GDN_SCAN_CHUNKS, nc)
    blk = pl.BlockSpec((batch, cs * C, W), lambda n: (0, n, 0))
    out = pl.pallas_call(
        _gdn_scan_kernel,
        grid=(nc // cs,),
        in_specs=[blk, blk, blk, blk,
                  pl.BlockSpec((batch, cs, W, C), lambda n: (0, n, 0, 0)),
                  pl.BlockSpec((batch, cs, 1, W), lambda n: (0, n, 0, 0)),
                  pl.BlockSpec((batch, cs * C, W), lambda n: (0, n, 3)),
                  pl.BlockSpec((1, W), lambda n: (0, 0))],
        out_specs=blk,
        out_shape=jax.ShapeDtypeStruct((batch, seq_len, W), BF16),
        scratch_shapes=[pltpu.VMEM((batch, W, W), F32)],
        compiler_params=_cp(("arbitrary",)),
        name="gdn_scan",
    )(u, w, qd, intra, kdt, gl, hg3, jnp.tile(norm_g, HEADS).reshape(1, W))
    return out.reshape(batch * seq_len, W)


def kernel(x, w_in, w_out, ffn1_w_gu, ffn1_w_down, ffn2_w_gu, ffn2_w_down, ln1_g, ln1_b, ln2_g, ln2_b, ln3_g, ln3_b, diff_lam_q1, diff_lam_k1, diff_lam_q2, diff_lam_k2, diff_subln_g, gdn_conv_w, gdn_a_log, gdn_dt_bias, gdn_norm_g, nsa_pe_k, nsa_pe_v, nsa_cmp_k_w1, nsa_cmp_k_w2, nsa_cmp_v_w1, nsa_cmp_v_w2):
    B, S, D = x.shape
    depth = w_in.shape[0]
    alpha = (2 * depth) ** 0.25
    tab = _rope_table(S)
    xf = x.reshape(B * S, D)
    for l in range(depth):
        lam_init = 0.8 - 0.6 * math.exp(-0.3 * l)
        xf = _ffn_ln(xf, ffn1_w_gu[l].astype(BF16), ffn1_w_down[l].astype(BF16), ln1_g[l], ln1_b[l], alpha)
        wm, wr, wt = _prep_w_in(w_in[l])
        hb, hg, kcvc, gates, ht = _proj(xf, wm, wr, wt, tab, S)
        lam_p = jnp.stack([diff_lam_q1[l], diff_lam_k1[l], diff_lam_q2[l], diff_lam_k2[l]])
        o_diff = _diff_attention(hb, ht, lam_p, diff_subln_g[l], lam_init, B, S)
        o_gdn = _gdn(hg, gdn_conv_w[l], gdn_a_log[l], gdn_dt_bias[l], gdn_norm_g[l], B, S)
        o_nsa = _nsa(hb, ht, kcvc, gates, nsa_pe_k[l], nsa_pe_v[l], nsa_cmp_k_w1[l], nsa_cmp_k_w2[l],
                     nsa_cmp_v_w1[l], nsa_cmp_v_w2[l], B, S)
        o_sb = _sb_attention(hb, B, S)
        xf = _outproj_ln(xf, (o_diff, o_gdn, o_nsa, o_sb), w_out[l].astype(BF16), ln2_g[l], ln2_b[l], alpha)
        xf = _ffn_ln(xf, ffn2_w_gu[l].astype(BF16), ffn2_w_down[l].astype(BF16), ln3_g[l], ln3_b[l], alpha)
    return xf.reshape(B, S, D)
```

```python
import functools
import math

import numpy as np
import jax
import jax.numpy as jnp
from jax import lax
from jax.experimental import pallas as pl
from jax.experimental.pallas import tpu as pltpu

F32 = jnp.float32
BF16 = jnp.bfloat16

DEPTH = 2
HEAD_DIM = 64
HEADS = 4
GROUP_W = HEADS * HEAD_DIM
DIFF_QK = HEAD_DIM // 2
GDN_CHUNK = 64
CMP_BLOCK, CMP_STRIDE = 32, 16
SLC_BLOCK, SLC_TOPN = 64, 16
WINDOW = 512
FORCE = 1e4
ROPE_THETA = 10000.0
LN_EPS = 1e-5
NORM_EPS = 1e-6
NEG = -1e30
LOG2E = 1.4426950408889634

LANE = 128
V7X_VMEM_BYTES = 64 * 1024 * 1024
VMEM_LIMIT = V7X_VMEM_BYTES - 8 * 1024 * 1024

TM_FFN = 512
FF_CHUNK = 256
TM_PROJ = 512
T_ATT = 512
TQ_CMP = 256
TQ_WIN = 512
TQ_SEL, TK_SEL = 512, 512


def _cp(sem):
    return pltpu.CompilerParams(dimension_semantics=sem, vmem_limit_bytes=VMEM_LIMIT)


def _iota(shape, dim):
    return lax.broadcasted_iota(jnp.int32, shape, dim)


def _dot(a, b):
    return jnp.dot(a, b, preferred_element_type=F32)


def _dot_nt(a, b):
    return lax.dot_general(a, b, (((1,), (1,)), ((), ())), preferred_element_type=F32)


def _split2(x):
    hi = x.astype(BF16)
    lo = (x - hi.astype(F32)).astype(BF16)
    return hi, lo


def _dot_x2(x, w):
    hi, lo = _split2(x)
    return _dot(hi, w) + _dot(lo, w)


def _dot_x3(x, w):
    hi = x.astype(BF16)
    r = x - hi.astype(F32)
    mid = r.astype(BF16)
    lo = (r - mid.astype(F32)).astype(BF16)
    return _dot(hi, w) + _dot(mid, w) + _dot(lo, w)


def _layer_norm(y, g, b):
    mu = jnp.mean(y, axis=-1, keepdims=True)
    d = y - mu
    var = jnp.mean(d * d, axis=-1, keepdims=True)
    return d * lax.rsqrt(var + LN_EPS) * g + b


def _const_spec(shape):
    nd = len(shape)
    return pl.BlockSpec(shape, lambda *_: (0,) * nd, pipeline_mode=pl.Buffered(1))


def _ffn_ln_kernel(x_ref, wgu_ref, wd_ref, g_ref, b_ref, o_ref, *, alpha, d_ff, ff_chunk):
    x = x_ref[...]
    xb = x.astype(BF16)
    acc = None
    for c in range(d_ff // ff_chunk):
        lo = c * ff_chunk
        g = _dot(xb, wgu_ref[:, lo:lo + ff_chunk])
        u = _dot(xb, wgu_ref[:, d_ff + lo:d_ff + lo + ff_chunk])
        a = (g * jax.nn.sigmoid(g) * u).astype(BF16)
        part = _dot(a, wd_ref[lo:lo + ff_chunk, :])
        acc = part if acc is None else acc + part
    o_ref[...] = _layer_norm(alpha * x + 0.5 * acc, g_ref[...], b_ref[...])


def _ffn_ln(x, w_gu, w_down, g, b, alpha):
    T, D = x.shape
    d_ff = w_down.shape[0]
    tm = min(TM_FFN, T)
    ff_chunk = FF_CHUNK if d_ff % FF_CHUNK == 0 else d_ff
    return pl.pallas_call(
        functools.partial(_ffn_ln_kernel, alpha=alpha, d_ff=d_ff, ff_chunk=ff_chunk),
        grid=(T // tm,),
        in_specs=[pl.BlockSpec((tm, D), lambda i: (i, 0)),
                  _const_spec((D, 2 * d_ff)), _const_spec((d_ff, D)),
                  _const_spec((1, D)), _const_spec((1, D))],
        out_specs=pl.BlockSpec((tm, D), lambda i: (i, 0)),
        out_shape=jax.ShapeDtypeStruct((T, D), F32),
        compiler_params=_cp(("parallel",)),
        name="ffn_ln",
    )(x, w_gu, w_down, g.reshape(1, D), b.reshape(1, D))


def _outproj_ln_kernel(x_ref, o0_ref, o1_ref, o2_ref, o3_ref, w_ref, g_ref, b_ref, out_ref, *, alpha):
    gw = o0_ref.shape[1]
    mix = None
    for k, o_ref in enumerate((o0_ref, o1_ref, o2_ref, o3_ref)):
        part = _dot(o_ref[...], w_ref[k * gw:(k + 1) * gw, :])
        mix = part if mix is None else mix + part
    out_ref[...] = _layer_norm(alpha * x_ref[...] + mix, g_ref[...], b_ref[...])


def _outproj_ln(x, outs, w_out, g, b, alpha):
    T, D = x.shape
    tm = min(TM_FFN, T)
    gw = outs[0].shape[1]
    return pl.pallas_call(
        functools.partial(_outproj_ln_kernel, alpha=alpha),
        grid=(T // tm,),
        in_specs=[pl.BlockSpec((tm, D), lambda i: (i, 0))]
        + [pl.BlockSpec((tm, gw), lambda i: (i, 0))] * 4
        + [_const_spec(w_out.shape), _const_spec((1, D)), _const_spec((1, D))],
        out_specs=pl.BlockSpec((tm, D), lambda i: (i, 0)),
        out_shape=jax.ShapeDtypeStruct((T, D), F32),
        compiler_params=_cp(("parallel",)),
        name="outproj_ln",
    )(x, *outs, w_out, g.reshape(1, D), b.reshape(1, D))


N_ROPE_BLK = 9
N_MAIN_BLK = 25


def _proj_kernel(x_ref, wm_ref, wt_ref, tab_ref,
                 dq_ref, dk_ref, nq_ref, ks_ref, kw_ref, sq_ref, sk_ref, sv_ref, hg_ref, kc_ref, gt_ref, ht_ref):
    xb = x_ref[...].astype(BF16)
    ht_ref[...] = _dot_nt(wt_ref[...], xb).astype(BF16)
    nr = N_ROPE_BLK * LANE
    h = _dot(xb, wm_ref[:, :nr])
    tab = tab_ref[...]
    lane = _iota((1, LANE), 1)
    for c in range(N_ROPE_BLK):
        t0 = 0 if c < 4 else (2 if c < 6 else 4)
        half = (DIFF_QK if c < 4 else HEAD_DIM) // 2
        cs = tab[:, t0 * LANE:(t0 + 1) * LANE]
        sn = tab[:, (t0 + 1) * LANE:(t0 + 2) * LANE]
        t = h[:, c * LANE:(c + 1) * LANE]
        rot = jnp.where(lane % (2 * half) < half, pltpu.roll(t, LANE - half, 1), pltpu.roll(t, half, 1))
        val = t * cs + rot * sn
        if c < 6:
            (dq_ref, dk_ref, nq_ref)[c // 2][:, (c % 2) * LANE:(c % 2 + 1) * LANE] = val.astype(BF16)
        elif c < 8:
            (ks_ref, kw_ref)[c - 6][...] = val.astype(BF16)
        else:
            kc_ref[...] = val
    hp = _dot(xb, wm_ref[:, 9 * LANE:15 * LANE]).astype(BF16)
    for n, ref in enumerate((sq_ref, sk_ref, sv_ref)):
        ref[...] = hp[:, n * GROUP_W:(n + 1) * GROUP_W]
    hf = _dot(xb, wm_ref[:, 15 * LANE:25 * LANE])
    hg_ref[...] = hf[:, :9 * LANE]
    gt_ref[...] = hf[:, 9 * LANE:10 * LANE]


def _proj(x, wm, wt, tab, seq_len):
    T, D = x.shape
    tm = min(TM_PROJ, seq_len)
    nst = seq_len // tm
    bf16_widths = (GROUP_W, GROUP_W, GROUP_W, LANE, LANE, GROUP_W, GROUP_W, GROUP_W)
    return pl.pallas_call(
        _proj_kernel,
        grid=(T // tm,),
        in_specs=[pl.BlockSpec((tm, D), lambda i: (i, 0)),
                  _const_spec(wm.shape), _const_spec(wt.shape),
                  pl.BlockSpec((tm, 6 * LANE), lambda i: (i % nst, 0))],
        out_specs=[pl.BlockSpec((tm, w), lambda i: (i, 0)) for w in bf16_widths]
        + [pl.BlockSpec((tm, 9 * LANE), lambda i: (i, 0)),
                   pl.BlockSpec((tm, LANE), lambda i: (i, 0)),
                   pl.BlockSpec((tm, LANE), lambda i: (i, 0)),
                   pl.BlockSpec((wt.shape[0], tm), lambda i: (0, i))],
        out_shape=[jax.ShapeDtypeStruct((T, w), BF16) for w in bf16_widths]
        + [jax.ShapeDtypeStruct((T, 9 * LANE), F32),
                   jax.ShapeDtypeStruct((T, LANE), F32),
                   jax.ShapeDtypeStruct((T, LANE), F32),
                   jax.ShapeDtypeStruct((wt.shape[0], T), BF16)],
        compiler_params=_cp(("parallel",)),
        name="in_proj",
    )(x, wm, wt, tab)


def _prep_w_in(w):
    k = w.shape[0]
    w = w.astype(BF16)
    sizes = ((HEADS * DIFF_QK,) * 4 + (GROUP_W,) + (GROUP_W,) * 4 + (HEADS,) * 2
             + (GROUP_W,) + (HEAD_DIM,) * 6 + (3 * HEADS,) + (GROUP_W,) * 3)
    offs = np.concatenate([[0], np.cumsum(sizes)])
    (dq1, dq2, dk1, dk2, dv, gq, gk, gv, gz, ga, gb,
     nq, nkc, nvc, nks, nvs, nkw, nvw, ngate, sq, sk, sv) = [w[:, offs[i]:offs[i + 1]] for i in range(len(sizes))]
    scale = HEAD_DIM ** -0.5
    pad = lambda a: jnp.concatenate([a, jnp.zeros((k, LANE - a.shape[1]), w.dtype)], axis=1)
    main = jnp.concatenate(
        [dq1, dq2, dk1, dk2, nq * scale, nks, nvs, nkw, nvw, nkc, nvc,
         sq * scale, sk, sv, gq, gk, gv, gz, pad(jnp.concatenate([ga, gb], axis=1)), pad(ngate)], axis=1)
    wt = jnp.concatenate([dv, sv, nvs, nvw], axis=1).T
    return main, wt


def _rope_table(seq_len):
    def cs(dim):
        inv = ROPE_THETA ** (-jnp.arange(0, dim, 2, dtype=F32) / dim)
        ang = jnp.arange(seq_len, dtype=F32)[:, None] * inv[None, :]
        c, sgn = jnp.cos(ang), jnp.sin(ang)
        return jnp.concatenate([c, c], axis=1), jnp.concatenate([-sgn, sgn], axis=1)
    cd, sd = cs(DIFF_QK)
    cn, sn = cs(HEAD_DIM)
    one = jnp.ones((seq_len, HEAD_DIM), F32)
    zero = jnp.zeros((seq_len, HEAD_DIM), F32)
    return jnp.concatenate(
        [jnp.tile(cd, (1, 4)), jnp.tile(sd, (1, 4)), jnp.tile(cn, (1, 2)), jnp.tile(sn, (1, 2)),
         cn, one, sn, zero], axis=1)


def _tri_pairs(n, descending=False):
    qi, kj = [], []
    for i in range(n):
        js = range(i, -1, -1) if descending else range(i + 1)
        for j in js:
            qi.append(i)
            kj.append(j)
    return jnp.asarray(qi, jnp.int32), jnp.asarray(kj, jnp.int32)


ONES_ROWS = 16
SKEW = 3
SKEW_SB = 1


def _diff_kernel(qi_ref, kj_ref, q_ref, k_ref, vt_ref, lam_ref, g_ref, o_ref,
                 m_scr, acc_scr, *, lam_init):
    p = pl.program_id(1)
    i = qi_ref[p]
    j = kj_ref[p]
    tq, tk = q_ref.shape[0], k_ref.shape[0]
    c = (DIFF_QK ** -0.5) * LOG2E
    head32 = _iota((1, LANE), 1) // DIFF_QK

    @pl.when(j == 0)
    def _init():
        m_scr[...] = jnp.full(m_scr.shape, NEG, F32)
        acc_scr[...] = jnp.zeros(acc_scr.shape, F32)

    def step(masked):
        ones = jnp.ones((ONES_ROWS, tk), BF16)
        qs = (q_ref[:, :LANE], q_ref[:, LANE:])
        ks = (k_ref[:, :LANE], k_ref[:, LANE:])
        chains = [(t, h) for t in range(2) for h in range(HEADS)]

        def scores(t, h):
            km = jnp.where(head32 == h, ks[t], jnp.zeros_like(ks[t]))
            st = _dot_nt(km, qs[t])
            if masked:
                st = jnp.where(_iota((tk, tq), 0) <= _iota((tk, tq), 1), st, NEG)
            return st

        pend = [scores(*chains[n]) for n in range(SKEW)]
        for n, (t, h) in enumerate(chains):
            st = pend.pop(0)
            if n + SKEW < len(chains):
                pend.append(scores(*chains[n + SKEW]))
            idx = t * HEADS + h
            m_prev = m_scr[idx:idx + 1, :]
            m_new = jnp.maximum(m_prev, jnp.max(st, axis=0, keepdims=True))
            pt = jnp.exp2((st - m_new) * c).astype(BF16)
            alpha = jnp.exp2((m_prev - m_new) * c)
            m_scr[idx:idx + 1, :] = m_new
            vh = jnp.concatenate([vt_ref[h * HEAD_DIM:(h + 1) * HEAD_DIM, :], ones], axis=0)
            acc_scr[idx] = acc_scr[idx] * alpha + _dot(vh, pt)

    @pl.when(j < i)
    def _off():
        step(False)

    @pl.when(j == i)
    def _diag():
        step(True)
        lp = lam_ref[...]
        lam = (jnp.exp(jnp.sum(lp[0:1] * lp[1:2], axis=-1, keepdims=True))
               - jnp.exp(jnp.sum(lp[2:3] * lp[3:4], axis=-1, keepdims=True)) + lam_init)
        parts = []
        for h in range(HEADS):
            a0, a1 = acc_scr[h], acc_scr[HEADS + h]
            oh = (a0[:HEAD_DIM] / a0[HEAD_DIM:HEAD_DIM + 1] - lam * (a1[:HEAD_DIM] / a1[HEAD_DIM:HEAD_DIM + 1]))
            ms = jnp.sum(oh * oh, axis=0, keepdims=True) * (1.0 / HEAD_DIM)
            parts.append(oh * lax.rsqrt(ms + NORM_EPS))
        y = jnp.concatenate(parts, axis=0).T * g_ref[...] * (1.0 - lam_init)
        o_ref[...] = y.astype(o_ref.dtype)


def _diff_attention(dq, dk, ht, lam_p, subln_g, lam_init, batch, seq_len):
    t = min(T_ATT, seq_len)
    nq = seq_len // t
    qi, kj = _tri_pairs(nq)
    g_full = jnp.tile(subln_g, HEADS).reshape(1, GROUP_W)
    grid_spec = pltpu.PrefetchScalarGridSpec(
        num_scalar_prefetch=2,
        grid=(batch, qi.shape[0]),
        in_specs=[pl.BlockSpec((t, GROUP_W), lambda b, p, qi, kj: (b * nq + qi[p], 0)),
                  pl.BlockSpec((t, GROUP_W), lambda b, p, qi, kj: (b * nq + kj[p], 0)),
                  pl.BlockSpec((GROUP_W, t), lambda b, p, qi, kj: (0, b * nq + kj[p])),
                  pl.BlockSpec((4, DIFF_QK), lambda b, p, qi, kj: (0, 0)),
                  pl.BlockSpec((1, GROUP_W), lambda b, p, qi, kj: (0, 0))],
        out_specs=pl.BlockSpec((t, GROUP_W), lambda b, p, qi, kj: (b * nq + qi[p], 0)),
        scratch_shapes=[pltpu.VMEM((2 * HEADS, t), F32),
                        pltpu.VMEM((2 * HEADS, HEAD_DIM + ONES_ROWS, t), F32)])
    return pl.pallas_call(
        functools.partial(_diff_kernel, lam_init=lam_init),
        grid_spec=grid_spec,
        out_shape=jax.ShapeDtypeStruct((batch * seq_len, GROUP_W), BF16),
        compiler_params=_cp(("parallel", "arbitrary")),
        name="diff_attn",
    )(qi, kj, dq, dk, ht, lam_p, g_full)


SB_CUM = 256


def _sb_kernel(qi_ref, kj_ref, q_ref, k_ref, v_ref, o_ref, carry_scr, acc_scr):
    p = pl.program_id(1)
    i = qi_ref[p]
    j = kj_ref[p]
    tq, tk = q_ref.shape[0], k_ref.shape[0]
    head64 = _iota((1, GROUP_W), 1) // HEAD_DIM
    cw = min(SB_CUM, tk)

    @pl.when(j == i)
    def _init():
        carry_scr[...] = jnp.zeros(carry_scr.shape, F32)
        acc_scr[...] = jnp.zeros(acc_scr.shape, F32)

    def step(masked):
        q, k, v = q_ref[...], k_ref[...], v_ref[...]
        m_incl = jnp.where(_iota((cw, cw), 0) >= _iota((cw, cw), 1), 1.0, 0.0).astype(BF16)
        m2 = jnp.concatenate([m_incl, m_incl], axis=0)
        if masked:
            before = _iota((tq, tk), 1) < _iota((tq, tk), 0)

        def scores(h):
            return _dot_nt(q, jnp.where(head64 == h, k, jnp.zeros_like(k)))

        pend = [scores(h) for h in range(SKEW_SB)]
        pv = None
        for h in range(HEADS):
            zl = pend.pop(0) * LOG2E
            if h + SKEW_SB < HEADS:
                pend.append(scores(h + SKEW_SB))
            sp = jnp.maximum(zl, 0.0) + jnp.log2(1.0 + jnp.exp2(-jnp.abs(zl)))
            if masked:
                sp = jnp.where(before, sp, 0.0)
            hi, lo = _split2(sp)
            blocks = []
            suffix = carry_scr[h]
            for blk in reversed(range(tk // cw)):
                sl = slice(blk * cw, (blk + 1) * cw)
                cb = _dot(jnp.concatenate([hi[:, sl], lo[:, sl]], axis=1), m2) + suffix
                blocks.insert(0, cb)
                suffix = cb[:, 0:1]
            carry_scr[h] = suffix
            att = jnp.exp2(zl - jnp.concatenate(blocks, axis=1))
            if masked:
                att = jnp.where(before, att, 0.0)
            vm = jnp.where(head64 == h, v, jnp.zeros_like(v))
            d = _dot(att.astype(BF16), vm)
            pv = d if pv is None else pv + d
        acc_scr[...] = acc_scr[...] + pv

    @pl.when(j == i)
    def _diag():
        step(True)

    @pl.when(j < i)
    def _off():
        step(False)

    @pl.when(j == 0)
    def _fin():
        o_ref[...] = acc_scr[...].astype(o_ref.dtype)


def _sb_attention(sq, sk, sv, batch, seq_len):
    t = min(T_ATT, seq_len)
    nq = seq_len // t
    qi, kj = _tri_pairs(nq, descending=True)
    grid_spec = pltpu.PrefetchScalarGridSpec(
        num_scalar_prefetch=2,
        grid=(batch, qi.shape[0]),
        in_specs=[pl.BlockSpec((t, GROUP_W), lambda b, p, qi, kj: (b * nq + qi[p], 0)),
                  pl.BlockSpec((t, GROUP_W), lambda b, p, qi, kj: (b * nq + kj[p], 0)),
                  pl.BlockSpec((t, GROUP_W), lambda b, p, qi, kj: (b * nq + kj[p], 0))],
        out_specs=pl.BlockSpec((t, GROUP_W), lambda b, p, qi, kj: (b * nq + qi[p], 0)),
        scratch_shapes=[pltpu.VMEM((HEADS, t, 1), F32), pltpu.VMEM((t, GROUP_W), F32)])
    return pl.pallas_call(
        _sb_kernel,
        grid_spec=grid_spec,
        out_shape=jax.ShapeDtypeStruct((batch * seq_len, GROUP_W), BF16),
        compiler_params=_cp(("parallel", "arbitrary")),
        name="sb_attn",
    )(qi, kj, sq, sk, sv)


def _stack_heads(q):
    qf = q.astype(F32)
    lo = _iota((1, LANE), 1) < HEAD_DIM
    parts = []
    for blk in (qf[:, :LANE], qf[:, LANE:]):
        parts.append(jnp.where(lo, blk, 0.0))
        parts.append(jnp.where(lo, pltpu.roll(blk, HEAD_DIM, 1), 0.0))
    return jnp.concatenate(parts, axis=0).astype(BF16)


def _nsa_compress_kernel(r_ref, pe_ref, w1lo_ref, w1hi_ref, w2_ref, w2vt_ref, o_ref, vt_ref):
    r = r_ref[...]
    n = r.shape[0]
    y1 = _dot((r + pe_ref[0:1, :]).astype(BF16), w1lo_ref[...])
    y2 = _dot((r + pe_ref[1:2, :]).astype(BF16), w1hi_ref[...])
    hid = y1 + pltpu.roll(y2, n - 1, 0)
    act = (hid * jax.nn.sigmoid(hid)).astype(BF16)
    o_ref[...] = _dot(act, w2_ref[...]).astype(o_ref.dtype)
    vt_ref[...] = _dot_nt(w2vt_ref[...], act).astype(vt_ref.dtype)


def _nsa_compress(kcvc, pe_k, pe_v, ck_w1, ck_w2, cv_w1, cv_w2, batch, seq_len):
    nrow = seq_len // CMP_STRIDE
    width = CMP_STRIDE * LANE
    r = kcvc.reshape(batch * nrow, width)
    hid = ck_w1.shape[1]
    pe = jnp.concatenate([pe_k, pe_v], axis=1).reshape(2, width)
    zk = jnp.zeros((CMP_BLOCK, HEAD_DIM, hid), BF16)
    w1 = jnp.concatenate(
        [jnp.concatenate([ck_w1.astype(BF16).reshape(CMP_BLOCK, HEAD_DIM, hid), zk], axis=2),
         jnp.concatenate([zk, cv_w1.astype(BF16).reshape(CMP_BLOCK, HEAD_DIM, hid)], axis=2)], axis=1)
    w1 = w1.reshape(2, width, 2 * hid)
    zo = jnp.zeros((hid, HEAD_DIM), BF16)
    w2 = jnp.concatenate([jnp.concatenate([ck_w2.astype(BF16), zo], axis=1),
                          jnp.concatenate([zo, cv_w2.astype(BF16)], axis=1)], axis=0)
    return pl.pallas_call(
        _nsa_compress_kernel,
        grid=(batch,),
        in_specs=[pl.BlockSpec((nrow, width), lambda b: (b, 0)),
                  _const_spec((2, width)), _const_spec((width, 2 * hid)), _const_spec((width, 2 * hid)),
                  _const_spec((2 * hid, LANE)), _const_spec((HEAD_DIM, 2 * hid))],
        out_specs=[pl.BlockSpec((nrow, LANE), lambda b: (b, 0)),
                   pl.BlockSpec((HEAD_DIM, nrow), lambda b: (b, 0))],
        out_shape=[jax.ShapeDtypeStruct((batch * nrow, LANE), BF16),
                   jax.ShapeDtypeStruct((batch * HEAD_DIM, nrow), BF16)],
        compiler_params=_cp(("parallel",)),
        name="nsa_compress",
    )(r, pe, w1[0], w1[1], w2, w2[:, HEAD_DIM:].T)


def _nsa_cmp_kernel(q_ref, kv_ref, vt_ref, ovt_ref, ocmp_ref, bias_ref):
    i = pl.program_id(1)
    tq = q_ref.shape[0]
    ncmp = kv_ref.shape[0]
    nslc = ovt_ref.shape[0]
    qs = _stack_heads(q_ref[...])
    kv = kv_ref[...]
    vt = vt_ref[...]
    tpos = i * tq + _iota((1, tq), 1)
    cm = _iota((ncmp, 1), 0) * CMP_STRIDE + (CMP_BLOCK - 1) <= tpos
    sts = [_dot_nt(kv, qs[h * tq:(h + 1) * tq]) for h in range(HEADS)]
    psum = None
    parts = []
    for h in range(HEADS):
        st = jnp.where(cm, sts[h], NEG)
        e = jnp.exp(st - jnp.max(st, axis=0, keepdims=True))
        pr = jnp.where(cm, e * (1.0 / jnp.sum(e, axis=0, keepdims=True)), 0.0)
        parts.append(_dot(vt, pr.astype(BF16)))
        psum = pr if psum is None else psum + pr
    ocmp_ref[...] = jnp.concatenate(parts, axis=0).T
    hi, lo = _split2(psum)
    imp = _dot(ovt_ref[...], hi) + _dot(ovt_ref[...], lo)
    blk = _iota((nslc, 1), 0)
    cur = tpos // SLC_BLOCK
    work = jnp.where(blk == 0, FORCE, jnp.where(blk == cur, FORCE, jnp.where(blk == cur - 1, FORCE, imp)))
    work = jnp.where(blk <= cur, work, -FORCE)
    sel = jnp.zeros((nslc, tq), F32)
    for _ in range(min(SLC_TOPN, nslc)):
        mx = jnp.max(work, axis=0, keepdims=True)
        first = jnp.min(jnp.where(work == mx, blk, nslc), axis=0, keepdims=True)
        hit = blk == first
        sel = jnp.where(hit, 1.0, sel)
        work = jnp.where(hit, -jnp.inf, work)
    bias_ref[...] = jnp.where(sel > 0.5, 0.0, NEG)


def _nsa_cmp(nq_arr, kvcmp, vtcmp, batch, seq_len):
    tq = min(TQ_CMP, seq_len)
    nq = seq_len // tq
    ncmp = seq_len // CMP_STRIDE
    nslc = seq_len // SLC_BLOCK
    cstart = np.arange(ncmp)[None, :] * CMP_STRIDE
    sstart = np.arange(nslc)[:, None] * SLC_BLOCK
    ovt = (cstart < sstart + SLC_BLOCK) & (cstart + CMP_BLOCK - 1 >= sstart)
    ovt &= (np.arange(ncmp)[None, :] < (seq_len - CMP_BLOCK) // CMP_STRIDE + 1)
    ovt = jnp.asarray(ovt, BF16)
    return pl.pallas_call(
        _nsa_cmp_kernel,
        grid=(batch, nq),
        in_specs=[pl.BlockSpec((tq, GROUP_W), lambda b, i: (b * nq + i, 0)),
                  pl.BlockSpec((ncmp, LANE), lambda b, i: (b, 0)),
                  pl.BlockSpec((HEAD_DIM, ncmp), lambda b, i: (b, 0)),
                  pl.BlockSpec((nslc, ncmp), lambda b, i: (0, 0))],
        out_specs=[pl.BlockSpec((tq, GROUP_W), lambda b, i: (b * nq + i, 0)),
                   pl.BlockSpec((nslc, tq), lambda b, i: (0, b * nq + i))],
        out_shape=[jax.ShapeDtypeStruct((batch * seq_len, GROUP_W), F32),
                   jax.ShapeDtypeStruct((nslc, batch * seq_len), F32)],
        compiler_params=_cp(("parallel", "parallel")),
        name="nsa_cmp_select",
    )(nq_arr, kvcmp, vtcmp, ovt)


def _nsa_win_kernel(q_ref, k_ref, vt_ref, o_ref, *, window):
    i = pl.program_id(1)
    tq = q_ref.shape[0]
    span = tq + window
    base = pl.multiple_of(jnp.maximum(i * tq - window, 0), LANE)
    kv = k_ref[pl.ds(base, span), :]
    vh = jnp.concatenate([vt_ref[:, pl.ds(base, span)], jnp.ones((ONES_ROWS, span), BF16)], axis=0)
    qs = _stack_heads(q_ref[...])
    rel = (i * tq + _iota((1, tq), 1)) - (base + _iota((span, 1), 0))
    bias = jnp.where(rel >= 0, jnp.where(rel < window, 0.0, NEG), NEG)
    sts = [_dot_nt(kv, qs[h * tq:(h + 1) * tq]) for h in range(HEADS)]
    parts = []
    for h in range(HEADS):
        st = sts[h] + bias
        e = jnp.exp(st - jnp.max(st, axis=0, keepdims=True)).astype(BF16)
        r = _dot(vh, e)
        parts.append(r[:HEAD_DIM] / r[HEAD_DIM:HEAD_DIM + 1])
    o_ref[...] = jnp.concatenate(parts, axis=0).T


def _nsa_window(nq_arr, kw, ht, batch, seq_len):
    tq = min(TQ_WIN, seq_len)
    nq = seq_len // tq
    window = min(WINDOW, seq_len - tq)
    vw_row_blk = 2 * GROUP_W // HEAD_DIM + 1
    return pl.pallas_call(
        functools.partial(_nsa_win_kernel, window=window),
        grid=(batch, nq),
        in_specs=[pl.BlockSpec((tq, GROUP_W), lambda b, i: (b * nq + i, 0)),
                  pl.BlockSpec((seq_len, LANE), lambda b, i: (b, 0)),
                  pl.BlockSpec((HEAD_DIM, seq_len), lambda b, i: (vw_row_blk, b))],
        out_specs=pl.BlockSpec((tq, GROUP_W), lambda b, i: (b * nq + i, 0)),
        out_shape=jax.ShapeDtypeStruct((batch * seq_len, GROUP_W), F32),
        compiler_params=_cp(("parallel", "parallel")),
        name="nsa_window",
    )(nq_arr, kw, ht)


def _nsa_sel_kernel(qi_ref, kj_ref, q_ref, k_ref, vt_ref, bias_ref, ocmp_ref, owin_ref, gate_ref, o_ref,
                    qs_scr, m_scr, acc_scr):
    p = pl.program_id(1)
    i = qi_ref[p]
    j = kj_ref[p]
    tq, tk = q_ref.shape[0], k_ref.shape[0]
    nblk = tk // SLC_BLOCK

    @pl.when(j == 0)
    def _init():
        qs = _stack_heads(q_ref[...])
        for h in range(HEADS):
            qs_scr[h] = qs[h * tq:(h + 1) * tq]
        m_scr[...] = jnp.full(m_scr.shape, NEG, F32)
        acc_scr[...] = jnp.zeros(acc_scr.shape, F32)

    def step(masked):
        kv = k_ref[...]
        bias = bias_ref[...][:, None, :]
        vh = jnp.concatenate([vt_ref[...], jnp.ones((ONES_ROWS, tk), BF16)], axis=0)

        def scores(h):
            st = _dot_nt(kv, qs_scr[h])
            st = (st.reshape(nblk, SLC_BLOCK, tq) + bias).reshape(tk, tq)
            if masked:
                st = jnp.where(j * tk + _iota((tk, tq), 0) <= i * tq + _iota((tk, tq), 1), st, NEG)
            return st

        pend = [scores(h) for h in range(SKEW)]
        for h in range(HEADS):
            st = pend.pop(0)
            if h + SKEW < HEADS:
                pend.append(scores(h + SKEW))
            m_prev = m_scr[h:h + 1, :]
            m_new = jnp.maximum(m_prev, jnp.max(st, axis=0, keepdims=True))
            pt = jnp.exp(st - m_new).astype(BF16)
            alpha = jnp.exp(m_prev - m_new)
            m_scr[h:h + 1, :] = m_new
            acc_scr[h] = acc_scr[h] * alpha + _dot(vh, pt)

    last = (i * tq + tq - 1) // tk

    @pl.when((j + 1) * tk <= i * tq)
    def _past():
        step(False)

    @pl.when((j + 1) * tk > i * tq)
    def _diag():
        step(True)

    @pl.when(j == last)
    def _fin():
        parts = []
        for h in range(HEADS):
            a = acc_scr[h]
            parts.append(a[:HEAD_DIM] / a[HEAD_DIM:HEAD_DIM + 1])
        osel = jnp.concatenate(parts, axis=0).T
        sig = jax.nn.sigmoid(gate_ref[...])
        grow = _iota((LANE, 1), 0)
        ghead = _iota((1, GROUP_W), 1) // HEAD_DIM
        out = None
        for br, o_br in enumerate((ocmp_ref[...], osel, owin_ref[...])):
            e_br = jnp.where(grow == 3 * ghead + br, 1.0, 0.0).astype(BF16)
            term = _dot_x3(sig, e_br) * o_br
            out = term if out is None else out + term
        o_ref[...] = out.astype(o_ref.dtype)


def _nsa_select(nq_arr, ks, ht, sel, ocmp, owin, gates, batch, seq_len):
    tq = min(TQ_SEL, seq_len)
    tk = min(TK_SEL, seq_len)
    nq, nk = seq_len // tq, seq_len // tk
    qi, kj = [], []
    for i in range(nq):
        for j in range((i * tq + tq - 1) // tk + 1):
            qi.append(i)
            kj.append(j)
    qi, kj = jnp.asarray(qi, jnp.int32), jnp.asarray(kj, jnp.int32)
    qrow = lambda b, p, qi, kj: (b * nq + qi[p], 0)
    vs_row_blk = 2 * GROUP_W // HEAD_DIM
    grid_spec = pltpu.PrefetchScalarGridSpec(
        num_scalar_prefetch=2,
        grid=(batch, qi.shape[0]),
        in_specs=[pl.BlockSpec((tq, GROUP_W), qrow),
                  pl.BlockSpec((tk, LANE), lambda b, p, qi, kj: (b * nk + kj[p], 0)),
                  pl.BlockSpec((HEAD_DIM, tk), lambda b, p, qi, kj: (vs_row_blk, b * nk + kj[p])),
                  pl.BlockSpec((tk // SLC_BLOCK, tq), lambda b, p, qi, kj: (kj[p], b * nq + qi[p])),
                  pl.BlockSpec((tq, GROUP_W), qrow), pl.BlockSpec((tq, GROUP_W), qrow),
                  pl.BlockSpec((tq, LANE), qrow)],
        out_specs=pl.BlockSpec((tq, GROUP_W), qrow),
        scratch_shapes=[pltpu.VMEM((HEADS, tq, LANE), BF16), pltpu.VMEM((HEADS, tq), F32),
                        pltpu.VMEM((HEADS, HEAD_DIM + ONES_ROWS, tq), F32)])
    return pl.pallas_call(
        _nsa_sel_kernel,
        grid_spec=grid_spec,
        out_shape=jax.ShapeDtypeStruct((batch * seq_len, GROUP_W), BF16),
        compiler_params=_cp(("parallel", "arbitrary")),
        name="nsa_select_gate",
    )(qi, kj, nq_arr, ks, ht, sel, ocmp, owin, gates)


def _nsa(nq_arr, ks, kw, ht, kcvc, gates, pe_k, pe_v, ck_w1, ck_w2, cv_w1, cv_w2, batch, seq_len):
    kvcmp, vtcmp = _nsa_compress(kcvc, pe_k, pe_v, ck_w1, ck_w2, cv_w1, cv_w2, batch, seq_len)
    ocmp, sel = _nsa_cmp(nq_arr, kvcmp, vtcmp, batch, seq_len)
    owin = _nsa_window(nq_arr, kw, ht, batch, seq_len)
    return _nsa_select(nq_arr, ks, ht, sel, ocmp, owin, gates, batch, seq_len)


def _bd(mc, mask_bd):
    return jnp.where(mask_bd, jnp.concatenate([mc] * HEADS, axis=0), jnp.zeros((), mc.dtype))


def _mm_bd(x, mc, mask_bd):
    return _dot(x.astype(BF16), _bd(mc.astype(BF16), mask_bd))


def _gdn_prep_kernel(x_ref, ab_ref, cw_ref, alog_ref, dtb_ref,
                     u_ref, w_ref, qd_ref, in_ref, kdt_ref, gl_ref, xpad_scr):
    n = pl.program_id(0)
    C = GDN_CHUNK
    W = GROUP_W
    nb, rows = x_ref.shape[0], x_ref.shape[1]
    inst = [(b, c) for b in range(nb) for c in range(rows // C)]

    @pl.when(n == 0)
    def _():
        xpad_scr[:, 0:8, :] = jnp.zeros((nb, 8, xpad_scr.shape[2]), F32)

    cw = cw_ref[...]
    qkv_b = []
    for b in range(nb):
        x = x_ref[b]
        xpad_scr[b, 8:8 + rows, :] = x
        conv = (cw[0:1] * xpad_scr[b, 5:5 + rows, :] + cw[1:2] * xpad_scr[b, 6:6 + rows, :]
                + cw[2:3] * xpad_scr[b, 7:7 + rows, :] + cw[3:4] * x)
        xpad_scr[b, 0:8, :] = x[rows - 8:rows, :]
        qkv_b.append(conv * jax.nn.sigmoid(conv))

    r256 = _iota((W, W), 0)
    c256 = _iota((W, W), 1)
    mask_bd = (r256 // HEAD_DIM) == (c256 // HEAD_DIM)
    ones_bd = jnp.where(mask_bd, 1.0, 0.0).astype(BF16)
    eye256 = jnp.where(r256 == c256, 1.0, 0.0).astype(BF16)
    row = _iota((C, W), 0)
    jl = _iota((C, W), 1) % HEAD_DIM
    ltri = jnp.where(_iota((C, C), 1) <= _iota((C, C), 0), 1.0, 0.0).astype(BF16)
    erow = _iota((LANE, W), 0)
    ehead = _iota((LANE, W), 1) // HEAD_DIM
    e_g = jnp.where(erow == ehead, 1.0, 0.0).astype(BF16)
    e_b = jnp.where(erow == ehead + HEADS, 1.0, 0.0).astype(BF16)

    def each(f, *lists):
        return [f(*args) for args in zip(*lists)]

    sl = [slice(c * C, (c + 1) * C) for _, c in inst]
    q = [qkv_b[b][sl[k], 0:W] for k, (b, _) in enumerate(inst)]
    kk = [qkv_b[b][sl[k], W:2 * W] for k, (b, _) in enumerate(inst)]
    v = [qkv_b[b][sl[k], 2 * W:3 * W] for k, (b, _) in enumerate(inst)]
    ab = [ab_ref[b, sl[k], :] for k, (b, _) in enumerate(inst)]

    qn = each(lambda t: t * lax.rsqrt(_dot_x2(t * t, ones_bd) + NORM_EPS) * (HEAD_DIM ** -0.5), q)
    kn = each(lambda t: t * lax.rsqrt(_dot_x2(t * t, ones_bd) + NORM_EPS), kk)

    def gate(a):
        z = a + dtb_ref[...]
        return -jnp.exp(alog_ref[...]) * (jnp.maximum(z, 0.0) + jnp.log1p(jnp.exp(-jnp.abs(z))))

    g_hl = each(lambda a: _dot_x3(gate(a), e_g), ab)
    beta = each(lambda a: _dot_x3(jax.nn.sigmoid(a), e_b), ab)
    gc = each(lambda g: _dot_x3_left(ltri, g), g_hl)
    glast = each(lambda g: g[C - 1:C, :], gc)
    exp_g = each(jnp.exp, gc)
    dmat = each(lambda g: _dot_x3_left(ltri, jnp.where(row > jl, g, 0.0)), g_hl)
    decay = each(lambda d: jnp.where(jl <= row, jnp.exp(d), 0.0), dmat)

    kt4 = each(lambda t: _dot_nt(eye256, jnp.concatenate([t.astype(BF16)] * HEADS, axis=0)), kn)
    kb_mat = each(lambda t: jnp.where(mask_bd, t, 0.0).astype(BF16), kt4)
    kbeta = each(lambda t, bb: t * bb, kn, beta)
    a_c = each(lambda t, m, d: jnp.where(jl < row, _dot(t.astype(BF16), m) * d, 0.0), kbeta, kb_mat, decay)
    intra = each(lambda t, m, d: _dot(t.astype(BF16), m) * d, qn, kb_mat, decay)

    t_c = each(lambda a: jnp.where(jl == row, 1.0, 0.0) - a, a_c)
    p_c = a_c
    for _ in range(5):
        p_c = each(lambda pc: _mm_bd(pc, pc, mask_bd), p_c)
        t_c = each(lambda tc, pc: tc + _mm_bd(tc, pc, mask_bd), t_c, p_c)

    u = each(lambda tc, t, bb: _mm_bd(tc, t * bb, mask_bd), t_c, v, beta)
    w = each(lambda tc, t, e: _mm_bd(tc, t * e, mask_bd), t_c, kbeta, exp_g)
    kdt = each(lambda t, gl, g: _dot_nt(eye256, (t * jnp.exp(gl - g)).astype(BF16)), kn, glast, gc)

    for k, (b, c) in enumerate(inst):
        u_ref[b, sl[k], :] = u[k]
        w_ref[b, sl[k], :] = w[k].astype(w_ref.dtype)
        qd_ref[b, sl[k], :] = (qn[k] * exp_g[k]).astype(qd_ref.dtype)
        in_ref[b, sl[k], :] = intra[k].astype(in_ref.dtype)
        kdt_ref[b, c] = kdt[k].astype(kdt_ref.dtype)
        gl_ref[b, c] = jnp.exp(glast[k])


def _dot_x3_left(w, x):
    hi = x.astype(BF16)
    r = x - hi.astype(F32)
    mid = r.astype(BF16)
    lo = (r - mid.astype(F32)).astype(BF16)
    return _dot(w, hi) + _dot(w, mid) + _dot(w, lo)


def _gdn_scan_kernel(u_ref, w_ref, qd_ref, in_ref, kdt_ref, gl_ref, z_ref, g_ref, o_ref, s_scr):
    n = pl.program_id(0)
    C = GDN_CHUNK
    W = GROUP_W
    nb, rows = u_ref.shape[0], u_ref.shape[1]

    @pl.when(n == 0)
    def _():
        s_scr[...] = jnp.zeros(s_scr.shape, F32)

    mask_bd = (_iota((W, W), 0) // HEAD_DIM) == (_iota((W, W), 1) // HEAD_DIM)
    ones_bd = jnp.where(mask_bd, 1.0, 0.0).astype(BF16)
    s = [s_scr[b] for b in range(nb)]
    for c in range(rows // C):
        sl = slice(c * C, (c + 1) * C)
        sb = [t.astype(BF16) for t in s]
        v_new = [u_ref[b, sl, :] - _dot(w_ref[b, sl, :], sb[b]) for b in range(nb)]
        vb = [t.astype(BF16) for t in v_new]
        s = [s[b] * gl_ref[b, c] + jnp.where(mask_bd, _dot(kdt_ref[b, c], vb[b]), 0.0) for b in range(nb)]
        o = [_dot(qd_ref[b, sl, :], sb[b]) + _dot(in_ref[b, sl, :], _bd(vb[b], mask_bd)) for b in range(nb)]
        for b in range(nb):
            ms = _dot_x2(o[b] * o[b], ones_bd) * (1.0 / HEAD_DIM)
            zz = z_ref[b, sl, :]
            y = o[b] * lax.rsqrt(ms + NORM_EPS) * g_ref[...] * (zz * jax.nn.sigmoid(zz))
            o_ref[b, sl, :] = y.astype(o_ref.dtype)
    for b in range(nb):
        s_scr[b] = s[b]


GDN_PREP_CHUNKS = 4
GDN_SCAN_CHUNKS = 4


def _gdn(hg, conv_w, a_log, dt_bias, norm_g, batch, seq_len):
    C = GDN_CHUNK
    nc = seq_len // C
    W = GROUP_W
    padl = lambda a: jnp.concatenate([a, jnp.zeros((LANE - a.shape[0],), F32)]).reshape(1, LANE)
    hg3 = hg.reshape(batch, seq_len, hg.shape[-1])
    cp = math.gcd(GDN_PREP_CHUNKS, nc)
    rp = cp * C
    blkp = pl.BlockSpec((batch, rp, W), lambda n: (0, n, 0))
    u, w, qd, intra, kdt, gl = pl.pallas_call(
        _gdn_prep_kernel,
        grid=(nc // cp,),
        in_specs=[pl.BlockSpec((batch, rp, 3 * W), lambda n: (0, n, 0)),
                  pl.BlockSpec((batch, rp, LANE), lambda n: (0, n, 8)),
                  pl.BlockSpec((4, 3 * W), lambda n: (0, 0)),
                  pl.BlockSpec((1, LANE), lambda n: (0, 0)),
                  pl.BlockSpec((1, LANE), lambda n: (0, 0))],
        out_specs=[blkp] * 4
        + [pl.BlockSpec((batch, cp, W, C), lambda n: (0, n, 0, 0)),
           pl.BlockSpec((batch, cp, 1, W), lambda n: (0, n, 0, 0))],
        out_shape=[jax.ShapeDtypeStruct((batch, seq_len, W), F32), jax.ShapeDtypeStruct((batch, seq_len, W), BF16),
                   jax.ShapeDtypeStruct((batch, seq_len, W), BF16), jax.ShapeDtypeStruct((batch, seq_len, W), BF16),
                   jax.ShapeDtypeStruct((batch, nc, W, C), BF16),
                   jax.ShapeDtypeStruct((batch, nc, 1, W), F32)],
        scratch_shapes=[pltpu.VMEM((batch, 8 + rp, 3 * W), F32)],
        compiler_params=_cp(("arbitrary",)),
        name="gdn_prep",
    )(hg3, hg3, conv_w, padl(a_log), padl(dt_bias))

    cs = math.gcd(GDN_SCAN_CHUNKS, nc)
    blk = pl.BlockSpec((batch, cs * C, W), lambda n: (0, n, 0))
    out = pl.pallas_call(
        _gdn_scan_kernel,
        grid=(nc // cs,),
        in_specs=[blk, blk, blk, blk,
                  pl.BlockSpec((batch, cs, W, C), lambda n: (0, n, 0, 0)),
                  pl.BlockSpec((batch, cs, 1, W), lambda n: (0, n, 0, 0)),
                  pl.BlockSpec((batch, cs * C, W), lambda n: (0, n, 3)),
                  pl.BlockSpec((1, W), lambda n: (0, 0))],
        out_specs=blk,
        out_shape=jax.ShapeDtypeStruct((batch, seq_len, W), BF16),
        scratch_shapes=[pltpu.VMEM((batch, W, W), F32)],
        compiler_params=_cp(("arbitrary",)),
        name="gdn_scan",
    )(u, w, qd, intra, kdt, gl, hg3, jnp.tile(norm_g, HEADS).reshape(1, W))
    return out.reshape(batch * seq_len, W)


def kernel(x, w_in, w_out, ffn1_w_gu, ffn1_w_down, ffn2_w_gu, ffn2_w_down, ln1_g, ln1_b, ln2_g, ln2_b, ln3_g, ln3_b, diff_lam_q1, diff_lam_k1, diff_lam_q2, diff_lam_k2, diff_subln_g, gdn_conv_w, gdn_a_log, gdn_dt_bias, gdn_norm_g, nsa_pe_k, nsa_pe_v, nsa_cmp_k_w1, nsa_cmp_k_w2, nsa_cmp_v_w1, nsa_cmp_v_w2):
    B, S, D = x.shape
    depth = w_in.shape[0]
    alpha = (2 * depth) ** 0.25
    tab = _rope_table(S)
    xf = x.reshape(B * S, D)
    for l in range(depth):
        lam_init = 0.8 - 0.6 * math.exp(-0.3 * l)
        xf = _ffn_ln(xf, ffn1_w_gu[l].astype(BF16), ffn1_w_down[l].astype(BF16), ln1_g[l], ln1_b[l], alpha)
        wm, wt = _prep_w_in(w_in[l])
        dq, dk, nq_arr, ks, kw, sq, sk, sv, hg, kcvc, gates, ht = _proj(xf, wm, wt, tab, S)
        lam_p = jnp.stack([diff_lam_q1[l], diff_lam_k1[l], diff_lam_q2[l], diff_lam_k2[l]])
        o_diff = _diff_attention(dq, dk, ht, lam_p, diff_subln_g[l], lam_init, B, S)
        o_gdn = _gdn(hg, gdn_conv_w[l], gdn_a_log[l], gdn_dt_bias[l], gdn_norm_g[l], B, S)
        o_nsa = _nsa(nq_arr, ks, kw, ht, kcvc, gates, nsa_pe_k[l], nsa_pe_v[l], nsa_cmp_k_w1[l], nsa_cmp_k_w2[l],
                     nsa_cmp_v_w1[l], nsa_cmp_v_w2[l], B, S)
        o_sb = _sb_attention(sq, sk, sv, B, S)
        xf = _outproj_ln(xf, (o_diff, o_gdn, o_nsa, o_sb), w_out[l].astype(BF16), ln2_g[l], ln2_b[l], alpha)
        xf = _ffn_ln(xf, ffn2_w_gu[l].astype(BF16), ffn2_w_down[l].astype(BF16), ln3_g[l], ln3_b[l], alpha)
    return xf.reshape(B, S, D)
```

```python
import functools
import math

import numpy as np
import jax
import jax.numpy as jnp
from jax import lax
from jax.experimental import pallas as pl
from jax.experimental.pallas import tpu as pltpu

F32 = jnp.float32
BF16 = jnp.bfloat16

DEPTH = 2
HEAD_DIM = 64
HEADS = 4
GROUP_W = HEADS * HEAD_DIM
DIFF_QK = HEAD_DIM // 2
GDN_CHUNK = 64
CMP_BLOCK, CMP_STRIDE = 32, 16
SLC_BLOCK, SLC_TOPN = 64, 16
WINDOW = 512
FORCE = 1e4
ROPE_THETA = 10000.0
LN_EPS = 1e-5
NORM_EPS = 1e-6
NEG = -1e30
LOG2E = 1.4426950408889634

LANE = 128
V7X_VMEM_BYTES = 64 * 1024 * 1024
VMEM_LIMIT = V7X_VMEM_BYTES - 8 * 1024 * 1024

TM_FFN = 512
FF_CHUNK = 256
TM_PROJ = 512
T_ATT = 1024
TQ_CMP = 256
TQ_WIN = 512
TQ_SEL, TK_SEL = 1024, 1024


def _cp(sem):
    return pltpu.CompilerParams(dimension_semantics=sem, vmem_limit_bytes=VMEM_LIMIT)


def _iota(shape, dim):
    return lax.broadcasted_iota(jnp.int32, shape, dim)


def _dot(a, b):
    return jnp.dot(a, b, preferred_element_type=F32)


def _dot_nt(a, b):
    return lax.dot_general(a, b, (((1,), (1,)), ((), ())), preferred_element_type=F32)


def _split2(x):
    hi = x.astype(BF16)
    lo = (x - hi.astype(F32)).astype(BF16)
    return hi, lo


def _dot_x2(x, w):
    hi, lo = _split2(x)
    return _dot(hi, w) + _dot(lo, w)


def _dot_x3(x, w):
    hi = x.astype(BF16)
    r = x - hi.astype(F32)
    mid = r.astype(BF16)
    lo = (r - mid.astype(F32)).astype(BF16)
    return _dot(hi, w) + _dot(mid, w) + _dot(lo, w)


def _layer_norm(y, g, b):
    mu = jnp.mean(y, axis=-1, keepdims=True)
    d = y - mu
    var = jnp.mean(d * d, axis=-1, keepdims=True)
    return d * lax.rsqrt(var + LN_EPS) * g + b


def _const_spec(shape):
    nd = len(shape)
    return pl.BlockSpec(shape, lambda *_: (0,) * nd, pipeline_mode=pl.Buffered(1))


def _ffn_ln_kernel(x_ref, wgu_ref, wd_ref, g_ref, b_ref, o_ref, *, alpha, d_ff, ff_chunk):
    x = x_ref[...]
    xb = x.astype(BF16)
    acc = None
    for c in range(d_ff // ff_chunk):
        lo = c * ff_chunk
        g = _dot(xb, wgu_ref[:, lo:lo + ff_chunk])
        u = _dot(xb, wgu_ref[:, d_ff + lo:d_ff + lo + ff_chunk])
        a = (g * jax.nn.sigmoid(g) * u).astype(BF16)
        part = _dot(a, wd_ref[lo:lo + ff_chunk, :])
        acc = part if acc is None else acc + part
    o_ref[...] = _layer_norm(alpha * x + 0.5 * acc, g_ref[...], b_ref[...])


def _ffn_ln(x, w_gu, w_down, g, b, alpha):
    T, D = x.shape
    d_ff = w_down.shape[0]
    tm = min(TM_FFN, T)
    ff_chunk = FF_CHUNK if d_ff % FF_CHUNK == 0 else d_ff
    return pl.pallas_call(
        functools.partial(_ffn_ln_kernel, alpha=alpha, d_ff=d_ff, ff_chunk=ff_chunk),
        grid=(T // tm,),
        in_specs=[pl.BlockSpec((tm, D), lambda i: (i, 0)),
                  _const_spec((D, 2 * d_ff)), _const_spec((d_ff, D)),
                  _const_spec((1, D)), _const_spec((1, D))],
        out_specs=pl.BlockSpec((tm, D), lambda i: (i, 0)),
        out_shape=jax.ShapeDtypeStruct((T, D), F32),
        compiler_params=_cp(("parallel",)),
        name="ffn_ln",
    )(x, w_gu, w_down, g.reshape(1, D), b.reshape(1, D))


def _outproj_ln_kernel(x_ref, o0_ref, o1_ref, o2_ref, o3_ref, w_ref, g_ref, b_ref, out_ref, *, alpha):
    gw = o0_ref.shape[1]
    mix = None
    for k, o_ref in enumerate((o0_ref, o1_ref, o2_ref, o3_ref)):
        part = _dot(o_ref[...], w_ref[k * gw:(k + 1) * gw, :])
        mix = part if mix is None else mix + part
    out_ref[...] = _layer_norm(alpha * x_ref[...] + mix, g_ref[...], b_ref[...])


def _outproj_ln(x, outs, w_out, g, b, alpha):
    T, D = x.shape
    tm = min(TM_FFN, T)
    gw = outs[0].shape[1]
    return pl.pallas_call(
        functools.partial(_outproj_ln_kernel, alpha=alpha),
        grid=(T // tm,),
        in_specs=[pl.BlockSpec((tm, D), lambda i: (i, 0))]
        + [pl.BlockSpec((tm, gw), lambda i: (i, 0))] * 4
        + [_const_spec(w_out.shape), _const_spec((1, D)), _const_spec((1, D))],
        out_specs=pl.BlockSpec((tm, D), lambda i: (i, 0)),
        out_shape=jax.ShapeDtypeStruct((T, D), F32),
        compiler_params=_cp(("parallel",)),
        name="outproj_ln",
    )(x, *outs, w_out, g.reshape(1, D), b.reshape(1, D))


N_ROPE_BLK = 9
N_MAIN_BLK = 25


def _proj_kernel(x_ref, wm_ref, wt_ref, tab_ref,
                 dq_ref, dk_ref, nq_ref, ks_ref, kw_ref, sq_ref, sk_ref, sv_ref, hg_ref, kc_ref, gt_ref, ht_ref):
    xb = x_ref[...].astype(BF16)
    ht_ref[...] = _dot_nt(wt_ref[...], xb).astype(BF16)
    nr = N_ROPE_BLK * LANE
    h = _dot(xb, wm_ref[:, :nr])
    tab = tab_ref[...]
    lane = _iota((1, LANE), 1)
    for c in range(N_ROPE_BLK):
        t0 = 0 if c < 4 else (2 if c < 6 else 4)
        half = (DIFF_QK if c < 4 else HEAD_DIM) // 2
        cs = tab[:, t0 * LANE:(t0 + 1) * LANE]
        sn = tab[:, (t0 + 1) * LANE:(t0 + 2) * LANE]
        t = h[:, c * LANE:(c + 1) * LANE]
        rot = jnp.where(lane % (2 * half) < half, pltpu.roll(t, LANE - half, 1), pltpu.roll(t, half, 1))
        val = t * cs + rot * sn
        if c < 6:
            (dq_ref, dk_ref, nq_ref)[c // 2][:, (c % 2) * LANE:(c % 2 + 1) * LANE] = val.astype(BF16)
        elif c < 8:
            (ks_ref, kw_ref)[c - 6][...] = val.astype(BF16)
        else:
            kc_ref[...] = val
    hp = _dot(xb, wm_ref[:, 9 * LANE:15 * LANE]).astype(BF16)
    for n, ref in enumerate((sq_ref, sk_ref, sv_ref)):
        ref[...] = hp[:, n * GROUP_W:(n + 1) * GROUP_W]
    hf = _dot(xb, wm_ref[:, 15 * LANE:25 * LANE])
    hg_ref[...] = hf[:, :9 * LANE]
    gt_ref[...] = hf[:, 9 * LANE:10 * LANE]


def _proj(x, wm, wt, tab, seq_len):
    T, D = x.shape
    tm = min(TM_PROJ, seq_len)
    nst = seq_len // tm
    bf16_widths = (GROUP_W, GROUP_W, GROUP_W, LANE, LANE, GROUP_W, GROUP_W, GROUP_W)
    return pl.pallas_call(
        _proj_kernel,
        grid=(T // tm,),
        in_specs=[pl.BlockSpec((tm, D), lambda i: (i, 0)),
                  _const_spec(wm.shape), _const_spec(wt.shape),
                  pl.BlockSpec((tm, 6 * LANE), lambda i: (i % nst, 0))],
        out_specs=[pl.BlockSpec((tm, w), lambda i: (i, 0)) for w in bf16_widths]
        + [pl.BlockSpec((tm, 9 * LANE), lambda i: (i, 0)),
                   pl.BlockSpec((tm, LANE), lambda i: (i, 0)),
                   pl.BlockSpec((tm, LANE), lambda i: (i, 0)),
                   pl.BlockSpec((wt.shape[0], tm), lambda i: (0, i))],
        out_shape=[jax.ShapeDtypeStruct((T, w), BF16) for w in bf16_widths]
        + [jax.ShapeDtypeStruct((T, 9 * LANE), F32),
                   jax.ShapeDtypeStruct((T, LANE), F32),
                   jax.ShapeDtypeStruct((T, LANE), F32),
                   jax.ShapeDtypeStruct((wt.shape[0], T), BF16)],
        compiler_params=_cp(("parallel",)),
        name="in_proj",
    )(x, wm, wt, tab)


def _prep_w_in(w):
    k = w.shape[0]
    w = w.astype(BF16)
    sizes = ((HEADS * DIFF_QK,) * 4 + (GROUP_W,) + (GROUP_W,) * 4 + (HEADS,) * 2
             + (GROUP_W,) + (HEAD_DIM,) * 6 + (3 * HEADS,) + (GROUP_W,) * 3)
    offs = np.concatenate([[0], np.cumsum(sizes)])
    (dq1, dq2, dk1, dk2, dv, gq, gk, gv, gz, ga, gb,
     nq, nkc, nvc, nks, nvs, nkw, nvw, ngate, sq, sk, sv) = [w[:, offs[i]:offs[i + 1]] for i in range(len(sizes))]
    scale = HEAD_DIM ** -0.5
    pad = lambda a: jnp.concatenate([a, jnp.zeros((k, LANE - a.shape[1]), w.dtype)], axis=1)
    main = jnp.concatenate(
        [dq1, dq2, dk1, dk2, nq * scale, nks, nvs, nkw, nvw, nkc, nvc,
         sq * scale, sk, sv, gq, gk, gv, gz, pad(jnp.concatenate([ga, gb], axis=1)), pad(ngate)], axis=1)
    wt = jnp.concatenate([dv, sv, nvs, nvw], axis=1).T
    return main, wt


def _rope_table(seq_len):
    def cs(dim):
        inv = ROPE_THETA ** (-jnp.arange(0, dim, 2, dtype=F32) / dim)
        ang = jnp.arange(seq_len, dtype=F32)[:, None] * inv[None, :]
        c, sgn = jnp.cos(ang), jnp.sin(ang)
        return jnp.concatenate([c, c], axis=1), jnp.concatenate([-sgn, sgn], axis=1)
    cd, sd = cs(DIFF_QK)
    cn, sn = cs(HEAD_DIM)
    one = jnp.ones((seq_len, HEAD_DIM), F32)
    zero = jnp.zeros((seq_len, HEAD_DIM), F32)
    return jnp.concatenate(
        [jnp.tile(cd, (1, 4)), jnp.tile(sd, (1, 4)), jnp.tile(cn, (1, 2)), jnp.tile(sn, (1, 2)),
         cn, one, sn, zero], axis=1)


def _tri_pairs(n, descending=False):
    qi, kj = [], []
    for i in range(n):
        js = range(i, -1, -1) if descending else range(i + 1)
        for j in js:
            qi.append(i)
            kj.append(j)
    return jnp.asarray(qi, jnp.int32), jnp.asarray(kj, jnp.int32)


ONES_ROWS = 16
SKEW = 3
SKEW_SB = 1


def _diff_kernel(qi_ref, kj_ref, q_ref, k_ref, vt_ref, lam_ref, g_ref, o_ref,
                 m_scr, acc_scr, *, lam_init):
    p = pl.program_id(1)
    i = qi_ref[p]
    j = kj_ref[p]
    tq, tk = q_ref.shape[0], k_ref.shape[0]
    c = (DIFF_QK ** -0.5) * LOG2E
    head32 = _iota((1, LANE), 1) // DIFF_QK

    @pl.when(j == 0)
    def _init():
        m_scr[...] = jnp.full(m_scr.shape, NEG, F32)
        acc_scr[...] = jnp.zeros(acc_scr.shape, F32)

    def step(masked):
        ones = jnp.ones((ONES_ROWS, tk), BF16)
        qs = (q_ref[:, :LANE], q_ref[:, LANE:])
        ks = (k_ref[:, :LANE], k_ref[:, LANE:])
        chains = [(t, h) for t in range(2) for h in range(HEADS)]

        def scores(t, h):
            km = jnp.where(head32 == h, ks[t], jnp.zeros_like(ks[t]))
            st = _dot_nt(km, qs[t])
            if masked:
                st = jnp.where(_iota((tk, tq), 0) <= _iota((tk, tq), 1), st, NEG)
            return st

        pend = [scores(*chains[n]) for n in range(SKEW)]
        for n, (t, h) in enumerate(chains):
            st = pend.pop(0)
            if n + SKEW < len(chains):
                pend.append(scores(*chains[n + SKEW]))
            idx = t * HEADS + h
            m_prev = m_scr[idx:idx + 1, :]
            m_new = jnp.maximum(m_prev, jnp.max(st, axis=0, keepdims=True))
            pt = jnp.exp2((st - m_new) * c).astype(BF16)
            alpha = jnp.exp2((m_prev - m_new) * c)
            m_scr[idx:idx + 1, :] = m_new
            vh = jnp.concatenate([vt_ref[h * HEAD_DIM:(h + 1) * HEAD_DIM, :], ones], axis=0)
            acc_scr[idx] = acc_scr[idx] * alpha + _dot(vh, pt)

    @pl.when(j < i)
    def _off():
        step(False)

    @pl.when(j == i)
    def _diag():
        step(True)
        lp = lam_ref[...]
        lam = (jnp.exp(jnp.sum(lp[0:1] * lp[1:2], axis=-1, keepdims=True))
               - jnp.exp(jnp.sum(lp[2:3] * lp[3:4], axis=-1, keepdims=True)) + lam_init)
        parts = []
        for h in range(HEADS):
            a0, a1 = acc_scr[h], acc_scr[HEADS + h]
            oh = (a0[:HEAD_DIM] / a0[HEAD_DIM:HEAD_DIM + 1] - lam * (a1[:HEAD_DIM] / a1[HEAD_DIM:HEAD_DIM + 1]))
            ms = jnp.sum(oh * oh, axis=0, keepdims=True) * (1.0 / HEAD_DIM)
            parts.append(oh * lax.rsqrt(ms + NORM_EPS))
        y = jnp.concatenate(parts, axis=0).T * g_ref[...] * (1.0 - lam_init)
        o_ref[...] = y.astype(o_ref.dtype)


def _diff_attention(dq, dk, ht, lam_p, subln_g, lam_init, batch, seq_len):
    t = min(T_ATT, seq_len)
    nq = seq_len // t
    qi, kj = _tri_pairs(nq)
    g_full = jnp.tile(subln_g, HEADS).reshape(1, GROUP_W)
    grid_spec = pltpu.PrefetchScalarGridSpec(
        num_scalar_prefetch=2,
        grid=(batch, qi.shape[0]),
        in_specs=[pl.BlockSpec((t, GROUP_W), lambda b, p, qi, kj: (b * nq + qi[p], 0)),
                  pl.BlockSpec((t, GROUP_W), lambda b, p, qi, kj: (b * nq + kj[p], 0)),
                  pl.BlockSpec((GROUP_W, t), lambda b, p, qi, kj: (0, b * nq + kj[p])),
                  pl.BlockSpec((4, DIFF_QK), lambda b, p, qi, kj: (0, 0)),
                  pl.BlockSpec((1, GROUP_W), lambda b, p, qi, kj: (0, 0))],
        out_specs=pl.BlockSpec((t, GROUP_W), lambda b, p, qi, kj: (b * nq + qi[p], 0)),
        scratch_shapes=[pltpu.VMEM((2 * HEADS, t), F32),
                        pltpu.VMEM((2 * HEADS, HEAD_DIM + ONES_ROWS, t), F32)])
    return pl.pallas_call(
        functools.partial(_diff_kernel, lam_init=lam_init),
        grid_spec=grid_spec,
        out_shape=jax.ShapeDtypeStruct((batch * seq_len, GROUP_W), BF16),
        compiler_params=_cp(("parallel", "arbitrary")),
        name="diff_attn",
    )(qi, kj, dq, dk, ht, lam_p, g_full)


SB_CUM = 256


def _sb_kernel(qi_ref, kj_ref, q_ref, k_ref, v_ref, o_ref, carry_scr, acc_scr):
    p = pl.program_id(1)
    i = qi_ref[p]
    j = kj_ref[p]
    tq, tk = q_ref.shape[0], k_ref.shape[0]
    head64 = _iota((1, GROUP_W), 1) // HEAD_DIM
    cw = min(SB_CUM, tk)

    @pl.when(j == i)
    def _init():
        carry_scr[...] = jnp.zeros(carry_scr.shape, F32)
        acc_scr[...] = jnp.zeros(acc_scr.shape, F32)

    def step(masked):
        q, k, v = q_ref[...], k_ref[...], v_ref[...]
        m_excl = jnp.where(_iota((cw, cw), 0) > _iota((cw, cw), 1), 1.0, 0.0).astype(BF16)
        if masked:
            before = _iota((tq, tk), 1) < _iota((tq, tk), 0)

        def scores(h):
            return _dot_nt(q, jnp.where(head64 == h, k, jnp.zeros_like(k)))

        pend = [scores(h) for h in range(SKEW_SB)]
        pv = None
        for h in range(HEADS):
            z = pend.pop(0)
            if h + SKEW_SB < HEADS:
                pend.append(scores(h + SKEW_SB))
            sp = jnp.maximum(z, 0.0) + jnp.log(1.0 + jnp.exp2(jnp.abs(z) * (-LOG2E)))
            log_beta = z - sp
            if masked:
                sp = jnp.where(before, sp, 0.0)
            x = sp.astype(BF16)
            blocks = []
            suffix = carry_scr[h]
            for blk in reversed(range(tk // cw)):
                sl = slice(blk * cw, (blk + 1) * cw)
                cb = _dot(x[:, sl], m_excl) + suffix
                blocks.insert(0, cb)
                suffix = cb[:, 0:1] + x[:, blk * cw:blk * cw + 1].astype(F32)
            carry_scr[h] = suffix
            att = jnp.exp(log_beta - jnp.concatenate(blocks, axis=1))
            if masked:
                att = jnp.where(before, att, 0.0)
            vm = jnp.where(head64 == h, v, jnp.zeros_like(v))
            d = _dot(att.astype(BF16), vm)
            pv = d if pv is None else pv + d
        acc_scr[...] = acc_scr[...] + pv

    @pl.when(j == i)
    def _diag():
        step(True)

    @pl.when(j < i)
    def _off():
        step(False)

    @pl.when(j == 0)
    def _fin():
        o_ref[...] = acc_scr[...].astype(o_ref.dtype)


def _sb_attention(sq, sk, sv, batch, seq_len):
    t = min(T_ATT, seq_len)
    nq = seq_len // t
    qi, kj = _tri_pairs(nq, descending=True)
    grid_spec = pltpu.PrefetchScalarGridSpec(
        num_scalar_prefetch=2,
        grid=(batch, qi.shape[0]),
        in_specs=[pl.BlockSpec((t, GROUP_W), lambda b, p, qi, kj: (b * nq + qi[p], 0)),
                  pl.BlockSpec((t, GROUP_W), lambda b, p, qi, kj: (b * nq + kj[p], 0)),
                  pl.BlockSpec((t, GROUP_W), lambda b, p, qi, kj: (b * nq + kj[p], 0))],
        out_specs=pl.BlockSpec((t, GROUP_W), lambda b, p, qi, kj: (b * nq + qi[p], 0)),
        scratch_shapes=[pltpu.VMEM((HEADS, t, 1), F32), pltpu.VMEM((t, GROUP_W), F32)])
    return pl.pallas_call(
        _sb_kernel,
        grid_spec=grid_spec,
        out_shape=jax.ShapeDtypeStruct((batch * seq_len, GROUP_W), BF16),
        compiler_params=_cp(("parallel", "arbitrary")),
        name="sb_attn",
    )(qi, kj, sq, sk, sv)


def _stack_heads(q):
    qf = q.astype(F32)
    lo = _iota((1, LANE), 1) < HEAD_DIM
    parts = []
    for blk in (qf[:, :LANE], qf[:, LANE:]):
        parts.append(jnp.where(lo, blk, 0.0))
        parts.append(jnp.where(lo, pltpu.roll(blk, HEAD_DIM, 1), 0.0))
    return jnp.concatenate(parts, axis=0).astype(BF16)


def _nsa_compress_kernel(r_ref, pe_ref, w1lo_ref, w1hi_ref, w2_ref, w2vt_ref, o_ref, vt_ref):
    r = r_ref[...]
    n = r.shape[0]
    y1 = _dot((r + pe_ref[0:1, :]).astype(BF16), w1lo_ref[...])
    y2 = _dot((r + pe_ref[1:2, :]).astype(BF16), w1hi_ref[...])
    hid = y1 + pltpu.roll(y2, n - 1, 0)
    act = (hid * jax.nn.sigmoid(hid)).astype(BF16)
    o_ref[...] = _dot(act, w2_ref[...]).astype(o_ref.dtype)
    vt_ref[...] = _dot_nt(w2vt_ref[...], act).astype(vt_ref.dtype)


def _nsa_compress(kcvc, pe_k, pe_v, ck_w1, ck_w2, cv_w1, cv_w2, batch, seq_len):
    nrow = seq_len // CMP_STRIDE
    width = CMP_STRIDE * LANE
    r = kcvc.reshape(batch * nrow, width)
    hid = ck_w1.shape[1]
    pe = jnp.concatenate([pe_k, pe_v], axis=1).reshape(2, width)
    zk = jnp.zeros((CMP_BLOCK, HEAD_DIM, hid), BF16)
    w1 = jnp.concatenate(
        [jnp.concatenate([ck_w1.astype(BF16).reshape(CMP_BLOCK, HEAD_DIM, hid), zk], axis=2),
         jnp.concatenate([zk, cv_w1.astype(BF16).reshape(CMP_BLOCK, HEAD_DIM, hid)], axis=2)], axis=1)
    w1 = w1.reshape(2, width, 2 * hid)
    zo = jnp.zeros((hid, HEAD_DIM), BF16)
    w2 = jnp.concatenate([jnp.concatenate([ck_w2.astype(BF16), zo], axis=1),
                          jnp.concatenate([zo, cv_w2.astype(BF16)], axis=1)], axis=0)
    return pl.pallas_call(
        _nsa_compress_kernel,
        grid=(batch,),
        in_specs=[pl.BlockSpec((nrow, width), lambda b: (b, 0)),
                  _const_spec((2, width)), _const_spec((width, 2 * hid)), _const_spec((width, 2 * hid)),
                  _const_spec((2 * hid, LANE)), _const_spec((HEAD_DIM, 2 * hid))],
        out_specs=[pl.BlockSpec((nrow, LANE), lambda b: (b, 0)),
                   pl.BlockSpec((HEAD_DIM, nrow), lambda b: (b, 0))],
        out_shape=[jax.ShapeDtypeStruct((batch * nrow, LANE), BF16),
                   jax.ShapeDtypeStruct((batch * HEAD_DIM, nrow), BF16)],
        compiler_params=_cp(("parallel",)),
        name="nsa_compress",
    )(r, pe, w1[0], w1[1], w2, w2[:, HEAD_DIM:].T)


def _nsa_cmp_kernel(q_ref, kv_ref, vt_ref, ovt_ref, ocmp_ref, bias_ref):
    i = pl.program_id(1)
    tq = q_ref.shape[0]
    ncmp = kv_ref.shape[0]
    nslc = ovt_ref.shape[0]
    qs = _stack_heads(q_ref[...])
    kv = kv_ref[...]
    vt = vt_ref[...]
    tpos = i * tq + _iota((1, tq), 1)
    cm = _iota((ncmp, 1), 0) * CMP_STRIDE + (CMP_BLOCK - 1) <= tpos
    sts = [_dot_nt(kv, qs[h * tq:(h + 1) * tq]) for h in range(HEADS)]
    psum = None
    parts = []
    for h in range(HEADS):
        st = jnp.where(cm, sts[h], NEG)
        e = jnp.exp(st - jnp.max(st, axis=0, keepdims=True))
        pr = jnp.where(cm, e * (1.0 / jnp.sum(e, axis=0, keepdims=True)), 0.0)
        parts.append(_dot(vt, pr.astype(BF16)))
        psum = pr if psum is None else psum + pr
    ocmp_ref[...] = jnp.concatenate(parts, axis=0).T
    hi, lo = _split2(psum)
    imp = _dot(ovt_ref[...], hi) + _dot(ovt_ref[...], lo)
    blk = _iota((nslc, 1), 0)
    cur = tpos // SLC_BLOCK
    work = jnp.where(blk == 0, FORCE, jnp.where(blk == cur, FORCE, jnp.where(blk == cur - 1, FORCE, imp)))
    work = jnp.where(blk <= cur, work, -FORCE)
    sel = jnp.zeros((nslc, tq), F32)
    for _ in range(min(SLC_TOPN, nslc)):
        mx = jnp.max(work, axis=0, keepdims=True)
        first = jnp.min(jnp.where(work == mx, blk, nslc), axis=0, keepdims=True)
        hit = blk == first
        sel = jnp.where(hit, 1.0, sel)
        work = jnp.where(hit, -jnp.inf, work)
    bias_ref[...] = jnp.where(sel > 0.5, 0.0, NEG)


def _nsa_cmp(nq_arr, kvcmp, vtcmp, batch, seq_len):
    tq = min(TQ_CMP, seq_len)
    nq = seq_len // tq
    ncmp = seq_len // CMP_STRIDE
    nslc = seq_len // SLC_BLOCK
    cstart = np.arange(ncmp)[None, :] * CMP_STRIDE
    sstart = np.arange(nslc)[:, None] * SLC_BLOCK
    ovt = (cstart < sstart + SLC_BLOCK) & (cstart + CMP_BLOCK - 1 >= sstart)
    ovt &= (np.arange(ncmp)[None, :] < (seq_len - CMP_BLOCK) // CMP_STRIDE + 1)
    ovt = jnp.asarray(ovt, BF16)
    return pl.pallas_call(
        _nsa_cmp_kernel,
        grid=(batch, nq),
        in_specs=[pl.BlockSpec((tq, GROUP_W), lambda b, i: (b * nq + i, 0)),
                  pl.BlockSpec((ncmp, LANE), lambda b, i: (b, 0)),
                  pl.BlockSpec((HEAD_DIM, ncmp), lambda b, i: (b, 0)),
                  pl.BlockSpec((nslc, ncmp), lambda b, i: (0, 0))],
        out_specs=[pl.BlockSpec((tq, GROUP_W), lambda b, i: (b * nq + i, 0)),
                   pl.BlockSpec((nslc, tq), lambda b, i: (0, b * nq + i))],
        out_shape=[jax.ShapeDtypeStruct((batch * seq_len, GROUP_W), F32),
                   jax.ShapeDtypeStruct((nslc, batch * seq_len), F32)],
        compiler_params=_cp(("parallel", "parallel")),
        name="nsa_cmp_select",
    )(nq_arr, kvcmp, vtcmp, ovt)


def _nsa_win_kernel(q_ref, k_ref, vt_ref, o_ref, *, window):
    i = pl.program_id(1)
    tq = q_ref.shape[0]
    span = tq + window
    base = pl.multiple_of(jnp.maximum(i * tq - window, 0), LANE)
    kv = k_ref[pl.ds(base, span), :]
    vh = jnp.concatenate([vt_ref[:, pl.ds(base, span)], jnp.ones((ONES_ROWS, span), BF16)], axis=0)
    qs = _stack_heads(q_ref[...])
    rel = (i * tq + _iota((1, tq), 1)) - (base + _iota((span, 1), 0))
    bias = jnp.where(rel >= 0, jnp.where(rel < window, 0.0, NEG), NEG)
    sts = [_dot_nt(kv, qs[h * tq:(h + 1) * tq]) for h in range(HEADS)]
    parts = []
    for h in range(HEADS):
        st = sts[h] + bias
        e = jnp.exp(st - jnp.max(st, axis=0, keepdims=True)).astype(BF16)
        r = _dot(vh, e)
        parts.append(r[:HEAD_DIM] / r[HEAD_DIM:HEAD_DIM + 1])
    o_ref[...] = jnp.concatenate(parts, axis=0).T


def _nsa_window(nq_arr, kw, ht, batch, seq_len):
    tq = min(TQ_WIN, seq_len)
    nq = seq_len // tq
    window = min(WINDOW, seq_len - tq)
    vw_row_blk = 2 * GROUP_W // HEAD_DIM + 1
    return pl.pallas_call(
        functools.partial(_nsa_win_kernel, window=window),
        grid=(batch, nq),
        in_specs=[pl.BlockSpec((tq, GROUP_W), lambda b, i: (b * nq + i, 0)),
                  pl.BlockSpec((seq_len, LANE), lambda b, i: (b, 0)),
                  pl.BlockSpec((HEAD_DIM, seq_len), lambda b, i: (vw_row_blk, b))],
        out_specs=pl.BlockSpec((tq, GROUP_W), lambda b, i: (b * nq + i, 0)),
        out_shape=jax.ShapeDtypeStruct((batch * seq_len, GROUP_W), F32),
        compiler_params=_cp(("parallel", "parallel")),
        name="nsa_window",
    )(nq_arr, kw, ht)


def _nsa_sel_kernel(qi_ref, kj_ref, q_ref, k_ref, vt_ref, bias_ref, ocmp_ref, owin_ref, gate_ref, o_ref,
                    qs_scr, m_scr, acc_scr):
    p = pl.program_id(1)
    i = qi_ref[p]
    j = kj_ref[p]
    tq, tk = q_ref.shape[0], k_ref.shape[0]
    nblk = tk // SLC_BLOCK

    @pl.when(j == 0)
    def _init():
        qs = _stack_heads(q_ref[...])
        for h in range(HEADS):
            qs_scr[h] = qs[h * tq:(h + 1) * tq]
        m_scr[...] = jnp.full(m_scr.shape, NEG, F32)
        acc_scr[...] = jnp.zeros(acc_scr.shape, F32)

    def step(masked):
        kv = k_ref[...]
        bias = bias_ref[...][:, None, :]
        vh = jnp.concatenate([vt_ref[...], jnp.ones((ONES_ROWS, tk), BF16)], axis=0)

        def scores(h):
            st = _dot_nt(kv, qs_scr[h])
            st = (st.reshape(nblk, SLC_BLOCK, tq) + bias).reshape(tk, tq)
            if masked:
                st = jnp.where(j * tk + _iota((tk, tq), 0) <= i * tq + _iota((tk, tq), 1), st, NEG)
            return st

        pend = [scores(h) for h in range(SKEW)]
        for h in range(HEADS):
            st = pend.pop(0)
            if h + SKEW < HEADS:
                pend.append(scores(h + SKEW))
            m_prev = m_scr[h:h + 1, :]
            m_new = jnp.maximum(m_prev, jnp.max(st, axis=0, keepdims=True))
            pt = jnp.exp(st - m_new).astype(BF16)
            alpha = jnp.exp(m_prev - m_new)
            m_scr[h:h + 1, :] = m_new
            acc_scr[h] = acc_scr[h] * alpha + _dot(vh, pt)

    last = (i * tq + tq - 1) // tk

    @pl.when((j + 1) * tk <= i * tq)
    def _past():
        step(False)

    @pl.when((j + 1) * tk > i * tq)
    def _diag():
        step(True)

    @pl.when(j == last)
    def _fin():
        parts = []
        for h in range(HEADS):
            a = acc_scr[h]
            parts.append(a[:HEAD_DIM] / a[HEAD_DIM:HEAD_DIM + 1])
        osel = jnp.concatenate(parts, axis=0).T
        sig = jax.nn.sigmoid(gate_ref[...])
        grow = _iota((LANE, 1), 0)
        ghead = _iota((1, GROUP_W), 1) // HEAD_DIM
        out = None
        for br, o_br in enumerate((ocmp_ref[...], osel, owin_ref[...])):
            e_br = jnp.where(grow == 3 * ghead + br, 1.0, 0.0).astype(BF16)
            term = _dot_x3(sig, e_br) * o_br
            out = term if out is None else out + term
        o_ref[...] = out.astype(o_ref.dtype)


def _nsa_select(nq_arr, ks, ht, sel, ocmp, owin, gates, batch, seq_len):
    tq = min(TQ_SEL, seq_len)
    tk = min(TK_SEL, seq_len)
    nq, nk = seq_len // tq, seq_len // tk
    qi, kj = [], []
    for i in range(nq):
        for j in range((i * tq + tq - 1) // tk + 1):
            qi.append(i)
            kj.append(j)
    qi, kj = jnp.asarray(qi, jnp.int32), jnp.asarray(kj, jnp.int32)
    qrow = lambda b, p, qi, kj: (b * nq + qi[p], 0)
    vs_row_blk = 2 * GROUP_W // HEAD_DIM
    grid_spec = pltpu.PrefetchScalarGridSpec(
        num_scalar_prefetch=2,
        grid=(batch, qi.shape[0]),
        in_specs=[pl.BlockSpec((tq, GROUP_W), qrow),
                  pl.BlockSpec((tk, LANE), lambda b, p, qi, kj: (b * nk + kj[p], 0)),
                  pl.BlockSpec((HEAD_DIM, tk), lambda b, p, qi, kj: (vs_row_blk, b * nk + kj[p])),
                  pl.BlockSpec((tk // SLC_BLOCK, tq), lambda b, p, qi, kj: (kj[p], b * nq + qi[p])),
                  pl.BlockSpec((tq, GROUP_W), qrow), pl.BlockSpec((tq, GROUP_W), qrow),
                  pl.BlockSpec((tq, LANE), qrow)],
        out_specs=pl.BlockSpec((tq, GROUP_W), qrow),
        scratch_shapes=[pltpu.VMEM((HEADS, tq, LANE), BF16), pltpu.VMEM((HEADS, tq), F32),
                        pltpu.VMEM((HEADS, HEAD_DIM + ONES_ROWS, tq), F32)])
    return pl.pallas_call(
        _nsa_sel_kernel,
        grid_spec=grid_spec,
        out_shape=jax.ShapeDtypeStruct((batch * seq_len, GROUP_W), BF16),
        compiler_params=_cp(("parallel", "arbitrary")),
        name="nsa_select_gate",
    )(qi, kj, nq_arr, ks, ht, sel, ocmp, owin, gates)


def _nsa(nq_arr, ks, kw, ht, kcvc, gates, pe_k, pe_v, ck_w1, ck_w2, cv_w1, cv_w2, batch, seq_len):
    kvcmp, vtcmp = _nsa_compress(kcvc, pe_k, pe_v, ck_w1, ck_w2, cv_w1, cv_w2, batch, seq_len)
    ocmp, sel = _nsa_cmp(nq_arr, kvcmp, vtcmp, batch, seq_len)
    owin = _nsa_window(nq_arr, kw, ht, batch, seq_len)
    return _nsa_select(nq_arr, ks, ht, sel, ocmp, owin, gates, batch, seq_len)


def _bd(mc, mask_bd):
    return jnp.where(mask_bd, jnp.concatenate([mc] * HEADS, axis=0), jnp.zeros((), mc.dtype))


def _mm_bd(x, mc, mask_bd):
    return _dot(x.astype(BF16), _bd(mc.astype(BF16), mask_bd))


def _gdn_prep_kernel(x_ref, ab_ref, cw_ref, alog_ref, dtb_ref,
                     u_ref, w_ref, qd_ref, in_ref, kdt_ref, gl_ref, xpad_scr):
    n = pl.program_id(0)
    C = GDN_CHUNK
    W = GROUP_W
    nb, rows = x_ref.shape[0], x_ref.shape[1]
    inst = [(b, c) for b in range(nb) for c in range(rows // C)]

    @pl.when(n == 0)
    def _():
        xpad_scr[:, 0:8, :] = jnp.zeros((nb, 8, xpad_scr.shape[2]), F32)

    cw = cw_ref[...]
    qkv_b = []
    for b in range(nb):
        x = x_ref[b]
        xpad_scr[b, 8:8 + rows, :] = x
        conv = (cw[0:1] * xpad_scr[b, 5:5 + rows, :] + cw[1:2] * xpad_scr[b, 6:6 + rows, :]
                + cw[2:3] * xpad_scr[b, 7:7 + rows, :] + cw[3:4] * x)
        xpad_scr[b, 0:8, :] = x[rows - 8:rows, :]
        qkv_b.append(conv * jax.nn.sigmoid(conv))

    r256 = _iota((W, W), 0)
    c256 = _iota((W, W), 1)
    mask_bd = (r256 // HEAD_DIM) == (c256 // HEAD_DIM)
    ones_bd = jnp.where(mask_bd, 1.0, 0.0).astype(BF16)
    eye256 = jnp.where(r256 == c256, 1.0, 0.0).astype(BF16)
    row = _iota((C, W), 0)
    jl = _iota((C, W), 1) % HEAD_DIM
    ltri = jnp.where(_iota((C, C), 1) <= _iota((C, C), 0), 1.0, 0.0).astype(BF16)
    erow = _iota((LANE, W), 0)
    ehead = _iota((LANE, W), 1) // HEAD_DIM
    e_g = jnp.where(erow == ehead, 1.0, 0.0).astype(BF16)
    e_b = jnp.where(erow == ehead + HEADS, 1.0, 0.0).astype(BF16)

    def each(f, *lists):
        return [f(*args) for args in zip(*lists)]

    sl = [slice(c * C, (c + 1) * C) for _, c in inst]
    q = [qkv_b[b][sl[k], 0:W] for k, (b, _) in enumerate(inst)]
    kk = [qkv_b[b][sl[k], W:2 * W] for k, (b, _) in enumerate(inst)]
    v = [qkv_b[b][sl[k], 2 * W:3 * W] for k, (b, _) in enumerate(inst)]
    ab = [ab_ref[b, sl[k], :] for k, (b, _) in enumerate(inst)]

    qn = each(lambda t: t * lax.rsqrt(_dot_x2(t * t, ones_bd) + NORM_EPS) * (HEAD_DIM ** -0.5), q)
    kn = each(lambda t: t * lax.rsqrt(_dot_x2(t * t, ones_bd) + NORM_EPS), kk)

    def gate(a):
        z = a + dtb_ref[...]
        return -jnp.exp(alog_ref[...]) * (jnp.maximum(z, 0.0) + jnp.log1p(jnp.exp(-jnp.abs(z))))

    g_hl = each(lambda a: _dot_x3(gate(a), e_g), ab)
    beta = each(lambda a: _dot_x3(jax.nn.sigmoid(a), e_b), ab)
    gc = each(lambda g: _dot_x3_left(ltri, g), g_hl)
    glast = each(lambda g: g[C - 1:C, :], gc)
    exp_g = each(jnp.exp, gc)
    dmat = each(lambda g: _dot_x3_left(ltri, jnp.where(row > jl, g, 0.0)), g_hl)
    decay = each(lambda d: jnp.where(jl <= row, jnp.exp(d), 0.0), dmat)

    kt4 = each(lambda t: _dot_nt(eye256, jnp.concatenate([t.astype(BF16)] * HEADS, axis=0)), kn)
    kb_mat = each(lambda t: jnp.where(mask_bd, t, 0.0).astype(BF16), kt4)
    kbeta = each(lambda t, bb: t * bb, kn, beta)
    a_c = each(lambda t, m, d: jnp.where(jl < row, _dot(t.astype(BF16), m) * d, 0.0), kbeta, kb_mat, decay)
    intra = each(lambda t, m, d: _dot(t.astype(BF16), m) * d, qn, kb_mat, decay)

    t_c = each(lambda a: jnp.where(jl == row, 1.0, 0.0) - a, a_c)
    p_c = a_c
    for _ in range(5):
        p_c = each(lambda pc: _mm_bd(pc, pc, mask_bd), p_c)
        t_c = each(lambda tc, pc: tc + _mm_bd(tc, pc, mask_bd), t_c, p_c)

    u = each(lambda tc, t, bb: _mm_bd(tc, t * bb, mask_bd), t_c, v, beta)
    w = each(lambda tc, t, e: _mm_bd(tc, t * e, mask_bd), t_c, kbeta, exp_g)
    kdt = each(lambda t, gl, g: _dot_nt(eye256, (t * jnp.exp(gl - g)).astype(BF16)), kn, glast, gc)

    for k, (b, c) in enumerate(inst):
        u_ref[b, sl[k], :] = u[k]
        w_ref[b, sl[k], :] = w[k].astype(w_ref.dtype)
        qd_ref[b, sl[k], :] = (qn[k] * exp_g[k]).astype(qd_ref.dtype)
        in_ref[b, sl[k], :] = intra[k].astype(in_ref.dtype)
        kdt_ref[b, c] = kdt[k].astype(kdt_ref.dtype)
        gl_ref[b, c] = jnp.exp(glast[k])


def _dot_x3_left(w, x):
    hi = x.astype(BF16)
    r = x - hi.astype(F32)
    mid = r.astype(BF16)
    lo = (r - mid.astype(F32)).astype(BF16)
    return _dot(w, hi) + _dot(w, mid) + _dot(w, lo)


def _gdn_scan_kernel(u_ref, w_ref, qd_ref, in_ref, kdt_ref, gl_ref, z_ref, g_ref, o_ref, s_scr):
    n = pl.program_id(0)
    C = GDN_CHUNK
    W = GROUP_W
    nb, rows = u_ref.shape[0], u_ref.shape[1]

    @pl.when(n == 0)
    def _():
        s_scr[...] = jnp.zeros(s_scr.shape, F32)

    mask_bd = (_iota((W, W), 0) // HEAD_DIM) == (_iota((W, W), 1) // HEAD_DIM)
    ones_bd = jnp.where(mask_bd, 1.0, 0.0).astype(BF16)
    s = [s_scr[b] for b in range(nb)]
    for c in range(rows // C):
        sl = slice(c * C, (c + 1) * C)
        sb = [t.astype(BF16) for t in s]
        v_new = [u_ref[b, sl, :] - _dot(w_ref[b, sl, :], sb[b]) for b in range(nb)]
        vb = [t.astype(BF16) for t in v_new]
        s = [s[b] * gl_ref[b, c] + jnp.where(mask_bd, _dot(kdt_ref[b, c], vb[b]), 0.0) for b in range(nb)]
        o = [_dot(qd_ref[b, sl, :], sb[b]) + _dot(in_ref[b, sl, :], _bd(vb[b], mask_bd)) for b in range(nb)]
        for b in range(nb):
            ms = _dot_x2(o[b] * o[b], ones_bd) * (1.0 / HEAD_DIM)
            zz = z_ref[b, sl, :]
            y = o[b] * lax.rsqrt(ms + NORM_EPS) * g_ref[...] * (zz * jax.nn.sigmoid(zz))
            o_ref[b, sl, :] = y.astype(o_ref.dtype)
    for b in range(nb):
        s_scr[b] = s[b]


GDN_PREP_CHUNKS = 8
GDN_SCAN_CHUNKS = 8


def _gdn(hg, conv_w, a_log, dt_bias, norm_g, batch, seq_len):
    C = GDN_CHUNK
    nc = seq_len // C
    W = GROUP_W
    padl = lambda a: jnp.concatenate([a, jnp.zeros((LANE - a.shape[0],), F32)]).reshape(1, LANE)
    hg3 = hg.reshape(batch, seq_len, hg.shape[-1])
    cp = math.gcd(GDN_PREP_CHUNKS, nc)
    rp = cp * C
    blkp = pl.BlockSpec((batch, rp, W), lambda n: (0, n, 0))
    u, w, qd, intra, kdt, gl = pl.pallas_call(
        _gdn_prep_kernel,
        grid=(nc // cp,),
        in_specs=[pl.BlockSpec((batch, rp, 3 * W), lambda n: (0, n, 0)),
                  pl.BlockSpec((batch, rp, LANE), lambda n: (0, n, 8)),
                  pl.BlockSpec((4, 3 * W), lambda n: (0, 0)),
                  pl.BlockSpec((1, LANE), lambda n: (0, 0)),
                  pl.BlockSpec((1, LANE), lambda n: (0, 0))],
        out_specs=[blkp] * 4
        + [pl.BlockSpec((batch, cp, W, C), lambda n: (0, n, 0, 0)),
           pl.BlockSpec((batch, cp, 1, W), lambda n: (0, n, 0, 0))],
        out_shape=[jax.ShapeDtypeStruct((batch, seq_len, W), F32), jax.ShapeDtypeStruct((batch, seq_len, W), BF16),
                   jax.ShapeDtypeStruct((batch, seq_len, W), BF16), jax.ShapeDtypeStruct((batch, seq_len, W), BF16),
                   jax.ShapeDtypeStruct((batch, nc, W, C), BF16),
                   jax.ShapeDtypeStruct((batch, nc, 1, W), F32)],
        scratch_shapes=[pltpu.VMEM((batch, 8 + rp, 3 * W), F32)],
        compiler_params=_cp(("arbitrary",)),
        name="gdn_prep",
    )(hg3, hg3, conv_w, padl(a_log), padl(dt_bias))

    cs = math.gcd(GDN_SCAN_CHUNKS, nc)
    blk = pl.BlockSpec((batch, cs * C, W), lambda n: (0, n, 0))
    out = pl.pallas_call(
        _gdn_scan_kernel,
        grid=(nc // cs,),
        in_specs=[blk, blk, blk, blk,
                  pl.BlockSpec((batch, cs, W, C), lambda n: (0, n, 0, 0)),
                  pl.BlockSpec((batch, cs, 1, W), lambda n: (0, n, 0, 0)),
                  pl.BlockSpec((batch, cs * C, W), lambda n: (0, n, 3)),
                  pl.BlockSpec((1, W), lambda n: (0, 0))],
        out_specs=blk,
        out_shape=jax.ShapeDtypeStruct((batch, seq_len, W), BF16),
        scratch_shapes=[pltpu.VMEM((batch, W, W), F32)],
        compiler_params=_cp(("arbitrary",)),
        name="gdn_scan",
    )(u, w, qd, intra, kdt, gl, hg3, jnp.tile(norm_g, HEADS).reshape(1, W))
    return out.reshape(batch * seq_len, W)


def kernel(x, w_in, w_out, ffn1_w_gu, ffn1_w_down, ffn2_w_gu, ffn2_w_down, ln1_g, ln1_b, ln2_g, ln2_b, ln3_g, ln3_b, diff_lam_q1, diff_lam_k1, diff_lam_q2, diff_lam_k2, diff_subln_g, gdn_conv_w, gdn_a_log, gdn_dt_bias, gdn_norm_g, nsa_pe_k, nsa_pe_v, nsa_cmp_k_w1, nsa_cmp_k_w2, nsa_cmp_v_w1, nsa_cmp_v_w2):
    B, S, D = x.shape
    depth = w_in.shape[0]
    alpha = (2 * depth) ** 0.25
    tab = _rope_table(S)
    xf = x.reshape(B * S, D)
    for l in range(depth):
        lam_init = 0.8 - 0.6 * math.exp(-0.3 * l)
        xf = _ffn_ln(xf, ffn1_w_gu[l].astype(BF16), ffn1_w_down[l].astype(BF16), ln1_g[l], ln1_b[l], alpha)
        wm, wt = _prep_w_in(w_in[l])
        dq, dk, nq_arr, ks, kw, sq, sk, sv, hg, kcvc, gates, ht = _proj(xf, wm, wt, tab, S)
        lam_p = jnp.stack([diff_lam_q1[l], diff_lam_k1[l], diff_lam_q2[l], diff_lam_k2[l]])
        o_diff = _diff_attention(dq, dk, ht, lam_p, diff_subln_g[l], lam_init, B, S)
        o_gdn = _gdn(hg, gdn_conv_w[l], gdn_a_log[l], gdn_dt_bias[l], gdn_norm_g[l], B, S)
        o_nsa = _nsa(nq_arr, ks, kw, ht, kcvc, gates, nsa_pe_k[l], nsa_pe_v[l], nsa_cmp_k_w1[l], nsa_cmp_k_w2[l],
                     nsa_cmp_v_w1[l], nsa_cmp_v_w2[l], B, S)
        o_sb = _sb_attention(sq, sk, sv, B, S)
        xf = _outproj_ln(xf, (o_diff, o_gdn, o_nsa, o_sb), w_out[l].astype(BF16), ln2_g[l], ln2_b[l], alpha)
        xf = _ffn_ln(xf, ffn2_w_gu[l].astype(BF16), ffn2_w_down[l].astype(BF16), ln3_g[l], ln3_b[l], alpha)
    return xf.reshape(B, S, D)
```

```python
import functools
import math

import numpy as np
import jax
import jax.numpy as jnp
from jax import lax
from jax.experimental import pallas as pl
from jax.experimental.pallas import tpu as pltpu

F32 = jnp.float32
BF16 = jnp.bfloat16

DEPTH = 2
HEAD_DIM = 64
HEADS = 4
GROUP_W = HEADS * HEAD_DIM
DIFF_QK = HEAD_DIM // 2
GDN_CHUNK = 64
CMP_BLOCK, CMP_STRIDE = 32, 16
SLC_BLOCK, SLC_TOPN = 64, 16
WINDOW = 512
FORCE = 1e4
ROPE_THETA = 10000.0
LN_EPS = 1e-5
NORM_EPS = 1e-6
NEG = -1e30
LOG2E = 1.4426950408889634

LANE = 128
V7X_VMEM_BYTES = 64 * 1024 * 1024
VMEM_LIMIT = V7X_VMEM_BYTES - 8 * 1024 * 1024

TM_FFN = 512
FF_CHUNK = 256
TM_PROJ = 512
T_ATT = 1024
TQ_CMP = 256
TQ_WIN = 512
T_SEL = 1024


def _cp(sem):
    return pltpu.CompilerParams(dimension_semantics=sem, vmem_limit_bytes=VMEM_LIMIT)


def _iota(shape, dim):
    return lax.broadcasted_iota(jnp.int32, shape, dim)


def _dot(a, b):
    return jnp.dot(a, b, preferred_element_type=F32)


def _dot_nt(a, b):
    return lax.dot_general(a, b, (((1,), (1,)), ((), ())), preferred_element_type=F32)


def _split2(x):
    hi = x.astype(BF16)
    lo = (x - hi.astype(F32)).astype(BF16)
    return hi, lo


def _dot_x2(x, w):
    hi, lo = _split2(x)
    return _dot(hi, w) + _dot(lo, w)


def _dot_x3(x, w):
    hi = x.astype(BF16)
    r = x - hi.astype(F32)
    mid = r.astype(BF16)
    lo = (r - mid.astype(F32)).astype(BF16)
    return _dot(hi, w) + _dot(mid, w) + _dot(lo, w)


def _layer_norm(y, g, b):
    mu = jnp.mean(y, axis=-1, keepdims=True)
    d = y - mu
    var = jnp.mean(d * d, axis=-1, keepdims=True)
    return d * lax.rsqrt(var + LN_EPS) * g + b


def _const_spec(shape):
    nd = len(shape)
    return pl.BlockSpec(shape, lambda *_: (0,) * nd, pipeline_mode=pl.Buffered(1))


def _ffn_ln_kernel(x_ref, wgu_ref, wd_ref, g_ref, b_ref, o_ref, *, alpha, d_ff, ff_chunk):
    x = x_ref[...]
    xb = x.astype(BF16)
    acc = None
    for c in range(d_ff // ff_chunk):
        lo = c * ff_chunk
        g = _dot(xb, wgu_ref[:, lo:lo + ff_chunk])
        u = _dot(xb, wgu_ref[:, d_ff + lo:d_ff + lo + ff_chunk])
        a = (g * jax.nn.sigmoid(g) * u).astype(BF16)
        part = _dot(a, wd_ref[lo:lo + ff_chunk, :])
        acc = part if acc is None else acc + part
    o_ref[...] = _layer_norm(alpha * x + 0.5 * acc, g_ref[...], b_ref[...])


def _ffn_ln(x, w_gu, w_down, g, b, alpha):
    T, D = x.shape
    d_ff = w_down.shape[0]
    tm = min(TM_FFN, T)
    ff_chunk = FF_CHUNK if d_ff % FF_CHUNK == 0 else d_ff
    return pl.pallas_call(
        functools.partial(_ffn_ln_kernel, alpha=alpha, d_ff=d_ff, ff_chunk=ff_chunk),
        grid=(T // tm,),
        in_specs=[pl.BlockSpec((tm, D), lambda i: (i, 0)),
                  _const_spec((D, 2 * d_ff)), _const_spec((d_ff, D)),
                  _const_spec((1, D)), _const_spec((1, D))],
        out_specs=pl.BlockSpec((tm, D), lambda i: (i, 0)),
        out_shape=jax.ShapeDtypeStruct((T, D), F32),
        compiler_params=_cp(("parallel",)),
        name="ffn_ln",
    )(x, w_gu, w_down, g.reshape(1, D), b.reshape(1, D))


def _outproj_ln_kernel(x_ref, o0_ref, o1_ref, o2_ref, o3_ref, w_ref, g_ref, b_ref, out_ref, *, alpha):
    gw = o0_ref.shape[1]
    mix = None
    for k, o_ref in enumerate((o0_ref, o1_ref, o2_ref, o3_ref)):
        part = _dot(o_ref[...], w_ref[k * gw:(k + 1) * gw, :])
        mix = part if mix is None else mix + part
    out_ref[...] = _layer_norm(alpha * x_ref[...] + mix, g_ref[...], b_ref[...])


def _outproj_ln(x, outs, w_out, g, b, alpha):
    T, D = x.shape
    tm = min(TM_FFN, T)
    gw = outs[0].shape[1]
    return pl.pallas_call(
        functools.partial(_outproj_ln_kernel, alpha=alpha),
        grid=(T // tm,),
        in_specs=[pl.BlockSpec((tm, D), lambda i: (i, 0))]
        + [pl.BlockSpec((tm, gw), lambda i: (i, 0))] * 4
        + [_const_spec(w_out.shape), _const_spec((1, D)), _const_spec((1, D))],
        out_specs=pl.BlockSpec((tm, D), lambda i: (i, 0)),
        out_shape=jax.ShapeDtypeStruct((T, D), F32),
        compiler_params=_cp(("parallel",)),
        name="outproj_ln",
    )(x, *outs, w_out, g.reshape(1, D), b.reshape(1, D))


N_ROPE_BLK = 9
N_MAIN_BLK = 25


def _proj_kernel(x_ref, wm_ref, wt_ref, tab_ref,
                 dq_ref, dk_ref, nq_ref, ks_ref, kw_ref, sq_ref, sk_ref, sv_ref, hg_ref, kc_ref, gt_ref, ht_ref):
    xb = x_ref[...].astype(BF16)
    ht_ref[...] = _dot_nt(wt_ref[...], xb).astype(BF16)
    nr = N_ROPE_BLK * LANE
    h = _dot(xb, wm_ref[:, :nr])
    tab = tab_ref[...]
    lane = _iota((1, LANE), 1)
    for c in range(N_ROPE_BLK):
        t0 = 0 if c < 4 else (2 if c < 6 else 4)
        half = (DIFF_QK if c < 4 else HEAD_DIM) // 2
        cs = tab[:, t0 * LANE:(t0 + 1) * LANE]
        sn = tab[:, (t0 + 1) * LANE:(t0 + 2) * LANE]
        t = h[:, c * LANE:(c + 1) * LANE]
        rot = jnp.where(lane % (2 * half) < half, pltpu.roll(t, LANE - half, 1), pltpu.roll(t, half, 1))
        val = t * cs + rot * sn
        if c < 6:
            (dq_ref, dk_ref, nq_ref)[c // 2][:, (c % 2) * LANE:(c % 2 + 1) * LANE] = val.astype(BF16)
        elif c < 8:
            (ks_ref, kw_ref)[c - 6][...] = val.astype(BF16)
        else:
            kc_ref[...] = val
    hp = _dot(xb, wm_ref[:, 9 * LANE:15 * LANE]).astype(BF16)
    for n, ref in enumerate((sq_ref, sk_ref, sv_ref)):
        ref[...] = hp[:, n * GROUP_W:(n + 1) * GROUP_W]
    hf = _dot(xb, wm_ref[:, 15 * LANE:25 * LANE])
    hg_ref[...] = hf[:, :9 * LANE]
    gt_ref[...] = hf[:, 9 * LANE:10 * LANE]


def _proj(x, wm, wt, tab, seq_len):
    T, D = x.shape
    tm = min(TM_PROJ, seq_len)
    nst = seq_len // tm
    bf16_widths = (GROUP_W, GROUP_W, GROUP_W, LANE, LANE, GROUP_W, GROUP_W, GROUP_W)
    return pl.pallas_call(
        _proj_kernel,
        grid=(T // tm,),
        in_specs=[pl.BlockSpec((tm, D), lambda i: (i, 0)),
                  _const_spec(wm.shape), _const_spec(wt.shape),
                  pl.BlockSpec((tm, 6 * LANE), lambda i: (i % nst, 0))],
        out_specs=[pl.BlockSpec((tm, w), lambda i: (i, 0)) for w in bf16_widths]
        + [pl.BlockSpec((tm, 9 * LANE), lambda i: (i, 0)),
                   pl.BlockSpec((tm, LANE), lambda i: (i, 0)),
                   pl.BlockSpec((tm, LANE), lambda i: (i, 0)),
                   pl.BlockSpec((wt.shape[0], tm), lambda i: (0, i))],
        out_shape=[jax.ShapeDtypeStruct((T, w), BF16) for w in bf16_widths]
        + [jax.ShapeDtypeStruct((T, 9 * LANE), F32),
                   jax.ShapeDtypeStruct((T, LANE), F32),
                   jax.ShapeDtypeStruct((T, LANE), F32),
                   jax.ShapeDtypeStruct((wt.shape[0], T), BF16)],
        compiler_params=_cp(("parallel",)),
        name="in_proj",
    )(x, wm, wt, tab)


def _prep_w_in(w):
    k = w.shape[0]
    w = w.astype(BF16)
    sizes = ((HEADS * DIFF_QK,) * 4 + (GROUP_W,) + (GROUP_W,) * 4 + (HEADS,) * 2
             + (GROUP_W,) + (HEAD_DIM,) * 6 + (3 * HEADS,) + (GROUP_W,) * 3)
    offs = np.concatenate([[0], np.cumsum(sizes)])
    (dq1, dq2, dk1, dk2, dv, gq, gk, gv, gz, ga, gb,
     nq, nkc, nvc, nks, nvs, nkw, nvw, ngate, sq, sk, sv) = [w[:, offs[i]:offs[i + 1]] for i in range(len(sizes))]
    scale = HEAD_DIM ** -0.5
    pad = lambda a: jnp.concatenate([a, jnp.zeros((k, LANE - a.shape[1]), w.dtype)], axis=1)
    main = jnp.concatenate(
        [dq1, dq2, dk1, dk2, nq * scale, nks, nvs, nkw, nvw, nkc, nvc,
         sq * scale, sk, sv, gq, gk, gv, gz, pad(jnp.concatenate([ga, gb], axis=1)), pad(ngate)], axis=1)
    wt = jnp.concatenate([dv, sv, nvs, nvw], axis=1).T
    return main, wt


def _rope_table(seq_len):
    def cs(dim):
        inv = ROPE_THETA ** (-jnp.arange(0, dim, 2, dtype=F32) / dim)
        ang = jnp.arange(seq_len, dtype=F32)[:, None] * inv[None, :]
        c, sgn = jnp.cos(ang), jnp.sin(ang)
        return jnp.concatenate([c, c], axis=1), jnp.concatenate([-sgn, sgn], axis=1)
    cd, sd, cn, sn = lax.optimization_barrier(cs(DIFF_QK) + cs(HEAD_DIM))
    one = jnp.ones((seq_len, HEAD_DIM), F32)
    zero = jnp.zeros((seq_len, HEAD_DIM), F32)
    return jnp.concatenate(
        [jnp.tile(cd, (1, 4)), jnp.tile(sd, (1, 4)), jnp.tile(cn, (1, 2)), jnp.tile(sn, (1, 2)),
         cn, one, sn, zero], axis=1)


def _tri_pairs(n, descending=False):
    qi, kj = [], []
    for i in range(n):
        js = range(i, -1, -1) if descending else range(i + 1)
        for j in js:
            qi.append(i)
            kj.append(j)
    return jnp.asarray(qi, jnp.int32), jnp.asarray(kj, jnp.int32)


ONES_ROWS = 16
SKEW = 3


def _diff_kernel(qi_ref, kj_ref, q_ref, k_ref, vt_ref, lam_ref, g_ref, o_ref,
                 m_scr, acc_scr, *, lam_init):
    p = pl.program_id(1)
    i = qi_ref[p]
    j = kj_ref[p]
    tq, tk = q_ref.shape[0], k_ref.shape[0]
    c = (DIFF_QK ** -0.5) * LOG2E
    head32 = _iota((1, LANE), 1) // DIFF_QK

    @pl.when(j == 0)
    def _init():
        m_scr[...] = jnp.full(m_scr.shape, NEG, F32)
        acc_scr[...] = jnp.zeros(acc_scr.shape, F32)

    def step(masked):
        parts = [(0, tk // 2, 0, tq), (tk // 2, tk, tq // 2, tq)] if masked else [(0, tk, 0, tq)]
        chains = [(t, h, part) for part in parts for t in range(2) for h in range(HEADS)]

        def scores(t, h, part):
            k0, k1, q0, q1 = part
            kt = k_ref[k0:k1, t * LANE:(t + 1) * LANE]
            km = jnp.where(head32 == h, kt, jnp.zeros_like(kt))
            st = _dot_nt(km, q_ref[q0:q1, t * LANE:(t + 1) * LANE])
            if masked:
                shape = (k1 - k0, q1 - q0)
                st = jnp.where(_iota(shape, 0) + k0 <= _iota(shape, 1) + q0, st, NEG)
            return st

        pend = [scores(*chains[n]) for n in range(SKEW)]
        for n, (t, h, (k0, k1, q0, q1)) in enumerate(chains):
            st = pend.pop(0)
            if n + SKEW < len(chains):
                pend.append(scores(*chains[n + SKEW]))
            idx = t * HEADS + h
            m_prev = m_scr[idx:idx + 1, q0:q1]
            m_new = jnp.maximum(m_prev, jnp.max(st, axis=0, keepdims=True))
            pt = jnp.exp2((st - m_new) * c).astype(BF16)
            alpha = jnp.exp2((m_prev - m_new) * c)
            m_scr[idx:idx + 1, q0:q1] = m_new
            vh = jnp.concatenate([vt_ref[h * HEAD_DIM:(h + 1) * HEAD_DIM, k0:k1],
                                  jnp.ones((ONES_ROWS, k1 - k0), BF16)], axis=0)
            acc_scr[idx, :, q0:q1] = acc_scr[idx, :, q0:q1] * alpha + _dot(vh, pt)

    @pl.when(j < i)
    def _off():
        step(False)

    @pl.when(j == i)
    def _diag():
        step(True)
        lp = lam_ref[...]
        lam = (jnp.exp(jnp.sum(lp[0:1] * lp[1:2], axis=-1, keepdims=True))
               - jnp.exp(jnp.sum(lp[2:3] * lp[3:4], axis=-1, keepdims=True)) + lam_init)
        parts = []
        for h in range(HEADS):
            a0, a1 = acc_scr[h], acc_scr[HEADS + h]
            oh = (a0[:HEAD_DIM] / a0[HEAD_DIM:HEAD_DIM + 1] - lam * (a1[:HEAD_DIM] / a1[HEAD_DIM:HEAD_DIM + 1]))
            ms = jnp.sum(oh * oh, axis=0, keepdims=True) * (1.0 / HEAD_DIM)
            parts.append(oh * lax.rsqrt(ms + NORM_EPS))
        y = jnp.concatenate(parts, axis=0).T * g_ref[...] * (1.0 - lam_init)
        o_ref[...] = y.astype(o_ref.dtype)


def _diff_attention(dq, dk, ht, lam_p, subln_g, lam_init, batch, seq_len):
    t = min(T_ATT, seq_len)
    nq = seq_len // t
    qi, kj = _tri_pairs(nq)
    g_full = jnp.tile(subln_g, HEADS).reshape(1, GROUP_W)
    grid_spec = pltpu.PrefetchScalarGridSpec(
        num_scalar_prefetch=2,
        grid=(batch, qi.shape[0]),
        in_specs=[pl.BlockSpec((t, GROUP_W), lambda b, p, qi, kj: (b * nq + qi[p], 0)),
                  pl.BlockSpec((t, GROUP_W), lambda b, p, qi, kj: (b * nq + kj[p], 0)),
                  pl.BlockSpec((GROUP_W, t), lambda b, p, qi, kj: (0, b * nq + kj[p])),
                  pl.BlockSpec((4, DIFF_QK), lambda b, p, qi, kj: (0, 0)),
                  pl.BlockSpec((1, GROUP_W), lambda b, p, qi, kj: (0, 0))],
        out_specs=pl.BlockSpec((t, GROUP_W), lambda b, p, qi, kj: (b * nq + qi[p], 0)),
        scratch_shapes=[pltpu.VMEM((2 * HEADS, t), F32),
                        pltpu.VMEM((2 * HEADS, HEAD_DIM + ONES_ROWS, t), F32)])
    return pl.pallas_call(
        functools.partial(_diff_kernel, lam_init=lam_init),
        grid_spec=grid_spec,
        out_shape=jax.ShapeDtypeStruct((batch * seq_len, GROUP_W), BF16),
        compiler_params=_cp(("parallel", "arbitrary")),
        name="diff_attn",
    )(qi, kj, dq, dk, ht, lam_p, g_full)


SB_CUM = 256
SB_Z_MIN = -87.0


def _sb_kernel(qi_ref, kj_ref, q_ref, k_ref, v_ref, o_ref, carry_scr, acc_scr):
    p = pl.program_id(1)
    i = qi_ref[p]
    j = kj_ref[p]
    tq, tk = q_ref.shape[0], k_ref.shape[0]
    head64 = _iota((1, GROUP_W), 1) // HEAD_DIM
    cw = min(SB_CUM, tk)

    @pl.when(j == i)
    def _init():
        carry_scr[...] = jnp.zeros(carry_scr.shape, F32)
        acc_scr[...] = jnp.zeros(acc_scr.shape, F32)

    def step(masked):
        m_excl = jnp.where(_iota((cw, cw), 0) > _iota((cw, cw), 1), 1.0, 0.0).astype(BF16)
        parts = [(tq // 2, tq, tk // 2, tk), (0, tq, 0, tk // 2)] if masked else [(0, tq, 0, tk)]
        chains = [(h, part) for part in parts for h in range(HEADS)]
        zs, lbs, xs, cums, atts = {}, {}, {}, {}, {}
        pvs = {part: [] for part in parts}

        def before(part):
            q0, q1, k0, k1 = part
            shape = (q1 - q0, k1 - k0)
            return _iota(shape, 1) + k0 < _iota(shape, 0) + q0

        def st_a(n):
            h, (q0, q1, k0, k1) = chains[n]
            kt = k_ref[k0:k1, :]
            zs[n] = _dot_nt(q_ref[q0:q1, :], jnp.where(head64 == h, kt, jnp.zeros_like(kt)))

        def st_b(n):
            z = jnp.maximum(zs.pop(n), SB_Z_MIN)
            nlb = jnp.log(1.0 + jnp.exp2(z * (-LOG2E)))
            sp = nlb + z
            lbs[n] = nlb
            if masked:
                sp = jnp.where(before(chains[n][1]), sp, 0.0)
            xs[n] = sp.astype(BF16)

        def st_c(n):
            h, (q0, q1, k0, k1) = chains[n]
            x = xs.pop(n)
            blocks = []
            suffix = carry_scr[h, q0:q1, :]
            for blk in reversed(range((k1 - k0) // cw)):
                sl = slice(blk * cw, (blk + 1) * cw)
                cb = _dot(x[:, sl], m_excl) + suffix
                blocks.insert(0, cb)
                suffix = cb[:, 0:1] + x[:, blk * cw:blk * cw + 1].astype(F32)
            carry_scr[h, q0:q1, :] = suffix
            cums[n] = jnp.concatenate(blocks, axis=1)

        def st_d(n):
            att = jnp.exp2((lbs.pop(n) + cums.pop(n)) * (-LOG2E))
            if masked:
                att = jnp.where(before(chains[n][1]), att, 0.0)
            atts[n] = att.astype(BF16)

        def st_e(n):
            h, part = chains[n]
            vt = v_ref[part[2]:part[3], :]
            pvs[part].append(_dot(atts.pop(n), jnp.where(head64 == h, vt, jnp.zeros_like(vt))))

        nch = len(chains)
        st_a(0)
        st_a(1)
        for n in range(nch):
            st_b(n)
            st_c(n)
            if n >= 1:
                st_d(n - 1)
                st_e(n - 1)
                if n + 1 < nch:
                    st_a(n + 1)
        st_d(nch - 1)
        st_e(nch - 1)
        for (q0, q1, _, _), terms in pvs.items():
            acc_scr[q0:q1, :] = acc_scr[q0:q1, :] + sum(terms[1:], terms[0])

    @pl.when(j == i)
    def _diag():
        step(True)

    @pl.when(j < i)
    def _off():
        step(False)

    @pl.when(j == 0)
    def _fin():
        o_ref[...] = acc_scr[...].astype(o_ref.dtype)


def _sb_attention(sq, sk, sv, batch, seq_len):
    t = min(T_ATT, seq_len)
    nq = seq_len // t
    qi, kj = _tri_pairs(nq, descending=True)
    grid_spec = pltpu.PrefetchScalarGridSpec(
        num_scalar_prefetch=2,
        grid=(batch, qi.shape[0]),
        in_specs=[pl.BlockSpec((t, GROUP_W), lambda b, p, qi, kj: (b * nq + qi[p], 0)),
                  pl.BlockSpec((t, GROUP_W), lambda b, p, qi, kj: (b * nq + kj[p], 0)),
                  pl.BlockSpec((t, GROUP_W), lambda b, p, qi, kj: (b * nq + kj[p], 0))],
        out_specs=pl.BlockSpec((t, GROUP_W), lambda b, p, qi, kj: (b * nq + qi[p], 0)),
        scratch_shapes=[pltpu.VMEM((HEADS, t, 1), F32), pltpu.VMEM((t, GROUP_W), F32)])
    return pl.pallas_call(
        _sb_kernel,
        grid_spec=grid_spec,
        out_shape=jax.ShapeDtypeStruct((batch * seq_len, GROUP_W), BF16),
        compiler_params=_cp(("parallel", "arbitrary")),
        name="sb_attn",
    )(qi, kj, sq, sk, sv)


def _stack_heads(q):
    qf = q.astype(F32)
    lo = _iota((1, LANE), 1) < HEAD_DIM
    parts = []
    for blk in (qf[:, :LANE], qf[:, LANE:]):
        parts.append(jnp.where(lo, blk, 0.0))
        parts.append(jnp.where(lo, pltpu.roll(blk, HEAD_DIM, 1), 0.0))
    return jnp.concatenate(parts, axis=0).astype(BF16)


def _nsa_compress_kernel(r_ref, pe_ref, w1lo_ref, w1hi_ref, w2_ref, w2vt_ref, o_ref, vt_ref):
    r = r_ref[...]
    n = r.shape[0]
    y1 = _dot((r + pe_ref[0:1, :]).astype(BF16), w1lo_ref[...])
    y2 = _dot((r + pe_ref[1:2, :]).astype(BF16), w1hi_ref[...])
    hid = y1 + pltpu.roll(y2, n - 1, 0)
    act = (hid * jax.nn.sigmoid(hid)).astype(BF16)
    o_ref[...] = _dot(act, w2_ref[...]).astype(o_ref.dtype)
    vt_ref[...] = _dot_nt(w2vt_ref[...], act).astype(vt_ref.dtype)


def _nsa_compress(kcvc, pe_k, pe_v, ck_w1, ck_w2, cv_w1, cv_w2, batch, seq_len):
    nrow = seq_len // CMP_STRIDE
    width = CMP_STRIDE * LANE
    r = kcvc.reshape(batch * nrow, width)
    hid = ck_w1.shape[1]
    pe = jnp.concatenate([pe_k, pe_v], axis=1).reshape(2, width)
    zk = jnp.zeros((CMP_BLOCK, HEAD_DIM, hid), BF16)
    w1 = jnp.concatenate(
        [jnp.concatenate([ck_w1.astype(BF16).reshape(CMP_BLOCK, HEAD_DIM, hid), zk], axis=2),
         jnp.concatenate([zk, cv_w1.astype(BF16).reshape(CMP_BLOCK, HEAD_DIM, hid)], axis=2)], axis=1)
    w1 = w1.reshape(2, width, 2 * hid)
    zo = jnp.zeros((hid, HEAD_DIM), BF16)
    w2 = jnp.concatenate([jnp.concatenate([ck_w2.astype(BF16), zo], axis=1),
                          jnp.concatenate([zo, cv_w2.astype(BF16)], axis=1)], axis=0)
    return pl.pallas_call(
        _nsa_compress_kernel,
        grid=(batch,),
        in_specs=[pl.BlockSpec((nrow, width), lambda b: (b, 0)),
                  _const_spec((2, width)), _const_spec((width, 2 * hid)), _const_spec((width, 2 * hid)),
                  _const_spec((2 * hid, LANE)), _const_spec((HEAD_DIM, 2 * hid))],
        out_specs=[pl.BlockSpec((nrow, LANE), lambda b: (b, 0)),
                   pl.BlockSpec((HEAD_DIM, nrow), lambda b: (b, 0))],
        out_shape=[jax.ShapeDtypeStruct((batch * nrow, LANE), BF16),
                   jax.ShapeDtypeStruct((batch * HEAD_DIM, nrow), BF16)],
        compiler_params=_cp(("parallel",)),
        name="nsa_compress",
    )(r, pe, w1[0], w1[1], w2, w2[:, HEAD_DIM:].T)


def _nsa_cmp_kernel(q_ref, kv_ref, vt_ref, ovt_ref, ocmp_ref, bias_ref):
    i = pl.program_id(1)
    tq = q_ref.shape[0]
    ncmp = kv_ref.shape[0]
    nslc = ovt_ref.shape[0]
    qs = _stack_heads(q_ref[...])
    kv = kv_ref[...]
    vt = vt_ref[...]
    tpos = i * tq + _iota((1, tq), 1)
    cm = _iota((ncmp, 1), 0) * CMP_STRIDE + (CMP_BLOCK - 1) <= tpos
    sts = [_dot_nt(kv, qs[h * tq:(h + 1) * tq]) for h in range(HEADS)]
    psum = None
    parts = []
    for h in range(HEADS):
        st = jnp.where(cm, sts[h], NEG)
        e = jnp.exp(st - jnp.max(st, axis=0, keepdims=True))
        pr = jnp.where(cm, e * (1.0 / jnp.sum(e, axis=0, keepdims=True)), 0.0)
        parts.append(_dot(vt, pr.astype(BF16)))
        psum = pr if psum is None else psum + pr
    ocmp_ref[...] = jnp.concatenate(parts, axis=0).T
    hi, lo = _split2(psum)
    imp = _dot(ovt_ref[...], hi) + _dot(ovt_ref[...], lo)
    blk = _iota((nslc, 1), 0)
    cur = tpos // SLC_BLOCK
    work = jnp.where(blk == 0, FORCE, jnp.where(blk == cur, FORCE, jnp.where(blk == cur - 1, FORCE, imp)))
    work = jnp.where(blk <= cur, work, -FORCE)
    sel = jnp.zeros((nslc, tq), F32)
    for _ in range(min(SLC_TOPN, nslc)):
        mx = jnp.max(work, axis=0, keepdims=True)
        first = jnp.min(jnp.where(work == mx, blk, nslc), axis=0, keepdims=True)
        hit = blk == first
        sel = jnp.where(hit, 1.0, sel)
        work = jnp.where(hit, -jnp.inf, work)
    bias_ref[...] = jnp.where(sel > 0.5, 0.0, NEG)


def _nsa_cmp(nq_arr, kvcmp, vtcmp, batch, seq_len):
    tq = min(TQ_CMP, seq_len)
    nq = seq_len // tq
    ncmp = seq_len // CMP_STRIDE
    nslc = seq_len // SLC_BLOCK
    cstart = np.arange(ncmp)[None, :] * CMP_STRIDE
    sstart = np.arange(nslc)[:, None] * SLC_BLOCK
    ovt = (cstart < sstart + SLC_BLOCK) & (cstart + CMP_BLOCK - 1 >= sstart)
    ovt &= (np.arange(ncmp)[None, :] < (seq_len - CMP_BLOCK) // CMP_STRIDE + 1)
    ovt = jnp.asarray(ovt, BF16)
    return pl.pallas_call(
        _nsa_cmp_kernel,
        grid=(batch, nq),
        in_specs=[pl.BlockSpec((tq, GROUP_W), lambda b, i: (b * nq + i, 0)),
                  pl.BlockSpec((ncmp, LANE), lambda b, i: (b, 0)),
                  pl.BlockSpec((HEAD_DIM, ncmp), lambda b, i: (b, 0)),
                  pl.BlockSpec((nslc, ncmp), lambda b, i: (0, 0))],
        out_specs=[pl.BlockSpec((tq, GROUP_W), lambda b, i: (b * nq + i, 0)),
                   pl.BlockSpec((nslc, tq), lambda b, i: (0, b * nq + i))],
        out_shape=[jax.ShapeDtypeStruct((batch * seq_len, GROUP_W), F32),
                   jax.ShapeDtypeStruct((nslc, batch * seq_len), F32)],
        compiler_params=_cp(("parallel", "parallel")),
        name="nsa_cmp_select",
    )(nq_arr, kvcmp, vtcmp, ovt)


def _nsa_win_kernel(q_ref, k_ref, vt_ref, o_ref, *, window):
    i = pl.program_id(1)
    tq = q_ref.shape[0]
    span = tq + window
    base = pl.multiple_of(jnp.maximum(i * tq - window, 0), LANE)
    kv = k_ref[pl.ds(base, span), :]
    vh = jnp.concatenate([vt_ref[:, pl.ds(base, span)], jnp.ones((ONES_ROWS, span), BF16)], axis=0)
    qs = _stack_heads(q_ref[...])
    rel = (i * tq + _iota((1, tq), 1)) - (base + _iota((span, 1), 0))
    bias = jnp.where(rel >= 0, jnp.where(rel < window, 0.0, NEG), NEG)
    sts = [_dot_nt(kv, qs[h * tq:(h + 1) * tq]) for h in range(HEADS)]
    parts = []
    for h in range(HEADS):
        st = sts[h] + bias
        e = jnp.exp(st - jnp.max(st, axis=0, keepdims=True)).astype(BF16)
        r = _dot(vh, e)
        parts.append(r[:HEAD_DIM] / r[HEAD_DIM:HEAD_DIM + 1])
    o_ref[...] = jnp.concatenate(parts, axis=0).T


def _nsa_window(nq_arr, kw, ht, batch, seq_len):
    tq = min(TQ_WIN, seq_len)
    nq = seq_len // tq
    window = min(WINDOW, seq_len - tq)
    vw_row_blk = 2 * GROUP_W // HEAD_DIM + 1
    return pl.pallas_call(
        functools.partial(_nsa_win_kernel, window=window),
        grid=(batch, nq),
        in_specs=[pl.BlockSpec((tq, GROUP_W), lambda b, i: (b * nq + i, 0)),
                  pl.BlockSpec((seq_len, LANE), lambda b, i: (b, 0)),
                  pl.BlockSpec((HEAD_DIM, seq_len), lambda b, i: (vw_row_blk, b))],
        out_specs=pl.BlockSpec((tq, GROUP_W), lambda b, i: (b * nq + i, 0)),
        out_shape=jax.ShapeDtypeStruct((batch * seq_len, GROUP_W), F32),
        compiler_params=_cp(("parallel", "parallel")),
        name="nsa_window",
    )(nq_arr, kw, ht)


def _nsa_sel_kernel(qi_ref, kj_ref, q_ref, k_ref, vt_ref, bias_ref, ocmp_ref, owin_ref, gate_ref, o_ref,
                    qs_scr, m_scr, acc_scr):
    p = pl.program_id(1)
    i = qi_ref[p]
    j = kj_ref[p]
    tq, tk = q_ref.shape[0], k_ref.shape[0]

    @pl.when(j == 0)
    def _init():
        qs = _stack_heads(q_ref[...])
        for h in range(HEADS):
            qs_scr[h] = qs[h * tq:(h + 1) * tq]
        m_scr[...] = jnp.full(m_scr.shape, NEG, F32)
        acc_scr[...] = jnp.zeros(acc_scr.shape, F32)

    def step(masked):
        parts = [(0, tk // 2, 0, tq), (tk // 2, tk, tq // 2, tq)] if masked else [(0, tk, 0, tq)]
        chains = [(h, part) for part in parts for h in range(HEADS)]

        def scores(h, part):
            k0, k1, q0, q1 = part
            st = _dot_nt(k_ref[k0:k1, :], qs_scr[h, q0:q1, :])
            bias = bias_ref[k0 // SLC_BLOCK:k1 // SLC_BLOCK, q0:q1][:, None, :]
            st = (st.reshape((k1 - k0) // SLC_BLOCK, SLC_BLOCK, q1 - q0) + bias).reshape(k1 - k0, q1 - q0)
            if masked:
                shape = (k1 - k0, q1 - q0)
                st = jnp.where(_iota(shape, 0) + k0 <= _iota(shape, 1) + q0, st, NEG)
            return st

        pend = [scores(*chains[n]) for n in range(SKEW)]
        for n, (h, (k0, k1, q0, q1)) in enumerate(chains):
            st = pend.pop(0)
            if n + SKEW < len(chains):
                pend.append(scores(*chains[n + SKEW]))
            m_prev = m_scr[h:h + 1, q0:q1]
            m_new = jnp.maximum(m_prev, jnp.max(st, axis=0, keepdims=True))
            pt = jnp.exp(st - m_new).astype(BF16)
            alpha = jnp.exp(m_prev - m_new)
            m_scr[h:h + 1, q0:q1] = m_new
            vh = jnp.concatenate([vt_ref[:, k0:k1], jnp.ones((ONES_ROWS, k1 - k0), BF16)], axis=0)
            acc_scr[h, :, q0:q1] = acc_scr[h, :, q0:q1] * alpha + _dot(vh, pt)

    @pl.when(j < i)
    def _past():
        step(False)

    @pl.when(j == i)
    def _diag():
        step(True)

    @pl.when(j == i)
    def _fin():
        parts = []
        for h in range(HEADS):
            a = acc_scr[h]
            parts.append(a[:HEAD_DIM] / a[HEAD_DIM:HEAD_DIM + 1])
        osel = jnp.concatenate(parts, axis=0).T
        sig = jax.nn.sigmoid(gate_ref[...])
        grow = _iota((LANE, 1), 0)
        ghead = _iota((1, GROUP_W), 1) // HEAD_DIM
        out = None
        for br, o_br in enumerate((ocmp_ref[...], osel, owin_ref[...])):
            e_br = jnp.where(grow == 3 * ghead + br, 1.0, 0.0).astype(BF16)
            term = _dot_x3(sig, e_br) * o_br
            out = term if out is None else out + term
        o_ref[...] = out.astype(o_ref.dtype)


def _nsa_select(nq_arr, ks, ht, sel, ocmp, owin, gates, batch, seq_len):
    tq = tk = min(T_SEL, seq_len)
    nq = nk = seq_len // tq
    qi, kj = _tri_pairs(nq)
    qrow = lambda b, p, qi, kj: (b * nq + qi[p], 0)
    vs_row_blk = 2 * GROUP_W // HEAD_DIM
    grid_spec = pltpu.PrefetchScalarGridSpec(
        num_scalar_prefetch=2,
        grid=(batch, qi.shape[0]),
        in_specs=[pl.BlockSpec((tq, GROUP_W), qrow),
                  pl.BlockSpec((tk, LANE), lambda b, p, qi, kj: (b * nk + kj[p], 0)),
                  pl.BlockSpec((HEAD_DIM, tk), lambda b, p, qi, kj: (vs_row_blk, b * nk + kj[p])),
                  pl.BlockSpec((tk // SLC_BLOCK, tq), lambda b, p, qi, kj: (kj[p], b * nq + qi[p])),
                  pl.BlockSpec((tq, GROUP_W), qrow), pl.BlockSpec((tq, GROUP_W), qrow),
                  pl.BlockSpec((tq, LANE), qrow)],
        out_specs=pl.BlockSpec((tq, GROUP_W), qrow),
        scratch_shapes=[pltpu.VMEM((HEADS, tq, LANE), BF16), pltpu.VMEM((HEADS, tq), F32),
                        pltpu.VMEM((HEADS, HEAD_DIM + ONES_ROWS, tq), F32)])
    return pl.pallas_call(
        _nsa_sel_kernel,
        grid_spec=grid_spec,
        out_shape=jax.ShapeDtypeStruct((batch * seq_len, GROUP_W), BF16),
        compiler_params=_cp(("parallel", "arbitrary")),
        name="nsa_select_gate",
    )(qi, kj, nq_arr, ks, ht, sel, ocmp, owin, gates)


def _nsa(nq_arr, ks, kw, ht, kcvc, gates, pe_k, pe_v, ck_w1, ck_w2, cv_w1, cv_w2, batch, seq_len):
    kvcmp, vtcmp = _nsa_compress(kcvc, pe_k, pe_v, ck_w1, ck_w2, cv_w1, cv_w2, batch, seq_len)
    ocmp, sel = _nsa_cmp(nq_arr, kvcmp, vtcmp, batch, seq_len)
    owin = _nsa_window(nq_arr, kw, ht, batch, seq_len)
    return _nsa_select(nq_arr, ks, ht, sel, ocmp, owin, gates, batch, seq_len)


def _bd(mc, mask_bd):
    return jnp.where(mask_bd, jnp.concatenate([mc] * HEADS, axis=0), jnp.zeros((), mc.dtype))


def _mm_bd(x, mc, mask_bd):
    return _dot(x.astype(BF16), _bd(mc.astype(BF16), mask_bd))


def _gdn_prep_kernel(x_ref, ab_ref, cw_ref, alog_ref, dtb_ref,
                     u_ref, w_ref, qd_ref, in_ref, kdt_ref, gl_ref, xpad_scr):
    n = pl.program_id(0)
    C = GDN_CHUNK
    W = GROUP_W
    nb, rows = x_ref.shape[0], x_ref.shape[1]
    inst = [(b, c) for b in range(nb) for c in range(rows // C)]

    @pl.when(n == 0)
    def _():
        xpad_scr[:, 0:8, :] = jnp.zeros((nb, 8, xpad_scr.shape[2]), F32)

    cw = cw_ref[...]
    qkv_b = []
    for b in range(nb):
        x = x_ref[b]
        xpad_scr[b, 8:8 + rows, :] = x
        conv = (cw[0:1] * xpad_scr[b, 5:5 + rows, :] + cw[1:2] * xpad_scr[b, 6:6 + rows, :]
                + cw[2:3] * xpad_scr[b, 7:7 + rows, :] + cw[3:4] * x)
        xpad_scr[b, 0:8, :] = x[rows - 8:rows, :]
        qkv_b.append(conv * jax.nn.sigmoid(conv))

    r256 = _iota((W, W), 0)
    c256 = _iota((W, W), 1)
    mask_bd = (r256 // HEAD_DIM) == (c256 // HEAD_DIM)
    ones_bd = jnp.where(mask_bd, 1.0, 0.0).astype(BF16)
    eye256 = jnp.where(r256 == c256, 1.0, 0.0).astype(BF16)
    row = _iota((C, W), 0)
    jl = _iota((C, W), 1) % HEAD_DIM
    ltri = jnp.where(_iota((C, C), 1) <= _iota((C, C), 0), 1.0, 0.0).astype(BF16)
    erow = _iota((LANE, W), 0)
    ehead = _iota((LANE, W), 1) // HEAD_DIM
    e_g = jnp.where(erow == ehead, 1.0, 0.0).astype(BF16)
    e_b = jnp.where(erow == ehead + HEADS, 1.0, 0.0).astype(BF16)

    def each(f, *lists):
        return [f(*args) for args in zip(*lists)]

    sl = [slice(c * C, (c + 1) * C) for _, c in inst]
    q = [qkv_b[b][sl[k], 0:W] for k, (b, _) in enumerate(inst)]
    kk = [qkv_b[b][sl[k], W:2 * W] for k, (b, _) in enumerate(inst)]
    v = [qkv_b[b][sl[k], 2 * W:3 * W] for k, (b, _) in enumerate(inst)]
    ab = [ab_ref[b, sl[k], :] for k, (b, _) in enumerate(inst)]

    qn = each(lambda t: t * lax.rsqrt(_dot_x2(t * t, ones_bd) + NORM_EPS) * (HEAD_DIM ** -0.5), q)
    kn = each(lambda t: t * lax.rsqrt(_dot_x2(t * t, ones_bd) + NORM_EPS), kk)

    def gate(a):
        z = a + dtb_ref[...]
        return -jnp.exp(alog_ref[...]) * (jnp.maximum(z, 0.0) + jnp.log1p(jnp.exp(-jnp.abs(z))))

    g_hl = each(lambda a: _dot_x3(gate(a), e_g), ab)
    beta = each(lambda a: _dot_x3(jax.nn.sigmoid(a), e_b), ab)
    gc = each(lambda g: _dot_x3_left(ltri, g), g_hl)
    glast = each(lambda g: g[C - 1:C, :], gc)
    exp_g = each(jnp.exp, gc)
    dmat = each(lambda g: _dot_x3_left(ltri, jnp.where(row > jl, g, 0.0)), g_hl)
    decay = each(lambda d: jnp.where(jl <= row, jnp.exp(d), 0.0), dmat)

    kt4 = each(lambda t: _dot_nt(eye256, jnp.concatenate([t.astype(BF16)] * HEADS, axis=0)), kn)
    kb_mat = each(lambda t: jnp.where(mask_bd, t, 0.0).astype(BF16), kt4)
    kbeta = each(lambda t, bb: t * bb, kn, beta)
    a_c = each(lambda t, m, d: jnp.where(jl < row, _dot(t.astype(BF16), m) * d, 0.0), kbeta, kb_mat, decay)
    intra = each(lambda t, m, d: _dot(t.astype(BF16), m) * d, qn, kb_mat, decay)

    t_c = each(lambda a: jnp.where(jl == row, 1.0, 0.0) - a, a_c)
    p_c = a_c
    for _ in range(5):
        p_c = each(lambda pc: _mm_bd(pc, pc, mask_bd), p_c)
        t_c = each(lambda tc, pc: tc + _mm_bd(tc, pc, mask_bd), t_c, p_c)

    u = each(lambda tc, t, bb: _mm_bd(tc, t * bb, mask_bd), t_c, v, beta)
    w = each(lambda tc, t, e: _mm_bd(tc, t * e, mask_bd), t_c, kbeta, exp_g)
    kdt = each(lambda t, gl, g: _dot_nt(eye256, (t * jnp.exp(gl - g)).astype(BF16)), kn, glast, gc)

    for k, (b, c) in enumerate(inst):
        u_ref[b, sl[k], :] = u[k]
        w_ref[b, sl[k], :] = w[k].astype(w_ref.dtype)
        qd_ref[b, sl[k], :] = (qn[k] * exp_g[k]).astype(qd_ref.dtype)
        in_ref[b, sl[k], :] = intra[k].astype(in_ref.dtype)
        kdt_ref[b, c] = kdt[k].astype(kdt_ref.dtype)
        gl_ref[b, c] = jnp.exp(glast[k])


def _dot_x3_left(w, x):
    hi = x.astype(BF16)
    r = x - hi.astype(F32)
    mid = r.astype(BF16)
    lo = (r - mid.astype(F32)).astype(BF16)
    return _dot(w, hi) + _dot(w, mid) + _dot(w, lo)


def _gdn_scan_kernel(u_ref, w_ref, qd_ref, in_ref, kdt_ref, gl_ref, z_ref, g_ref, o_ref, s_scr):
    n = pl.program_id(0)
    C = GDN_CHUNK
    W = GROUP_W
    nb, rows = u_ref.shape[0], u_ref.shape[1]

    @pl.when(n == 0)
    def _():
        s_scr[...] = jnp.zeros(s_scr.shape, F32)

    mask_bd = (_iota((W, W), 0) // HEAD_DIM) == (_iota((W, W), 1) // HEAD_DIM)
    ones_bd = jnp.where(mask_bd, 1.0, 0.0).astype(BF16)
    s = [s_scr[b] for b in range(nb)]
    for c in range(rows // C):
        sl = slice(c * C, (c + 1) * C)
        sb = [t.astype(BF16) for t in s]
        v_new = [u_ref[b, sl, :] - _dot(w_ref[b, sl, :], sb[b]) for b in range(nb)]
        vb = [t.astype(BF16) for t in v_new]
        s = [s[b] * gl_ref[b, c] + jnp.where(mask_bd, _dot(kdt_ref[b, c], vb[b]), 0.0) for b in range(nb)]
        o = [_dot(qd_ref[b, sl, :], sb[b]) + _dot(in_ref[b, sl, :], _bd(vb[b], mask_bd)) for b in range(nb)]
        for b in range(nb):
            ms = _dot_x2(o[b] * o[b], ones_bd) * (1.0 / HEAD_DIM)
            zz = z_ref[b, sl, :]
            y = o[b] * lax.rsqrt(ms + NORM_EPS) * g_ref[...] * (zz * jax.nn.sigmoid(zz))
            o_ref[b, sl, :] = y.astype(o_ref.dtype)
    for b in range(nb):
        s_scr[b] = s[b]


GDN_PREP_CHUNKS = 8
GDN_SCAN_CHUNKS = 8


def _gdn(hg, conv_w, a_log, dt_bias, norm_g, batch, seq_len):
    C = GDN_CHUNK
    nc = seq_len // C
    W = GROUP_W
    padl = lambda a: jnp.concatenate([a, jnp.zeros((LANE - a.shape[0],), F32)]).reshape(1, LANE)
    hg3 = hg.reshape(batch, seq_len, hg.shape[-1])
    cp = math.gcd(GDN_PREP_CHUNKS, nc)
    rp = cp * C
    blkp = pl.BlockSpec((batch, rp, W), lambda n: (0, n, 0))
    u, w, qd, intra, kdt, gl = pl.pallas_call(
        _gdn_prep_kernel,
        grid=(nc // cp,),
        in_specs=[pl.BlockSpec((batch, rp, 3 * W), lambda n: (0, n, 0)),
                  pl.BlockSpec((batch, rp, LANE), lambda n: (0, n, 8)),
                  pl.BlockSpec((4, 3 * W), lambda n: (0, 0)),
                  pl.BlockSpec((1, LANE), lambda n: (0, 0)),
                  pl.BlockSpec((1, LANE), lambda n: (0, 0))],
        out_specs=[blkp] * 4
        + [pl.BlockSpec((batch, cp, W, C), lambda n: (0, n, 0, 0)),
           pl.BlockSpec((batch, cp, 1, W), lambda n: (0, n, 0, 0))],
        out_shape=[jax.ShapeDtypeStruct((batch, seq_len, W), F32), jax.ShapeDtypeStruct((batch, seq_len, W), BF16),
                   jax.ShapeDtypeStruct((batch, seq_len, W), BF16), jax.ShapeDtypeStruct((batch, seq_len, W), BF16),
                   jax.ShapeDtypeStruct((batch, nc, W, C), BF16),
                   jax.ShapeDtypeStruct((batch, nc, 1, W), F32)],
        scratch_shapes=[pltpu.VMEM((batch, 8 + rp, 3 * W), F32)],
        compiler_params=_cp(("arbitrary",)),
        name="gdn_prep",
    )(hg3, hg3, conv_w, padl(a_log), padl(dt_bias))

    cs = math.gcd(GDN_SCAN_CHUNKS, nc)
    blk = pl.BlockSpec((batch, cs * C, W), lambda n: (0, n, 0))
    out = pl.pallas_call(
        _gdn_scan_kernel,
        grid=(nc // cs,),
        in_specs=[blk, blk, blk, blk,
                  pl.BlockSpec((batch, cs, W, C), lambda n: (0, n, 0, 0)),
                  pl.BlockSpec((batch, cs, 1, W), lambda n: (0, n, 0, 0)),
                  pl.BlockSpec((batch, cs * C, W), lambda n: (0, n, 3)),
                  pl.BlockSpec((1, W), lambda n: (0, 0))],
        out_specs=blk,
        out_shape=jax.ShapeDtypeStruct((batch, seq_len, W), BF16),
        scratch_shapes=[pltpu.VMEM((batch, W, W), F32)],
        compiler_params=_cp(("arbitrary",)),
        name="gdn_scan",
    )(u, w, qd, intra, kdt, gl, hg3, jnp.tile(norm_g, HEADS).reshape(1, W))
    return out.reshape(batch * seq_len, W)


def kernel(x, w_in, w_out, ffn1_w_gu, ffn1_w_down, ffn2_w_gu, ffn2_w_down, ln1_g, ln1_b, ln2_g, ln2_b, ln3_g, ln3_b, diff_lam_q1, diff_lam_k1, diff_lam_q2, diff_lam_k2, diff_subln_g, gdn_conv_w, gdn_a_log, gdn_dt_bias, gdn_norm_g, nsa_pe_k, nsa_pe_v, nsa_cmp_k_w1, nsa_cmp_k_w2, nsa_cmp_v_w1, nsa_cmp_v_w2):
    B, S, D = x.shape
    depth = w_in.shape[0]
    alpha = (2 * depth) ** 0.25
    tab = _rope_table(S)
    xf = x.reshape(B * S, D)
    for l in range(depth):
        lam_init = 0.8 - 0.6 * math.exp(-0.3 * l)
        xf = _ffn_ln(xf, ffn1_w_gu[l].astype(BF16), ffn1_w_down[l].astype(BF16), ln1_g[l], ln1_b[l], alpha)
        wm, wt = _prep_w_in(w_in[l])
        dq, dk, nq_arr, ks, kw, sq, sk, sv, hg, kcvc, gates, ht = _proj(xf, wm, wt, tab, S)
        lam_p = jnp.stack([diff_lam_q1[l], diff_lam_k1[l], diff_lam_q2[l], diff_lam_k2[l]])
        o_diff = _diff_attention(dq, dk, ht, lam_p, diff_subln_g[l], lam_init, B, S)
        o_gdn = _gdn(hg, gdn_conv_w[l], gdn_a_log[l], gdn_dt_bias[l], gdn_norm_g[l], B, S)
        o_nsa = _nsa(nq_arr, ks, kw, ht, kcvc, gates, nsa_pe_k[l], nsa_pe_v[l], nsa_cmp_k_w1[l], nsa_cmp_k_w2[l],
                     nsa_cmp_v_w1[l], nsa_cmp_v_w2[l], B, S)
        o_sb = _sb_attention(sq, sk, sv, B, S)
        xf = _outproj_ln(xf, (o_diff, o_gdn, o_nsa, o_sb), w_out[l].astype(BF16), ln2_g[l], ln2_b[l], alpha)
        xf = _ffn_ln(xf, ffn2_w_gu[l].astype(BF16), ffn2_w_down[l].astype(BF16), ln3_g[l], ln3_b[l], alpha)
    return xf.reshape(B, S, D)
```

```python
import functools
import math

import numpy as np
import jax
import jax.numpy as jnp
from jax import lax
from jax.experimental import pallas as pl
from jax.experimental.pallas import tpu as pltpu

F32 = jnp.float32
BF16 = jnp.bfloat16

DEPTH = 2
HEAD_DIM = 64
HEADS = 4
GROUP_W = HEADS * HEAD_DIM
DIFF_QK = HEAD_DIM // 2
GDN_CHUNK = 64
CMP_BLOCK, CMP_STRIDE = 32, 16
SLC_BLOCK, SLC_TOPN = 64, 16
WINDOW = 512
FORCE = 1e4
ROPE_THETA = 10000.0
LN_EPS = 1e-5
NORM_EPS = 1e-6
NEG = -1e30
LOG2E = 1.4426950408889634

LANE = 128
V7X_VMEM_BYTES = 64 * 1024 * 1024
VMEM_LIMIT = V7X_VMEM_BYTES - 8 * 1024 * 1024

TM_FFN = 512
FF_CHUNK = 256
TM_PROJ = 512
T_ATT = 1024
TQ_CMP = 512
TQ_WIN = 512
T_SEL = 1024


def _cp(sem):
    return pltpu.CompilerParams(dimension_semantics=sem, vmem_limit_bytes=VMEM_LIMIT)


def _iota(shape, dim):
    return lax.broadcasted_iota(jnp.int32, shape, dim)


def _dot(a, b):
    return jnp.dot(a, b, preferred_element_type=F32)


def _dot_nt(a, b):
    return lax.dot_general(a, b, (((1,), (1,)), ((), ())), preferred_element_type=F32)


def _split2(x):
    hi = x.astype(BF16)
    lo = (x - hi.astype(F32)).astype(BF16)
    return hi, lo


def _dot_x2(x, w):
    hi, lo = _split2(x)
    return _dot(hi, w) + _dot(lo, w)


def _dot_x3(x, w):
    hi = x.astype(BF16)
    r = x - hi.astype(F32)
    mid = r.astype(BF16)
    lo = (r - mid.astype(F32)).astype(BF16)
    return _dot(hi, w) + _dot(mid, w) + _dot(lo, w)


def _layer_norm(y, g, b):
    mu = jnp.mean(y, axis=-1, keepdims=True)
    d = y - mu
    var = jnp.mean(d * d, axis=-1, keepdims=True)
    return d * lax.rsqrt(var + LN_EPS) * g + b


def _const_spec(shape):
    nd = len(shape)
    return pl.BlockSpec(shape, lambda *_: (0,) * nd, pipeline_mode=pl.Buffered(1))


def _ffn_ln_kernel(x_ref, wgu_ref, wd_ref, g_ref, b_ref, o_ref, *, alpha, d_ff, ff_chunk):
    x = x_ref[...]
    xb = x.astype(BF16)
    acc = None
    for c in range(d_ff // ff_chunk):
        lo = c * ff_chunk
        g = _dot(xb, wgu_ref[:, lo:lo + ff_chunk])
        u = _dot(xb, wgu_ref[:, d_ff + lo:d_ff + lo + ff_chunk])
        a = (g * jax.nn.sigmoid(g) * u).astype(BF16)
        part = _dot(a, wd_ref[lo:lo + ff_chunk, :])
        acc = part if acc is None else acc + part
    o_ref[...] = _layer_norm(alpha * x + 0.5 * acc, g_ref[...], b_ref[...])


def _ffn_ln(x, w_gu, w_down, g, b, alpha):
    T, D = x.shape
    d_ff = w_down.shape[0]
    tm = min(TM_FFN, T)
    ff_chunk = FF_CHUNK if d_ff % FF_CHUNK == 0 else d_ff
    return pl.pallas_call(
        functools.partial(_ffn_ln_kernel, alpha=alpha, d_ff=d_ff, ff_chunk=ff_chunk),
        grid=(T // tm,),
        in_specs=[pl.BlockSpec((tm, D), lambda i: (i, 0)),
                  _const_spec((D, 2 * d_ff)), _const_spec((d_ff, D)),
                  _const_spec((1, D)), _const_spec((1, D))],
        out_specs=pl.BlockSpec((tm, D), lambda i: (i, 0)),
        out_shape=jax.ShapeDtypeStruct((T, D), F32),
        compiler_params=_cp(("parallel",)),
        name="ffn_ln",
    )(x, w_gu, w_down, g.reshape(1, D), b.reshape(1, D))


def _outproj_ffn_kernel(x_ref, o0_ref, o1_ref, o2_ref, o3_ref, wo_ref, g2_ref, b2_ref,
                        wgu_ref, wd_ref, g3_ref, b3_ref, out_ref, *, alpha, d_ff, ff_chunk):
    gw = o0_ref.shape[1]
    mix = None
    for k, o_ref in enumerate((o0_ref, o1_ref, o2_ref, o3_ref)):
        part = _dot(o_ref[...], wo_ref[k * gw:(k + 1) * gw, :])
        mix = part if mix is None else mix + part
    y = _layer_norm(alpha * x_ref[...] + mix, g2_ref[...], b2_ref[...])
    yb = y.astype(BF16)
    acc = None
    for c in range(d_ff // ff_chunk):
        lo = c * ff_chunk
        g = _dot(yb, wgu_ref[:, lo:lo + ff_chunk])
        u = _dot(yb, wgu_ref[:, d_ff + lo:d_ff + lo + ff_chunk])
        a = (g * jax.nn.sigmoid(g) * u).astype(BF16)
        part = _dot(a, wd_ref[lo:lo + ff_chunk, :])
        acc = part if acc is None else acc + part
    out_ref[...] = _layer_norm(alpha * y + 0.5 * acc, g3_ref[...], b3_ref[...])


def _outproj_ffn_ln(x, outs, w_out, g2, b2, w_gu, w_down, g3, b3, alpha):
    T, D = x.shape
    d_ff = w_down.shape[0]
    tm = min(TM_FFN, T)
    gw = outs[0].shape[1]
    ff_chunk = FF_CHUNK if d_ff % FF_CHUNK == 0 else d_ff
    vec = lambda a: a.reshape(1, D)
    return pl.pallas_call(
        functools.partial(_outproj_ffn_kernel, alpha=alpha, d_ff=d_ff, ff_chunk=ff_chunk),
        grid=(T // tm,),
        in_specs=[pl.BlockSpec((tm, D), lambda i: (i, 0))]
        + [pl.BlockSpec((tm, gw), lambda i: (i, 0))] * 4
        + [_const_spec(w_out.shape), _const_spec((1, D)), _const_spec((1, D)),
           _const_spec((D, 2 * d_ff)), _const_spec((d_ff, D)), _const_spec((1, D)), _const_spec((1, D))],
        out_specs=pl.BlockSpec((tm, D), lambda i: (i, 0)),
        out_shape=jax.ShapeDtypeStruct((T, D), F32),
        compiler_params=_cp(("parallel",)),
        name="outproj_ffn_ln",
    )(x, *outs, w_out, vec(g2), vec(b2), w_gu, w_down, vec(g3), vec(b3))


N_ROPE_BLK = 9
N_MAIN_BLK = 25


def _proj_kernel(x_ref, wm_ref, wt_ref, tab_ref,
                 dq_ref, dk_ref, nq_ref, ks_ref, kw_ref, sq_ref, sk_ref, sv_ref, hg_ref, kc_ref, gt_ref, ht_ref):
    xb = x_ref[...].astype(BF16)
    ht_ref[...] = _dot_nt(wt_ref[...], xb).astype(BF16)
    nr = N_ROPE_BLK * LANE
    h = _dot(xb, wm_ref[:, :nr])
    tab = tab_ref[...]
    lane = _iota((1, LANE), 1)
    for c in range(N_ROPE_BLK):
        t0 = 0 if c < 4 else (2 if c < 6 else 4)
        half = (DIFF_QK if c < 4 else HEAD_DIM) // 2
        cs = tab[:, t0 * LANE:(t0 + 1) * LANE]
        sn = tab[:, (t0 + 1) * LANE:(t0 + 2) * LANE]
        t = h[:, c * LANE:(c + 1) * LANE]
        rot = jnp.where(lane % (2 * half) < half, pltpu.roll(t, LANE - half, 1), pltpu.roll(t, half, 1))
        val = t * cs + rot * sn
        if c < 6:
            (dq_ref, dk_ref, nq_ref)[c // 2][:, (c % 2) * LANE:(c % 2 + 1) * LANE] = val.astype(BF16)
        elif c < 8:
            (ks_ref, kw_ref)[c - 6][...] = val.astype(BF16)
        else:
            kc_ref[...] = val
    hp = _dot(xb, wm_ref[:, 9 * LANE:15 * LANE]).astype(BF16)
    for n, ref in enumerate((sq_ref, sk_ref, sv_ref)):
        ref[...] = hp[:, n * GROUP_W:(n + 1) * GROUP_W]
    hf = _dot(xb, wm_ref[:, 15 * LANE:25 * LANE])
    hg_ref[...] = hf[:, :9 * LANE]
    gt_ref[...] = hf[:, 9 * LANE:10 * LANE]


def _proj(x, wm, wt, tab, seq_len):
    T, D = x.shape
    tm = min(TM_PROJ, seq_len)
    nst = seq_len // tm
    bf16_widths = (GROUP_W, GROUP_W, GROUP_W, LANE, LANE, GROUP_W, GROUP_W, GROUP_W)
    return pl.pallas_call(
        _proj_kernel,
        grid=(T // tm,),
        in_specs=[pl.BlockSpec((tm, D), lambda i: (i, 0)),
                  _const_spec(wm.shape), _const_spec(wt.shape),
                  pl.BlockSpec((tm, 6 * LANE), lambda i: (i % nst, 0))],
        out_specs=[pl.BlockSpec((tm, w), lambda i: (i, 0)) for w in bf16_widths]
        + [pl.BlockSpec((tm, 9 * LANE), lambda i: (i, 0)),
                   pl.BlockSpec((tm, LANE), lambda i: (i, 0)),
                   pl.BlockSpec((tm, LANE), lambda i: (i, 0)),
                   pl.BlockSpec((wt.shape[0], tm), lambda i: (0, i))],
        out_shape=[jax.ShapeDtypeStruct((T, w), BF16) for w in bf16_widths]
        + [jax.ShapeDtypeStruct((T, 9 * LANE), F32),
                   jax.ShapeDtypeStruct((T, LANE), F32),
                   jax.ShapeDtypeStruct((T, LANE), F32),
                   jax.ShapeDtypeStruct((wt.shape[0], T), BF16)],
        compiler_params=_cp(("parallel",)),
        name="in_proj",
    )(x, wm, wt, tab)


def _prep_w_in(w):
    k = w.shape[0]
    w = w.astype(BF16)
    sizes = ((HEADS * DIFF_QK,) * 4 + (GROUP_W,) + (GROUP_W,) * 4 + (HEADS,) * 2
             + (GROUP_W,) + (HEAD_DIM,) * 6 + (3 * HEADS,) + (GROUP_W,) * 3)
    offs = np.concatenate([[0], np.cumsum(sizes)])
    (dq1, dq2, dk1, dk2, dv, gq, gk, gv, gz, ga, gb,
     nq, nkc, nvc, nks, nvs, nkw, nvw, ngate, sq, sk, sv) = [w[:, offs[i]:offs[i + 1]] for i in range(len(sizes))]
    scale = HEAD_DIM ** -0.5
    pad = lambda a: jnp.concatenate([a, jnp.zeros((k, LANE - a.shape[1]), w.dtype)], axis=1)
    seg = lambda first, last: w[:, offs[first]:offs[last + 1]]
    main = jnp.concatenate(
        [seg(0, 3), nq * scale, seg(14, 17), seg(12, 13),
         sq * scale, seg(20, 21), seg(5, 8), pad(seg(9, 10)), pad(ngate)], axis=1)
    wt = jnp.concatenate([dv, sv, nvs, nvw], axis=1).T
    return main, wt


def _rope_table(seq_len):
    def cs(dim):
        inv = ROPE_THETA ** (-jnp.arange(0, dim, 2, dtype=F32) / dim)
        ang = jnp.arange(seq_len, dtype=F32)[:, None] * inv[None, :]
        c, sgn = jnp.cos(ang), jnp.sin(ang)
        return jnp.concatenate([c, c], axis=1), jnp.concatenate([-sgn, sgn], axis=1)
    cd, sd, cn, sn = lax.optimization_barrier(cs(DIFF_QK) + cs(HEAD_DIM))
    one = jnp.ones((seq_len, HEAD_DIM), F32)
    zero = jnp.zeros((seq_len, HEAD_DIM), F32)
    return jnp.concatenate(
        [jnp.tile(cd, (1, 4)), jnp.tile(sd, (1, 4)), jnp.tile(cn, (1, 2)), jnp.tile(sn, (1, 2)),
         cn, one, sn, zero], axis=1)


def _tri_pairs(n, descending=False):
    qi, kj = [], []
    for i in range(n):
        js = range(i, -1, -1) if descending else range(i + 1)
        for j in js:
            qi.append(i)
            kj.append(j)
    return jnp.asarray(qi, jnp.int32), jnp.asarray(kj, jnp.int32)


ONES_ROWS = 16
SKEW = 3


def _diff_kernel(qi_ref, kj_ref, q_ref, k_ref, vt_ref, lam_ref, g_ref, o_ref,
                 m_scr, acc_scr, *, lam_init):
    p = pl.program_id(1)
    i = qi_ref[p]
    j = kj_ref[p]
    tq, tk = q_ref.shape[0], k_ref.shape[0]
    c = (DIFF_QK ** -0.5) * LOG2E
    head32 = _iota((1, LANE), 1) // DIFF_QK

    @pl.when(j == 0)
    def _init():
        m_scr[...] = jnp.full(m_scr.shape, NEG, F32)
        acc_scr[...] = jnp.zeros(acc_scr.shape, F32)

    def step(masked):
        parts = [(0, tk // 2, 0, tq), (tk // 2, tk, tq // 2, tq)] if masked else [(0, tk, 0, tq)]
        chains = [(t, h, part) for part in parts for t in range(2) for h in range(HEADS)]

        def scores(t, h, part):
            k0, k1, q0, q1 = part
            kt = k_ref[k0:k1, t * LANE:(t + 1) * LANE]
            km = jnp.where(head32 == h, kt, jnp.zeros_like(kt))
            st = _dot_nt(km, q_ref[q0:q1, t * LANE:(t + 1) * LANE])
            if masked:
                shape = (k1 - k0, q1 - q0)
                st = jnp.where(_iota(shape, 0) + k0 <= _iota(shape, 1) + q0, st, NEG)
            return st

        pend = [scores(*chains[n]) for n in range(SKEW)]
        for n, (t, h, (k0, k1, q0, q1)) in enumerate(chains):
            st = pend.pop(0)
            if n + SKEW < len(chains):
                pend.append(scores(*chains[n + SKEW]))
            idx = t * HEADS + h
            m_prev = m_scr[idx:idx + 1, q0:q1]
            m_new = jnp.maximum(m_prev, jnp.max(st, axis=0, keepdims=True))
            pt = jnp.exp2((st - m_new) * c).astype(BF16)
            alpha = jnp.exp2((m_prev - m_new) * c)
            m_scr[idx:idx + 1, q0:q1] = m_new
            vh = jnp.concatenate([vt_ref[h * HEAD_DIM:(h + 1) * HEAD_DIM, k0:k1],
                                  jnp.ones((ONES_ROWS, k1 - k0), BF16)], axis=0)
            acc_scr[idx, :, q0:q1] = acc_scr[idx, :, q0:q1] * alpha + _dot(vh, pt)

    @pl.when(j < i)
    def _off():
        step(False)

    @pl.when(j == i)
    def _diag():
        step(True)
        lp = lam_ref[...]
        lam = (jnp.exp(jnp.sum(lp[0:1] * lp[1:2], axis=-1, keepdims=True))
               - jnp.exp(jnp.sum(lp[2:3] * lp[3:4], axis=-1, keepdims=True)) + lam_init)
        parts = []
        for h in range(HEADS):
            a0, a1 = acc_scr[h], acc_scr[HEADS + h]
            oh = (a0[:HEAD_DIM] / a0[HEAD_DIM:HEAD_DIM + 1] - lam * (a1[:HEAD_DIM] / a1[HEAD_DIM:HEAD_DIM + 1]))
            ms = jnp.sum(oh * oh, axis=0, keepdims=True) * (1.0 / HEAD_DIM)
            parts.append(oh * lax.rsqrt(ms + NORM_EPS))
        y = jnp.concatenate(parts, axis=0).T * g_ref[...] * (1.0 - lam_init)
        o_ref[...] = y.astype(o_ref.dtype)


def _diff_attention(dq, dk, ht, lam_p, subln_g, lam_init, batch, seq_len):
    t = min(T_ATT, seq_len)
    nq = seq_len // t
    qi, kj = _tri_pairs(nq)
    g_full = jnp.tile(subln_g, HEADS).reshape(1, GROUP_W)
    grid_spec = pltpu.PrefetchScalarGridSpec(
        num_scalar_prefetch=2,
        grid=(batch, qi.shape[0]),
        in_specs=[pl.BlockSpec((t, GROUP_W), lambda b, p, qi, kj: (b * nq + qi[p], 0)),
                  pl.BlockSpec((t, GROUP_W), lambda b, p, qi, kj: (b * nq + kj[p], 0)),
                  pl.BlockSpec((GROUP_W, t), lambda b, p, qi, kj: (0, b * nq + kj[p])),
                  pl.BlockSpec((4, DIFF_QK), lambda b, p, qi, kj: (0, 0)),
                  pl.BlockSpec((1, GROUP_W), lambda b, p, qi, kj: (0, 0))],
        out_specs=pl.BlockSpec((t, GROUP_W), lambda b, p, qi, kj: (b * nq + qi[p], 0)),
        scratch_shapes=[pltpu.VMEM((2 * HEADS, t), F32),
                        pltpu.VMEM((2 * HEADS, HEAD_DIM + ONES_ROWS, t), F32)])
    return pl.pallas_call(
        functools.partial(_diff_kernel, lam_init=lam_init),
        grid_spec=grid_spec,
        out_shape=jax.ShapeDtypeStruct((batch * seq_len, GROUP_W), BF16),
        compiler_params=_cp(("parallel", "arbitrary")),
        name="diff_attn",
    )(qi, kj, dq, dk, ht, lam_p, g_full)


SB_CUM = 256
SB_Z_MIN = -87.0


def _sb_kernel(qi_ref, kj_ref, q_ref, k_ref, v_ref, o_ref, carry_scr, acc_scr):
    p = pl.program_id(1)
    i = qi_ref[p]
    j = kj_ref[p]
    tq, tk = q_ref.shape[0], k_ref.shape[0]
    head64 = _iota((1, GROUP_W), 1) // HEAD_DIM
    cw = min(SB_CUM, tk)

    @pl.when(j == i)
    def _init():
        carry_scr[...] = jnp.zeros(carry_scr.shape, F32)
        acc_scr[...] = jnp.zeros(acc_scr.shape, F32)

    def step(masked):
        m_excl = jnp.where(_iota((cw, cw), 0) > _iota((cw, cw), 1), 1.0, 0.0).astype(BF16)
        parts = [(tq // 2, tq, tk // 2, tk), (0, tq, 0, tk // 2)] if masked else [(0, tq, 0, tk)]
        chains = [(h, part) for part in parts for h in range(HEADS)]
        zs, lbs, xs, cums, atts = {}, {}, {}, {}, {}
        pvs = {part: [] for part in parts}

        def before(part):
            q0, q1, k0, k1 = part
            shape = (q1 - q0, k1 - k0)
            return _iota(shape, 1) + k0 < _iota(shape, 0) + q0

        def st_a(n):
            h, (q0, q1, k0, k1) = chains[n]
            kt = k_ref[k0:k1, :]
            zs[n] = _dot_nt(q_ref[q0:q1, :], jnp.where(head64 == h, kt, jnp.zeros_like(kt)))

        def st_b(n):
            z = jnp.maximum(zs.pop(n), SB_Z_MIN)
            nlb = jnp.log(1.0 + jnp.exp2(z * (-LOG2E)))
            sp = nlb + z
            lbs[n] = nlb
            if masked:
                sp = jnp.where(before(chains[n][1]), sp, 0.0)
            xs[n] = sp.astype(BF16)

        def st_c(n):
            h, (q0, q1, k0, k1) = chains[n]
            x = xs.pop(n)
            blocks = []
            suffix = carry_scr[h, q0:q1, :]
            for blk in reversed(range((k1 - k0) // cw)):
                sl = slice(blk * cw, (blk + 1) * cw)
                cb = _dot(x[:, sl], m_excl) + suffix
                blocks.insert(0, cb)
                suffix = cb[:, 0:1] + x[:, blk * cw:blk * cw + 1].astype(F32)
            carry_scr[h, q0:q1, :] = suffix
            cums[n] = jnp.concatenate(blocks, axis=1)

        def st_d(n):
            att = jnp.exp2((lbs.pop(n) + cums.pop(n)) * (-LOG2E))
            if masked:
                att = jnp.where(before(chains[n][1]), att, 0.0)
            atts[n] = att.astype(BF16)

        def st_e(n):
            h, part = chains[n]
            vt = v_ref[part[2]:part[3], :]
            pvs[part].append(_dot(atts.pop(n), jnp.where(head64 == h, vt, jnp.zeros_like(vt))))

        nch = len(chains)
        st_a(0)
        st_a(1)
        for n in range(nch):
            st_b(n)
            st_c(n)
            if n >= 1:
                st_d(n - 1)
                st_e(n - 1)
                if n + 1 < nch:
                    st_a(n + 1)
        st_d(nch - 1)
        st_e(nch - 1)
        for (q0, q1, _, _), terms in pvs.items():
            acc_scr[q0:q1, :] = acc_scr[q0:q1, :] + sum(terms[1:], terms[0])

    @pl.when(j == i)
    def _diag():
        step(True)

    @pl.when(j < i)
    def _off():
        step(False)

    @pl.when(j == 0)
    def _fin():
        o_ref[...] = acc_scr[...].astype(o_ref.dtype)


def _sb_attention(sq, sk, sv, batch, seq_len):
    t = min(T_ATT, seq_len)
    nq = seq_len // t
    qi, kj = _tri_pairs(nq, descending=True)
    grid_spec = pltpu.PrefetchScalarGridSpec(
        num_scalar_prefetch=2,
        grid=(batch, qi.shape[0]),
        in_specs=[pl.BlockSpec((t, GROUP_W), lambda b, p, qi, kj: (b * nq + qi[p], 0)),
                  pl.BlockSpec((t, GROUP_W), lambda b, p, qi, kj: (b * nq + kj[p], 0)),
                  pl.BlockSpec((t, GROUP_W), lambda b, p, qi, kj: (b * nq + kj[p], 0))],
        out_specs=pl.BlockSpec((t, GROUP_W), lambda b, p, qi, kj: (b * nq + qi[p], 0)),
        scratch_shapes=[pltpu.VMEM((HEADS, t, 1), F32), pltpu.VMEM((t, GROUP_W), F32)])
    return pl.pallas_call(
        _sb_kernel,
        grid_spec=grid_spec,
        out_shape=jax.ShapeDtypeStruct((batch * seq_len, GROUP_W), BF16),
        compiler_params=_cp(("parallel", "arbitrary")),
        name="sb_attn",
    )(qi, kj, sq, sk, sv)


def _stack_heads(q):
    qf = q.astype(F32)
    lo = _iota((1, LANE), 1) < HEAD_DIM
    parts = []
    for blk in (qf[:, :LANE], qf[:, LANE:]):
        parts.append(jnp.where(lo, blk, 0.0))
        parts.append(jnp.where(lo, pltpu.roll(blk, HEAD_DIM, 1), 0.0))
    return jnp.concatenate(parts, axis=0).astype(BF16)


def _nsa_compress_kernel(r_ref, pe_ref, w1lo_ref, w1hi_ref, w2_ref, w2vt_ref, o_ref, vt_ref):
    r = r_ref[...]
    n = r.shape[0]
    y1 = _dot((r + pe_ref[0:1, :]).astype(BF16), w1lo_ref[...])
    y2 = _dot((r + pe_ref[1:2, :]).astype(BF16), w1hi_ref[...])
    hid = y1 + pltpu.roll(y2, n - 1, 0)
    act = (hid * jax.nn.sigmoid(hid)).astype(BF16)
    o_ref[...] = _dot(act, w2_ref[...]).astype(o_ref.dtype)
    vt_ref[...] = _dot_nt(w2vt_ref[...], act).astype(vt_ref.dtype)


def _nsa_compress(kcvc, pe_k, pe_v, ck_w1, ck_w2, cv_w1, cv_w2, batch, seq_len):
    nrow = seq_len // CMP_STRIDE
    width = CMP_STRIDE * LANE
    r = kcvc.reshape(batch * nrow, width)
    hid = ck_w1.shape[1]
    pe = jnp.concatenate([pe_k, pe_v], axis=1).reshape(2, width)
    zk = jnp.zeros((CMP_BLOCK, HEAD_DIM, hid), BF16)
    w1 = jnp.concatenate(
        [jnp.concatenate([ck_w1.astype(BF16).reshape(CMP_BLOCK, HEAD_DIM, hid), zk], axis=2),
         jnp.concatenate([zk, cv_w1.astype(BF16).reshape(CMP_BLOCK, HEAD_DIM, hid)], axis=2)], axis=1)
    w1 = w1.reshape(2, width, 2 * hid)
    zo = jnp.zeros((hid, HEAD_DIM), BF16)
    w2 = jnp.concatenate([jnp.concatenate([ck_w2.astype(BF16), zo], axis=1),
                          jnp.concatenate([zo, cv_w2.astype(BF16)], axis=1)], axis=0)
    return pl.pallas_call(
        _nsa_compress_kernel,
        grid=(batch,),
        in_specs=[pl.BlockSpec((nrow, width), lambda b: (b, 0)),
                  _const_spec((2, width)), _const_spec((width, 2 * hid)), _const_spec((width, 2 * hid)),
                  _const_spec((2 * hid, LANE)), _const_spec((HEAD_DIM, 2 * hid))],
        out_specs=[pl.BlockSpec((nrow, LANE), lambda b: (b, 0)),
                   pl.BlockSpec((HEAD_DIM, nrow), lambda b: (b, 0))],
        out_shape=[jax.ShapeDtypeStruct((batch * nrow, LANE), BF16),
                   jax.ShapeDtypeStruct((batch * HEAD_DIM, nrow), BF16)],
        compiler_params=_cp(("parallel",)),
        name="nsa_compress",
    )(r, pe, w1[0], w1[1], w2, w2[:, HEAD_DIM:].T)


def _nsa_cmp_kernel(q_ref, kv_ref, vt_ref, ovt_ref, ocmp_ref, bias_ref):
    i = pl.program_id(1)
    tq = q_ref.shape[0]
    ncmp = kv_ref.shape[0]
    nslc = ovt_ref.shape[0]
    qs = _stack_heads(q_ref[...])
    kv = kv_ref[...]
    vt = vt_ref[...]
    tpos = i * tq + _iota((1, tq), 1)
    cm = _iota((ncmp, 1), 0) * CMP_STRIDE + (CMP_BLOCK - 1) <= tpos
    sts = [_dot_nt(kv, qs[h * tq:(h + 1) * tq]) for h in range(HEADS)]
    psum = None
    parts = []
    for h in range(HEADS):
        st = jnp.where(cm, sts[h], NEG)
        e = jnp.exp(st - jnp.max(st, axis=0, keepdims=True))
        pr = jnp.where(cm, e * (1.0 / jnp.sum(e, axis=0, keepdims=True)), 0.0)
        parts.append(_dot(vt, pr.astype(BF16)))
        psum = pr if psum is None else psum + pr
    ocmp_ref[...] = jnp.concatenate(parts, axis=0).T
    hi, lo = _split2(psum)
    imp = _dot(ovt_ref[...], hi) + _dot(ovt_ref[...], lo)
    blk = _iota((nslc, 1), 0)
    cur = tpos // SLC_BLOCK
    work = jnp.where(blk == 0, FORCE, jnp.where(blk == cur, FORCE, jnp.where(blk == cur - 1, FORCE, imp)))
    work = jnp.where(blk <= cur, work, -FORCE)
    sel = jnp.zeros((nslc, tq), F32)
    for _ in range(min(SLC_TOPN, nslc)):
        mx = jnp.max(work, axis=0, keepdims=True)
        first = jnp.min(jnp.where(work == mx, blk, nslc), axis=0, keepdims=True)
        hit = blk == first
        sel = jnp.where(hit, 1.0, sel)
        work = jnp.where(hit, -jnp.inf, work)
    bias_ref[...] = jnp.where(sel > 0.5, 0.0, NEG)


def _nsa_cmp(nq_arr, kvcmp, vtcmp, batch, seq_len):
    tq = min(TQ_CMP, seq_len)
    nq = seq_len // tq
    ncmp = seq_len // CMP_STRIDE
    nslc = seq_len // SLC_BLOCK
    cstart = np.arange(ncmp)[None, :] * CMP_STRIDE
    sstart = np.arange(nslc)[:, None] * SLC_BLOCK
    ovt = (cstart < sstart + SLC_BLOCK) & (cstart + CMP_BLOCK - 1 >= sstart)
    ovt &= (np.arange(ncmp)[None, :] < (seq_len - CMP_BLOCK) // CMP_STRIDE + 1)
    ovt = jnp.asarray(ovt, BF16)
    return pl.pallas_call(
        _nsa_cmp_kernel,
        grid=(batch, nq),
        in_specs=[pl.BlockSpec((tq, GROUP_W), lambda b, i: (b * nq + i, 0)),
                  pl.BlockSpec((ncmp, LANE), lambda b, i: (b, 0)),
                  pl.BlockSpec((HEAD_DIM, ncmp), lambda b, i: (b, 0)),
                  pl.BlockSpec((nslc, ncmp), lambda b, i: (0, 0))],
        out_specs=[pl.BlockSpec((tq, GROUP_W), lambda b, i: (b * nq + i, 0)),
                   pl.BlockSpec((nslc, tq), lambda b, i: (0, b * nq + i))],
        out_shape=[jax.ShapeDtypeStruct((batch * seq_len, GROUP_W), F32),
                   jax.ShapeDtypeStruct((nslc, batch * seq_len), F32)],
        compiler_params=_cp(("parallel", "parallel")),
        name="nsa_cmp_select",
    )(nq_arr, kvcmp, vtcmp, ovt)


def _nsa_win_kernel(q_ref, k_ref, vt_ref, o_ref, *, window):
    i = pl.program_id(1)
    tq = q_ref.shape[0]
    span = tq + window
    base = pl.multiple_of(jnp.maximum(i * tq - window, 0), LANE)
    kv = k_ref[pl.ds(base, span), :]
    vh = jnp.concatenate([vt_ref[:, pl.ds(base, span)], jnp.ones((ONES_ROWS, span), BF16)], axis=0)
    qs = _stack_heads(q_ref[...])
    rel = (i * tq + _iota((1, tq), 1)) - (base + _iota((span, 1), 0))
    bias = jnp.where(rel >= 0, jnp.where(rel < window, 0.0, NEG), NEG)
    sts = [_dot_nt(kv, qs[h * tq:(h + 1) * tq]) for h in range(HEADS)]
    parts = []
    for h in range(HEADS):
        st = sts[h] + bias
        e = jnp.exp(st - jnp.max(st, axis=0, keepdims=True)).astype(BF16)
        r = _dot(vh, e)
        parts.append(r[:HEAD_DIM] / r[HEAD_DIM:HEAD_DIM + 1])
    o_ref[...] = jnp.concatenate(parts, axis=0).T


def _nsa_window(nq_arr, kw, ht, batch, seq_len):
    tq = min(TQ_WIN, seq_len)
    nq = seq_len // tq
    window = min(WINDOW, seq_len - tq)
    vw_row_blk = 2 * GROUP_W // HEAD_DIM + 1
    return pl.pallas_call(
        functools.partial(_nsa_win_kernel, window=window),
        grid=(batch, nq),
        in_specs=[pl.BlockSpec((tq, GROUP_W), lambda b, i: (b * nq + i, 0)),
                  pl.BlockSpec((seq_len, LANE), lambda b, i: (b, 0)),
                  pl.BlockSpec((HEAD_DIM, seq_len), lambda b, i: (vw_row_blk, b))],
        out_specs=pl.BlockSpec((tq, GROUP_W), lambda b, i: (b * nq + i, 0)),
        out_shape=jax.ShapeDtypeStruct((batch * seq_len, GROUP_W), F32),
        compiler_params=_cp(("parallel", "parallel")),
        name="nsa_window",
    )(nq_arr, kw, ht)


def _nsa_sel_kernel(qi_ref, kj_ref, q_ref, k_ref, vt_ref, bias_ref, ocmp_ref, owin_ref, gate_ref, o_ref,
                    qs_scr, m_scr, acc_scr):
    p = pl.program_id(1)
    i = qi_ref[p]
    j = kj_ref[p]
    tq, tk = q_ref.shape[0], k_ref.shape[0]

    @pl.when(j == 0)
    def _init():
        qs = _stack_heads(q_ref[...])
        for h in range(HEADS):
            qs_scr[h] = qs[h * tq:(h + 1) * tq]
        m_scr[...] = jnp.full(m_scr.shape, NEG, F32)
        acc_scr[...] = jnp.zeros(acc_scr.shape, F32)

    def step(masked):
        parts = [(0, tk // 2, 0, tq), (tk // 2, tk, tq // 2, tq)] if masked else [(0, tk, 0, tq)]
        chains = [(h, part) for part in parts for h in range(HEADS)]

        def scores(h, part):
            k0, k1, q0, q1 = part
            st = _dot_nt(k_ref[k0:k1, :], qs_scr[h, q0:q1, :])
            bias = bias_ref[k0 // SLC_BLOCK:k1 // SLC_BLOCK, q0:q1][:, None, :]
            st = (st.reshape((k1 - k0) // SLC_BLOCK, SLC_BLOCK, q1 - q0) + bias).reshape(k1 - k0, q1 - q0)
            if masked:
                shape = (k1 - k0, q1 - q0)
                st = jnp.where(_iota(shape, 0) + k0 <= _iota(shape, 1) + q0, st, NEG)
            return st

        pend = [scores(*chains[n]) for n in range(SKEW)]
        for n, (h, (k0, k1, q0, q1)) in enumerate(chains):
            st = pend.pop(0)
            if n + SKEW < len(chains):
                pend.append(scores(*chains[n + SKEW]))
            m_prev = m_scr[h:h + 1, q0:q1]
            m_new = jnp.maximum(m_prev, jnp.max(st, axis=0, keepdims=True))
            pt = jnp.exp(st - m_new).astype(BF16)
            alpha = jnp.exp(m_prev - m_new)
            m_scr[h:h + 1, q0:q1] = m_new
            vh = jnp.concatenate([vt_ref[:, k0:k1], jnp.ones((ONES_ROWS, k1 - k0), BF16)], axis=0)
            acc_scr[h, :, q0:q1] = acc_scr[h, :, q0:q1] * alpha + _dot(vh, pt)

    @pl.when(j < i)
    def _past():
        step(False)

    @pl.when(j == i)
    def _diag():
        step(True)

    @pl.when(j == i)
    def _fin():
        parts = []
        for h in range(HEADS):
            a = acc_scr[h]
            parts.append(a[:HEAD_DIM] / a[HEAD_DIM:HEAD_DIM + 1])
        osel = jnp.concatenate(parts, axis=0).T
        sig = jax.nn.sigmoid(gate_ref[...])
        grow = _iota((LANE, 1), 0)
        ghead = _iota((1, GROUP_W), 1) // HEAD_DIM
        out = None
        for br, o_br in enumerate((ocmp_ref[...], osel, owin_ref[...])):
            e_br = jnp.where(grow == 3 * ghead + br, 1.0, 0.0).astype(BF16)
            term = _dot_x3(sig, e_br) * o_br
            out = term if out is None else out + term
        o_ref[...] = out.astype(o_ref.dtype)


def _nsa_select(nq_arr, ks, ht, sel, ocmp, owin, gates, batch, seq_len):
    tq = tk = min(T_SEL, seq_len)
    nq = nk = seq_len // tq
    qi, kj = _tri_pairs(nq)
    qrow = lambda b, p, qi, kj: (b * nq + qi[p], 0)
    vs_row_blk = 2 * GROUP_W // HEAD_DIM
    grid_spec = pltpu.PrefetchScalarGridSpec(
        num_scalar_prefetch=2,
        grid=(batch, qi.shape[0]),
        in_specs=[pl.BlockSpec((tq, GROUP_W), qrow),
                  pl.BlockSpec((tk, LANE), lambda b, p, qi, kj: (b * nk + kj[p], 0)),
                  pl.BlockSpec((HEAD_DIM, tk), lambda b, p, qi, kj: (vs_row_blk, b * nk + kj[p])),
                  pl.BlockSpec((tk // SLC_BLOCK, tq), lambda b, p, qi, kj: (kj[p], b * nq + qi[p])),
                  pl.BlockSpec((tq, GROUP_W), qrow), pl.BlockSpec((tq, GROUP_W), qrow),
                  pl.BlockSpec((tq, LANE), qrow)],
        out_specs=pl.BlockSpec((tq, GROUP_W), qrow),
        scratch_shapes=[pltpu.VMEM((HEADS, tq, LANE), BF16), pltpu.VMEM((HEADS, tq), F32),
                        pltpu.VMEM((HEADS, HEAD_DIM + ONES_ROWS, tq), F32)])
    return pl.pallas_call(
        _nsa_sel_kernel,
        grid_spec=grid_spec,
        out_shape=jax.ShapeDtypeStruct((batch * seq_len, GROUP_W), BF16),
        compiler_params=_cp(("parallel", "arbitrary")),
        name="nsa_select_gate",
    )(qi, kj, nq_arr, ks, ht, sel, ocmp, owin, gates)


def _nsa(nq_arr, ks, kw, ht, kcvc, gates, pe_k, pe_v, ck_w1, ck_w2, cv_w1, cv_w2, batch, seq_len):
    kvcmp, vtcmp = _nsa_compress(kcvc, pe_k, pe_v, ck_w1, ck_w2, cv_w1, cv_w2, batch, seq_len)
    ocmp, sel = _nsa_cmp(nq_arr, kvcmp, vtcmp, batch, seq_len)
    owin = _nsa_window(nq_arr, kw, ht, batch, seq_len)
    return _nsa_select(nq_arr, ks, ht, sel, ocmp, owin, gates, batch, seq_len)


def _bd(mc, mask_bd):
    return jnp.where(mask_bd, jnp.concatenate([mc] * HEADS, axis=0), jnp.zeros((), mc.dtype))


def _mm_bd(x, mc, mask_bd):
    return _dot(x.astype(BF16), _bd(mc.astype(BF16), mask_bd))


def _gdn_prep_kernel(x_ref, ab_ref, cw_ref, alog_ref, dtb_ref,
                     u_ref, w_ref, qd_ref, in_ref, kdt_ref, gl_ref, xpad_scr, qkv_scr):
    n = pl.program_id(0)
    C = GDN_CHUNK
    W = GROUP_W
    nb, rows = x_ref.shape[0], x_ref.shape[1]
    inst = [(b, c) for b in range(nb) for c in range(rows // C)]

    @pl.when(n == 0)
    def _():
        xpad_scr[:, 0:8, :] = jnp.zeros((nb, 8, xpad_scr.shape[2]), F32)

    cw = cw_ref[...]
    for b in range(nb):
        x = x_ref[b]
        xpad_scr[b, 8:8 + rows, :] = x
        conv = (cw[0:1] * xpad_scr[b, 5:5 + rows, :] + cw[1:2] * xpad_scr[b, 6:6 + rows, :]
                + cw[2:3] * xpad_scr[b, 7:7 + rows, :] + cw[3:4] * x)
        xpad_scr[b, 0:8, :] = x[rows - 8:rows, :]
        qkv_scr[b] = conv * jax.nn.sigmoid(conv)

    r256 = _iota((W, W), 0)
    c256 = _iota((W, W), 1)
    mask_bd = (r256 // HEAD_DIM) == (c256 // HEAD_DIM)
    ones_bd = jnp.where(mask_bd, 1.0, 0.0).astype(BF16)
    eye256 = jnp.where(r256 == c256, 1.0, 0.0).astype(BF16)
    row = _iota((C, W), 0)
    jl = _iota((C, W), 1) % HEAD_DIM
    ltri = jnp.where(_iota((C, C), 1) <= _iota((C, C), 0), 1.0, 0.0).astype(BF16)
    erow = _iota((LANE, W), 0)
    ehead = _iota((LANE, W), 1) // HEAD_DIM
    e_g = jnp.where(erow == ehead, 1.0, 0.0).astype(BF16)
    e_b = jnp.where(erow == ehead + HEADS, 1.0, 0.0).astype(BF16)

    def each(f, *lists):
        return [f(*args) for args in zip(*lists)]

    sl = [slice(c * C, (c + 1) * C) for _, c in inst]
    q = [qkv_scr[b, sl[k], 0:W] for k, (b, _) in enumerate(inst)]
    kk = [qkv_scr[b, sl[k], W:2 * W] for k, (b, _) in enumerate(inst)]
    v = [qkv_scr[b, sl[k], 2 * W:3 * W] for k, (b, _) in enumerate(inst)]
    ab = [ab_ref[b, sl[k], :] for k, (b, _) in enumerate(inst)]

    qn = each(lambda t: t * lax.rsqrt(_dot_x2(t * t, ones_bd) + NORM_EPS) * (HEAD_DIM ** -0.5), q)
    kn = each(lambda t: t * lax.rsqrt(_dot_x2(t * t, ones_bd) + NORM_EPS), kk)

    def gate(a):
        z = a + dtb_ref[...]
        return -jnp.exp(alog_ref[...]) * (jnp.maximum(z, 0.0) + jnp.log1p(jnp.exp(-jnp.abs(z))))

    g_hl = each(lambda a: _dot_x3(gate(a), e_g), ab)
    beta = each(lambda a: _dot_x3(jax.nn.sigmoid(a), e_b), ab)
    gc = each(lambda g: _dot_x3_left(ltri, g), g_hl)
    glast = each(lambda g: g[C - 1:C, :], gc)
    exp_g = each(jnp.exp, gc)
    dmat = each(lambda g: _dot_x3_left(ltri, jnp.where(row > jl, g, 0.0)), g_hl)
    decay = each(lambda d: jnp.where(jl <= row, jnp.exp(d), 0.0), dmat)

    kt4 = each(lambda t: _dot_nt(eye256, jnp.concatenate([t.astype(BF16)] * HEADS, axis=0)), kn)
    kb_mat = each(lambda t: jnp.where(mask_bd, t, 0.0).astype(BF16), kt4)
    kbeta = each(lambda t, bb: t * bb, kn, beta)
    a_c = each(lambda t, m, d: jnp.where(jl < row, _dot(t.astype(BF16), m) * d, 0.0), kbeta, kb_mat, decay)
    intra = each(lambda t, m, d: _dot(t.astype(BF16), m) * d, qn, kb_mat, decay)

    t_c = each(lambda a: jnp.where(jl == row, 1.0, 0.0) - a, a_c)
    p_c = a_c
    for _ in range(5):
        p_c = each(lambda pc: _mm_bd(pc, pc, mask_bd), p_c)
        t_c = each(lambda tc, pc: tc + _mm_bd(tc, pc, mask_bd), t_c, p_c)

    u = each(lambda tc, t, bb: _mm_bd(tc, t * bb, mask_bd), t_c, v, beta)
    w = each(lambda tc, t, e: _mm_bd(tc, t * e, mask_bd), t_c, kbeta, exp_g)
    kdt = each(lambda t, gl, g: _dot_nt(eye256, (t * jnp.exp(gl - g)).astype(BF16)), kn, glast, gc)

    for k, (b, c) in enumerate(inst):
        u_ref[b, sl[k], :] = u[k]
        w_ref[b, sl[k], :] = w[k].astype(w_ref.dtype)
        qd_ref[b, sl[k], :] = (qn[k] * exp_g[k]).astype(qd_ref.dtype)
        in_ref[b, sl[k], :] = intra[k].astype(in_ref.dtype)
        kdt_ref[b, c] = kdt[k].astype(kdt_ref.dtype)
        gl_ref[b, c] = jnp.exp(glast[k])


def _dot_x3_left(w, x):
    hi = x.astype(BF16)
    r = x - hi.astype(F32)
    mid = r.astype(BF16)
    lo = (r - mid.astype(F32)).astype(BF16)
    return _dot(w, hi) + _dot(w, mid) + _dot(w, lo)


def _gdn_scan_kernel(u_ref, w_ref, qd_ref, in_ref, kdt_ref, gl_ref, z_ref, g_ref, o_ref, s_scr):
    n = pl.program_id(0)
    C = GDN_CHUNK
    W = GROUP_W
    nb, rows = u_ref.shape[0], u_ref.shape[1]

    @pl.when(n == 0)
    def _():
        s_scr[...] = jnp.zeros(s_scr.shape, F32)

    mask_bd = (_iota((W, W), 0) // HEAD_DIM) == (_iota((W, W), 1) // HEAD_DIM)
    ones_bd = jnp.where(mask_bd, 1.0, 0.0).astype(BF16)
    s = [s_scr[b] for b in range(nb)]
    for c in range(rows // C):
        sl = slice(c * C, (c + 1) * C)
        sb = [t.astype(BF16) for t in s]
        v_new = [u_ref[b, sl, :] - _dot(w_ref[b, sl, :], sb[b]) for b in range(nb)]
        vb = [t.astype(BF16) for t in v_new]
        s = [s[b] * gl_ref[b, c] + jnp.where(mask_bd, _dot(kdt_ref[b, c], vb[b]), 0.0) for b in range(nb)]
        o = [_dot(qd_ref[b, sl, :], sb[b]) + _dot(in_ref[b, sl, :], _bd(vb[b], mask_bd)) for b in range(nb)]
        for b in range(nb):
            ms = _dot_x2(o[b] * o[b], ones_bd) * (1.0 / HEAD_DIM)
            zz = z_ref[b, sl, :]
            y = o[b] * lax.rsqrt(ms + NORM_EPS) * g_ref[...] * (zz * jax.nn.sigmoid(zz))
            o_ref[b, sl, :] = y.astype(o_ref.dtype)
    for b in range(nb):
        s_scr[b] = s[b]


GDN_PREP_CHUNKS = 8
GDN_SCAN_CHUNKS = 8


def _gdn(hg, conv_w, a_log, dt_bias, norm_g, batch, seq_len):
    C = GDN_CHUNK
    nc = seq_len // C
    W = GROUP_W
    padl = lambda a: jnp.concatenate([a, jnp.zeros((LANE - a.shape[0],), F32)]).reshape(1, LANE)
    hg3 = hg.reshape(batch, seq_len, hg.shape[-1])
    cp = math.gcd(GDN_PREP_CHUNKS, nc)
    rp = cp * C
    blkp = pl.BlockSpec((batch, rp, W), lambda n: (0, n, 0))
    u, w, qd, intra, kdt, gl = pl.pallas_call(
        _gdn_prep_kernel,
        grid=(nc // cp,),
        in_specs=[pl.BlockSpec((batch, rp, 3 * W), lambda n: (0, n, 0)),
                  pl.BlockSpec((batch, rp, LANE), lambda n: (0, n, 8)),
                  pl.BlockSpec((4, 3 * W), lambda n: (0, 0)),
                  pl.BlockSpec((1, LANE), lambda n: (0, 0)),
                  pl.BlockSpec((1, LANE), lambda n: (0, 0))],
        out_specs=[blkp] * 4
        + [pl.BlockSpec((batch, cp, W, C), lambda n: (0, n, 0, 0)),
           pl.BlockSpec((batch, cp, 1, W), lambda n: (0, n, 0, 0))],
        out_shape=[jax.ShapeDtypeStruct((batch, seq_len, W), F32), jax.ShapeDtypeStruct((batch, seq_len, W), BF16),
                   jax.ShapeDtypeStruct((batch, seq_len, W), BF16), jax.ShapeDtypeStruct((batch, seq_len, W), BF16),
                   jax.ShapeDtypeStruct((batch, nc, W, C), BF16),
                   jax.ShapeDtypeStruct((batch, nc, 1, W), F32)],
        scratch_shapes=[pltpu.VMEM((batch, 8 + rp, 3 * W), F32), pltpu.VMEM((batch, rp, 3 * W), F32)],
        compiler_params=_cp(("arbitrary",)),
        name="gdn_prep",
    )(hg3, hg3, conv_w, padl(a_log), padl(dt_bias))

    cs = math.gcd(GDN_SCAN_CHUNKS, nc)
    blk = pl.BlockSpec((batch, cs * C, W), lambda n: (0, n, 0))
    out = pl.pallas_call(
        _gdn_scan_kernel,
        grid=(nc // cs,),
        in_specs=[blk, blk, blk, blk,
                  pl.BlockSpec((batch, cs, W, C), lambda n: (0, n, 0, 0)),
                  pl.BlockSpec((batch, cs, 1, W), lambda n: (0, n, 0, 0)),
                  pl.BlockSpec((batch, cs * C, W), lambda n: (0, n, 3)),
                  pl.BlockSpec((1, W), lambda n: (0, 0))],
        out_specs=blk,
        out_shape=jax.ShapeDtypeStruct((batch, seq_len, W), BF16),
        scratch_shapes=[pltpu.VMEM((batch, W, W), F32)],
        compiler_params=_cp(("arbitrary",)),
        name="gdn_scan",
    )(u, w, qd, intra, kdt, gl, hg3, jnp.tile(norm_g, HEADS).reshape(1, W))
    return out.reshape(batch * seq_len, W)


def kernel(x, w_in, w_out, ffn1_w_gu, ffn1_w_down, ffn2_w_gu, ffn2_w_down, ln1_g, ln1_b, ln2_g, ln2_b, ln3_g, ln3_b, diff_lam_q1, diff_lam_k1, diff_lam_q2, diff_lam_k2, diff_subln_g, gdn_conv_w, gdn_a_log, gdn_dt_bias, gdn_norm_g, nsa_pe_k, nsa_pe_v, nsa_cmp_k_w1, nsa_cmp_k_w2, nsa_cmp_v_w1, nsa_cmp_v_w2):
    B, S, D = x.shape
    depth = w_in.shape[0]
    alpha = (2 * depth) ** 0.25
    tab = _rope_table(S)
    xf = x.reshape(B * S, D)
    for l in range(depth):
        lam_init = 0.8 - 0.6 * math.exp(-0.3 * l)
        xf = _ffn_ln(xf, ffn1_w_gu[l].astype(BF16), ffn1_w_down[l].astype(BF16), ln1_g[l], ln1_b[l], alpha)
        wm, wt = _prep_w_in(w_in[l])
        dq, dk, nq_arr, ks, kw, sq, sk, sv, hg, kcvc, gates, ht = _proj(xf, wm, wt, tab, S)
        lam_p = jnp.stack([diff_lam_q1[l], diff_lam_k1[l], diff_lam_q2[l], diff_lam_k2[l]])
        o_diff = _diff_attention(dq, dk, ht, lam_p, diff_subln_g[l], lam_init, B, S)
        o_gdn = _gdn(hg, gdn_conv_w[l], gdn_a_log[l], gdn_dt_bias[l], gdn_norm_g[l], B, S)
        o_nsa = _nsa(nq_arr, ks, kw, ht, kcvc, gates, nsa_pe_k[l], nsa_pe_v[l], nsa_cmp_k_w1[l], nsa_cmp_k_w2[l],
                     nsa_cmp_v_w1[l], nsa_cmp_v_w2[l], B, S)
        o_sb = _sb_attention(sq, sk, sv, B, S)
        xf = _outproj_ffn_ln(xf, (o_diff, o_gdn, o_nsa, o_sb), w_out[l].astype(BF16), ln2_g[l], ln2_b[l],
                             ffn2_w_gu[l].astype(BF16), ffn2_w_down[l].astype(BF16), ln3_g[l], ln3_b[l], alpha)
    return xf.reshape(B, S, D)
```

```python
import functools
import math

import numpy as np
import jax
import jax.numpy as jnp
from jax import lax
from jax.experimental import pallas as pl
from jax.experimental.pallas import tpu as pltpu

F32 = jnp.float32
BF16 = jnp.bfloat16

HEAD_DIM = 64
HEADS = 4
GROUP_W = HEADS * HEAD_DIM
DIFF_QK = HEAD_DIM // 2
GDN_CHUNK = 64
CMP_BLOCK, CMP_STRIDE = 32, 16
SLC_BLOCK, SLC_TOPN = 64, 16
WINDOW = 512
FORCE = 1e4
ROPE_THETA = 10000.0
LN_EPS = 1e-5
NORM_EPS = 1e-6
NEG = -1e30
LOG2E = 1.4426950408889634

LANE = 128
V7X_VMEM_BYTES = 64 * 1024 * 1024
VMEM_LIMIT = V7X_VMEM_BYTES - 8 * 1024 * 1024

TM_FFN = 512
FF_CHUNK = 256
TM_PROJ = 512
T_ATT = 1024
TQ_CMP = 512
TQ_WIN = 512
T_SEL = 1024


def _cp(sem):
    return pltpu.CompilerParams(dimension_semantics=sem, vmem_limit_bytes=VMEM_LIMIT)


def _iota(shape, dim):
    return lax.broadcasted_iota(jnp.int32, shape, dim)


def _dot(a, b):
    return jnp.dot(a, b, preferred_element_type=F32)


def _dot_nt(a, b):
    return lax.dot_general(a, b, (((1,), (1,)), ((), ())), preferred_element_type=F32)


def _split2(x):
    hi = x.astype(BF16)
    lo = (x - hi.astype(F32)).astype(BF16)
    return hi, lo


def _dot_x2(x, w):
    hi, lo = _split2(x)
    return _dot(hi, w) + _dot(lo, w)


def _dot_x3(x, w):
    hi = x.astype(BF16)
    r = x - hi.astype(F32)
    mid = r.astype(BF16)
    lo = (r - mid.astype(F32)).astype(BF16)
    return _dot(hi, w) + _dot(mid, w) + _dot(lo, w)


def _layer_norm(y, g, b):
    mu = jnp.mean(y, axis=-1, keepdims=True)
    d = y - mu
    var = jnp.mean(d * d, axis=-1, keepdims=True)
    return d * lax.rsqrt(var + LN_EPS) * g + b


def _const_spec(shape):
    nd = len(shape)
    return pl.BlockSpec(shape, lambda *_: (0,) * nd, pipeline_mode=pl.Buffered(1))


def _ffn_ln_kernel(x_ref, wgu_ref, wd_ref, g_ref, b_ref, o_ref, *, alpha, d_ff, ff_chunk):
    x = x_ref[...]
    xb = x.astype(BF16)
    acc = None
    for c in range(d_ff // ff_chunk):
        lo = c * ff_chunk
        g = _dot(xb, wgu_ref[:, lo:lo + ff_chunk])
        u = _dot(xb, wgu_ref[:, d_ff + lo:d_ff + lo + ff_chunk])
        a = (g * jax.nn.sigmoid(g) * u).astype(BF16)
        part = _dot(a, wd_ref[lo:lo + ff_chunk, :])
        acc = part if acc is None else acc + part
    o_ref[...] = _layer_norm(alpha * x + 0.5 * acc, g_ref[...], b_ref[...])


def _ffn_ln(x, w_gu, w_down, g, b, alpha):
    T, D = x.shape
    d_ff = w_down.shape[0]
    tm = min(TM_FFN, T)
    ff_chunk = FF_CHUNK if d_ff % FF_CHUNK == 0 else d_ff
    return pl.pallas_call(
        functools.partial(_ffn_ln_kernel, alpha=alpha, d_ff=d_ff, ff_chunk=ff_chunk),
        grid=(T // tm,),
        in_specs=[pl.BlockSpec((tm, D), lambda i: (i, 0)),
                  _const_spec((D, 2 * d_ff)), _const_spec((d_ff, D)),
                  _const_spec((1, D)), _const_spec((1, D))],
        out_specs=pl.BlockSpec((tm, D), lambda i: (i, 0)),
        out_shape=jax.ShapeDtypeStruct((T, D), F32),
        compiler_params=_cp(("parallel",)),
        name="ffn_ln",
    )(x, w_gu, w_down, g.reshape(1, D), b.reshape(1, D))


def _outproj_ffn_kernel(x_ref, o0_ref, o1_ref, o2_ref, o3_ref, wo_ref, g2_ref, b2_ref,
                        wgu_ref, wd_ref, g3_ref, b3_ref, out_ref, *, alpha, d_ff, ff_chunk):
    gw = o0_ref.shape[1]
    mix = None
    for k, o_ref in enumerate((o0_ref, o1_ref, o2_ref, o3_ref)):
        part = _dot(o_ref[...], wo_ref[k * gw:(k + 1) * gw, :])
        mix = part if mix is None else mix + part
    y = _layer_norm(alpha * x_ref[...] + mix, g2_ref[...], b2_ref[...])
    yb = y.astype(BF16)
    acc = None
    for c in range(d_ff // ff_chunk):
        lo = c * ff_chunk
        g = _dot(yb, wgu_ref[:, lo:lo + ff_chunk])
        u = _dot(yb, wgu_ref[:, d_ff + lo:d_ff + lo + ff_chunk])
        a = (g * jax.nn.sigmoid(g) * u).astype(BF16)
        part = _dot(a, wd_ref[lo:lo + ff_chunk, :])
        acc = part if acc is None else acc + part
    out_ref[...] = _layer_norm(alpha * y + 0.5 * acc, g3_ref[...], b3_ref[...])


def _outproj_ffn_ln(x, outs, w_out, g2, b2, w_gu, w_down, g3, b3, alpha):
    T, D = x.shape
    d_ff = w_down.shape[0]
    tm = min(TM_FFN, T)
    gw = outs[0].shape[1]
    ff_chunk = FF_CHUNK if d_ff % FF_CHUNK == 0 else d_ff
    vec = lambda a: a.reshape(1, D)
    return pl.pallas_call(
        functools.partial(_outproj_ffn_kernel, alpha=alpha, d_ff=d_ff, ff_chunk=ff_chunk),
        grid=(T // tm,),
        in_specs=[pl.BlockSpec((tm, D), lambda i: (i, 0))]
        + [pl.BlockSpec((tm, gw), lambda i: (i, 0))] * 4
        + [_const_spec(w_out.shape), _const_spec((1, D)), _const_spec((1, D)),
           _const_spec((D, 2 * d_ff)), _const_spec((d_ff, D)), _const_spec((1, D)), _const_spec((1, D))],
        out_specs=pl.BlockSpec((tm, D), lambda i: (i, 0)),
        out_shape=jax.ShapeDtypeStruct((T, D), F32),
        compiler_params=_cp(("parallel",)),
        name="outproj_ffn_ln",
    )(x, *outs, w_out, vec(g2), vec(b2), w_gu, w_down, vec(g3), vec(b3))


N_ROPE_BLK = 9


def _proj_kernel(x_ref, wm_ref, wt_ref, tab_ref,
                 dq_ref, dk_ref, nq_ref, ks_ref, kw_ref, sq_ref, sk_ref, sv_ref, hg_ref, kc_ref, gt_ref, ht_ref):
    xb = x_ref[...].astype(BF16)
    ht_ref[...] = _dot_nt(wt_ref[...], xb).astype(BF16)
    nr = N_ROPE_BLK * LANE
    h = _dot(xb, wm_ref[:, :nr])
    tab = tab_ref[...]
    lane = _iota((1, LANE), 1)
    for c in range(N_ROPE_BLK):
        t0 = 0 if c < 4 else (2 if c < 6 else 4)
        half = (DIFF_QK if c < 4 else HEAD_DIM) // 2
        cs = tab[:, t0 * LANE:(t0 + 1) * LANE]
        sn = tab[:, (t0 + 1) * LANE:(t0 + 2) * LANE]
        t = h[:, c * LANE:(c + 1) * LANE]
        rot = jnp.where(lane % (2 * half) < half, pltpu.roll(t, LANE - half, 1), pltpu.roll(t, half, 1))
        val = t * cs + rot * sn
        if c < 6:
            (dq_ref, dk_ref, nq_ref)[c // 2][:, (c % 2) * LANE:(c % 2 + 1) * LANE] = val.astype(BF16)
        elif c < 8:
            (ks_ref, kw_ref)[c - 6][...] = val.astype(BF16)
        else:
            kc_ref[...] = val
    hp = _dot(xb, wm_ref[:, 9 * LANE:15 * LANE]).astype(BF16)
    for n, ref in enumerate((sq_ref, sk_ref, sv_ref)):
        ref[...] = hp[:, n * GROUP_W:(n + 1) * GROUP_W]
    hf = _dot(xb, wm_ref[:, 15 * LANE:25 * LANE])
    hg_ref[...] = hf[:, :9 * LANE]
    gt_ref[...] = hf[:, 9 * LANE:10 * LANE]


def _proj(x, wm, wt, tab, seq_len):
    T, D = x.shape
    tm = min(TM_PROJ, seq_len)
    nst = seq_len // tm
    bf16_widths = (GROUP_W, GROUP_W, GROUP_W, LANE, LANE, GROUP_W, GROUP_W, GROUP_W)
    return pl.pallas_call(
        _proj_kernel,
        grid=(T // tm,),
        in_specs=[pl.BlockSpec((tm, D), lambda i: (i, 0)),
                  _const_spec(wm.shape), _const_spec(wt.shape),
                  pl.BlockSpec((tm, 6 * LANE), lambda i: (i % nst, 0))],
        out_specs=[pl.BlockSpec((tm, w), lambda i: (i, 0)) for w in bf16_widths]
        + [pl.BlockSpec((tm, 9 * LANE), lambda i: (i, 0)),
                   pl.BlockSpec((tm, LANE), lambda i: (i, 0)),
                   pl.BlockSpec((tm, LANE), lambda i: (i, 0)),
                   pl.BlockSpec((wt.shape[0], tm), lambda i: (0, i))],
        out_shape=[jax.ShapeDtypeStruct((T, w), BF16) for w in bf16_widths]
        + [jax.ShapeDtypeStruct((T, 9 * LANE), F32),
                   jax.ShapeDtypeStruct((T, LANE), F32),
                   jax.ShapeDtypeStruct((T, LANE), F32),
                   jax.ShapeDtypeStruct((wt.shape[0], T), BF16)],
        compiler_params=_cp(("parallel",)),
        name="in_proj",
    )(x, wm, wt, tab)


def _prep_w_in(w):
    k = w.shape[0]
    w = w.astype(BF16)
    sizes = ((HEADS * DIFF_QK,) * 4 + (GROUP_W,) + (GROUP_W,) * 4 + (HEADS,) * 2
             + (GROUP_W,) + (HEAD_DIM,) * 6 + (3 * HEADS,) + (GROUP_W,) * 3)
    offs = np.concatenate([[0], np.cumsum(sizes)])
    (dq1, dq2, dk1, dk2, dv, gq, gk, gv, gz, ga, gb,
     nq, nkc, nvc, nks, nvs, nkw, nvw, ngate, sq, sk, sv) = [w[:, offs[i]:offs[i + 1]] for i in range(len(sizes))]
    scale = HEAD_DIM ** -0.5
    pad = lambda a: jnp.concatenate([a, jnp.zeros((k, LANE - a.shape[1]), w.dtype)], axis=1)
    seg = lambda first, last: w[:, offs[first]:offs[last + 1]]
    main = jnp.concatenate(
        [seg(0, 3), nq * scale, seg(14, 17), seg(12, 13),
         sq * scale, seg(20, 21), seg(5, 8), pad(seg(9, 10)), pad(ngate)], axis=1)
    wt = jnp.concatenate([dv, sv, nvs, nvw], axis=1).T
    return main, wt


def _rope_table(seq_len):
    def cs(dim):
        inv = ROPE_THETA ** (-jnp.arange(0, dim, 2, dtype=F32) / dim)
        ang = jnp.arange(seq_len, dtype=F32)[:, None] * inv[None, :]
        c, sgn = jnp.cos(ang), jnp.sin(ang)
        return jnp.concatenate([c, c], axis=1), jnp.concatenate([-sgn, sgn], axis=1)
    cd, sd, cn, sn = lax.optimization_barrier(cs(DIFF_QK) + cs(HEAD_DIM))
    one = jnp.ones((seq_len, HEAD_DIM), F32)
    zero = jnp.zeros((seq_len, HEAD_DIM), F32)
    return jnp.concatenate(
        [jnp.tile(cd, (1, 4)), jnp.tile(sd, (1, 4)), jnp.tile(cn, (1, 2)), jnp.tile(sn, (1, 2)),
         cn, one, sn, zero], axis=1)


def _tri_pairs(n, descending=False):
    qi, kj = [], []
    for i in range(n):
        js = range(i, -1, -1) if descending else range(i + 1)
        for j in js:
            qi.append(i)
            kj.append(j)
    return jnp.asarray(qi, jnp.int32), jnp.asarray(kj, jnp.int32)


ONES_ROWS = 16
SKEW = 4


def _diff_kernel(qi_ref, kj_ref, q_ref, k_ref, vt_ref, lam_ref, g_ref, o_ref,
                 m_scr, acc_scr, *, lam_init):
    p = pl.program_id(1)
    i = qi_ref[p]
    j = kj_ref[p]
    tq, tk = q_ref.shape[0], k_ref.shape[0]
    c = (DIFF_QK ** -0.5) * LOG2E
    head32 = _iota((1, LANE), 1) // DIFF_QK

    @pl.when(j == 0)
    def _init():
        m_scr[...] = jnp.full(m_scr.shape, NEG, F32)
        acc_scr[...] = jnp.zeros(acc_scr.shape, F32)

    def step(masked):
        parts = [(0, tk // 2, 0, tq), (tk // 2, tk, tq // 2, tq)] if masked else [(0, tk, 0, tq)]
        chains = [(t, h, part) for part in parts for t in range(2) for h in range(HEADS)]

        def scores(t, h, part):
            k0, k1, q0, q1 = part
            kt = k_ref[k0:k1, t * LANE:(t + 1) * LANE]
            km = jnp.where(head32 == h, kt, jnp.zeros_like(kt))
            st = _dot_nt(km, q_ref[q0:q1, t * LANE:(t + 1) * LANE])
            if masked:
                shape = (k1 - k0, q1 - q0)
                st = jnp.where(_iota(shape, 0) + k0 <= _iota(shape, 1) + q0, st, NEG)
            return st

        pend = [scores(*chains[n]) for n in range(SKEW)]
        for n, (t, h, (k0, k1, q0, q1)) in enumerate(chains):
            st = pend.pop(0)
            if n + SKEW < len(chains):
                pend.append(scores(*chains[n + SKEW]))
            idx = t * HEADS + h
            m_prev = m_scr[idx:idx + 1, q0:q1]
            m_new = jnp.maximum(m_prev, jnp.max(st, axis=0, keepdims=True))
            pt = jnp.exp2((st - m_new) * c).astype(BF16)
            alpha = jnp.exp2((m_prev - m_new) * c)
            m_scr[idx:idx + 1, q0:q1] = m_new
            vh = jnp.concatenate([vt_ref[h * HEAD_DIM:(h + 1) * HEAD_DIM, k0:k1],
                                  jnp.ones((ONES_ROWS, k1 - k0), BF16)], axis=0)
            acc_scr[idx, :, q0:q1] = acc_scr[idx, :, q0:q1] * alpha + _dot(vh, pt)

    @pl.when(j < i)
    def _off():
        step(False)

    @pl.when(j == i)
    def _diag():
        step(True)
        lp = lam_ref[...]
        lam = (jnp.exp(jnp.sum(lp[0:1] * lp[1:2], axis=-1, keepdims=True))
               - jnp.exp(jnp.sum(lp[2:3] * lp[3:4], axis=-1, keepdims=True)) + lam_init)
        parts = []
        for h in range(HEADS):
            a0, a1 = acc_scr[h], acc_scr[HEADS + h]
            oh = (a0[:HEAD_DIM] / a0[HEAD_DIM:HEAD_DIM + 1] - lam * (a1[:HEAD_DIM] / a1[HEAD_DIM:HEAD_DIM + 1]))
            ms = jnp.sum(oh * oh, axis=0, keepdims=True) * (1.0 / HEAD_DIM)
            parts.append(oh * lax.rsqrt(ms + NORM_EPS))
        y = jnp.concatenate(parts, axis=0).T * g_ref[...] * (1.0 - lam_init)
        o_ref[...] = y.astype(o_ref.dtype)


def _diff_attention(dq, dk, ht, lam_p, subln_g, lam_init, batch, seq_len):
    t = min(T_ATT, seq_len)
    nq = seq_len // t
    qi, kj = _tri_pairs(nq)
    g_full = jnp.tile(subln_g, HEADS).reshape(1, GROUP_W)
    grid_spec = pltpu.PrefetchScalarGridSpec(
        num_scalar_prefetch=2,
        grid=(batch, qi.shape[0]),
        in_specs=[pl.BlockSpec((t, GROUP_W), lambda b, p, qi, kj: (b * nq + qi[p], 0)),
                  pl.BlockSpec((t, GROUP_W), lambda b, p, qi, kj: (b * nq + kj[p], 0)),
                  pl.BlockSpec((GROUP_W, t), lambda b, p, qi, kj: (0, b * nq + kj[p])),
                  pl.BlockSpec((4, DIFF_QK), lambda b, p, qi, kj: (0, 0)),
                  pl.BlockSpec((1, GROUP_W), lambda b, p, qi, kj: (0, 0))],
        out_specs=pl.BlockSpec((t, GROUP_W), lambda b, p, qi, kj: (b * nq + qi[p], 0)),
        scratch_shapes=[pltpu.VMEM((2 * HEADS, t), F32),
                        pltpu.VMEM((2 * HEADS, HEAD_DIM + ONES_ROWS, t), F32)])
    return pl.pallas_call(
        functools.partial(_diff_kernel, lam_init=lam_init),
        grid_spec=grid_spec,
        out_shape=jax.ShapeDtypeStruct((batch * seq_len, GROUP_W), BF16),
        compiler_params=_cp(("parallel", "arbitrary")),
        name="diff_attn",
    )(qi, kj, dq, dk, ht, lam_p, g_full)


SB_CUM = 256
SB_Z_MIN = -87.0


def _sb_kernel(qi_ref, kj_ref, q_ref, k_ref, v_ref, o_ref, carry_scr, acc_scr):
    p = pl.program_id(1)
    i = qi_ref[p]
    j = kj_ref[p]
    tq, tk = q_ref.shape[0], k_ref.shape[0]
    head64 = _iota((1, GROUP_W), 1) // HEAD_DIM
    cw = min(SB_CUM, tk)

    @pl.when(j == i)
    def _init():
        carry_scr[...] = jnp.zeros(carry_scr.shape, F32)
        acc_scr[...] = jnp.zeros(acc_scr.shape, F32)

    def step(masked):
        m_excl = jnp.where(_iota((cw, cw), 0) > _iota((cw, cw), 1), 1.0, 0.0).astype(BF16)
        parts = [(tq // 2, tq, tk // 2, tk), (0, tq, 0, tk // 2)] if masked else [(0, tq, 0, tk)]
        chains = [(h, part) for part in parts for h in range(HEADS)]
        zs, lbs, xs, cums, atts = {}, {}, {}, {}, {}
        pvs = {part: [] for part in parts}

        def before(part):
            q0, q1, k0, k1 = part
            shape = (q1 - q0, k1 - k0)
            return _iota(shape, 1) + k0 < _iota(shape, 0) + q0

        def st_a(n):
            h, (q0, q1, k0, k1) = chains[n]
            kt = k_ref[k0:k1, :]
            zs[n] = _dot_nt(q_ref[q0:q1, :], jnp.where(head64 == h, kt, jnp.zeros_like(kt)))

        def st_b(n):
            z = jnp.maximum(zs.pop(n), SB_Z_MIN)
            nlb = jnp.log(1.0 + jnp.exp2(z * (-LOG2E)))
            sp = nlb + z
            lbs[n] = nlb
            if masked:
                sp = jnp.where(before(chains[n][1]), sp, 0.0)
            xs[n] = sp.astype(BF16)

        def st_c(n):
            h, (q0, q1, k0, k1) = chains[n]
            x = xs.pop(n)
            blocks = []
            suffix = carry_scr[h, q0:q1, :]
            for blk in reversed(range((k1 - k0) // cw)):
                sl = slice(blk * cw, (blk + 1) * cw)
                cb = _dot(x[:, sl], m_excl) + suffix
                blocks.insert(0, cb)
                suffix = cb[:, 0:1] + x[:, blk * cw:blk * cw + 1].astype(F32)
            carry_scr[h, q0:q1, :] = suffix
            cums[n] = jnp.concatenate(blocks, axis=1)

        def st_d(n):
            att = jnp.exp2((lbs.pop(n) + cums.pop(n)) * (-LOG2E))
            if masked:
                att = jnp.where(before(chains[n][1]), att, 0.0)
            atts[n] = att.astype(BF16)

        def st_e(n):
            h, part = chains[n]
            vt = v_ref[part[2]:part[3], :]
            pvs[part].append(_dot(atts.pop(n), jnp.where(head64 == h, vt, jnp.zeros_like(vt))))

        nch = len(chains)
        st_a(0)
        st_a(1)
        for n in range(nch):
            st_b(n)
            st_c(n)
            if n >= 1:
                st_d(n - 1)
                st_e(n - 1)
                if n + 1 < nch:
                    st_a(n + 1)
        st_d(nch - 1)
        st_e(nch - 1)
        for (q0, q1, _, _), terms in pvs.items():
            acc_scr[q0:q1, :] = acc_scr[q0:q1, :] + sum(terms[1:], terms[0])

    @pl.when(j == i)
    def _diag():
        step(True)

    @pl.when(j < i)
    def _off():
        step(False)

    @pl.when(j == 0)
    def _fin():
        o_ref[...] = acc_scr[...].astype(o_ref.dtype)


def _sb_attention(sq, sk, sv, batch, seq_len):
    t = min(T_ATT, seq_len)
    nq = seq_len // t
    qi, kj = _tri_pairs(nq, descending=True)
    grid_spec = pltpu.PrefetchScalarGridSpec(
        num_scalar_prefetch=2,
        grid=(batch, qi.shape[0]),
        in_specs=[pl.BlockSpec((t, GROUP_W), lambda b, p, qi, kj: (b * nq + qi[p], 0)),
                  pl.BlockSpec((t, GROUP_W), lambda b, p, qi, kj: (b * nq + kj[p], 0)),
                  pl.BlockSpec((t, GROUP_W), lambda b, p, qi, kj: (b * nq + kj[p], 0))],
        out_specs=pl.BlockSpec((t, GROUP_W), lambda b, p, qi, kj: (b * nq + qi[p], 0)),
        scratch_shapes=[pltpu.VMEM((HEADS, t, 1), F32), pltpu.VMEM((t, GROUP_W), F32)])
    return pl.pallas_call(
        _sb_kernel,
        grid_spec=grid_spec,
        out_shape=jax.ShapeDtypeStruct((batch * seq_len, GROUP_W), BF16),
        compiler_params=_cp(("parallel", "arbitrary")),
        name="sb_attn",
    )(qi, kj, sq, sk, sv)


def _stack_heads(q):
    qf = q.astype(F32)
    lo = _iota((1, LANE), 1) < HEAD_DIM
    parts = []
    for blk in (qf[:, :LANE], qf[:, LANE:]):
        parts.append(jnp.where(lo, blk, 0.0))
        parts.append(jnp.where(lo, pltpu.roll(blk, HEAD_DIM, 1), 0.0))
    return jnp.concatenate(parts, axis=0).astype(BF16)


def _nsa_compress_kernel(r_ref, pe_ref, w1lo_ref, w1hi_ref, w2_ref, w2vt_ref, o_ref, vt_ref):
    r = r_ref[...]
    n = r.shape[0]
    y1 = _dot((r + pe_ref[0:1, :]).astype(BF16), w1lo_ref[...])
    y2 = _dot((r + pe_ref[1:2, :]).astype(BF16), w1hi_ref[...])
    hid = y1 + pltpu.roll(y2, n - 1, 0)
    act = (hid * jax.nn.sigmoid(hid)).astype(BF16)
    o_ref[...] = _dot(act, w2_ref[...]).astype(o_ref.dtype)
    vt_ref[...] = _dot_nt(w2vt_ref[...], act).astype(vt_ref.dtype)


def _nsa_compress(kcvc, pe_k, pe_v, ck_w1, ck_w2, cv_w1, cv_w2, batch, seq_len):
    nrow = seq_len // CMP_STRIDE
    width = CMP_STRIDE * LANE
    r = kcvc.reshape(batch * nrow, width)
    hid = ck_w1.shape[1]
    pe = jnp.concatenate([pe_k, pe_v], axis=1).reshape(2, width)
    zk = jnp.zeros((CMP_BLOCK, HEAD_DIM, hid), BF16)
    w1 = jnp.concatenate(
        [jnp.concatenate([ck_w1.astype(BF16).reshape(CMP_BLOCK, HEAD_DIM, hid), zk], axis=2),
         jnp.concatenate([zk, cv_w1.astype(BF16).reshape(CMP_BLOCK, HEAD_DIM, hid)], axis=2)], axis=1)
    w1 = w1.reshape(2, width, 2 * hid)
    zo = jnp.zeros((hid, HEAD_DIM), BF16)
    w2 = jnp.concatenate([jnp.concatenate([ck_w2.astype(BF16), zo], axis=1),
                          jnp.concatenate([zo, cv_w2.astype(BF16)], axis=1)], axis=0)
    return pl.pallas_call(
        _nsa_compress_kernel,
        grid=(batch,),
        in_specs=[pl.BlockSpec((nrow, width), lambda b: (b, 0)),
                  _const_spec((2, width)), _const_spec((width, 2 * hid)), _const_spec((width, 2 * hid)),
                  _const_spec((2 * hid, LANE)), _const_spec((HEAD_DIM, 2 * hid))],
        out_specs=[pl.BlockSpec((nrow, LANE), lambda b: (b, 0)),
                   pl.BlockSpec((HEAD_DIM, nrow), lambda b: (b, 0))],
        out_shape=[jax.ShapeDtypeStruct((batch * nrow, LANE), BF16),
                   jax.ShapeDtypeStruct((batch * HEAD_DIM, nrow), BF16)],
        compiler_params=_cp(("parallel",)),
        name="nsa_compress",
    )(r, pe, w1[0], w1[1], w2, w2[:, HEAD_DIM:].T)


def _nsa_cmp_kernel(q_ref, kv_ref, vt_ref, ovt_ref, ocmp_ref, bias_ref):
    i = pl.program_id(1)
    tq = q_ref.shape[0]
    ncmp = kv_ref.shape[0]
    nslc = ovt_ref.shape[0]
    qs = _stack_heads(q_ref[...])
    kv = kv_ref[...]
    vt = vt_ref[...]
    tpos = i * tq + _iota((1, tq), 1)
    cm = _iota((ncmp, 1), 0) * CMP_STRIDE + (CMP_BLOCK - 1) <= tpos
    sts = [_dot_nt(kv, qs[h * tq:(h + 1) * tq]) for h in range(HEADS)]
    psum = None
    parts = []
    for h in range(HEADS):
        st = jnp.where(cm, sts[h], NEG)
        e = jnp.exp(st - jnp.max(st, axis=0, keepdims=True))
        pr = jnp.where(cm, e * (1.0 / jnp.sum(e, axis=0, keepdims=True)), 0.0)
        parts.append(_dot(vt, pr.astype(BF16)))
        psum = pr if psum is None else psum + pr
    ocmp_ref[...] = jnp.concatenate(parts, axis=0).T
    hi, lo = _split2(psum)
    imp = _dot(ovt_ref[...], hi) + _dot(ovt_ref[...], lo)
    blk = _iota((nslc, 1), 0)
    cur = tpos // SLC_BLOCK
    work = jnp.where(blk == 0, FORCE, jnp.where(blk == cur, FORCE, jnp.where(blk == cur - 1, FORCE, imp)))
    work = jnp.where(blk <= cur, work, -FORCE)
    sel = jnp.zeros((nslc, tq), F32)
    for _ in range(min(SLC_TOPN, nslc)):
        mx = jnp.max(work, axis=0, keepdims=True)
        first = jnp.min(jnp.where(work == mx, blk, nslc), axis=0, keepdims=True)
        hit = blk == first
        sel = jnp.where(hit, 1.0, sel)
        work = jnp.where(hit, -jnp.inf, work)
    bias_ref[...] = jnp.where(sel > 0.5, 0.0, NEG)


def _nsa_cmp(nq_arr, kvcmp, vtcmp, batch, seq_len):
    tq = min(TQ_CMP, seq_len)
    nq = seq_len // tq
    ncmp = seq_len // CMP_STRIDE
    nslc = seq_len // SLC_BLOCK
    cstart = np.arange(ncmp)[None, :] * CMP_STRIDE
    sstart = np.arange(nslc)[:, None] * SLC_BLOCK
    ovt = (cstart < sstart + SLC_BLOCK) & (cstart + CMP_BLOCK - 1 >= sstart)
    ovt &= (np.arange(ncmp)[None, :] < (seq_len - CMP_BLOCK) // CMP_STRIDE + 1)
    ovt = jnp.asarray(ovt, BF16)
    return pl.pallas_call(
        _nsa_cmp_kernel,
        grid=(batch, nq),
        in_specs=[pl.BlockSpec((tq, GROUP_W), lambda b, i: (b * nq + i, 0)),
                  pl.BlockSpec((ncmp, LANE), lambda b, i: (b, 0)),
                  pl.BlockSpec((HEAD_DIM, ncmp), lambda b, i: (b, 0)),
                  pl.BlockSpec((nslc, ncmp), lambda b, i: (0, 0))],
        out_specs=[pl.BlockSpec((tq, GROUP_W), lambda b, i: (b * nq + i, 0)),
                   pl.BlockSpec((nslc, tq), lambda b, i: (0, b * nq + i))],
        out_shape=[jax.ShapeDtypeStruct((batch * seq_len, GROUP_W), F32),
                   jax.ShapeDtypeStruct((nslc, batch * seq_len), F32)],
        compiler_params=_cp(("parallel", "parallel")),
        name="nsa_cmp_select",
    )(nq_arr, kvcmp, vtcmp, ovt)


def _nsa_win_kernel(q_ref, k_ref, vt_ref, o_ref, *, window):
    i = pl.program_id(1)
    tq = q_ref.shape[0]
    span = tq + window
    base = pl.multiple_of(jnp.maximum(i * tq - window, 0), LANE)
    kv = k_ref[pl.ds(base, span), :]
    vh = jnp.concatenate([vt_ref[:, pl.ds(base, span)], jnp.ones((ONES_ROWS, span), BF16)], axis=0)
    qs = _stack_heads(q_ref[...])
    rel = (i * tq + _iota((1, tq), 1)) - (base + _iota((span, 1), 0))
    bias = jnp.where(rel >= 0, jnp.where(rel < window, 0.0, NEG), NEG)
    sts = [_dot_nt(kv, qs[h * tq:(h + 1) * tq]) for h in range(HEADS)]
    parts = []
    for h in range(HEADS):
        st = sts[h] + bias
        e = jnp.exp(st - jnp.max(st, axis=0, keepdims=True)).astype(BF16)
        r = _dot(vh, e)
        parts.append(r[:HEAD_DIM] / r[HEAD_DIM:HEAD_DIM + 1])
    o_ref[...] = jnp.concatenate(parts, axis=0).T


def _nsa_window(nq_arr, kw, ht, batch, seq_len):
    tq = min(TQ_WIN, seq_len)
    nq = seq_len // tq
    window = min(WINDOW, seq_len - tq)
    vw_row_blk = 2 * GROUP_W // HEAD_DIM + 1
    return pl.pallas_call(
        functools.partial(_nsa_win_kernel, window=window),
        grid=(batch, nq),
        in_specs=[pl.BlockSpec((tq, GROUP_W), lambda b, i: (b * nq + i, 0)),
                  pl.BlockSpec((seq_len, LANE), lambda b, i: (b, 0)),
                  pl.BlockSpec((HEAD_DIM, seq_len), lambda b, i: (vw_row_blk, b))],
        out_specs=pl.BlockSpec((tq, GROUP_W), lambda b, i: (b * nq + i, 0)),
        out_shape=jax.ShapeDtypeStruct((batch * seq_len, GROUP_W), F32),
        compiler_params=_cp(("parallel", "parallel")),
        name="nsa_window",
    )(nq_arr, kw, ht)


def _nsa_sel_kernel(qi_ref, kj_ref, q_ref, k_ref, vt_ref, bias_ref, ocmp_ref, owin_ref, gate_ref, o_ref,
                    qs_scr, m_scr, acc_scr):
    p = pl.program_id(1)
    i = qi_ref[p]
    j = kj_ref[p]
    tq, tk = q_ref.shape[0], k_ref.shape[0]

    @pl.when(j == 0)
    def _init():
        qs = _stack_heads(q_ref[...])
        for h in range(HEADS):
            qs_scr[h] = qs[h * tq:(h + 1) * tq]
        m_scr[...] = jnp.full(m_scr.shape, NEG, F32)
        acc_scr[...] = jnp.zeros(acc_scr.shape, F32)

    def step(masked):
        parts = [(0, tk // 2, 0, tq), (tk // 2, tk, tq // 2, tq)] if masked else [(0, tk, 0, tq)]
        chains = [(h, part) for part in parts for h in range(HEADS)]

        def scores(h, part):
            k0, k1, q0, q1 = part
            st = _dot_nt(k_ref[k0:k1, :], qs_scr[h, q0:q1, :])
            bias = bias_ref[k0 // SLC_BLOCK:k1 // SLC_BLOCK, q0:q1][:, None, :]
            st = (st.reshape((k1 - k0) // SLC_BLOCK, SLC_BLOCK, q1 - q0) + bias).reshape(k1 - k0, q1 - q0)
            if masked:
                shape = (k1 - k0, q1 - q0)
                st = jnp.where(_iota(shape, 0) + k0 <= _iota(shape, 1) + q0, st, NEG)
            return st

        pend = [scores(*chains[n]) for n in range(SKEW)]
        for n, (h, (k0, k1, q0, q1)) in enumerate(chains):
            st = pend.pop(0)
            if n + SKEW < len(chains):
                pend.append(scores(*chains[n + SKEW]))
            m_prev = m_scr[h:h + 1, q0:q1]
            m_new = jnp.maximum(m_prev, jnp.max(st, axis=0, keepdims=True))
            pt = jnp.exp(st - m_new).astype(BF16)
            alpha = jnp.exp(m_prev - m_new)
            m_scr[h:h + 1, q0:q1] = m_new
            vh = jnp.concatenate([vt_ref[:, k0:k1], jnp.ones((ONES_ROWS, k1 - k0), BF16)], axis=0)
            acc_scr[h, :, q0:q1] = acc_scr[h, :, q0:q1] * alpha + _dot(vh, pt)

    @pl.when(j < i)
    def _past():
        step(False)

    @pl.when(j == i)
    def _diag():
        step(True)

    @pl.when(j == i)
    def _fin():
        parts = []
        for h in range(HEADS):
            a = acc_scr[h]
            parts.append(a[:HEAD_DIM] / a[HEAD_DIM:HEAD_DIM + 1])
        osel = jnp.concatenate(parts, axis=0).T
        sig = jax.nn.sigmoid(gate_ref[...])
        grow = _iota((LANE, 1), 0)
        ghead = _iota((1, GROUP_W), 1) // HEAD_DIM
        out = None
        for br, o_br in enumerate((ocmp_ref[...], osel, owin_ref[...])):
            e_br = jnp.where(grow == 3 * ghead + br, 1.0, 0.0).astype(BF16)
            term = _dot_x3(sig, e_br) * o_br
            out = term if out is None else out + term
        o_ref[...] = out.astype(o_ref.dtype)


def _nsa_select(nq_arr, ks, ht, sel, ocmp, owin, gates, batch, seq_len):
    tq = tk = min(T_SEL, seq_len)
    nq = nk = seq_len // tq
    qi, kj = _tri_pairs(nq)
    qrow = lambda b, p, qi, kj: (b * nq + qi[p], 0)
    vs_row_blk = 2 * GROUP_W // HEAD_DIM
    grid_spec = pltpu.PrefetchScalarGridSpec(
        num_scalar_prefetch=2,
        grid=(batch, qi.shape[0]),
        in_specs=[pl.BlockSpec((tq, GROUP_W), qrow),
                  pl.BlockSpec((tk, LANE), lambda b, p, qi, kj: (b * nk + kj[p], 0)),
                  pl.BlockSpec((HEAD_DIM, tk), lambda b, p, qi, kj: (vs_row_blk, b * nk + kj[p])),
                  pl.BlockSpec((tk // SLC_BLOCK, tq), lambda b, p, qi, kj: (kj[p], b * nq + qi[p])),
                  pl.BlockSpec((tq, GROUP_W), qrow), pl.BlockSpec((tq, GROUP_W), qrow),
                  pl.BlockSpec((tq, LANE), qrow)],
        out_specs=pl.BlockSpec((tq, GROUP_W), qrow),
        scratch_shapes=[pltpu.VMEM((HEADS, tq, LANE), BF16), pltpu.VMEM((HEADS, tq), F32),
                        pltpu.VMEM((HEADS, HEAD_DIM + ONES_ROWS, tq), F32)])
    return pl.pallas_call(
        _nsa_sel_kernel,
        grid_spec=grid_spec,
        out_shape=jax.ShapeDtypeStruct((batch * seq_len, GROUP_W), BF16),
        compiler_params=_cp(("parallel", "arbitrary")),
        name="nsa_select_gate",
    )(qi, kj, nq_arr, ks, ht, sel, ocmp, owin, gates)


def _nsa(nq_arr, ks, kw, ht, kcvc, gates, pe_k, pe_v, ck_w1, ck_w2, cv_w1, cv_w2, batch, seq_len):
    kvcmp, vtcmp = _nsa_compress(kcvc, pe_k, pe_v, ck_w1, ck_w2, cv_w1, cv_w2, batch, seq_len)
    ocmp, sel = _nsa_cmp(nq_arr, kvcmp, vtcmp, batch, seq_len)
    owin = _nsa_window(nq_arr, kw, ht, batch, seq_len)
    return _nsa_select(nq_arr, ks, ht, sel, ocmp, owin, gates, batch, seq_len)


def _bd(mc, mask_bd):
    return jnp.where(mask_bd, jnp.concatenate([mc] * HEADS, axis=0), jnp.zeros((), mc.dtype))


def _mm_bd(x, mc, mask_bd):
    return _dot(x.astype(BF16), _bd(mc.astype(BF16), mask_bd))


def _gdn_prep_kernel(x_ref, ab_ref, cw_ref, alog_ref, dtb_ref,
                     u_ref, w_ref, qd_ref, in_ref, kdt_ref, gl_ref, xpad_scr, qkv_scr):
    n = pl.program_id(0)
    C = GDN_CHUNK
    W = GROUP_W
    nb, rows = x_ref.shape[0], x_ref.shape[1]
    inst = [(b, c) for b in range(nb) for c in range(rows // C)]

    @pl.when(n == 0)
    def _():
        xpad_scr[:, 0:8, :] = jnp.zeros((nb, 8, xpad_scr.shape[2]), F32)

    cw = cw_ref[...]
    for b in range(nb):
        x = x_ref[b]
        xpad_scr[b, 8:8 + rows, :] = x
        conv = (cw[0:1] * xpad_scr[b, 5:5 + rows, :] + cw[1:2] * xpad_scr[b, 6:6 + rows, :]
                + cw[2:3] * xpad_scr[b, 7:7 + rows, :] + cw[3:4] * x)
        xpad_scr[b, 0:8, :] = x[rows - 8:rows, :]
        qkv_scr[b] = conv * jax.nn.sigmoid(conv)

    r256 = _iota((W, W), 0)
    c256 = _iota((W, W), 1)
    mask_bd = (r256 // HEAD_DIM) == (c256 // HEAD_DIM)
    ones_bd = jnp.where(mask_bd, 1.0, 0.0).astype(BF16)
    eye256 = jnp.where(r256 == c256, 1.0, 0.0).astype(BF16)
    row = _iota((C, W), 0)
    jl = _iota((C, W), 1) % HEAD_DIM
    ltri = jnp.where(_iota((C, C), 1) <= _iota((C, C), 0), 1.0, 0.0).astype(BF16)
    erow = _iota((LANE, W), 0)
    ehead = _iota((LANE, W), 1) // HEAD_DIM
    e_g = jnp.where(erow == ehead, 1.0, 0.0).astype(BF16)
    e_b = jnp.where(erow == ehead + HEADS, 1.0, 0.0).astype(BF16)

    def each(f, *lists):
        return [f(*args) for args in zip(*lists)]

    sl = [slice(c * C, (c + 1) * C) for _, c in inst]
    q = [qkv_scr[b, sl[k], 0:W] for k, (b, _) in enumerate(inst)]
    kk = [qkv_scr[b, sl[k], W:2 * W] for k, (b, _) in enumerate(inst)]
    v = [qkv_scr[b, sl[k], 2 * W:3 * W] for k, (b, _) in enumerate(inst)]
    ab = [ab_ref[b, sl[k], :] for k, (b, _) in enumerate(inst)]

    qn = each(lambda t: t * lax.rsqrt(_dot_x2(t * t, ones_bd) + NORM_EPS) * (HEAD_DIM ** -0.5), q)
    kn = each(lambda t: t * lax.rsqrt(_dot_x2(t * t, ones_bd) + NORM_EPS), kk)

    def gate(a):
        z = a + dtb_ref[...]
        return -jnp.exp(alog_ref[...]) * (jnp.maximum(z, 0.0) + jnp.log1p(jnp.exp(-jnp.abs(z))))

    g_hl = each(lambda a: _dot_x3(gate(a), e_g), ab)
    beta = each(lambda a: _dot_x3(jax.nn.sigmoid(a), e_b), ab)
    gc = each(lambda g: _dot_x3_left(ltri, g), g_hl)
    glast = each(lambda g: g[C - 1:C, :], gc)
    exp_g = each(jnp.exp, gc)
    dmat = each(lambda g: _dot_x3_left(ltri, jnp.where(row > jl, g, 0.0)), g_hl)
    decay = each(lambda d: jnp.where(jl <= row, jnp.exp(d), 0.0), dmat)

    kt4 = each(lambda t: _dot_nt(eye256, jnp.concatenate([t.astype(BF16)] * HEADS, axis=0)), kn)
    kb_mat = each(lambda t: jnp.where(mask_bd, t, 0.0).astype(BF16), kt4)
    kbeta = each(lambda t, bb: t * bb, kn, beta)
    a_c = each(lambda t, m, d: jnp.where(jl < row, _dot(t.astype(BF16), m) * d, 0.0), kbeta, kb_mat, decay)
    intra = each(lambda t, m, d: _dot(t.astype(BF16), m) * d, qn, kb_mat, decay)

    t_c = each(lambda a: jnp.where(jl == row, 1.0, 0.0) - a, a_c)
    p_c = a_c
    for _ in range(5):
        p_c = each(lambda pc: _mm_bd(pc, pc, mask_bd), p_c)
        t_c = each(lambda tc, pc: tc + _mm_bd(tc, pc, mask_bd), t_c, p_c)

    u = each(lambda tc, t, bb: _mm_bd(tc, t * bb, mask_bd), t_c, v, beta)
    w = each(lambda tc, t, e: _mm_bd(tc, t * e, mask_bd), t_c, kbeta, exp_g)
    kdt = each(lambda t, gl, g: _dot_nt(eye256, (t * jnp.exp(gl - g)).astype(BF16)), kn, glast, gc)

    for k, (b, c) in enumerate(inst):
        u_ref[b, sl[k], :] = u[k]
        w_ref[b, sl[k], :] = w[k].astype(w_ref.dtype)
        qd_ref[b, sl[k], :] = (qn[k] * exp_g[k]).astype(qd_ref.dtype)
        in_ref[b, sl[k], :] = intra[k].astype(in_ref.dtype)
        kdt_ref[b, c] = kdt[k].astype(kdt_ref.dtype)
        gl_ref[b, c] = jnp.exp(glast[k])


def _dot_x3_left(w, x):
    hi = x.astype(BF16)
    r = x - hi.astype(F32)
    mid = r.astype(BF16)
    lo = (r - mid.astype(F32)).astype(BF16)
    return _dot(w, hi) + _dot(w, mid) + _dot(w, lo)


def _gdn_scan_kernel(u_ref, w_ref, qd_ref, in_ref, kdt_ref, gl_ref, z_ref, g_ref, o_ref, s_scr):
    n = pl.program_id(0)
    C = GDN_CHUNK
    W = GROUP_W
    nb, rows = u_ref.shape[0], u_ref.shape[1]

    @pl.when(n == 0)
    def _():
        s_scr[...] = jnp.zeros(s_scr.shape, F32)

    mask_bd = (_iota((W, W), 0) // HEAD_DIM) == (_iota((W, W), 1) // HEAD_DIM)
    ones_bd = jnp.where(mask_bd, 1.0, 0.0).astype(BF16)
    s = [s_scr[b] for b in range(nb)]
    for c in range(rows // C):
        sl = slice(c * C, (c + 1) * C)
        sb = [t.astype(BF16) for t in s]
        v_new = [u_ref[b, sl, :] - _dot(w_ref[b, sl, :], sb[b]) for b in range(nb)]
        vb = [t.astype(BF16) for t in v_new]
        s = [s[b] * gl_ref[b, c] + jnp.where(mask_bd, _dot(kdt_ref[b, c], vb[b]), 0.0) for b in range(nb)]
        o = [_dot(qd_ref[b, sl, :], sb[b]) + _dot(in_ref[b, sl, :], _bd(vb[b], mask_bd)) for b in range(nb)]
        for b in range(nb):
            ms = _dot_x2(o[b] * o[b], ones_bd) * (1.0 / HEAD_DIM)
            zz = z_ref[b, sl, :]
            y = o[b] * lax.rsqrt(ms + NORM_EPS) * g_ref[...] * (zz * jax.nn.sigmoid(zz))
            o_ref[b, sl, :] = y.astype(o_ref.dtype)
    for b in range(nb):
        s_scr[b] = s[b]


GDN_PREP_CHUNKS = 8
GDN_SCAN_CHUNKS = 8


def _gdn(hg, conv_w, a_log, dt_bias, norm_g, batch, seq_len):
    C = GDN_CHUNK
    nc = seq_len // C
    W = GROUP_W
    padl = lambda a: jnp.concatenate([a, jnp.zeros((LANE - a.shape[0],), F32)]).reshape(1, LANE)
    hg3 = hg.reshape(batch, seq_len, hg.shape[-1])
    cp = math.gcd(GDN_PREP_CHUNKS, nc)
    rp = cp * C
    blkp = pl.BlockSpec((batch, rp, W), lambda n: (0, n, 0))
    u, w, qd, intra, kdt, gl = pl.pallas_call(
        _gdn_prep_kernel,
        grid=(nc // cp,),
        in_specs=[pl.BlockSpec((batch, rp, 3 * W), lambda n: (0, n, 0)),
                  pl.BlockSpec((batch, rp, LANE), lambda n: (0, n, 8)),
                  pl.BlockSpec((4, 3 * W), lambda n: (0, 0)),
                  pl.BlockSpec((1, LANE), lambda n: (0, 0)),
                  pl.BlockSpec((1, LANE), lambda n: (0, 0))],
        out_specs=[blkp] * 4
        + [pl.BlockSpec((batch, cp, W, C), lambda n: (0, n, 0, 0)),
           pl.BlockSpec((batch, cp, 1, W), lambda n: (0, n, 0, 0))],
        out_shape=[jax.ShapeDtypeStruct((batch, seq_len, W), F32), jax.ShapeDtypeStruct((batch, seq_len, W), BF16),
                   jax.ShapeDtypeStruct((batch, seq_len, W), BF16), jax.ShapeDtypeStruct((batch, seq_len, W), BF16),
                   jax.ShapeDtypeStruct((batch, nc, W, C), BF16),
                   jax.ShapeDtypeStruct((batch, nc, 1, W), F32)],
        scratch_shapes=[pltpu.VMEM((batch, 8 + rp, 3 * W), F32), pltpu.VMEM((batch, rp, 3 * W), F32)],
        compiler_params=_cp(("arbitrary",)),
        name="gdn_prep",
    )(hg3, hg3, conv_w, padl(a_log), padl(dt_bias))

    cs = math.gcd(GDN_SCAN_CHUNKS, nc)
    blk = pl.BlockSpec((batch, cs * C, W), lambda n: (0, n, 0))
    out = pl.pallas_call(
        _gdn_scan_kernel,
        grid=(nc // cs,),
        in_specs=[blk, blk, blk, blk,
                  pl.BlockSpec((batch, cs, W, C), lambda n: (0, n, 0, 0)),
                  pl.BlockSpec((batch, cs, 1, W), lambda n: (0, n, 0, 0)),
                  pl.BlockSpec((batch, cs * C, W), lambda n: (0, n, 3)),
                  pl.BlockSpec((1, W), lambda n: (0, 0))],
        out_specs=blk,
        out_shape=jax.ShapeDtypeStruct((batch, seq_len, W), BF16),
        scratch_shapes=[pltpu.VMEM((batch, W, W), F32)],
        compiler_params=_cp(("arbitrary",)),
        name="gdn_scan",
    )(u, w, qd, intra, kdt, gl, hg3, jnp.tile(norm_g, HEADS).reshape(1, W))
    return out.reshape(batch * seq_len, W)


def kernel(x, w_in, w_out, ffn1_w_gu, ffn1_w_down, ffn2_w_gu, ffn2_w_down, ln1_g, ln1_b, ln2_g, ln2_b, ln3_g, ln3_b, diff_lam_q1, diff_lam_k1, diff_lam_q2, diff_lam_k2, diff_subln_g, gdn_conv_w, gdn_a_log, gdn_dt_bias, gdn_norm_g, nsa_pe_k, nsa_pe_v, nsa_cmp_k_w1, nsa_cmp_k_w2, nsa_cmp_v_w1, nsa_cmp_v_w2):
    B, S, D = x.shape
    depth = w_in.shape[0]
    alpha = (2 * depth) ** 0.25
    tab = _rope_table(S)
    xf = x.reshape(B * S, D)
    for l in range(depth):
        lam_init = 0.8 - 0.6 * math.exp(-0.3 * l)
        xf = _ffn_ln(xf, ffn1_w_gu[l].astype(BF16), ffn1_w_down[l].astype(BF16), ln1_g[l], ln1_b[l], alpha)
        wm, wt = _prep_w_in(w_in[l])
        dq, dk, nq_arr, ks, kw, sq, sk, sv, hg, kcvc, gates, ht = _proj(xf, wm, wt, tab, S)
        lam_p = jnp.stack([diff_lam_q1[l], diff_lam_k1[l], diff_lam_q2[l], diff_lam_k2[l]])
        o_diff = _diff_attention(dq, dk, ht, lam_p, diff_subln_g[l], lam_init, B, S)
        o_gdn = _gdn(hg, gdn_conv_w[l], gdn_a_log[l], gdn_dt_bias[l], gdn_norm_g[l], B, S)
        o_nsa = _nsa(nq_arr, ks, kw, ht, kcvc, gates, nsa_pe_k[l], nsa_pe_v[l], nsa_cmp_k_w1[l], nsa_cmp_k_w2[l],
                     nsa_cmp_v_w1[l], nsa_cmp_v_w2[l], B, S)
        o_sb = _sb_attention(sq, sk, sv, B, S)
        xf = _outproj_ffn_ln(xf, (o_diff, o_gdn, o_nsa, o_sb), w_out[l].astype(BF16), ln2_g[l], ln2_b[l],
                             ffn2_w_gu[l].astype(BF16), ffn2_w_down[l].astype(BF16), ln3_g[l], ln3_b[l], alpha)
    return xf.reshape(B, S, D)
```

```python
import functools
import math

import numpy as np
import jax
import jax.numpy as jnp
from jax import lax
from jax.experimental import pallas as pl
from jax.experimental.pallas import tpu as pltpu

F32 = jnp.float32
BF16 = jnp.bfloat16

HEAD_DIM = 64
HEADS = 4
GROUP_W = HEADS * HEAD_DIM
DIFF_QK = HEAD_DIM // 2
GDN_CHUNK = 64
CMP_BLOCK, CMP_STRIDE = 32, 16
SLC_BLOCK, SLC_TOPN = 64, 16
WINDOW = 512
FORCE = 1e4
ROPE_THETA = 10000.0
LN_EPS = 1e-5
NORM_EPS = 1e-6
NEG = -1e30
LOG2E = 1.4426950408889634

LANE = 128
V7X_VMEM_BYTES = 64 * 1024 * 1024
VMEM_LIMIT = V7X_VMEM_BYTES - 8 * 1024 * 1024

TM_FFN = 512
FF_CHUNK = 256
TM_PROJ = 512
T_ATT = 1024
TQ_CMP = 512
TQ_WIN = 512
T_SEL = 1024


def _cp(sem):
    return pltpu.CompilerParams(dimension_semantics=sem, vmem_limit_bytes=VMEM_LIMIT)


def _iota(shape, dim):
    return lax.broadcasted_iota(jnp.int32, shape, dim)


def _dot(a, b):
    return jnp.dot(a, b, preferred_element_type=F32)


def _dot_nt(a, b):
    return lax.dot_general(a, b, (((1,), (1,)), ((), ())), preferred_element_type=F32)


def _split2(x):
    hi = x.astype(BF16)
    lo = (x - hi.astype(F32)).astype(BF16)
    return hi, lo


def _dot_x2(x, w):
    hi, lo = _split2(x)
    return _dot(hi, w) + _dot(lo, w)


def _dot_x3(x, w):
    hi = x.astype(BF16)
    r = x - hi.astype(F32)
    mid = r.astype(BF16)
    lo = (r - mid.astype(F32)).astype(BF16)
    return _dot(hi, w) + _dot(mid, w) + _dot(lo, w)


def _layer_norm(y, g, b):
    mu = jnp.mean(y, axis=-1, keepdims=True)
    d = y - mu
    var = jnp.mean(d * d, axis=-1, keepdims=True)
    return d * lax.rsqrt(var + LN_EPS) * g + b


def _const_spec(shape):
    nd = len(shape)
    return pl.BlockSpec(shape, lambda *_: (0,) * nd, pipeline_mode=pl.Buffered(1))


def _ffn_ln_kernel(x_ref, wgu_ref, wd_ref, g_ref, b_ref, o_ref, *, alpha, d_ff, ff_chunk):
    x = x_ref[...]
    xb = x.astype(BF16)
    acc = None
    for c in range(d_ff // ff_chunk):
        lo = c * ff_chunk
        g = _dot(xb, wgu_ref[:, lo:lo + ff_chunk])
        u = _dot(xb, wgu_ref[:, d_ff + lo:d_ff + lo + ff_chunk])
        a = (g * jax.nn.sigmoid(g) * u).astype(BF16)
        part = _dot(a, wd_ref[lo:lo + ff_chunk, :])
        acc = part if acc is None else acc + part
    o_ref[...] = _layer_norm(alpha * x + 0.5 * acc, g_ref[...], b_ref[...])


def _ffn_ln(x, w_gu, w_down, g, b, alpha):
    T, D = x.shape
    d_ff = w_down.shape[0]
    tm = min(TM_FFN, T)
    ff_chunk = FF_CHUNK if d_ff % FF_CHUNK == 0 else d_ff
    return pl.pallas_call(
        functools.partial(_ffn_ln_kernel, alpha=alpha, d_ff=d_ff, ff_chunk=ff_chunk),
        grid=(T // tm,),
        in_specs=[pl.BlockSpec((tm, D), lambda i: (i, 0)),
                  _const_spec((D, 2 * d_ff)), _const_spec((d_ff, D)),
                  _const_spec((1, D)), _const_spec((1, D))],
        out_specs=pl.BlockSpec((tm, D), lambda i: (i, 0)),
        out_shape=jax.ShapeDtypeStruct((T, D), F32),
        compiler_params=_cp(("parallel",)),
        name="ffn_ln",
    )(x, w_gu, w_down, g.reshape(1, D), b.reshape(1, D))


def _outproj_ffn_kernel(x_ref, o0_ref, o1_ref, o2_ref, o3_ref, wo_ref, g2_ref, b2_ref,
                        wgu_ref, wd_ref, g3_ref, b3_ref, out_ref, *, alpha, d_ff, ff_chunk):
    gw = o0_ref.shape[1]
    mix = None
    for k, o_ref in enumerate((o0_ref, o1_ref, o2_ref, o3_ref)):
        part = _dot(o_ref[...], wo_ref[k * gw:(k + 1) * gw, :])
        mix = part if mix is None else mix + part
    y = _layer_norm(alpha * x_ref[...] + mix, g2_ref[...], b2_ref[...])
    yb = y.astype(BF16)
    acc = None
    for c in range(d_ff // ff_chunk):
        lo = c * ff_chunk
        g = _dot(yb, wgu_ref[:, lo:lo + ff_chunk])
        u = _dot(yb, wgu_ref[:, d_ff + lo:d_ff + lo + ff_chunk])
        a = (g * jax.nn.sigmoid(g) * u).astype(BF16)
        part = _dot(a, wd_ref[lo:lo + ff_chunk, :])
        acc = part if acc is None else acc + part
    out_ref[...] = _layer_norm(alpha * y + 0.5 * acc, g3_ref[...], b3_ref[...])


def _outproj_ffn_ln(x, outs, w_out, g2, b2, w_gu, w_down, g3, b3, alpha):
    T, D = x.shape
    d_ff = w_down.shape[0]
    tm = min(TM_FFN, T)
    gw = outs[0].shape[1]
    ff_chunk = FF_CHUNK if d_ff % FF_CHUNK == 0 else d_ff
    vec = lambda a: a.reshape(1, D)
    return pl.pallas_call(
        functools.partial(_outproj_ffn_kernel, alpha=alpha, d_ff=d_ff, ff_chunk=ff_chunk),
        grid=(T // tm,),
        in_specs=[pl.BlockSpec((tm, D), lambda i: (i, 0))]
        + [pl.BlockSpec((tm, gw), lambda i: (i, 0))] * 4
        + [_const_spec(w_out.shape), _const_spec((1, D)), _const_spec((1, D)),
           _const_spec((D, 2 * d_ff)), _const_spec((d_ff, D)), _const_spec((1, D)), _const_spec((1, D))],
        out_specs=pl.BlockSpec((tm, D), lambda i: (i, 0)),
        out_shape=jax.ShapeDtypeStruct((T, D), F32),
        compiler_params=_cp(("parallel",)),
        name="outproj_ffn_ln",
    )(x, *outs, w_out, vec(g2), vec(b2), w_gu, w_down, vec(g3), vec(b3))


N_ROPE_BLK = 9


def _proj_kernel(x_ref, wm_ref, wt_ref, tab_ref,
                 dq_ref, dk_ref, nq_ref, ks_ref, kw_ref, sq_ref, sk_ref, sv_ref, hg_ref, kc_ref, vc_ref, gt_ref, ht_ref):
    xb = x_ref[...].astype(BF16)
    ht_ref[...] = _dot_nt(wt_ref[...], xb).astype(BF16)
    nr = N_ROPE_BLK * LANE
    h = _dot(xb, wm_ref[:, :nr])
    tab = tab_ref[...]
    lane = _iota((1, LANE), 1)
    for c in range(N_ROPE_BLK):
        t0 = 0 if c < 4 else (2 if c < 6 else 4)
        half = (DIFF_QK if c < 4 else HEAD_DIM) // 2
        cs = tab[:, t0 * LANE:(t0 + 1) * LANE]
        sn = tab[:, (t0 + 1) * LANE:(t0 + 2) * LANE]
        t = h[:, c * LANE:(c + 1) * LANE]
        rot = jnp.where(lane % (2 * half) < half, pltpu.roll(t, LANE - half, 1), pltpu.roll(t, half, 1))
        val = t * cs + rot * sn
        if c < 6:
            (dq_ref, dk_ref, nq_ref)[c // 2][:, (c % 2) * LANE:(c % 2 + 1) * LANE] = val.astype(BF16)
        elif c < 8:
            (ks_ref, kw_ref)[c - 6][...] = val.astype(BF16)
        else:
            kc_ref[...] = val[:, :HEAD_DIM]
            vc_ref[...] = val[:, HEAD_DIM:]
    hp = _dot(xb, wm_ref[:, 9 * LANE:15 * LANE]).astype(BF16)
    for n, ref in enumerate((sq_ref, sk_ref, sv_ref)):
        ref[...] = hp[:, n * GROUP_W:(n + 1) * GROUP_W]
    hf = _dot(xb, wm_ref[:, 15 * LANE:25 * LANE])
    hg_ref[...] = hf[:, :9 * LANE]
    gt_ref[...] = hf[:, 9 * LANE:10 * LANE]


def _proj(x, wm, wt, tab, seq_len):
    T, D = x.shape
    tm = min(TM_PROJ, seq_len)
    nst = seq_len // tm
    bf16_widths = (GROUP_W, GROUP_W, GROUP_W, LANE, LANE, GROUP_W, GROUP_W, GROUP_W)
    return pl.pallas_call(
        _proj_kernel,
        grid=(T // tm,),
        in_specs=[pl.BlockSpec((tm, D), lambda i: (i, 0)),
                  _const_spec(wm.shape), _const_spec(wt.shape),
                  pl.BlockSpec((tm, 6 * LANE), lambda i: (i % nst, 0))],
        out_specs=[pl.BlockSpec((tm, w), lambda i: (i, 0)) for w in bf16_widths]
        + [pl.BlockSpec((tm, 9 * LANE), lambda i: (i, 0)),
                   pl.BlockSpec((tm, HEAD_DIM), lambda i: (i, 0)),
                   pl.BlockSpec((tm, HEAD_DIM), lambda i: (i, 0)),
                   pl.BlockSpec((tm, LANE), lambda i: (i, 0)),
                   pl.BlockSpec((wt.shape[0], tm), lambda i: (0, i))],
        out_shape=[jax.ShapeDtypeStruct((T, w), BF16) for w in bf16_widths]
        + [jax.ShapeDtypeStruct((T, 9 * LANE), F32),
                   jax.ShapeDtypeStruct((T, HEAD_DIM), F32),
                   jax.ShapeDtypeStruct((T, HEAD_DIM), F32),
                   jax.ShapeDtypeStruct((T, LANE), F32),
                   jax.ShapeDtypeStruct((wt.shape[0], T), BF16)],
        compiler_params=_cp(("parallel",)),
        name="in_proj",
    )(x, wm, wt, tab)


def _prep_w_in(w):
    k = w.shape[0]
    w = w.astype(BF16)
    sizes = ((HEADS * DIFF_QK,) * 4 + (GROUP_W,) + (GROUP_W,) * 4 + (HEADS,) * 2
             + (GROUP_W,) + (HEAD_DIM,) * 6 + (3 * HEADS,) + (GROUP_W,) * 3)
    offs = np.concatenate([[0], np.cumsum(sizes)])
    (dq1, dq2, dk1, dk2, dv, gq, gk, gv, gz, ga, gb,
     nq, nkc, nvc, nks, nvs, nkw, nvw, ngate, sq, sk, sv) = [w[:, offs[i]:offs[i + 1]] for i in range(len(sizes))]
    scale = HEAD_DIM ** -0.5
    pad = lambda a: jnp.concatenate([a, jnp.zeros((k, LANE - a.shape[1]), w.dtype)], axis=1)
    seg = lambda first, last: w[:, offs[first]:offs[last + 1]]
    main = jnp.concatenate(
        [seg(0, 3), nq * scale, seg(14, 17), seg(12, 13),
         sq * scale, seg(20, 21), seg(5, 8), pad(seg(9, 10)), pad(ngate)], axis=1)
    wt = jnp.concatenate([dv, sv, nvs, nvw], axis=1).T
    return main, wt


def _rope_table(seq_len):
    def cs(dim):
        inv = ROPE_THETA ** (-jnp.arange(0, dim, 2, dtype=F32) / dim)
        ang = jnp.arange(seq_len, dtype=F32)[:, None] * inv[None, :]
        c, sgn = jnp.cos(ang), jnp.sin(ang)
        return jnp.concatenate([c, c], axis=1), jnp.concatenate([-sgn, sgn], axis=1)
    cd, sd, cn, sn = lax.optimization_barrier(cs(DIFF_QK) + cs(HEAD_DIM))
    one = jnp.ones((seq_len, HEAD_DIM), F32)
    zero = jnp.zeros((seq_len, HEAD_DIM), F32)
    return jnp.concatenate(
        [jnp.tile(cd, (1, 4)), jnp.tile(sd, (1, 4)), jnp.tile(cn, (1, 2)), jnp.tile(sn, (1, 2)),
         cn, one, sn, zero], axis=1)


def _tri_pairs(n, descending=False):
    qi, kj = [], []
    for i in range(n):
        js = range(i, -1, -1) if descending else range(i + 1)
        for j in js:
            qi.append(i)
            kj.append(j)
    return jnp.asarray(qi, jnp.int32), jnp.asarray(kj, jnp.int32)


ONES_ROWS = 16
SKEW = 4


def _diff_kernel(qi_ref, kj_ref, q_ref, k_ref, vt_ref, lam_ref, g_ref, o_ref,
                 m_scr, acc_scr, *, lam_init):
    p = pl.program_id(1)
    i = qi_ref[p]
    j = kj_ref[p]
    tq, tk = q_ref.shape[0], k_ref.shape[0]
    c = (DIFF_QK ** -0.5) * LOG2E
    head32 = _iota((1, LANE), 1) // DIFF_QK

    @pl.when(j == 0)
    def _init():
        m_scr[...] = jnp.full(m_scr.shape, NEG, F32)
        acc_scr[...] = jnp.zeros(acc_scr.shape, F32)

    def step(masked):
        parts = [(0, tk // 2, 0, tq), (tk // 2, tk, tq // 2, tq)] if masked else [(0, tk, 0, tq)]
        chains = [(t, h, part) for part in parts for t in range(2) for h in range(HEADS)]

        def scores(t, h, part):
            k0, k1, q0, q1 = part
            kt = k_ref[k0:k1, t * LANE:(t + 1) * LANE]
            km = jnp.where(head32 == h, kt, jnp.zeros_like(kt))
            st = _dot_nt(km, q_ref[q0:q1, t * LANE:(t + 1) * LANE])
            if masked:
                shape = (k1 - k0, q1 - q0)
                st = jnp.where(_iota(shape, 0) + k0 <= _iota(shape, 1) + q0, st, NEG)
            return st

        pend = [scores(*chains[n]) for n in range(SKEW)]
        for n, (t, h, (k0, k1, q0, q1)) in enumerate(chains):
            st = pend.pop(0)
            if n + SKEW < len(chains):
                pend.append(scores(*chains[n + SKEW]))
            idx = t * HEADS + h
            m_prev = m_scr[idx:idx + 1, q0:q1]
            m_new = jnp.maximum(m_prev, jnp.max(st, axis=0, keepdims=True))
            pt = jnp.exp2((st - m_new) * c).astype(BF16)
            alpha = jnp.exp2((m_prev - m_new) * c)
            m_scr[idx:idx + 1, q0:q1] = m_new
            vh = jnp.concatenate([vt_ref[h * HEAD_DIM:(h + 1) * HEAD_DIM, k0:k1],
                                  jnp.ones((ONES_ROWS, k1 - k0), BF16)], axis=0)
            acc_scr[idx, :, q0:q1] = acc_scr[idx, :, q0:q1] * alpha + _dot(vh, pt)

    @pl.when(j < i)
    def _off():
        step(False)

    @pl.when(j == i)
    def _diag():
        step(True)
        lp = lam_ref[...]
        lam = (jnp.exp(jnp.sum(lp[0:1] * lp[1:2], axis=-1, keepdims=True))
               - jnp.exp(jnp.sum(lp[2:3] * lp[3:4], axis=-1, keepdims=True)) + lam_init)
        parts = []
        for h in range(HEADS):
            a0, a1 = acc_scr[h], acc_scr[HEADS + h]
            oh = (a0[:HEAD_DIM] / a0[HEAD_DIM:HEAD_DIM + 1] - lam * (a1[:HEAD_DIM] / a1[HEAD_DIM:HEAD_DIM + 1]))
            ms = jnp.sum(oh * oh, axis=0, keepdims=True) * (1.0 / HEAD_DIM)
            parts.append(oh * lax.rsqrt(ms + NORM_EPS))
        y = jnp.concatenate(parts, axis=0).T * g_ref[...] * (1.0 - lam_init)
        o_ref[...] = y.astype(o_ref.dtype)


def _diff_attention(dq, dk, ht, lam_p, subln_g, lam_init, batch, seq_len):
    t = min(T_ATT, seq_len)
    nq = seq_len // t
    qi, kj = _tri_pairs(nq)
    g_full = jnp.tile(subln_g, HEADS).reshape(1, GROUP_W)
    grid_spec = pltpu.PrefetchScalarGridSpec(
        num_scalar_prefetch=2,
        grid=(batch, qi.shape[0]),
        in_specs=[pl.BlockSpec((t, GROUP_W), lambda b, p, qi, kj: (b * nq + qi[p], 0)),
                  pl.BlockSpec((t, GROUP_W), lambda b, p, qi, kj: (b * nq + kj[p], 0)),
                  pl.BlockSpec((GROUP_W, t), lambda b, p, qi, kj: (0, b * nq + kj[p])),
                  pl.BlockSpec((4, DIFF_QK), lambda b, p, qi, kj: (0, 0)),
                  pl.BlockSpec((1, GROUP_W), lambda b, p, qi, kj: (0, 0))],
        out_specs=pl.BlockSpec((t, GROUP_W), lambda b, p, qi, kj: (b * nq + qi[p], 0)),
        scratch_shapes=[pltpu.VMEM((2 * HEADS, t), F32),
                        pltpu.VMEM((2 * HEADS, HEAD_DIM + ONES_ROWS, t), F32)])
    return pl.pallas_call(
        functools.partial(_diff_kernel, lam_init=lam_init),
        grid_spec=grid_spec,
        out_shape=jax.ShapeDtypeStruct((batch * seq_len, GROUP_W), BF16),
        compiler_params=_cp(("parallel", "arbitrary")),
        name="diff_attn",
    )(qi, kj, dq, dk, ht, lam_p, g_full)


SB_CUM = 256
SB_Z_MIN = -87.0


def _sb_kernel(qi_ref, kj_ref, q_ref, k_ref, v_ref, o_ref, carry_scr, acc_scr):
    p = pl.program_id(1)
    i = qi_ref[p]
    j = kj_ref[p]
    tq, tk = q_ref.shape[0], k_ref.shape[0]
    head64 = _iota((1, GROUP_W), 1) // HEAD_DIM
    cw = min(SB_CUM, tk)

    @pl.when(j == i)
    def _init():
        carry_scr[...] = jnp.zeros(carry_scr.shape, F32)
        acc_scr[...] = jnp.zeros(acc_scr.shape, F32)

    def step(masked):
        m_excl = jnp.where(_iota((cw, cw), 0) > _iota((cw, cw), 1), 1.0, 0.0).astype(BF16)
        parts = [(tq // 2, tq, tk // 2, tk), (0, tq, 0, tk // 2)] if masked else [(0, tq, 0, tk)]
        chains = [(h, part) for part in parts for h in range(HEADS)]
        zs, lbs, xs, cums, atts = {}, {}, {}, {}, {}
        pvs = {part: [] for part in parts}

        def before(part):
            q0, q1, k0, k1 = part
            shape = (q1 - q0, k1 - k0)
            return _iota(shape, 1) + k0 < _iota(shape, 0) + q0

        def st_a(n):
            h, (q0, q1, k0, k1) = chains[n]
            kt = k_ref[k0:k1, :]
            zs[n] = _dot_nt(q_ref[q0:q1, :], jnp.where(head64 == h, kt, jnp.zeros_like(kt)))

        def st_b(n):
            z = jnp.maximum(zs.pop(n), SB_Z_MIN)
            nlb = jnp.log(1.0 + jnp.exp2(z * (-LOG2E)))
            sp = nlb + z
            lbs[n] = nlb
            if masked:
                sp = jnp.where(before(chains[n][1]), sp, 0.0)
            xs[n] = sp.astype(BF16)

        def st_c(n):
            h, (q0, q1, k0, k1) = chains[n]
            x = xs.pop(n)
            blocks = []
            suffix = carry_scr[h, q0:q1, :]
            for blk in reversed(range((k1 - k0) // cw)):
                sl = slice(blk * cw, (blk + 1) * cw)
                cb = _dot(x[:, sl], m_excl) + suffix
                blocks.insert(0, cb)
                suffix = cb[:, 0:1] + x[:, blk * cw:blk * cw + 1].astype(F32)
            carry_scr[h, q0:q1, :] = suffix
            cums[n] = jnp.concatenate(blocks, axis=1)

        def st_d(n):
            att = jnp.exp2((lbs.pop(n) + cums.pop(n)) * (-LOG2E))
            if masked:
                att = jnp.where(before(chains[n][1]), att, 0.0)
            atts[n] = att.astype(BF16)

        def st_e(n):
            h, part = chains[n]
            vt = v_ref[part[2]:part[3], :]
            pvs[part].append(_dot(atts.pop(n), jnp.where(head64 == h, vt, jnp.zeros_like(vt))))

        nch = len(chains)
        st_a(0)
        st_a(1)
        for n in range(nch):
            st_b(n)
            st_c(n)
            if n >= 1:
                st_d(n - 1)
                st_e(n - 1)
                if n + 1 < nch:
                    st_a(n + 1)
        st_d(nch - 1)
        st_e(nch - 1)
        for (q0, q1, _, _), terms in pvs.items():
            acc_scr[q0:q1, :] = acc_scr[q0:q1, :] + sum(terms[1:], terms[0])

    @pl.when(j == i)
    def _diag():
        step(True)

    @pl.when(j < i)
    def _off():
        step(False)

    @pl.when(j == 0)
    def _fin():
        o_ref[...] = acc_scr[...].astype(o_ref.dtype)


def _sb_attention(sq, sk, sv, batch, seq_len):
    t = min(T_ATT, seq_len)
    nq = seq_len // t
    qi, kj = _tri_pairs(nq, descending=True)
    grid_spec = pltpu.PrefetchScalarGridSpec(
        num_scalar_prefetch=2,
        grid=(batch, qi.shape[0]),
        in_specs=[pl.BlockSpec((t, GROUP_W), lambda b, p, qi, kj: (b * nq + qi[p], 0)),
                  pl.BlockSpec((t, GROUP_W), lambda b, p, qi, kj: (b * nq + kj[p], 0)),
                  pl.BlockSpec((t, GROUP_W), lambda b, p, qi, kj: (b * nq + kj[p], 0))],
        out_specs=pl.BlockSpec((t, GROUP_W), lambda b, p, qi, kj: (b * nq + qi[p], 0)),
        scratch_shapes=[pltpu.VMEM((HEADS, t, 1), F32), pltpu.VMEM((t, GROUP_W), F32)])
    return pl.pallas_call(
        _sb_kernel,
        grid_spec=grid_spec,
        out_shape=jax.ShapeDtypeStruct((batch * seq_len, GROUP_W), BF16),
        compiler_params=_cp(("parallel", "arbitrary")),
        name="sb_attn",
    )(qi, kj, sq, sk, sv)


def _stack_heads(q):
    qf = q.astype(F32)
    lo = _iota((1, LANE), 1) < HEAD_DIM
    parts = []
    for blk in (qf[:, :LANE], qf[:, LANE:]):
        parts.append(jnp.where(lo, blk, 0.0))
        parts.append(jnp.where(lo, pltpu.roll(blk, HEAD_DIM, 1), 0.0))
    return jnp.concatenate(parts, axis=0).astype(BF16)


def _nsa_compress_kernel(rk_ref, rv_ref, pek_ref, pev_ref, wk1_ref, wv1_ref, w2_ref, w2vt_ref, o_ref, vt_ref):
    n = rk_ref.shape[0]
    half = rk_ref.shape[1]

    def hidden(r_ref, pe_ref, w1_ref):
        r = r_ref[...]
        y1 = _dot((r + pe_ref[0:1, :]).astype(BF16), w1_ref[0:half, :])
        y2 = _dot((r + pe_ref[1:2, :]).astype(BF16), w1_ref[half:2 * half, :])
        hid = y1 + pltpu.roll(y2, n - 1, 0)
        return (hid * jax.nn.sigmoid(hid)).astype(BF16)

    act = jnp.concatenate([hidden(rk_ref, pek_ref, wk1_ref), hidden(rv_ref, pev_ref, wv1_ref)], axis=1)
    o_ref[...] = _dot(act, w2_ref[...]).astype(o_ref.dtype)
    vt_ref[...] = _dot_nt(w2vt_ref[...], act).astype(vt_ref.dtype)


def _nsa_compress(kc, vc, pe_k, pe_v, ck_w1, ck_w2, cv_w1, cv_w2, batch, seq_len):
    nrow = seq_len // CMP_STRIDE
    width = CMP_STRIDE * HEAD_DIM
    hid = ck_w1.shape[1]
    zo = jnp.zeros((hid, HEAD_DIM), BF16)
    w2 = jnp.concatenate([jnp.concatenate([ck_w2.astype(BF16), zo], axis=1),
                          jnp.concatenate([zo, cv_w2.astype(BF16)], axis=1)], axis=0)
    row_blk = pl.BlockSpec((nrow, width), lambda b: (b, 0))
    return pl.pallas_call(
        _nsa_compress_kernel,
        grid=(batch,),
        in_specs=[row_blk, row_blk, _const_spec((2, width)), _const_spec((2, width)),
                  _const_spec((2 * width, hid)), _const_spec((2 * width, hid)),
                  _const_spec((2 * hid, LANE)), _const_spec((HEAD_DIM, 2 * hid))],
        out_specs=[pl.BlockSpec((nrow, LANE), lambda b: (b, 0)),
                   pl.BlockSpec((HEAD_DIM, nrow), lambda b: (b, 0))],
        out_shape=[jax.ShapeDtypeStruct((batch * nrow, LANE), BF16),
                   jax.ShapeDtypeStruct((batch * HEAD_DIM, nrow), BF16)],
        compiler_params=_cp(("parallel",)),
        name="nsa_compress",
    )(kc.reshape(batch * nrow, width), vc.reshape(batch * nrow, width),
      pe_k.reshape(2, width), pe_v.reshape(2, width), ck_w1.astype(BF16), cv_w1.astype(BF16),
      w2, w2[:, HEAD_DIM:].T)


def _nsa_cmp_kernel(q_ref, kv_ref, vt_ref, ovt_ref, ocmp_ref, bias_ref):
    i = pl.program_id(1)
    tq = q_ref.shape[0]
    ncmp = kv_ref.shape[0]
    nslc = ovt_ref.shape[0]
    qs = _stack_heads(q_ref[...])
    kv = kv_ref[...]
    vt = vt_ref[...]
    tpos = i * tq + _iota((1, tq), 1)
    cm = _iota((ncmp, 1), 0) * CMP_STRIDE + (CMP_BLOCK - 1) <= tpos
    sts = [_dot_nt(kv, qs[h * tq:(h + 1) * tq]) for h in range(HEADS)]
    psum = None
    parts = []
    for h in range(HEADS):
        st = jnp.where(cm, sts[h], NEG)
        e = jnp.exp(st - jnp.max(st, axis=0, keepdims=True))
        pr = jnp.where(cm, e * (1.0 / jnp.sum(e, axis=0, keepdims=True)), 0.0)
        parts.append(_dot(vt, pr.astype(BF16)))
        psum = pr if psum is None else psum + pr
    ocmp_ref[...] = jnp.concatenate(parts, axis=0).T
    hi, lo = _split2(psum)
    imp = _dot(ovt_ref[...], hi) + _dot(ovt_ref[...], lo)
    blk = _iota((nslc, 1), 0)
    cur = tpos // SLC_BLOCK
    work = jnp.where(blk == 0, FORCE, jnp.where(blk == cur, FORCE, jnp.where(blk == cur - 1, FORCE, imp)))
    work = jnp.where(blk <= cur, work, -FORCE)
    sel = jnp.zeros((nslc, tq), F32)
    for _ in range(min(SLC_TOPN, nslc)):
        mx = jnp.max(work, axis=0, keepdims=True)
        first = jnp.min(jnp.where(work == mx, blk, nslc), axis=0, keepdims=True)
        hit = blk == first
        sel = jnp.where(hit, 1.0, sel)
        work = jnp.where(hit, -jnp.inf, work)
    bias_ref[...] = jnp.where(sel > 0.5, 0.0, NEG)


def _nsa_cmp(nq_arr, kvcmp, vtcmp, batch, seq_len):
    tq = min(TQ_CMP, seq_len)
    nq = seq_len // tq
    ncmp = seq_len // CMP_STRIDE
    nslc = seq_len // SLC_BLOCK
    cstart = np.arange(ncmp)[None, :] * CMP_STRIDE
    sstart = np.arange(nslc)[:, None] * SLC_BLOCK
    ovt = (cstart < sstart + SLC_BLOCK) & (cstart + CMP_BLOCK - 1 >= sstart)
    ovt &= (np.arange(ncmp)[None, :] < (seq_len - CMP_BLOCK) // CMP_STRIDE + 1)
    ovt = jnp.asarray(ovt, BF16)
    return pl.pallas_call(
        _nsa_cmp_kernel,
        grid=(batch, nq),
        in_specs=[pl.BlockSpec((tq, GROUP_W), lambda b, i: (b * nq + i, 0)),
                  pl.BlockSpec((ncmp, LANE), lambda b, i: (b, 0)),
                  pl.BlockSpec((HEAD_DIM, ncmp), lambda b, i: (b, 0)),
                  pl.BlockSpec((nslc, ncmp), lambda b, i: (0, 0))],
        out_specs=[pl.BlockSpec((tq, GROUP_W), lambda b, i: (b * nq + i, 0)),
                   pl.BlockSpec((nslc, tq), lambda b, i: (0, b * nq + i))],
        out_shape=[jax.ShapeDtypeStruct((batch * seq_len, GROUP_W), F32),
                   jax.ShapeDtypeStruct((nslc, batch * seq_len), F32)],
        compiler_params=_cp(("parallel", "parallel")),
        name="nsa_cmp_select",
    )(nq_arr, kvcmp, vtcmp, ovt)


def _nsa_win_kernel(q_ref, k_ref, vt_ref, o_ref, *, window):
    i = pl.program_id(1)
    tq = q_ref.shape[0]
    span = tq + window
    base = pl.multiple_of(jnp.maximum(i * tq - window, 0), LANE)
    kv = k_ref[pl.ds(base, span), :]
    vh = jnp.concatenate([vt_ref[:, pl.ds(base, span)], jnp.ones((ONES_ROWS, span), BF16)], axis=0)
    qs = _stack_heads(q_ref[...])
    rel = (i * tq + _iota((1, tq), 1)) - (base + _iota((span, 1), 0))
    bias = jnp.where(rel >= 0, jnp.where(rel < window, 0.0, NEG), NEG)
    sts = [_dot_nt(kv, qs[h * tq:(h + 1) * tq]) for h in range(HEADS)]
    parts = []
    for h in range(HEADS):
        st = sts[h] + bias
        e = jnp.exp(st - jnp.max(st, axis=0, keepdims=True)).astype(BF16)
        r = _dot(vh, e)
        parts.append(r[:HEAD_DIM] / r[HEAD_DIM:HEAD_DIM + 1])
    o_ref[...] = jnp.concatenate(parts, axis=0).T


def _nsa_window(nq_arr, kw, ht, batch, seq_len):
    tq = min(TQ_WIN, seq_len)
    nq = seq_len // tq
    window = min(WINDOW, seq_len - tq)
    vw_row_blk = 2 * GROUP_W // HEAD_DIM + 1
    return pl.pallas_call(
        functools.partial(_nsa_win_kernel, window=window),
        grid=(batch, nq),
        in_specs=[pl.BlockSpec((tq, GROUP_W), lambda b, i: (b * nq + i, 0)),
                  pl.BlockSpec((seq_len, LANE), lambda b, i: (b, 0)),
                  pl.BlockSpec((HEAD_DIM, seq_len), lambda b, i: (vw_row_blk, b))],
        out_specs=pl.BlockSpec((tq, GROUP_W), lambda b, i: (b * nq + i, 0)),
        out_shape=jax.ShapeDtypeStruct((batch * seq_len, GROUP_W), F32),
        compiler_params=_cp(("parallel", "parallel")),
        name="nsa_window",
    )(nq_arr, kw, ht)


def _nsa_sel_kernel(qi_ref, kj_ref, q_ref, k_ref, vt_ref, bias_ref, ocmp_ref, owin_ref, gate_ref, o_ref,
                    qs_scr, m_scr, acc_scr):
    p = pl.program_id(1)
    i = qi_ref[p]
    j = kj_ref[p]
    tq, tk = q_ref.shape[0], k_ref.shape[0]

    @pl.when(j == 0)
    def _init():
        qs = _stack_heads(q_ref[...])
        for h in range(HEADS):
            qs_scr[h] = qs[h * tq:(h + 1) * tq]
        m_scr[...] = jnp.full(m_scr.shape, NEG, F32)
        acc_scr[...] = jnp.zeros(acc_scr.shape, F32)

    def step(masked):
        parts = [(0, tk // 2, 0, tq), (tk // 2, tk, tq // 2, tq)] if masked else [(0, tk, 0, tq)]
        chains = [(h, part) for part in parts for h in range(HEADS)]

        def scores(h, part):
            k0, k1, q0, q1 = part
            st = _dot_nt(k_ref[k0:k1, :], qs_scr[h, q0:q1, :])
            bias = bias_ref[k0 // SLC_BLOCK:k1 // SLC_BLOCK, q0:q1][:, None, :]
            st = (st.reshape((k1 - k0) // SLC_BLOCK, SLC_BLOCK, q1 - q0) + bias).reshape(k1 - k0, q1 - q0)
            if masked:
                shape = (k1 - k0, q1 - q0)
                st = jnp.where(_iota(shape, 0) + k0 <= _iota(shape, 1) + q0, st, NEG)
            return st

        pend = [scores(*chains[n]) for n in range(SKEW)]
        for n, (h, (k0, k1, q0, q1)) in enumerate(chains):
            st = pend.pop(0)
            if n + SKEW < len(chains):
                pend.append(scores(*chains[n + SKEW]))
            m_prev = m_scr[h:h + 1, q0:q1]
            m_new = jnp.maximum(m_prev, jnp.max(st, axis=0, keepdims=True))
            pt = jnp.exp(st - m_new).astype(BF16)
            alpha = jnp.exp(m_prev - m_new)
            m_scr[h:h + 1, q0:q1] = m_new
            vh = jnp.concatenate([vt_ref[:, k0:k1], jnp.ones((ONES_ROWS, k1 - k0), BF16)], axis=0)
            acc_scr[h, :, q0:q1] = acc_scr[h, :, q0:q1] * alpha + _dot(vh, pt)

    @pl.when(j < i)
    def _past():
        step(False)

    @pl.when(j == i)
    def _diag():
        step(True)

    @pl.when(j == i)
    def _fin():
        parts = []
        for h in range(HEADS):
            a = acc_scr[h]
            parts.append(a[:HEAD_DIM] / a[HEAD_DIM:HEAD_DIM + 1])
        osel = jnp.concatenate(parts, axis=0).T
        sig = jax.nn.sigmoid(gate_ref[...])
        grow = _iota((LANE, 1), 0)
        ghead = _iota((1, GROUP_W), 1) // HEAD_DIM
        out = None
        for br, o_br in enumerate((ocmp_ref[...], osel, owin_ref[...])):
            e_br = jnp.where(grow == 3 * ghead + br, 1.0, 0.0).astype(BF16)
            term = _dot_x3(sig, e_br) * o_br
            out = term if out is None else out + term
        o_ref[...] = out.astype(o_ref.dtype)


def _nsa_select(nq_arr, ks, ht, sel, ocmp, owin, gates, batch, seq_len):
    tq = tk = min(T_SEL, seq_len)
    nq = nk = seq_len // tq
    qi, kj = _tri_pairs(nq)
    qrow = lambda b, p, qi, kj: (b * nq + qi[p], 0)
    vs_row_blk = 2 * GROUP_W // HEAD_DIM
    grid_spec = pltpu.PrefetchScalarGridSpec(
        num_scalar_prefetch=2,
        grid=(batch, qi.shape[0]),
        in_specs=[pl.BlockSpec((tq, GROUP_W), qrow),
                  pl.BlockSpec((tk, LANE), lambda b, p, qi, kj: (b * nk + kj[p], 0)),
                  pl.BlockSpec((HEAD_DIM, tk), lambda b, p, qi, kj: (vs_row_blk, b * nk + kj[p])),
                  pl.BlockSpec((tk // SLC_BLOCK, tq), lambda b, p, qi, kj: (kj[p], b * nq + qi[p])),
                  pl.BlockSpec((tq, GROUP_W), qrow), pl.BlockSpec((tq, GROUP_W), qrow),
                  pl.BlockSpec((tq, LANE), qrow)],
        out_specs=pl.BlockSpec((tq, GROUP_W), qrow),
        scratch_shapes=[pltpu.VMEM((HEADS, tq, LANE), BF16), pltpu.VMEM((HEADS, tq), F32),
                        pltpu.VMEM((HEADS, HEAD_DIM + ONES_ROWS, tq), F32)])
    return pl.pallas_call(
        _nsa_sel_kernel,
        grid_spec=grid_spec,
        out_shape=jax.ShapeDtypeStruct((batch * seq_len, GROUP_W), BF16),
        compiler_params=_cp(("parallel", "arbitrary")),
        name="nsa_select_gate",
    )(qi, kj, nq_arr, ks, ht, sel, ocmp, owin, gates)


def _nsa(nq_arr, ks, kw, ht, kc, vc, gates, pe_k, pe_v, ck_w1, ck_w2, cv_w1, cv_w2, batch, seq_len):
    kvcmp, vtcmp = _nsa_compress(kc, vc, pe_k, pe_v, ck_w1, ck_w2, cv_w1, cv_w2, batch, seq_len)
    ocmp, sel = _nsa_cmp(nq_arr, kvcmp, vtcmp, batch, seq_len)
    owin = _nsa_window(nq_arr, kw, ht, batch, seq_len)
    return _nsa_select(nq_arr, ks, ht, sel, ocmp, owin, gates, batch, seq_len)


def _bd(mc, mask_bd):
    return jnp.where(mask_bd, jnp.concatenate([mc] * HEADS, axis=0), jnp.zeros((), mc.dtype))


def _mm_bd(x, mc, mask_bd):
    return _dot(x.astype(BF16), _bd(mc.astype(BF16), mask_bd))


def _gdn_prep_kernel(x_ref, ab_ref, cw_ref, alog_ref, dtb_ref,
                     u_ref, w_ref, qd_ref, in_ref, kdt_ref, gl_ref, xpad_scr, qkv_scr):
    n = pl.program_id(0)
    C = GDN_CHUNK
    W = GROUP_W
    nb, rows = x_ref.shape[0], x_ref.shape[1]
    inst = [(b, c) for b in range(nb) for c in range(rows // C)]

    @pl.when(n == 0)
    def _():
        xpad_scr[:, 0:8, :] = jnp.zeros((nb, 8, xpad_scr.shape[2]), F32)

    cw = cw_ref[...]
    for b in range(nb):
        x = x_ref[b]
        xpad_scr[b, 8:8 + rows, :] = x
        conv = (cw[0:1] * xpad_scr[b, 5:5 + rows, :] + cw[1:2] * xpad_scr[b, 6:6 + rows, :]
                + cw[2:3] * xpad_scr[b, 7:7 + rows, :] + cw[3:4] * x)
        xpad_scr[b, 0:8, :] = x[rows - 8:rows, :]
        qkv_scr[b] = conv * jax.nn.sigmoid(conv)

    r256 = _iota((W, W), 0)
    c256 = _iota((W, W), 1)
    mask_bd = (r256 // HEAD_DIM) == (c256 // HEAD_DIM)
    ones_bd = jnp.where(mask_bd, 1.0, 0.0).astype(BF16)
    eye256 = jnp.where(r256 == c256, 1.0, 0.0).astype(BF16)
    row = _iota((C, W), 0)
    jl = _iota((C, W), 1) % HEAD_DIM
    ltri = jnp.where(_iota((C, C), 1) <= _iota((C, C), 0), 1.0, 0.0).astype(BF16)
    erow = _iota((LANE, W), 0)
    ehead = _iota((LANE, W), 1) // HEAD_DIM
    e_g = jnp.where(erow == ehead, 1.0, 0.0).astype(BF16)
    e_b = jnp.where(erow == ehead + HEADS, 1.0, 0.0).astype(BF16)

    def each(f, *lists):
        return [f(*args) for args in zip(*lists)]

    sl = [slice(c * C, (c + 1) * C) for _, c in inst]
    q = [qkv_scr[b, sl[k], 0:W] for k, (b, _) in enumerate(inst)]
    kk = [qkv_scr[b, sl[k], W:2 * W] for k, (b, _) in enumerate(inst)]
    v = [qkv_scr[b, sl[k], 2 * W:3 * W] for k, (b, _) in enumerate(inst)]
    ab = [ab_ref[b, sl[k], :] for k, (b, _) in enumerate(inst)]

    qn = each(lambda t: t * lax.rsqrt(_dot_x2(t * t, ones_bd) + NORM_EPS) * (HEAD_DIM ** -0.5), q)
    kn = each(lambda t: t * lax.rsqrt(_dot_x2(t * t, ones_bd) + NORM_EPS), kk)

    def gate(a):
        z = a + dtb_ref[...]
        return -jnp.exp(alog_ref[...]) * (jnp.maximum(z, 0.0) + jnp.log1p(jnp.exp(-jnp.abs(z))))

    g_hl = each(lambda a: _dot_x3(gate(a), e_g), ab)
    beta = each(lambda a: _dot_x3(jax.nn.sigmoid(a), e_b), ab)
    gc = each(lambda g: _dot_x3_left(ltri, g), g_hl)
    glast = each(lambda g: g[C - 1:C, :], gc)
    exp_g = each(jnp.exp, gc)
    dmat = each(lambda g: _dot_x3_left(ltri, jnp.where(row > jl, g, 0.0)), g_hl)
    decay = each(lambda d: jnp.where(jl <= row, jnp.exp(d), 0.0), dmat)

    kt4 = each(lambda t: _dot_nt(eye256, jnp.concatenate([t.astype(BF16)] * HEADS, axis=0)), kn)
    kb_mat = each(lambda t: jnp.where(mask_bd, t, 0.0).astype(BF16), kt4)
    kbeta = each(lambda t, bb: t * bb, kn, beta)
    a_c = each(lambda t, m, d: jnp.where(jl < row, _dot(t.astype(BF16), m) * d, 0.0), kbeta, kb_mat, decay)
    intra = each(lambda t, m, d: _dot(t.astype(BF16), m) * d, qn, kb_mat, decay)

    t_c = each(lambda a: jnp.where(jl == row, 1.0, 0.0) - a, a_c)
    p_c = a_c
    for _ in range(5):
        p_c = each(lambda pc: _mm_bd(pc, pc, mask_bd), p_c)
        t_c = each(lambda tc, pc: tc + _mm_bd(tc, pc, mask_bd), t_c, p_c)

    u = each(lambda tc, t, bb: _mm_bd(tc, t * bb, mask_bd), t_c, v, beta)
    w = each(lambda tc, t, e: _mm_bd(tc, t * e, mask_bd), t_c, kbeta, exp_g)
    kdt = each(lambda t, gl, g: _dot_nt(eye256, (t * jnp.exp(gl - g)).astype(BF16)), kn, glast, gc)

    for k, (b, c) in enumerate(inst):
        u_ref[b, sl[k], :] = u[k]
        w_ref[b, sl[k], :] = w[k].astype(w_ref.dtype)
        qd_ref[b, sl[k], :] = (qn[k] * exp_g[k]).astype(qd_ref.dtype)
        in_ref[b, sl[k], :] = intra[k].astype(in_ref.dtype)
        kdt_ref[b, c] = kdt[k].astype(kdt_ref.dtype)
        gl_ref[b, c] = jnp.exp(glast[k])


def _dot_x3_left(w, x):
    hi = x.astype(BF16)
    r = x - hi.astype(F32)
    mid = r.astype(BF16)
    lo = (r - mid.astype(F32)).astype(BF16)
    return _dot(w, hi) + _dot(w, mid) + _dot(w, lo)


def _gdn_scan_kernel(u_ref, w_ref, qd_ref, in_ref, kdt_ref, gl_ref, z_ref, g_ref, o_ref, s_scr):
    n = pl.program_id(0)
    C = GDN_CHUNK
    W = GROUP_W
    nb, rows = u_ref.shape[0], u_ref.shape[1]

    @pl.when(n == 0)
    def _():
        s_scr[...] = jnp.zeros(s_scr.shape, F32)

    mask_bd = (_iota((W, W), 0) // HEAD_DIM) == (_iota((W, W), 1) // HEAD_DIM)
    ones_bd = jnp.where(mask_bd, 1.0, 0.0).astype(BF16)
    s = [s_scr[b] for b in range(nb)]
    for c in range(rows // C):
        sl = slice(c * C, (c + 1) * C)
        sb = [t.astype(BF16) for t in s]
        v_new = [u_ref[b, sl, :] - _dot(w_ref[b, sl, :], sb[b]) for b in range(nb)]
        vb = [t.astype(BF16) for t in v_new]
        s = [s[b] * gl_ref[b, c] + jnp.where(mask_bd, _dot(kdt_ref[b, c], vb[b]), 0.0) for b in range(nb)]
        o = [_dot(qd_ref[b, sl, :], sb[b]) + _dot(in_ref[b, sl, :], _bd(vb[b], mask_bd)) for b in range(nb)]
        for b in range(nb):
            ms = _dot_x2(o[b] * o[b], ones_bd) * (1.0 / HEAD_DIM)
            zz = z_ref[b, sl, :]
            y = o[b] * lax.rsqrt(ms + NORM_EPS) * g_ref[...] * (zz * jax.nn.sigmoid(zz))
            o_ref[b, sl, :] = y.astype(o_ref.dtype)
    for b in range(nb):
        s_scr[b] = s[b]


GDN_PREP_CHUNKS = 8
GDN_SCAN_CHUNKS = 8


def _gdn(hg, conv_w, a_log, dt_bias, norm_g, batch, seq_len):
    C = GDN_CHUNK
    nc = seq_len // C
    W = GROUP_W
    padl = lambda a: jnp.concatenate([a, jnp.zeros((LANE - a.shape[0],), F32)]).reshape(1, LANE)
    hg3 = hg.reshape(batch, seq_len, hg.shape[-1])
    cp = math.gcd(GDN_PREP_CHUNKS, nc)
    rp = cp * C
    blkp = pl.BlockSpec((batch, rp, W), lambda n: (0, n, 0))
    u, w, qd, intra, kdt, gl = pl.pallas_call(
        _gdn_prep_kernel,
        grid=(nc // cp,),
        in_specs=[pl.BlockSpec((batch, rp, 3 * W), lambda n: (0, n, 0)),
                  pl.BlockSpec((batch, rp, LANE), lambda n: (0, n, 8)),
                  pl.BlockSpec((4, 3 * W), lambda n: (0, 0)),
                  pl.BlockSpec((1, LANE), lambda n: (0, 0)),
                  pl.BlockSpec((1, LANE), lambda n: (0, 0))],
        out_specs=[blkp] * 4
        + [pl.BlockSpec((batch, cp, W, C), lambda n: (0, n, 0, 0)),
           pl.BlockSpec((batch, cp, 1, W), lambda n: (0, n, 0, 0))],
        out_shape=[jax.ShapeDtypeStruct((batch, seq_len, W), F32), jax.ShapeDtypeStruct((batch, seq_len, W), BF16),
                   jax.ShapeDtypeStruct((batch, seq_len, W), BF16), jax.ShapeDtypeStruct((batch, seq_len, W), BF16),
                   jax.ShapeDtypeStruct((batch, nc, W, C), BF16),
                   jax.ShapeDtypeStruct((batch, nc, 1, W), F32)],
        scratch_shapes=[pltpu.VMEM((batch, 8 + rp, 3 * W), F32), pltpu.VMEM((batch, rp, 3 * W), F32)],
        compiler_params=_cp(("arbitrary",)),
        name="gdn_prep",
    )(hg3, hg3, conv_w, padl(a_log), padl(dt_bias))

    cs = math.gcd(GDN_SCAN_CHUNKS, nc)
    blk = pl.BlockSpec((batch, cs * C, W), lambda n: (0, n, 0))
    out = pl.pallas_call(
        _gdn_scan_kernel,
        grid=(nc // cs,),
        in_specs=[blk, blk, blk, blk,
                  pl.BlockSpec((batch, cs, W, C), lambda n: (0, n, 0, 0)),
                  pl.BlockSpec((batch, cs, 1, W), lambda n: (0, n, 0, 0)),
                  pl.BlockSpec((batch, cs * C, W), lambda n: (0, n, 3)),
                  pl.BlockSpec((1, W), lambda n: (0, 0))],
        out_specs=blk,
        out_shape=jax.ShapeDtypeStruct((batch, seq_len, W), BF16),
        scratch_shapes=[pltpu.VMEM((batch, W, W), F32)],
        compiler_params=_cp(("arbitrary",)),
        name="gdn_scan",
    )(u, w, qd, intra, kdt, gl, hg3, jnp.tile(norm_g, HEADS).reshape(1, W))
    return out.reshape(batch * seq_len, W)


def kernel(x, w_in, w_out, ffn1_w_gu, ffn1_w_down, ffn2_w_gu, ffn2_w_down, ln1_g, ln1_b, ln2_g, ln2_b, ln3_g, ln3_b, diff_lam_q1, diff_lam_k1, diff_lam_q2, diff_lam_k2, diff_subln_g, gdn_conv_w, gdn_a_log, gdn_dt_bias, gdn_norm_g, nsa_pe_k, nsa_pe_v, nsa_cmp_k_w1, nsa_cmp_k_w2, nsa_cmp_v_w1, nsa_cmp_v_w2):
    B, S, D = x.shape
    depth = w_in.shape[0]
    alpha = (2 * depth) ** 0.25
    tab = _rope_table(S)
    xf = x.reshape(B * S, D)
    for l in range(depth):
        lam_init = 0.8 - 0.6 * math.exp(-0.3 * l)
        xf = _ffn_ln(xf, ffn1_w_gu[l].astype(BF16), ffn1_w_down[l].astype(BF16), ln1_g[l], ln1_b[l], alpha)
        wm, wt = _prep_w_in(w_in[l])
        dq, dk, nq_arr, ks, kw, sq, sk, sv, hg, kc, vc, gates, ht = _proj(xf, wm, wt, tab, S)
        lam_p = jnp.stack([diff_lam_q1[l], diff_lam_k1[l], diff_lam_q2[l], diff_lam_k2[l]])
        o_diff = _diff_attention(dq, dk, ht, lam_p, diff_subln_g[l], lam_init, B, S)
        o_gdn = _gdn(hg, gdn_conv_w[l], gdn_a_log[l], gdn_dt_bias[l], gdn_norm_g[l], B, S)
        o_nsa = _nsa(nq_arr, ks, kw, ht, kc, vc, gates, nsa_pe_k[l], nsa_pe_v[l], nsa_cmp_k_w1[l], nsa_cmp_k_w2[l],
                     nsa_cmp_v_w1[l], nsa_cmp_v_w2[l], B, S)
        o_sb = _sb_attention(sq, sk, sv, B, S)
        xf = _outproj_ffn_ln(xf, (o_diff, o_gdn, o_nsa, o_sb), w_out[l].astype(BF16), ln2_g[l], ln2_b[l],
                             ffn2_w_gu[l].astype(BF16), ffn2_w_down[l].astype(BF16), ln3_g[l], ln3_b[l], alpha)
    return xf.reshape(B, S, D)
```

```python
import functools
import math

import numpy as np
import jax
import jax.numpy as jnp
from jax import lax
from jax.experimental import pallas as pl
from jax.experimental.pallas import tpu as pltpu

F32 = jnp.float32
BF16 = jnp.bfloat16

HEAD_DIM = 64
HEADS = 4
GROUP_W = HEADS * HEAD_DIM
DIFF_QK = HEAD_DIM // 2
GDN_CHUNK = 64
CMP_BLOCK, CMP_STRIDE = 32, 16
SLC_BLOCK, SLC_TOPN = 64, 16
WINDOW = 512
FORCE = 1e4
ROPE_THETA = 10000.0
LN_EPS = 1e-5
NORM_EPS = 1e-6
NEG = -1e30
LOG2E = 1.4426950408889634

LANE = 128
V7X_VMEM_BYTES = 64 * 1024 * 1024
VMEM_LIMIT = V7X_VMEM_BYTES - 8 * 1024 * 1024

TM_FFN = 512
FF_CHUNK = 256
TM_PROJ = 512
T_ATT = 1024
TQ_CMP = 512
TQ_WIN = 512
T_SEL = 1024


def _cp(sem):
    return pltpu.CompilerParams(dimension_semantics=sem, vmem_limit_bytes=VMEM_LIMIT)


def _iota(shape, dim):
    return lax.broadcasted_iota(jnp.int32, shape, dim)


def _dot(a, b):
    return jnp.dot(a, b, preferred_element_type=F32)


def _dot_nt(a, b):
    return lax.dot_general(a, b, (((1,), (1,)), ((), ())), preferred_element_type=F32)


def _split2(x):
    hi = x.astype(BF16)
    lo = (x - hi.astype(F32)).astype(BF16)
    return hi, lo


def _dot_x2(x, w):
    hi, lo = _split2(x)
    return _dot(hi, w) + _dot(lo, w)


def _dot_x3(x, w):
    hi = x.astype(BF16)
    r = x - hi.astype(F32)
    mid = r.astype(BF16)
    lo = (r - mid.astype(F32)).astype(BF16)
    return _dot(hi, w) + _dot(mid, w) + _dot(lo, w)


def _layer_norm(y, g, b):
    mu = jnp.mean(y, axis=-1, keepdims=True)
    d = y - mu
    var = jnp.mean(d * d, axis=-1, keepdims=True)
    return d * lax.rsqrt(var + LN_EPS) * g + b


def _const_spec(shape):
    nd = len(shape)
    return pl.BlockSpec(shape, lambda *_: (0,) * nd, pipeline_mode=pl.Buffered(1))


def _layer_spec(stacked, layer):
    return pl.BlockSpec((None,) + stacked.shape[1:], lambda *_: (layer, 0, 0), pipeline_mode=pl.Buffered(1))


def _ffn_ln_kernel(x_ref, wgu_ref, wd_ref, g_ref, b_ref, o_ref, *, alpha, d_ff, ff_chunk):
    x = x_ref[...]
    xb = x.astype(BF16)
    acc = None
    for c in range(d_ff // ff_chunk):
        lo = c * ff_chunk
        g = _dot(xb, wgu_ref[:, lo:lo + ff_chunk])
        u = _dot(xb, wgu_ref[:, d_ff + lo:d_ff + lo + ff_chunk])
        a = (g * jax.nn.sigmoid(g) * u).astype(BF16)
        part = _dot(a, wd_ref[lo:lo + ff_chunk, :])
        acc = part if acc is None else acc + part
    o_ref[...] = _layer_norm(alpha * x + 0.5 * acc, g_ref[...], b_ref[...])


def _ffn_ln(x, w_gu, w_down, layer, g, b, alpha):
    T, D = x.shape
    d_ff = w_down.shape[1]
    tm = min(TM_FFN, T)
    ff_chunk = FF_CHUNK if d_ff % FF_CHUNK == 0 else d_ff
    return pl.pallas_call(
        functools.partial(_ffn_ln_kernel, alpha=alpha, d_ff=d_ff, ff_chunk=ff_chunk),
        grid=(T // tm,),
        in_specs=[pl.BlockSpec((tm, D), lambda i: (i, 0)),
                  _layer_spec(w_gu, layer), _layer_spec(w_down, layer),
                  _const_spec((1, D)), _const_spec((1, D))],
        out_specs=pl.BlockSpec((tm, D), lambda i: (i, 0)),
        out_shape=jax.ShapeDtypeStruct((T, D), F32),
        compiler_params=_cp(("parallel",)),
        name="ffn_ln",
    )(x, w_gu, w_down, g.reshape(1, D), b.reshape(1, D))


def _outproj_ffn_kernel(x_ref, o0_ref, o1_ref, o2_ref, o3_ref, wo_ref, g2_ref, b2_ref,
                        wgu_ref, wd_ref, g3_ref, b3_ref, out_ref, *, alpha, d_ff, ff_chunk):
    gw = o0_ref.shape[1]
    mix = None
    for k, o_ref in enumerate((o0_ref, o1_ref, o2_ref, o3_ref)):
        part = _dot(o_ref[...], wo_ref[k * gw:(k + 1) * gw, :])
        mix = part if mix is None else mix + part
    y = _layer_norm(alpha * x_ref[...] + mix, g2_ref[...], b2_ref[...])
    yb = y.astype(BF16)
    acc = None
    for c in range(d_ff // ff_chunk):
        lo = c * ff_chunk
        g = _dot(yb, wgu_ref[:, lo:lo + ff_chunk])
        u = _dot(yb, wgu_ref[:, d_ff + lo:d_ff + lo + ff_chunk])
        a = (g * jax.nn.sigmoid(g) * u).astype(BF16)
        part = _dot(a, wd_ref[lo:lo + ff_chunk, :])
        acc = part if acc is None else acc + part
    out_ref[...] = _layer_norm(alpha * y + 0.5 * acc, g3_ref[...], b3_ref[...])


def _outproj_ffn_ln(x, outs, w_out, g2, b2, w_gu, w_down, layer, g3, b3, alpha):
    T, D = x.shape
    d_ff = w_down.shape[1]
    tm = min(TM_FFN, T)
    gw = outs[0].shape[1]
    ff_chunk = FF_CHUNK if d_ff % FF_CHUNK == 0 else d_ff
    vec = lambda a: a.reshape(1, D)
    return pl.pallas_call(
        functools.partial(_outproj_ffn_kernel, alpha=alpha, d_ff=d_ff, ff_chunk=ff_chunk),
        grid=(T // tm,),
        in_specs=[pl.BlockSpec((tm, D), lambda i: (i, 0))]
        + [pl.BlockSpec((tm, gw), lambda i: (i, 0))] * 4
        + [_layer_spec(w_out, layer), _const_spec((1, D)), _const_spec((1, D)),
           _layer_spec(w_gu, layer), _layer_spec(w_down, layer), _const_spec((1, D)), _const_spec((1, D))],
        out_specs=pl.BlockSpec((tm, D), lambda i: (i, 0)),
        out_shape=jax.ShapeDtypeStruct((T, D), F32),
        compiler_params=_cp(("parallel",)),
        name="outproj_ffn_ln",
    )(x, *outs, w_out, vec(g2), vec(b2), w_gu, w_down, vec(g3), vec(b3))


N_ROPE_BLK = 9


def _proj_kernel(x_ref, wm_ref, wt_ref, tab_ref,
                 dq_ref, dk_ref, nq_ref, ks_ref, kw_ref, sq_ref, sk_ref, sv_ref, hg_ref, kc_ref, vc_ref, gt_ref, ht_ref):
    xb = x_ref[...].astype(BF16)
    ht_ref[...] = _dot_nt(wt_ref[...], xb).astype(BF16)
    nr = N_ROPE_BLK * LANE
    h = _dot(xb, wm_ref[:, :nr])
    tab = tab_ref[...]
    lane = _iota((1, LANE), 1)
    for c in range(N_ROPE_BLK):
        t0 = 0 if c < 4 else (2 if c < 6 else 4)
        half = (DIFF_QK if c < 4 else HEAD_DIM) // 2
        cs = tab[:, t0 * LANE:(t0 + 1) * LANE]
        sn = tab[:, (t0 + 1) * LANE:(t0 + 2) * LANE]
        t = h[:, c * LANE:(c + 1) * LANE]
        rot = jnp.where(lane % (2 * half) < half, pltpu.roll(t, LANE - half, 1), pltpu.roll(t, half, 1))
        val = t * cs + rot * sn
        if c < 6:
            (dq_ref, dk_ref, nq_ref)[c // 2][:, (c % 2) * LANE:(c % 2 + 1) * LANE] = val.astype(BF16)
        elif c < 8:
            (ks_ref, kw_ref)[c - 6][...] = val.astype(BF16)
        else:
            kc_ref[...] = val[:, :HEAD_DIM]
            vc_ref[...] = val[:, HEAD_DIM:]
    hp = _dot(xb, wm_ref[:, 9 * LANE:15 * LANE]).astype(BF16)
    for n, ref in enumerate((sq_ref, sk_ref, sv_ref)):
        ref[...] = hp[:, n * GROUP_W:(n + 1) * GROUP_W]
    hf = _dot(xb, wm_ref[:, 15 * LANE:25 * LANE])
    hg_ref[...] = hf[:, :9 * LANE]
    gt_ref[...] = hf[:, 9 * LANE:10 * LANE]


def _proj(x, wm, wt, tab, seq_len):
    T, D = x.shape
    tm = min(TM_PROJ, seq_len)
    nst = seq_len // tm
    bf16_widths = (GROUP_W, GROUP_W, GROUP_W, LANE, LANE, GROUP_W, GROUP_W, GROUP_W)
    return pl.pallas_call(
        _proj_kernel,
        grid=(T // tm,),
        in_specs=[pl.BlockSpec((tm, D), lambda i: (i, 0)),
                  _const_spec(wm.shape), _const_spec(wt.shape),
                  pl.BlockSpec((tm, 6 * LANE), lambda i: (i % nst, 0))],
        out_specs=[pl.BlockSpec((tm, w), lambda i: (i, 0)) for w in bf16_widths]
        + [pl.BlockSpec((tm, 9 * LANE), lambda i: (i, 0)),
                   pl.BlockSpec((tm, HEAD_DIM), lambda i: (i, 0)),
                   pl.BlockSpec((tm, HEAD_DIM), lambda i: (i, 0)),
                   pl.BlockSpec((tm, LANE), lambda i: (i, 0)),
                   pl.BlockSpec((wt.shape[0], tm), lambda i: (0, i))],
        out_shape=[jax.ShapeDtypeStruct((T, w), BF16) for w in bf16_widths]
        + [jax.ShapeDtypeStruct((T, 9 * LANE), F32),
                   jax.ShapeDtypeStruct((T, HEAD_DIM), F32),
                   jax.ShapeDtypeStruct((T, HEAD_DIM), F32),
                   jax.ShapeDtypeStruct((T, LANE), F32),
                   jax.ShapeDtypeStruct((wt.shape[0], T), BF16)],
        compiler_params=_cp(("parallel",)),
        name="in_proj",
    )(x, wm, wt, tab)


def _prep_w_in(w):
    k = w.shape[0]
    w32 = w
    w = w.astype(BF16)
    sizes = ((HEADS * DIFF_QK,) * 4 + (GROUP_W,) + (GROUP_W,) * 4 + (HEADS,) * 2
             + (GROUP_W,) + (HEAD_DIM,) * 6 + (3 * HEADS,) + (GROUP_W,) * 3)
    offs = np.concatenate([[0], np.cumsum(sizes)])
    (dq1, dq2, dk1, dk2, dv, gq, gk, gv, gz, ga, gb,
     nq, nkc, nvc, nks, nvs, nkw, nvw, ngate, sq, sk, sv) = [w[:, offs[i]:offs[i + 1]] for i in range(len(sizes))]
    scale = HEAD_DIM ** -0.5
    pad = lambda a: jnp.concatenate([a, jnp.zeros((k, LANE - a.shape[1]), w.dtype)], axis=1)
    seg = lambda first, last: w[:, offs[first]:offs[last + 1]]
    main = jnp.concatenate(
        [seg(0, 3), nq * scale, seg(14, 17), seg(12, 13),
         sq * scale, seg(20, 21), seg(5, 8), pad(seg(9, 10)), pad(ngate)], axis=1)
    wt = jnp.concatenate([w32[:, offs[n]:offs[n + 1]] for n in (4, 21, 15, 17)], axis=1).T.astype(BF16)
    return main, wt


def _rope_table(seq_len):
    def cs(dim):
        inv = ROPE_THETA ** (-jnp.arange(0, dim, 2, dtype=F32) / dim)
        ang = jnp.arange(seq_len, dtype=F32)[:, None] * inv[None, :]
        c, sgn = jnp.cos(ang), jnp.sin(ang)
        return jnp.concatenate([c, c], axis=1), jnp.concatenate([-sgn, sgn], axis=1)
    cd, sd, cn, sn = lax.optimization_barrier(cs(DIFF_QK) + cs(HEAD_DIM))
    one = jnp.ones((seq_len, HEAD_DIM), F32)
    zero = jnp.zeros((seq_len, HEAD_DIM), F32)
    return jnp.concatenate(
        [jnp.tile(cd, (1, 4)), jnp.tile(sd, (1, 4)), jnp.tile(cn, (1, 2)), jnp.tile(sn, (1, 2)),
         cn, one, sn, zero], axis=1)


def _tri_pairs(n, descending=False):
    qi, kj = [], []
    for i in range(n):
        js = range(i, -1, -1) if descending else range(i + 1)
        for j in js:
            qi.append(i)
            kj.append(j)
    return jnp.asarray(qi, jnp.int32), jnp.asarray(kj, jnp.int32)


ONES_ROWS = 16
SKEW = 4


def _diff_kernel(qi_ref, kj_ref, q_ref, k_ref, vt_ref, lam_ref, g_ref, o_ref,
                 m_scr, acc_scr, *, lam_init):
    p = pl.program_id(1)
    i = qi_ref[p]
    j = kj_ref[p]
    tq, tk = q_ref.shape[0], k_ref.shape[0]
    c = (DIFF_QK ** -0.5) * LOG2E
    head32 = _iota((1, LANE), 1) // DIFF_QK

    @pl.when(j == 0)
    def _init():
        m_scr[...] = jnp.full(m_scr.shape, NEG, F32)
        acc_scr[...] = jnp.zeros(acc_scr.shape, F32)

    def step(masked):
        parts = [(0, tk // 2, 0, tq), (tk // 2, tk, tq // 2, tq)] if masked else [(0, tk, 0, tq)]
        chains = [(t, h, part) for part in parts for t in range(2) for h in range(HEADS)]

        def scores(t, h, part):
            k0, k1, q0, q1 = part
            kt = k_ref[k0:k1, t * LANE:(t + 1) * LANE]
            km = jnp.where(head32 == h, kt, jnp.zeros_like(kt))
            st = _dot_nt(km, q_ref[q0:q1, t * LANE:(t + 1) * LANE])
            if masked:
                shape = (k1 - k0, q1 - q0)
                st = jnp.where(_iota(shape, 0) + k0 <= _iota(shape, 1) + q0, st, NEG)
            return st

        pend = [scores(*chains[n]) for n in range(SKEW)]
        for n, (t, h, (k0, k1, q0, q1)) in enumerate(chains):
            st = pend.pop(0)
            if n + SKEW < len(chains):
                pend.append(scores(*chains[n + SKEW]))
            idx = t * HEADS + h
            m_prev = m_scr[idx:idx + 1, q0:q1]
            m_new = jnp.maximum(m_prev, jnp.max(st, axis=0, keepdims=True))
            pt = jnp.exp2((st - m_new) * c).astype(BF16)
            alpha = jnp.exp2((m_prev - m_new) * c)
            m_scr[idx:idx + 1, q0:q1] = m_new
            vh = jnp.concatenate([vt_ref[h * HEAD_DIM:(h + 1) * HEAD_DIM, k0:k1],
                                  jnp.ones((ONES_ROWS, k1 - k0), BF16)], axis=0)
            acc_scr[idx, :, q0:q1] = acc_scr[idx, :, q0:q1] * alpha + _dot(vh, pt)

    @pl.when(j < i)
    def _off():
        step(False)

    @pl.when(j == i)
    def _diag():
        step(True)
        lp = lam_ref[...]
        lam = (jnp.exp(jnp.sum(lp[0:1] * lp[1:2], axis=-1, keepdims=True))
               - jnp.exp(jnp.sum(lp[2:3] * lp[3:4], axis=-1, keepdims=True)) + lam_init)
        parts = []
        for h in range(HEADS):
            a0, a1 = acc_scr[h], acc_scr[HEADS + h]
            oh = (a0[:HEAD_DIM] / a0[HEAD_DIM:HEAD_DIM + 1] - lam * (a1[:HEAD_DIM] / a1[HEAD_DIM:HEAD_DIM + 1]))
            ms = jnp.sum(oh * oh, axis=0, keepdims=True) * (1.0 / HEAD_DIM)
            parts.append(oh * lax.rsqrt(ms + NORM_EPS))
        y = jnp.concatenate(parts, axis=0).T * g_ref[...] * (1.0 - lam_init)
        o_ref[...] = y.astype(o_ref.dtype)


def _diff_attention(dq, dk, ht, lam_p, subln_g, lam_init, batch, seq_len):
    t = min(T_ATT, seq_len)
    nq = seq_len // t
    qi, kj = _tri_pairs(nq)
    g_full = jnp.tile(subln_g, HEADS).reshape(1, GROUP_W)
    grid_spec = pltpu.PrefetchScalarGridSpec(
        num_scalar_prefetch=2,
        grid=(batch, qi.shape[0]),
        in_specs=[pl.BlockSpec((t, GROUP_W), lambda b, p, qi, kj: (b * nq + qi[p], 0)),
                  pl.BlockSpec((t, GROUP_W), lambda b, p, qi, kj: (b * nq + kj[p], 0)),
                  pl.BlockSpec((GROUP_W, t), lambda b, p, qi, kj: (0, b * nq + kj[p])),
                  pl.BlockSpec((4, DIFF_QK), lambda b, p, qi, kj: (0, 0)),
                  pl.BlockSpec((1, GROUP_W), lambda b, p, qi, kj: (0, 0))],
        out_specs=pl.BlockSpec((t, GROUP_W), lambda b, p, qi, kj: (b * nq + qi[p], 0)),
        scratch_shapes=[pltpu.VMEM((2 * HEADS, t), F32),
                        pltpu.VMEM((2 * HEADS, HEAD_DIM + ONES_ROWS, t), F32)])
    return pl.pallas_call(
        functools.partial(_diff_kernel, lam_init=lam_init),
        grid_spec=grid_spec,
        out_shape=jax.ShapeDtypeStruct((batch * seq_len, GROUP_W), BF16),
        compiler_params=_cp(("parallel", "arbitrary")),
        name="diff_attn",
    )(qi, kj, dq, dk, ht, lam_p, g_full)


SB_CUM = 256
SB_Z_MIN = -87.0


def _sb_kernel(qi_ref, kj_ref, q_ref, k_ref, v_ref, o_ref, carry_scr, acc_scr):
    p = pl.program_id(1)
    i = qi_ref[p]
    j = kj_ref[p]
    tq, tk = q_ref.shape[0], k_ref.shape[0]
    head64 = _iota((1, GROUP_W), 1) // HEAD_DIM
    cw = min(SB_CUM, tk)

    @pl.when(j == i)
    def _init():
        carry_scr[...] = jnp.zeros(carry_scr.shape, F32)
        acc_scr[...] = jnp.zeros(acc_scr.shape, F32)

    def step(masked):
        m_excl = jnp.where(_iota((cw, cw), 0) > _iota((cw, cw), 1), 1.0, 0.0).astype(BF16)
        parts = [(tq // 2, tq, tk // 2, tk), (0, tq, 0, tk // 2)] if masked else [(0, tq, 0, tk)]
        chains = [(h, part) for part in parts for h in range(HEADS)]
        zs, lbs, xs, cums, atts = {}, {}, {}, {}, {}
        pvs = {part: [] for part in parts}

        def before(part):
            q0, q1, k0, k1 = part
            shape = (q1 - q0, k1 - k0)
            return _iota(shape, 1) + k0 < _iota(shape, 0) + q0

        def st_a(n):
            h, (q0, q1, k0, k1) = chains[n]
            kt = k_ref[k0:k1, :]
            zs[n] = _dot_nt(q_ref[q0:q1, :], jnp.where(head64 == h, kt, jnp.zeros_like(kt)))

        def st_b(n):
            z = jnp.maximum(zs.pop(n), SB_Z_MIN)
            nlb = jnp.log(1.0 + jnp.exp2(z * (-LOG2E)))
            sp = nlb + z
            lbs[n] = nlb
            if masked:
                sp = jnp.where(before(chains[n][1]), sp, 0.0)
            xs[n] = sp.astype(BF16)

        def st_c(n):
            h, (q0, q1, k0, k1) = chains[n]
            x = xs.pop(n)
            blocks = []
            suffix = carry_scr[h, q0:q1, :]
            for blk in reversed(range((k1 - k0) // cw)):
                sl = slice(blk * cw, (blk + 1) * cw)
                cb = _dot(x[:, sl], m_excl) + suffix
                blocks.insert(0, cb)
                suffix = cb[:, 0:1] + x[:, blk * cw:blk * cw + 1].astype(F32)
            carry_scr[h, q0:q1, :] = suffix
            cums[n] = jnp.concatenate(blocks, axis=1)

        def st_d(n):
            att = jnp.exp2((lbs.pop(n) + cums.pop(n)) * (-LOG2E))
            if masked:
                att = jnp.where(before(chains[n][1]), att, 0.0)
            atts[n] = att.astype(BF16)

        def st_e(n):
            h, part = chains[n]
            vt = v_ref[part[2]:part[3], :]
            pvs[part].append(_dot(atts.pop(n), jnp.where(head64 == h, vt, jnp.zeros_like(vt))))

        nch = len(chains)
        st_a(0)
        st_a(1)
        for n in range(nch):
            st_b(n)
            st_c(n)
            if n >= 1:
                st_d(n - 1)
                st_e(n - 1)
                if n + 1 < nch:
                    st_a(n + 1)
        st_d(nch - 1)
        st_e(nch - 1)
        for (q0, q1, _, _), terms in pvs.items():
            acc_scr[q0:q1, :] = acc_scr[q0:q1, :] + sum(terms[1:], terms[0])

    @pl.when(j == i)
    def _diag():
        step(True)

    @pl.when(j < i)
    def _off():
        step(False)

    @pl.when(j == 0)
    def _fin():
        o_ref[...] = acc_scr[...].astype(o_ref.dtype)


def _sb_attention(sq, sk, sv, batch, seq_len):
    t = min(T_ATT, seq_len)
    nq = seq_len // t
    qi, kj = _tri_pairs(nq, descending=True)
    grid_spec = pltpu.PrefetchScalarGridSpec(
        num_scalar_prefetch=2,
        grid=(batch, qi.shape[0]),
        in_specs=[pl.BlockSpec((t, GROUP_W), lambda b, p, qi, kj: (b * nq + qi[p], 0)),
                  pl.BlockSpec((t, GROUP_W), lambda b, p, qi, kj: (b * nq + kj[p], 0)),
                  pl.BlockSpec((t, GROUP_W), lambda b, p, qi, kj: (b * nq + kj[p], 0))],
        out_specs=pl.BlockSpec((t, GROUP_W), lambda b, p, qi, kj: (b * nq + qi[p], 0)),
        scratch_shapes=[pltpu.VMEM((HEADS, t, 1), F32), pltpu.VMEM((t, GROUP_W), F32)])
    return pl.pallas_call(
        _sb_kernel,
        grid_spec=grid_spec,
        out_shape=jax.ShapeDtypeStruct((batch * seq_len, GROUP_W), BF16),
        compiler_params=_cp(("parallel", "arbitrary")),
        name="sb_attn",
    )(qi, kj, sq, sk, sv)


def _stack_heads(q):
    qf = q.astype(F32)
    lo = _iota((1, LANE), 1) < HEAD_DIM
    parts = []
    for blk in (qf[:, :LANE], qf[:, LANE:]):
        parts.append(jnp.where(lo, blk, 0.0))
        parts.append(jnp.where(lo, pltpu.roll(blk, HEAD_DIM, 1), 0.0))
    return jnp.concatenate(parts, axis=0).astype(BF16)


def _nsa_compress_kernel(rk_ref, rv_ref, pek_ref, pev_ref, wk1_ref, wv1_ref, w2_ref, w2vt_ref, o_ref, vt_ref):
    n = rk_ref.shape[0]
    half = rk_ref.shape[1]

    def hidden(r_ref, pe_ref, w1_ref):
        r = r_ref[...]
        y1 = _dot((r + pe_ref[0:1, :]).astype(BF16), w1_ref[0:half, :])
        y2 = _dot((r + pe_ref[1:2, :]).astype(BF16), w1_ref[half:2 * half, :])
        hid = y1 + pltpu.roll(y2, n - 1, 0)
        return (hid * jax.nn.sigmoid(hid)).astype(BF16)

    act = jnp.concatenate([hidden(rk_ref, pek_ref, wk1_ref), hidden(rv_ref, pev_ref, wv1_ref)], axis=1)
    o_ref[...] = _dot(act, w2_ref[...]).astype(o_ref.dtype)
    vt_ref[...] = _dot_nt(w2vt_ref[...], act).astype(vt_ref.dtype)


def _nsa_compress(kc, vc, pe_k, pe_v, ck_w1, ck_w2, cv_w1, cv_w2, batch, seq_len):
    nrow = seq_len // CMP_STRIDE
    width = CMP_STRIDE * HEAD_DIM
    hid = ck_w1.shape[1]
    zo = jnp.zeros((hid, HEAD_DIM), BF16)
    w2 = jnp.concatenate([jnp.concatenate([ck_w2.astype(BF16), zo], axis=1),
                          jnp.concatenate([zo, cv_w2.astype(BF16)], axis=1)], axis=0)
    row_blk = pl.BlockSpec((nrow, width), lambda b: (b, 0))
    return pl.pallas_call(
        _nsa_compress_kernel,
        grid=(batch,),
        in_specs=[row_blk, row_blk, _const_spec((2, width)), _const_spec((2, width)),
                  _const_spec((2 * width, hid)), _const_spec((2 * width, hid)),
                  _const_spec((2 * hid, LANE)), _const_spec((HEAD_DIM, 2 * hid))],
        out_specs=[pl.BlockSpec((nrow, LANE), lambda b: (b, 0)),
                   pl.BlockSpec((HEAD_DIM, nrow), lambda b: (b, 0))],
        out_shape=[jax.ShapeDtypeStruct((batch * nrow, LANE), BF16),
                   jax.ShapeDtypeStruct((batch * HEAD_DIM, nrow), BF16)],
        compiler_params=_cp(("parallel",)),
        name="nsa_compress",
    )(kc.reshape(batch * nrow, width), vc.reshape(batch * nrow, width),
      pe_k.reshape(2, width), pe_v.reshape(2, width), ck_w1.astype(BF16), cv_w1.astype(BF16),
      w2, w2[:, HEAD_DIM:].T)


def _nsa_cmp_kernel(q_ref, kv_ref, vt_ref, ovt_ref, ocmp_ref, bias_ref):
    i = pl.program_id(1)
    tq = q_ref.shape[0]
    ncmp = kv_ref.shape[0]
    nslc = ovt_ref.shape[0]
    qs = _stack_heads(q_ref[...])
    kv = kv_ref[...]
    vt = vt_ref[...]
    tpos = i * tq + _iota((1, tq), 1)
    cm = _iota((ncmp, 1), 0) * CMP_STRIDE + (CMP_BLOCK - 1) <= tpos
    sts = [_dot_nt(kv, qs[h * tq:(h + 1) * tq]) for h in range(HEADS)]
    psum = None
    parts = []
    for h in range(HEADS):
        st = jnp.where(cm, sts[h], NEG)
        e = jnp.exp(st - jnp.max(st, axis=0, keepdims=True))
        pr = jnp.where(cm, e * (1.0 / jnp.sum(e, axis=0, keepdims=True)), 0.0)
        parts.append(_dot(vt, pr.astype(BF16)))
        psum = pr if psum is None else psum + pr
    ocmp_ref[...] = jnp.concatenate(parts, axis=0).T
    hi, lo = _split2(psum)
    imp = _dot(ovt_ref[...], hi) + _dot(ovt_ref[...], lo)
    blk = _iota((nslc, 1), 0)
    cur = tpos // SLC_BLOCK
    work = jnp.where(blk == 0, FORCE, jnp.where(blk == cur, FORCE, jnp.where(blk == cur - 1, FORCE, imp)))
    work = jnp.where(blk <= cur, work, -FORCE)
    sel = jnp.zeros((nslc, tq), F32)
    for _ in range(min(SLC_TOPN, nslc)):
        mx = jnp.max(work, axis=0, keepdims=True)
        first = jnp.min(jnp.where(work == mx, blk, nslc), axis=0, keepdims=True)
        hit = blk == first
        sel = jnp.where(hit, 1.0, sel)
        work = jnp.where(hit, -jnp.inf, work)
    bias_ref[...] = jnp.where(sel > 0.5, 0.0, NEG)


def _nsa_cmp(nq_arr, kvcmp, vtcmp, batch, seq_len):
    tq = min(TQ_CMP, seq_len)
    nq = seq_len // tq
    ncmp = seq_len // CMP_STRIDE
    nslc = seq_len // SLC_BLOCK
    cstart = np.arange(ncmp)[None, :] * CMP_STRIDE
    sstart = np.arange(nslc)[:, None] * SLC_BLOCK
    ovt = (cstart < sstart + SLC_BLOCK) & (cstart + CMP_BLOCK - 1 >= sstart)
    ovt &= (np.arange(ncmp)[None, :] < (seq_len - CMP_BLOCK) // CMP_STRIDE + 1)
    ovt = jnp.asarray(ovt, BF16)
    return pl.pallas_call(
        _nsa_cmp_kernel,
        grid=(batch, nq),
        in_specs=[pl.BlockSpec((tq, GROUP_W), lambda b, i: (b * nq + i, 0)),
                  pl.BlockSpec((ncmp, LANE), lambda b, i: (b, 0)),
                  pl.BlockSpec((HEAD_DIM, ncmp), lambda b, i: (b, 0)),
                  pl.BlockSpec((nslc, ncmp), lambda b, i: (0, 0))],
        out_specs=[pl.BlockSpec((tq, GROUP_W), lambda b, i: (b * nq + i, 0)),
                   pl.BlockSpec((nslc, tq), lambda b, i: (0, b * nq + i))],
        out_shape=[jax.ShapeDtypeStruct((batch * seq_len, GROUP_W), F32),
                   jax.ShapeDtypeStruct((nslc, batch * seq_len), F32)],
        compiler_params=_cp(("parallel", "parallel")),
        name="nsa_cmp_select",
    )(nq_arr, kvcmp, vtcmp, ovt)


def _nsa_win_kernel(q_ref, k_ref, vt_ref, o_ref, *, window):
    i = pl.program_id(1)
    tq = q_ref.shape[0]
    span = tq + window
    base = pl.multiple_of(jnp.maximum(i * tq - window, 0), LANE)
    kv = k_ref[pl.ds(base, span), :]
    vh = jnp.concatenate([vt_ref[:, pl.ds(base, span)], jnp.ones((ONES_ROWS, span), BF16)], axis=0)
    qs = _stack_heads(q_ref[...])
    rel = (i * tq + _iota((1, tq), 1)) - (base + _iota((span, 1), 0))
    bias = jnp.where(rel >= 0, jnp.where(rel < window, 0.0, NEG), NEG)
    sts = [_dot_nt(kv, qs[h * tq:(h + 1) * tq]) for h in range(HEADS)]
    parts = []
    for h in range(HEADS):
        st = sts[h] + bias
        e = jnp.exp(st - jnp.max(st, axis=0, keepdims=True)).astype(BF16)
        r = _dot(vh, e)
        parts.append(r[:HEAD_DIM] / r[HEAD_DIM:HEAD_DIM + 1])
    o_ref[...] = jnp.concatenate(parts, axis=0).T


def _nsa_window(nq_arr, kw, ht, batch, seq_len):
    tq = min(TQ_WIN, seq_len)
    nq = seq_len // tq
    window = min(WINDOW, seq_len - tq)
    vw_row_blk = 2 * GROUP_W // HEAD_DIM + 1
    return pl.pallas_call(
        functools.partial(_nsa_win_kernel, window=window),
        grid=(batch, nq),
        in_specs=[pl.BlockSpec((tq, GROUP_W), lambda b, i: (b * nq + i, 0)),
                  pl.BlockSpec((seq_len, LANE), lambda b, i: (b, 0)),
                  pl.BlockSpec((HEAD_DIM, seq_len), lambda b, i: (vw_row_blk, b))],
        out_specs=pl.BlockSpec((tq, GROUP_W), lambda b, i: (b * nq + i, 0)),
        out_shape=jax.ShapeDtypeStruct((batch * seq_len, GROUP_W), F32),
        compiler_params=_cp(("parallel", "parallel")),
        name="nsa_window",
    )(nq_arr, kw, ht)


def _nsa_sel_kernel(qi_ref, kj_ref, q_ref, k_ref, vt_ref, bias_ref, ocmp_ref, owin_ref, gate_ref, o_ref,
                    qs_scr, m_scr, acc_scr):
    p = pl.program_id(1)
    i = qi_ref[p]
    j = kj_ref[p]
    tq, tk = q_ref.shape[0], k_ref.shape[0]

    @pl.when(j == 0)
    def _init():
        qs = _stack_heads(q_ref[...])
        for h in range(HEADS):
            qs_scr[h] = qs[h * tq:(h + 1) * tq]
        m_scr[...] = jnp.full(m_scr.shape, NEG, F32)
        acc_scr[...] = jnp.zeros(acc_scr.shape, F32)

    def step(masked):
        parts = [(0, tk // 2, 0, tq), (tk // 2, tk, tq // 2, tq)] if masked else [(0, tk, 0, tq)]
        chains = [(h, part) for part in parts for h in range(HEADS)]

        def scores(h, part):
            k0, k1, q0, q1 = part
            st = _dot_nt(k_ref[k0:k1, :], qs_scr[h, q0:q1, :])
            bias = bias_ref[k0 // SLC_BLOCK:k1 // SLC_BLOCK, q0:q1][:, None, :]
            st = (st.reshape((k1 - k0) // SLC_BLOCK, SLC_BLOCK, q1 - q0) + bias).reshape(k1 - k0, q1 - q0)
            if masked:
                shape = (k1 - k0, q1 - q0)
                st = jnp.where(_iota(shape, 0) + k0 <= _iota(shape, 1) + q0, st, NEG)
            return st

        pend = [scores(*chains[n]) for n in range(SKEW)]
        for n, (h, (k0, k1, q0, q1)) in enumerate(chains):
            st = pend.pop(0)
            if n + SKEW < len(chains):
                pend.append(scores(*chains[n + SKEW]))
            m_prev = m_scr[h:h + 1, q0:q1]
            m_new = jnp.maximum(m_prev, jnp.max(st, axis=0, keepdims=True))
            pt = jnp.exp(st - m_new).astype(BF16)
            alpha = jnp.exp(m_prev - m_new)
            m_scr[h:h + 1, q0:q1] = m_new
            vh = jnp.concatenate([vt_ref[:, k0:k1], jnp.ones((ONES_ROWS, k1 - k0), BF16)], axis=0)
            acc_scr[h, :, q0:q1] = acc_scr[h, :, q0:q1] * alpha + _dot(vh, pt)

    @pl.when(j < i)
    def _past():
        step(False)

    @pl.when(j == i)
    def _diag():
        step(True)

    @pl.when(j == i)
    def _fin():
        parts = []
        for h in range(HEADS):
            a = acc_scr[h]
            parts.append(a[:HEAD_DIM] / a[HEAD_DIM:HEAD_DIM + 1])
        osel = jnp.concatenate(parts, axis=0).T
        sig = jax.nn.sigmoid(gate_ref[...])
        grow = _iota((LANE, 1), 0)
        ghead = _iota((1, GROUP_W), 1) // HEAD_DIM
        out = None
        for br, o_br in enumerate((ocmp_ref[...], osel, owin_ref[...])):
            e_br = jnp.where(grow == 3 * ghead + br, 1.0, 0.0).astype(BF16)
            term = _dot_x3(sig, e_br) * o_br
            out = term if out is None else out + term
        o_ref[...] = out.astype(o_ref.dtype)


def _nsa_select(nq_arr, ks, ht, sel, ocmp, owin, gates, batch, seq_len):
    tq = tk = min(T_SEL, seq_len)
    nq = nk = seq_len // tq
    qi, kj = _tri_pairs(nq)
    qrow = lambda b, p, qi, kj: (b * nq + qi[p], 0)
    vs_row_blk = 2 * GROUP_W // HEAD_DIM
    grid_spec = pltpu.PrefetchScalarGridSpec(
        num_scalar_prefetch=2,
        grid=(batch, qi.shape[0]),
        in_specs=[pl.BlockSpec((tq, GROUP_W), qrow),
                  pl.BlockSpec((tk, LANE), lambda b, p, qi, kj: (b * nk + kj[p], 0)),
                  pl.BlockSpec((HEAD_DIM, tk), lambda b, p, qi, kj: (vs_row_blk, b * nk + kj[p])),
                  pl.BlockSpec((tk // SLC_BLOCK, tq), lambda b, p, qi, kj: (kj[p], b * nq + qi[p])),
                  pl.BlockSpec((tq, GROUP_W), qrow), pl.BlockSpec((tq, GROUP_W), qrow),
                  pl.BlockSpec((tq, LANE), qrow)],
        out_specs=pl.BlockSpec((tq, GROUP_W), qrow),
        scratch_shapes=[pltpu.VMEM((HEADS, tq, LANE), BF16), pltpu.VMEM((HEADS, tq), F32),
                        pltpu.VMEM((HEADS, HEAD_DIM + ONES_ROWS, tq), F32)])
    return pl.pallas_call(
        _nsa_sel_kernel,
        grid_spec=grid_spec,
        out_shape=jax.ShapeDtypeStruct((batch * seq_len, GROUP_W), BF16),
        compiler_params=_cp(("parallel", "arbitrary")),
        name="nsa_select_gate",
    )(qi, kj, nq_arr, ks, ht, sel, ocmp, owin, gates)


def _nsa(nq_arr, ks, kw, ht, kc, vc, gates, pe_k, pe_v, ck_w1, ck_w2, cv_w1, cv_w2, batch, seq_len):
    kvcmp, vtcmp = _nsa_compress(kc, vc, pe_k, pe_v, ck_w1, ck_w2, cv_w1, cv_w2, batch, seq_len)
    ocmp, sel = _nsa_cmp(nq_arr, kvcmp, vtcmp, batch, seq_len)
    owin = _nsa_window(nq_arr, kw, ht, batch, seq_len)
    return _nsa_select(nq_arr, ks, ht, sel, ocmp, owin, gates, batch, seq_len)


def _bd(mc, mask_bd):
    return jnp.where(mask_bd, jnp.concatenate([mc] * HEADS, axis=0), jnp.zeros((), mc.dtype))


def _mm_bd(x, mc, mask_bd):
    return _dot(x.astype(BF16), _bd(mc.astype(BF16), mask_bd))


def _gdn_prep_kernel(x_ref, ab_ref, cw_ref, alog_ref, dtb_ref,
                     u_ref, w_ref, qd_ref, in_ref, kdt_ref, gl_ref, xpad_scr, qkv_scr):
    n = pl.program_id(0)
    C = GDN_CHUNK
    W = GROUP_W
    nb, rows = x_ref.shape[0], x_ref.shape[1]
    inst = [(b, c) for b in range(nb) for c in range(rows // C)]

    @pl.when(n == 0)
    def _():
        xpad_scr[:, 0:8, :] = jnp.zeros((nb, 8, xpad_scr.shape[2]), F32)

    cw = cw_ref[...]
    for b in range(nb):
        x = x_ref[b]
        xpad_scr[b, 8:8 + rows, :] = x
        conv = (cw[0:1] * xpad_scr[b, 5:5 + rows, :] + cw[1:2] * xpad_scr[b, 6:6 + rows, :]
                + cw[2:3] * xpad_scr[b, 7:7 + rows, :] + cw[3:4] * x)
        xpad_scr[b, 0:8, :] = x[rows - 8:rows, :]
        qkv_scr[b] = conv * jax.nn.sigmoid(conv)

    r256 = _iota((W, W), 0)
    c256 = _iota((W, W), 1)
    mask_bd = (r256 // HEAD_DIM) == (c256 // HEAD_DIM)
    ones_bd = jnp.where(mask_bd, 1.0, 0.0).astype(BF16)
    eye256 = jnp.where(r256 == c256, 1.0, 0.0).astype(BF16)
    row = _iota((C, W), 0)
    jl = _iota((C, W), 1) % HEAD_DIM
    ltri = jnp.where(_iota((C, C), 1) <= _iota((C, C), 0), 1.0, 0.0).astype(BF16)
    erow = _iota((LANE, W), 0)
    ehead = _iota((LANE, W), 1) // HEAD_DIM
    e_g = jnp.where(erow == ehead, 1.0, 0.0).astype(BF16)
    e_b = jnp.where(erow == ehead + HEADS, 1.0, 0.0).astype(BF16)

    def each(f, *lists):
        return [f(*args) for args in zip(*lists)]

    sl = [slice(c * C, (c + 1) * C) for _, c in inst]
    q = [qkv_scr[b, sl[k], 0:W] for k, (b, _) in enumerate(inst)]
    kk = [qkv_scr[b, sl[k], W:2 * W] for k, (b, _) in enumerate(inst)]
    v = [qkv_scr[b, sl[k], 2 * W:3 * W] for k, (b, _) in enumerate(inst)]
    ab = [ab_ref[b, sl[k], :] for k, (b, _) in enumerate(inst)]

    qn = each(lambda t: t * lax.rsqrt(_dot_x2(t * t, ones_bd) + NORM_EPS) * (HEAD_DIM ** -0.5), q)
    kn = each(lambda t: t * lax.rsqrt(_dot_x2(t * t, ones_bd) + NORM_EPS), kk)

    def gate(a):
        z = a + dtb_ref[...]
        return -jnp.exp(alog_ref[...]) * (jnp.maximum(z, 0.0) + jnp.log1p(jnp.exp(-jnp.abs(z))))

    g_hl = each(lambda a: _dot_x3(gate(a), e_g), ab)
    beta = each(lambda a: _dot_x3(jax.nn.sigmoid(a), e_b), ab)
    gc = each(lambda g: _dot_x3_left(ltri, g), g_hl)
    glast = each(lambda g: g[C - 1:C, :], gc)
    exp_g = each(jnp.exp, gc)
    dmat = each(lambda g: _dot_x3_left(ltri, jnp.where(row > jl, g, 0.0)), g_hl)
    decay = each(lambda d: jnp.where(jl <= row, jnp.exp(d), 0.0), dmat)

    kt4 = each(lambda t: _dot_nt(eye256, jnp.concatenate([t.astype(BF16)] * HEADS, axis=0)), kn)
    kb_mat = each(lambda t: jnp.where(mask_bd, t, 0.0).astype(BF16), kt4)
    kbeta = each(lambda t, bb: t * bb, kn, beta)
    a_c = each(lambda t, m, d: jnp.where(jl < row, _dot(t.astype(BF16), m) * d, 0.0), kbeta, kb_mat, decay)
    intra = each(lambda t, m, d: _dot(t.astype(BF16), m) * d, qn, kb_mat, decay)

    t_c = each(lambda a: jnp.where(jl == row, 1.0, 0.0) - a, a_c)
    p_c = a_c
    for _ in range(5):
        p_c = each(lambda pc: _mm_bd(pc, pc, mask_bd), p_c)
        t_c = each(lambda tc, pc: tc + _mm_bd(tc, pc, mask_bd), t_c, p_c)

    u = each(lambda tc, t, bb: _mm_bd(tc, t * bb, mask_bd), t_c, v, beta)
    w = each(lambda tc, t, e: _mm_bd(tc, t * e, mask_bd), t_c, kbeta, exp_g)
    kdt = each(lambda t, gl, g: _dot_nt(eye256, (t * jnp.exp(gl - g)).astype(BF16)), kn, glast, gc)

    for k, (b, c) in enumerate(inst):
        u_ref[b, sl[k], :] = u[k]
        w_ref[b, sl[k], :] = w[k].astype(w_ref.dtype)
        qd_ref[b, sl[k], :] = (qn[k] * exp_g[k]).astype(qd_ref.dtype)
        in_ref[b, sl[k], :] = intra[k].astype(in_ref.dtype)
        kdt_ref[b, c] = kdt[k].astype(kdt_ref.dtype)
        gl_ref[b, c] = jnp.exp(glast[k])


def _dot_x3_left(w, x):
    hi = x.astype(BF16)
    r = x - hi.astype(F32)
    mid = r.astype(BF16)
    lo = (r - mid.astype(F32)).astype(BF16)
    return _dot(w, hi) + _dot(w, mid) + _dot(w, lo)


def _gdn_scan_kernel(u_ref, w_ref, qd_ref, in_ref, kdt_ref, gl_ref, z_ref, g_ref, o_ref, s_scr):
    n = pl.program_id(0)
    C = GDN_CHUNK
    W = GROUP_W
    nb, rows = u_ref.shape[0], u_ref.shape[1]

    @pl.when(n == 0)
    def _():
        s_scr[...] = jnp.zeros(s_scr.shape, F32)

    mask_bd = (_iota((W, W), 0) // HEAD_DIM) == (_iota((W, W), 1) // HEAD_DIM)
    ones_bd = jnp.where(mask_bd, 1.0, 0.0).astype(BF16)
    s = [s_scr[b] for b in range(nb)]
    for c in range(rows // C):
        sl = slice(c * C, (c + 1) * C)
        sb = [t.astype(BF16) for t in s]
        v_new = [u_ref[b, sl, :] - _dot(w_ref[b, sl, :], sb[b]) for b in range(nb)]
        vb = [t.astype(BF16) for t in v_new]
        s = [s[b] * gl_ref[b, c] + jnp.where(mask_bd, _dot(kdt_ref[b, c], vb[b]), 0.0) for b in range(nb)]
        o = [_dot(qd_ref[b, sl, :], sb[b]) + _dot(in_ref[b, sl, :], _bd(vb[b], mask_bd)) for b in range(nb)]
        for b in range(nb):
            ms = _dot_x2(o[b] * o[b], ones_bd) * (1.0 / HEAD_DIM)
            zz = z_ref[b, sl, :]
            y = o[b] * lax.rsqrt(ms + NORM_EPS) * g_ref[...] * (zz * jax.nn.sigmoid(zz))
            o_ref[b, sl, :] = y.astype(o_ref.dtype)
    for b in range(nb):
        s_scr[b] = s[b]


GDN_PREP_CHUNKS = 8
GDN_SCAN_CHUNKS = 8


def _gdn(hg, conv_w, a_log, dt_bias, norm_g, batch, seq_len):
    C = GDN_CHUNK
    nc = seq_len // C
    W = GROUP_W
    padl = lambda a: jnp.concatenate([a, jnp.zeros((LANE - a.shape[0],), F32)]).reshape(1, LANE)
    hg3 = hg.reshape(batch, seq_len, hg.shape[-1])
    cp = math.gcd(GDN_PREP_CHUNKS, nc)
    rp = cp * C
    blkp = pl.BlockSpec((batch, rp, W), lambda n: (0, n, 0))
    u, w, qd, intra, kdt, gl = pl.pallas_call(
        _gdn_prep_kernel,
        grid=(nc // cp,),
        in_specs=[pl.BlockSpec((batch, rp, 3 * W), lambda n: (0, n, 0)),
                  pl.BlockSpec((batch, rp, LANE), lambda n: (0, n, 8)),
                  pl.BlockSpec((4, 3 * W), lambda n: (0, 0)),
                  pl.BlockSpec((1, LANE), lambda n: (0, 0)),
                  pl.BlockSpec((1, LANE), lambda n: (0, 0))],
        out_specs=[blkp] * 4
        + [pl.BlockSpec((batch, cp, W, C), lambda n: (0, n, 0, 0)),
           pl.BlockSpec((batch, cp, 1, W), lambda n: (0, n, 0, 0))],
        out_shape=[jax.ShapeDtypeStruct((batch, seq_len, W), F32), jax.ShapeDtypeStruct((batch, seq_len, W), BF16),
                   jax.ShapeDtypeStruct((batch, seq_len, W), BF16), jax.ShapeDtypeStruct((batch, seq_len, W), BF16),
                   jax.ShapeDtypeStruct((batch, nc, W, C), BF16),
                   jax.ShapeDtypeStruct((batch, nc, 1, W), F32)],
        scratch_shapes=[pltpu.VMEM((batch, 8 + rp, 3 * W), F32), pltpu.VMEM((batch, rp, 3 * W), F32)],
        compiler_params=_cp(("arbitrary",)),
        name="gdn_prep",
    )(hg3, hg3, conv_w, padl(a_log), padl(dt_bias))

    cs = math.gcd(GDN_SCAN_CHUNKS, nc)
    blk = pl.BlockSpec((batch, cs * C, W), lambda n: (0, n, 0))
    out = pl.pallas_call(
        _gdn_scan_kernel,
        grid=(nc // cs,),
        in_specs=[blk, blk, blk, blk,
                  pl.BlockSpec((batch, cs, W, C), lambda n: (0, n, 0, 0)),
                  pl.BlockSpec((batch, cs, 1, W), lambda n: (0, n, 0, 0)),
                  pl.BlockSpec((batch, cs * C, W), lambda n: (0, n, 3)),
                  pl.BlockSpec((1, W), lambda n: (0, 0))],
        out_specs=blk,
        out_shape=jax.ShapeDtypeStruct((batch, seq_len, W), BF16),
        scratch_shapes=[pltpu.VMEM((batch, W, W), F32)],
        compiler_params=_cp(("arbitrary",)),
        name="gdn_scan",
    )(u, w, qd, intra, kdt, gl, hg3, jnp.tile(norm_g, HEADS).reshape(1, W))
    return out.reshape(batch * seq_len, W)


def kernel(x, w_in, w_out, ffn1_w_gu, ffn1_w_down, ffn2_w_gu, ffn2_w_down, ln1_g, ln1_b, ln2_g, ln2_b, ln3_g, ln3_b, diff_lam_q1, diff_lam_k1, diff_lam_q2, diff_lam_k2, diff_subln_g, gdn_conv_w, gdn_a_log, gdn_dt_bias, gdn_norm_g, nsa_pe_k, nsa_pe_v, nsa_cmp_k_w1, nsa_cmp_k_w2, nsa_cmp_v_w1, nsa_cmp_v_w2):
    B, S, D = x.shape
    depth = w_in.shape[0]
    alpha = (2 * depth) ** 0.25
    tab = _rope_table(S)
    w_out, ffn1_w_gu, ffn1_w_down, ffn2_w_gu, ffn2_w_down = (
        a.astype(BF16) for a in (w_out, ffn1_w_gu, ffn1_w_down, ffn2_w_gu, ffn2_w_down))
    xf = x.reshape(B * S, D)
    for l in range(depth):
        lam_init = 0.8 - 0.6 * math.exp(-0.3 * l)
        xf = _ffn_ln(xf, ffn1_w_gu, ffn1_w_down, l, ln1_g[l], ln1_b[l], alpha)
        wm, wt = _prep_w_in(w_in[l])
        dq, dk, nq_arr, ks, kw, sq, sk, sv, hg, kc, vc, gates, ht = _proj(xf, wm, wt, tab, S)
        lam_p = jnp.stack([diff_lam_q1[l], diff_lam_k1[l], diff_lam_q2[l], diff_lam_k2[l]])
        o_diff = _diff_attention(dq, dk, ht, lam_p, diff_subln_g[l], lam_init, B, S)
        o_gdn = _gdn(hg, gdn_conv_w[l], gdn_a_log[l], gdn_dt_bias[l], gdn_norm_g[l], B, S)
        o_nsa = _nsa(nq_arr, ks, kw, ht, kc, vc, gates, nsa_pe_k[l], nsa_pe_v[l], nsa_cmp_k_w1[l], nsa_cmp_k_w2[l],
                     nsa_cmp_v_w1[l], nsa_cmp_v_w2[l], B, S)
        o_sb = _sb_attention(sq, sk, sv, B, S)
        xf = _outproj_ffn_ln(xf, (o_diff, o_gdn, o_nsa, o_sb), w_out, ln2_g[l], ln2_b[l],
                             ffn2_w_gu, ffn2_w_down, l, ln3_g[l], ln3_b[l], alpha)
    return xf.reshape(B, S, D)
```

```python
import functools
import math

import numpy as np
import jax
import jax.numpy as jnp
from jax import lax
from jax.experimental import pallas as pl
from jax.experimental.pallas import tpu as pltpu

F32 = jnp.float32
BF16 = jnp.bfloat16

HEAD_DIM = 64
HEADS = 4
GROUP_W = HEADS * HEAD_DIM
DIFF_QK = HEAD_DIM // 2
GDN_CHUNK = 64
CMP_BLOCK, CMP_STRIDE = 32, 16
SLC_BLOCK, SLC_TOPN = 64, 16
WINDOW = 512
FORCE = 1e4
ROPE_THETA = 10000.0
LN_EPS = 1e-5
NORM_EPS = 1e-6
NEG = -1e30
LOG2E = 1.4426950408889634

LANE = 128
V7X_VMEM_BYTES = 64 * 1024 * 1024
VMEM_LIMIT = V7X_VMEM_BYTES - 8 * 1024 * 1024

TM_FFN = 512
FF_CHUNK = 256
TM_PROJ = 512
T_ATT = 1024
TQ_CMP = 512
TQ_WIN = 512
T_SEL = 1024


def _cp(sem):
    return pltpu.CompilerParams(dimension_semantics=sem, vmem_limit_bytes=VMEM_LIMIT)


def _iota(shape, dim):
    return lax.broadcasted_iota(jnp.int32, shape, dim)


def _dot(a, b):
    return jnp.dot(a, b, preferred_element_type=F32)


def _dot_nt(a, b):
    return lax.dot_general(a, b, (((1,), (1,)), ((), ())), preferred_element_type=F32)


def _split2(x):
    hi = x.astype(BF16)
    lo = (x - hi.astype(F32)).astype(BF16)
    return hi, lo


def _dot_x2(x, w):
    hi, lo = _split2(x)
    return _dot(hi, w) + _dot(lo, w)


def _dot_x3(x, w):
    hi = x.astype(BF16)
    r = x - hi.astype(F32)
    mid = r.astype(BF16)
    lo = (r - mid.astype(F32)).astype(BF16)
    return _dot(hi, w) + _dot(mid, w) + _dot(lo, w)


def _layer_norm(y, g, b):
    mu = jnp.mean(y, axis=-1, keepdims=True)
    d = y - mu
    var = jnp.mean(d * d, axis=-1, keepdims=True)
    return d * lax.rsqrt(var + LN_EPS) * g + b


def _const_spec(shape):
    nd = len(shape)
    return pl.BlockSpec(shape, lambda *_: (0,) * nd, pipeline_mode=pl.Buffered(1))


def _layer_spec(stacked, layer):
    return pl.BlockSpec((None,) + stacked.shape[1:], lambda *_: (layer, 0, 0), pipeline_mode=pl.Buffered(1))


def _ffn_ln_kernel(x_ref, wgu_ref, wd_ref, g_ref, b_ref, o_ref, *, alpha, d_ff, ff_chunk):
    x = x_ref[...]
    xb = x.astype(BF16)
    acc = None
    for c in range(d_ff // ff_chunk):
        lo = c * ff_chunk
        g = _dot(xb, wgu_ref[:, lo:lo + ff_chunk])
        u = _dot(xb, wgu_ref[:, d_ff + lo:d_ff + lo + ff_chunk])
        a = (g * jax.nn.sigmoid(g) * u).astype(BF16)
        part = _dot(a, wd_ref[lo:lo + ff_chunk, :])
        acc = part if acc is None else acc + part
    o_ref[...] = _layer_norm(alpha * x + 0.5 * acc, g_ref[...], b_ref[...])


def _ffn_ln(x, w_gu, w_down, layer, g, b, alpha):
    T, D = x.shape
    d_ff = w_down.shape[1]
    tm = min(TM_FFN, T)
    ff_chunk = FF_CHUNK if d_ff % FF_CHUNK == 0 else d_ff
    return pl.pallas_call(
        functools.partial(_ffn_ln_kernel, alpha=alpha, d_ff=d_ff, ff_chunk=ff_chunk),
        grid=(T // tm,),
        in_specs=[pl.BlockSpec((tm, D), lambda i: (i, 0)),
                  _layer_spec(w_gu, layer), _layer_spec(w_down, layer),
                  _const_spec((1, D)), _const_spec((1, D))],
        out_specs=pl.BlockSpec((tm, D), lambda i: (i, 0)),
        out_shape=jax.ShapeDtypeStruct((T, D), F32),
        compiler_params=_cp(("parallel",)),
        name="ffn_ln",
    )(x, w_gu, w_down, g.reshape(1, D), b.reshape(1, D))


def _outproj_ffn_kernel(x_ref, o0_ref, o1_ref, o2_ref, o3_ref, wo_ref, g2_ref, b2_ref,
                        wgu_ref, wd_ref, g3_ref, b3_ref, out_ref, *, alpha, d_ff, ff_chunk):
    gw = o0_ref.shape[1]
    mix = None
    for k, o_ref in enumerate((o0_ref, o1_ref, o2_ref, o3_ref)):
        part = _dot(o_ref[...], wo_ref[k * gw:(k + 1) * gw, :])
        mix = part if mix is None else mix + part
    y = _layer_norm(alpha * x_ref[...] + mix, g2_ref[...], b2_ref[...])
    yb = y.astype(BF16)
    acc = None
    for c in range(d_ff // ff_chunk):
        lo = c * ff_chunk
        g = _dot(yb, wgu_ref[:, lo:lo + ff_chunk])
        u = _dot(yb, wgu_ref[:, d_ff + lo:d_ff + lo + ff_chunk])
        a = (g * jax.nn.sigmoid(g) * u).astype(BF16)
        part = _dot(a, wd_ref[lo:lo + ff_chunk, :])
        acc = part if acc is None else acc + part
    out_ref[...] = _layer_norm(alpha * y + 0.5 * acc, g3_ref[...], b3_ref[...])


def _outproj_ffn_ln(x, outs, w_out, g2, b2, w_gu, w_down, layer, g3, b3, alpha):
    T, D = x.shape
    d_ff = w_down.shape[1]
    tm = min(TM_FFN, T)
    gw = outs[0].shape[1]
    ff_chunk = FF_CHUNK if d_ff % FF_CHUNK == 0 else d_ff
    vec = lambda a: a.reshape(1, D)
    return pl.pallas_call(
        functools.partial(_outproj_ffn_kernel, alpha=alpha, d_ff=d_ff, ff_chunk=ff_chunk),
        grid=(T // tm,),
        in_specs=[pl.BlockSpec((tm, D), lambda i: (i, 0))]
        + [pl.BlockSpec((tm, gw), lambda i: (i, 0))] * 4
        + [_layer_spec(w_out, layer), _const_spec((1, D)), _const_spec((1, D)),
           _layer_spec(w_gu, layer), _layer_spec(w_down, layer), _const_spec((1, D)), _const_spec((1, D))],
        out_specs=pl.BlockSpec((tm, D), lambda i: (i, 0)),
        out_shape=jax.ShapeDtypeStruct((T, D), F32),
        compiler_params=_cp(("parallel",)),
        name="outproj_ffn_ln",
    )(x, *outs, w_out, vec(g2), vec(b2), w_gu, w_down, vec(g3), vec(b3))


N_ROPE_BLK = 9


def _proj_kernel(x_ref, wm_ref, wt_ref, tab_ref,
                 dq_ref, dk_ref, nq_ref, ks_ref, kw_ref, sq_ref, sk_ref, sv_ref, hg_ref, kc_ref, vc_ref, gt_ref, ht_ref):
    xb = x_ref[...].astype(BF16)
    ht_ref[...] = _dot_nt(wt_ref[...], xb).astype(BF16)
    nr = N_ROPE_BLK * LANE
    h = _dot(xb, wm_ref[:, :nr])
    tab = tab_ref[...]
    lane = _iota((1, LANE), 1)
    for c in range(N_ROPE_BLK):
        t0 = 0 if c < 4 else (2 if c < 6 else 4)
        half = (DIFF_QK if c < 4 else HEAD_DIM) // 2
        cs = tab[:, t0 * LANE:(t0 + 1) * LANE]
        sn = tab[:, (t0 + 1) * LANE:(t0 + 2) * LANE]
        t = h[:, c * LANE:(c + 1) * LANE]
        rot = jnp.where(lane % (2 * half) < half, pltpu.roll(t, LANE - half, 1), pltpu.roll(t, half, 1))
        val = t * cs + rot * sn
        if c < 6:
            (dq_ref, dk_ref, nq_ref)[c // 2][:, (c % 2) * LANE:(c % 2 + 1) * LANE] = val.astype(BF16)
        elif c < 8:
            (ks_ref, kw_ref)[c - 6][...] = val.astype(BF16)
        else:
            kc_ref[...] = val[:, :HEAD_DIM]
            vc_ref[...] = val[:, HEAD_DIM:]
    hp = _dot(xb, wm_ref[:, 9 * LANE:15 * LANE]).astype(BF16)
    for n, ref in enumerate((sq_ref, sk_ref, sv_ref)):
        ref[...] = hp[:, n * GROUP_W:(n + 1) * GROUP_W]
    hf = _dot(xb, wm_ref[:, 15 * LANE:25 * LANE])
    hg_ref[...] = hf[:, :9 * LANE]
    gt_ref[...] = hf[:, 9 * LANE:10 * LANE]


def _proj(x, wm, wt, tab, seq_len):
    T, D = x.shape
    tm = min(TM_PROJ, seq_len)
    nst = seq_len // tm
    bf16_widths = (GROUP_W, GROUP_W, GROUP_W, LANE, LANE, GROUP_W, GROUP_W, GROUP_W)
    return pl.pallas_call(
        _proj_kernel,
        grid=(T // tm,),
        in_specs=[pl.BlockSpec((tm, D), lambda i: (i, 0)),
                  _const_spec(wm.shape), _const_spec(wt.shape),
                  pl.BlockSpec((tm, 6 * LANE), lambda i: (i % nst, 0))],
        out_specs=[pl.BlockSpec((tm, w), lambda i: (i, 0)) for w in bf16_widths]
        + [pl.BlockSpec((tm, 9 * LANE), lambda i: (i, 0)),
                   pl.BlockSpec((tm, HEAD_DIM), lambda i: (i, 0)),
                   pl.BlockSpec((tm, HEAD_DIM), lambda i: (i, 0)),
                   pl.BlockSpec((tm, LANE), lambda i: (i, 0)),
                   pl.BlockSpec((wt.shape[0], tm), lambda i: (0, i))],
        out_shape=[jax.ShapeDtypeStruct((T, w), BF16) for w in bf16_widths]
        + [jax.ShapeDtypeStruct((T, 9 * LANE), F32),
                   jax.ShapeDtypeStruct((T, HEAD_DIM), F32),
                   jax.ShapeDtypeStruct((T, HEAD_DIM), F32),
                   jax.ShapeDtypeStruct((T, LANE), F32),
                   jax.ShapeDtypeStruct((wt.shape[0], T), BF16)],
        compiler_params=_cp(("parallel",)),
        name="in_proj",
    )(x, wm, wt, tab)


def _prep_w_in(w):
    k = w.shape[0]
    w = w.astype(BF16)
    sizes = ((HEADS * DIFF_QK,) * 4 + (GROUP_W,) + (GROUP_W,) * 4 + (HEADS,) * 2
             + (GROUP_W,) + (HEAD_DIM,) * 6 + (3 * HEADS,) + (GROUP_W,) * 3)
    offs = np.concatenate([[0], np.cumsum(sizes)])
    (dq1, dq2, dk1, dk2, dv, gq, gk, gv, gz, ga, gb,
     nq, nkc, nvc, nks, nvs, nkw, nvw, ngate, sq, sk, sv) = [w[:, offs[i]:offs[i + 1]] for i in range(len(sizes))]
    scale = HEAD_DIM ** -0.5
    pad = lambda a: jnp.concatenate([a, jnp.zeros((k, LANE - a.shape[1]), w.dtype)], axis=1)
    seg = lambda first, last: w[:, offs[first]:offs[last + 1]]
    main = jnp.concatenate(
        [seg(0, 3), nq * scale, seg(14, 17), seg(12, 13),
         sq * scale, seg(20, 21), seg(5, 8), pad(seg(9, 10)), pad(ngate)], axis=1)
    wt = jnp.concatenate([dv, sv, nvs, nvw], axis=1).T
    return main, wt


def _rope_table(seq_len):
    def cs(dim):
        inv = ROPE_THETA ** (-jnp.arange(0, dim, 2, dtype=F32) / dim)
        ang = jnp.arange(seq_len, dtype=F32)[:, None] * inv[None, :]
        c, sgn = jnp.cos(ang), jnp.sin(ang)
        return jnp.concatenate([c, c], axis=1), jnp.concatenate([-sgn, sgn], axis=1)
    cd, sd, cn, sn = lax.optimization_barrier(cs(DIFF_QK) + cs(HEAD_DIM))
    one = jnp.ones((seq_len, HEAD_DIM), F32)
    zero = jnp.zeros((seq_len, HEAD_DIM), F32)
    return jnp.concatenate(
        [jnp.tile(cd, (1, 4)), jnp.tile(sd, (1, 4)), jnp.tile(cn, (1, 2)), jnp.tile(sn, (1, 2)),
         cn, one, sn, zero], axis=1)


def _tri_pairs(n, descending=False):
    qi, kj = [], []
    for i in range(n):
        js = range(i, -1, -1) if descending else range(i + 1)
        for j in js:
            qi.append(i)
            kj.append(j)
    return jnp.asarray(qi, jnp.int32), jnp.asarray(kj, jnp.int32)


ONES_ROWS = 16
SKEW = 4


def _diff_kernel(qi_ref, kj_ref, q_ref, k_ref, vt_ref, lam_ref, g_ref, o_ref,
                 m_scr, acc_scr, *, lam_init):
    p = pl.program_id(1)
    i = qi_ref[p]
    j = kj_ref[p]
    tq, tk = q_ref.shape[0], k_ref.shape[0]
    c = (DIFF_QK ** -0.5) * LOG2E
    head32 = _iota((1, LANE), 1) // DIFF_QK

    @pl.when(j == 0)
    def _init():
        m_scr[...] = jnp.full(m_scr.shape, NEG, F32)
        acc_scr[...] = jnp.zeros(acc_scr.shape, F32)

    def step(masked):
        parts = [(0, tk // 2, 0, tq), (tk // 2, tk, tq // 2, tq)] if masked else [(0, tk, 0, tq)]
        chains = [(t, h, part) for part in parts for t in range(2) for h in range(HEADS)]

        def scores(t, h, part):
            k0, k1, q0, q1 = part
            kt = k_ref[k0:k1, t * LANE:(t + 1) * LANE]
            km = jnp.where(head32 == h, kt, jnp.zeros_like(kt))
            st = _dot_nt(km, q_ref[q0:q1, t * LANE:(t + 1) * LANE])
            if masked:
                shape = (k1 - k0, q1 - q0)
                st = jnp.where(_iota(shape, 0) + k0 <= _iota(shape, 1) + q0, st, NEG)
            return st

        pend = [scores(*chains[n]) for n in range(SKEW)]
        for n, (t, h, (k0, k1, q0, q1)) in enumerate(chains):
            st = pend.pop(0)
            if n + SKEW < len(chains):
                pend.append(scores(*chains[n + SKEW]))
            idx = t * HEADS + h
            m_prev = m_scr[idx:idx + 1, q0:q1]
            m_new = jnp.maximum(m_prev, jnp.max(st, axis=0, keepdims=True))
            pt = jnp.exp2((st - m_new) * c).astype(BF16)
            alpha = jnp.exp2((m_prev - m_new) * c)
            m_scr[idx:idx + 1, q0:q1] = m_new
            vh = jnp.concatenate([vt_ref[h * HEAD_DIM:(h + 1) * HEAD_DIM, k0:k1],
                                  jnp.ones((ONES_ROWS, k1 - k0), BF16)], axis=0)
            acc_scr[idx, :, q0:q1] = acc_scr[idx, :, q0:q1] * alpha + _dot(vh, pt)

    @pl.when(j < i)
    def _off():
        step(False)

    @pl.when(j == i)
    def _diag():
        step(True)
        lp = lam_ref[...]
        lam = (jnp.exp(jnp.sum(lp[0:1] * lp[1:2], axis=-1, keepdims=True))
               - jnp.exp(jnp.sum(lp[2:3] * lp[3:4], axis=-1, keepdims=True)) + lam_init)
        parts = []
        for h in range(HEADS):
            a0, a1 = acc_scr[h], acc_scr[HEADS + h]
            oh = (a0[:HEAD_DIM] / a0[HEAD_DIM:HEAD_DIM + 1] - lam * (a1[:HEAD_DIM] / a1[HEAD_DIM:HEAD_DIM + 1]))
            ms = jnp.sum(oh * oh, axis=0, keepdims=True) * (1.0 / HEAD_DIM)
            parts.append(oh * lax.rsqrt(ms + NORM_EPS))
        y = jnp.concatenate(parts, axis=0).T * g_ref[...] * (1.0 - lam_init)
        o_ref[...] = y.astype(o_ref.dtype)


def _diff_attention(dq, dk, ht, lam_p, subln_g, lam_init, batch, seq_len):
    t = min(T_ATT, seq_len)
    nq = seq_len // t
    qi, kj = _tri_pairs(nq)
    g_full = jnp.tile(subln_g, HEADS).reshape(1, GROUP_W)
    grid_spec = pltpu.PrefetchScalarGridSpec(
        num_scalar_prefetch=2,
        grid=(batch, qi.shape[0]),
        in_specs=[pl.BlockSpec((t, GROUP_W), lambda b, p, qi, kj: (b * nq + qi[p], 0)),
                  pl.BlockSpec((t, GROUP_W), lambda b, p, qi, kj: (b * nq + kj[p], 0)),
                  pl.BlockSpec((GROUP_W, t), lambda b, p, qi, kj: (0, b * nq + kj[p])),
                  pl.BlockSpec((4, DIFF_QK), lambda b, p, qi, kj: (0, 0)),
                  pl.BlockSpec((1, GROUP_W), lambda b, p, qi, kj: (0, 0))],
        out_specs=pl.BlockSpec((t, GROUP_W), lambda b, p, qi, kj: (b * nq + qi[p], 0)),
        scratch_shapes=[pltpu.VMEM((2 * HEADS, t), F32),
                        pltpu.VMEM((2 * HEADS, HEAD_DIM + ONES_ROWS, t), F32)])
    return pl.pallas_call(
        functools.partial(_diff_kernel, lam_init=lam_init),
        grid_spec=grid_spec,
        out_shape=jax.ShapeDtypeStruct((batch * seq_len, GROUP_W), BF16),
        compiler_params=_cp(("parallel", "arbitrary")),
        name="diff_attn",
    )(qi, kj, dq, dk, ht, lam_p, g_full)


SB_CUM = 256
SB_Z_MIN = -87.0


def _sb_kernel(qi_ref, kj_ref, q_ref, k_ref, v_ref, o_ref, carry_scr, acc_scr):
    p = pl.program_id(1)
    i = qi_ref[p]
    j = kj_ref[p]
    tq, tk = q_ref.shape[0], k_ref.shape[0]
    head64 = _iota((1, GROUP_W), 1) // HEAD_DIM
    cw = min(SB_CUM, tk)

    @pl.when(j == i)
    def _init():
        carry_scr[...] = jnp.zeros(carry_scr.shape, F32)
        acc_scr[...] = jnp.zeros(acc_scr.shape, F32)

    def step(masked):
        m_excl = jnp.where(_iota((cw, cw), 0) > _iota((cw, cw), 1), 1.0, 0.0).astype(BF16)
        parts = [(tq // 2, tq, tk // 2, tk), (0, tq, 0, tk // 2)] if masked else [(0, tq, 0, tk)]
        chains = [(h, part) for part in parts for h in range(HEADS)]
        zs, lbs, xs, cums, atts = {}, {}, {}, {}, {}
        pvs = {part: [] for part in parts}

        def before(part):
            q0, q1, k0, k1 = part
            shape = (q1 - q0, k1 - k0)
            return _iota(shape, 1) + k0 < _iota(shape, 0) + q0

        def st_a(n):
            h, (q0, q1, k0, k1) = chains[n]
            kt = k_ref[k0:k1, :]
            zs[n] = _dot_nt(q_ref[q0:q1, :], jnp.where(head64 == h, kt, jnp.zeros_like(kt)))

        def st_b(n):
            z = jnp.maximum(zs.pop(n), SB_Z_MIN)
            nlb = jnp.log(1.0 + jnp.exp2(z * (-LOG2E)))
            sp = nlb + z
            lbs[n] = nlb
            if masked:
                sp = jnp.where(before(chains[n][1]), sp, 0.0)
            xs[n] = sp.astype(BF16)

        def st_c(n):
            h, (q0, q1, k0, k1) = chains[n]
            x = xs.pop(n)
            blocks = []
            suffix = carry_scr[h, q0:q1, :]
            for blk in reversed(range((k1 - k0) // cw)):
                sl = slice(blk * cw, (blk + 1) * cw)
                cb = _dot(x[:, sl], m_excl) + suffix
                blocks.insert(0, cb)
                suffix = cb[:, 0:1] + x[:, blk * cw:blk * cw + 1].astype(F32)
            carry_scr[h, q0:q1, :] = suffix
            cums[n] = jnp.concatenate(blocks, axis=1)

        def st_d(n):
            att = jnp.exp2((lbs.pop(n) + cums.pop(n)) * (-LOG2E))
            if masked:
                att = jnp.where(before(chains[n][1]), att, 0.0)
            atts[n] = att.astype(BF16)

        def st_e(n):
            h, part = chains[n]
            vt = v_ref[part[2]:part[3], :]
            pvs[part].append(_dot(atts.pop(n), jnp.where(head64 == h, vt, jnp.zeros_like(vt))))

        nch = len(chains)
        st_a(0)
        st_a(1)
        for n in range(nch):
            st_b(n)
            st_c(n)
            if n >= 1:
                st_d(n - 1)
                st_e(n - 1)
                if n + 1 < nch:
                    st_a(n + 1)
        st_d(nch - 1)
        st_e(nch - 1)
        for (q0, q1, _, _), terms in pvs.items():
            acc_scr[q0:q1, :] = acc_scr[q0:q1, :] + sum(terms[1:], terms[0])

    @pl.when(j == i)
    def _diag():
        step(True)

    @pl.when(j < i)
    def _off():
        step(False)

    @pl.when(j == 0)
    def _fin():
        o_ref[...] = acc_scr[...].astype(o_ref.dtype)


def _sb_attention(sq, sk, sv, batch, seq_len):
    t = min(T_ATT, seq_len)
    nq = seq_len // t
    qi, kj = _tri_pairs(nq, descending=True)
    grid_spec = pltpu.PrefetchScalarGridSpec(
        num_scalar_prefetch=2,
        grid=(batch, qi.shape[0]),
        in_specs=[pl.BlockSpec((t, GROUP_W), lambda b, p, qi, kj: (b * nq + qi[p], 0)),
                  pl.BlockSpec((t, GROUP_W), lambda b, p, qi, kj: (b * nq + kj[p], 0)),
                  pl.BlockSpec((t, GROUP_W), lambda b, p, qi, kj: (b * nq + kj[p], 0))],
        out_specs=pl.BlockSpec((t, GROUP_W), lambda b, p, qi, kj: (b * nq + qi[p], 0)),
        scratch_shapes=[pltpu.VMEM((HEADS, t, 1), F32), pltpu.VMEM((t, GROUP_W), F32)])
    return pl.pallas_call(
        _sb_kernel,
        grid_spec=grid_spec,
        out_shape=jax.ShapeDtypeStruct((batch * seq_len, GROUP_W), BF16),
        compiler_params=_cp(("parallel", "arbitrary")),
        name="sb_attn",
    )(qi, kj, sq, sk, sv)


def _stack_heads(q):
    qf = q.astype(F32)
    lo = _iota((1, LANE), 1) < HEAD_DIM
    parts = []
    for blk in (qf[:, :LANE], qf[:, LANE:]):
        parts.append(jnp.where(lo, blk, 0.0))
        parts.append(jnp.where(lo, pltpu.roll(blk, HEAD_DIM, 1), 0.0))
    return jnp.concatenate(parts, axis=0).astype(BF16)


def _nsa_compress_kernel(rk_ref, rv_ref, pek_ref, pev_ref, wk1_ref, wv1_ref, w2_ref, w2vt_ref, o_ref, vt_ref):
    n = rk_ref.shape[0]
    half = rk_ref.shape[1]

    def hidden(r_ref, pe_ref, w1_ref):
        r = r_ref[...]
        y1 = _dot((r + pe_ref[0:1, :]).astype(BF16), w1_ref[0:half, :])
        y2 = _dot((r + pe_ref[1:2, :]).astype(BF16), w1_ref[half:2 * half, :])
        hid = y1 + pltpu.roll(y2, n - 1, 0)
        return (hid * jax.nn.sigmoid(hid)).astype(BF16)

    act = jnp.concatenate([hidden(rk_ref, pek_ref, wk1_ref), hidden(rv_ref, pev_ref, wv1_ref)], axis=1)
    o_ref[...] = _dot(act, w2_ref[...]).astype(o_ref.dtype)
    vt_ref[...] = _dot_nt(w2vt_ref[...], act).astype(vt_ref.dtype)


def _nsa_compress(kc, vc, pe_k, pe_v, ck_w1, ck_w2, cv_w1, cv_w2, batch, seq_len):
    nrow = seq_len // CMP_STRIDE
    width = CMP_STRIDE * HEAD_DIM
    hid = ck_w1.shape[1]
    zo = jnp.zeros((hid, HEAD_DIM), BF16)
    w2 = jnp.concatenate([jnp.concatenate([ck_w2.astype(BF16), zo], axis=1),
                          jnp.concatenate([zo, cv_w2.astype(BF16)], axis=1)], axis=0)
    row_blk = pl.BlockSpec((nrow, width), lambda b: (b, 0))
    return pl.pallas_call(
        _nsa_compress_kernel,
        grid=(batch,),
        in_specs=[row_blk, row_blk, _const_spec((2, width)), _const_spec((2, width)),
                  _const_spec((2 * width, hid)), _const_spec((2 * width, hid)),
                  _const_spec((2 * hid, LANE)), _const_spec((HEAD_DIM, 2 * hid))],
        out_specs=[pl.BlockSpec((nrow, LANE), lambda b: (b, 0)),
                   pl.BlockSpec((HEAD_DIM, nrow), lambda b: (b, 0))],
        out_shape=[jax.ShapeDtypeStruct((batch * nrow, LANE), BF16),
                   jax.ShapeDtypeStruct((batch * HEAD_DIM, nrow), BF16)],
        compiler_params=_cp(("parallel",)),
        name="nsa_compress",
    )(kc.reshape(batch * nrow, width), vc.reshape(batch * nrow, width),
      pe_k.reshape(2, width), pe_v.reshape(2, width), ck_w1.astype(BF16), cv_w1.astype(BF16),
      w2, w2[:, HEAD_DIM:].T)


def _nsa_cmp_kernel(q_ref, kv_ref, vt_ref, ovt_ref, ocmp_ref, bias_ref):
    i = pl.program_id(1)
    tq = q_ref.shape[0]
    ncmp = kv_ref.shape[0]
    nslc = ovt_ref.shape[0]
    qs = _stack_heads(q_ref[...])
    kv = kv_ref[...]
    vt = vt_ref[...]
    tpos = i * tq + _iota((1, tq), 1)
    cm = _iota((ncmp, 1), 0) * CMP_STRIDE + (CMP_BLOCK - 1) <= tpos
    sts = [_dot_nt(kv, qs[h * tq:(h + 1) * tq]) for h in range(HEADS)]
    psum = None
    parts = []
    for h in range(HEADS):
        st = jnp.where(cm, sts[h], NEG)
        e = jnp.exp(st - jnp.max(st, axis=0, keepdims=True))
        pr = jnp.where(cm, e * (1.0 / jnp.sum(e, axis=0, keepdims=True)), 0.0)
        parts.append(_dot(vt, pr.astype(BF16)))
        psum = pr if psum is None else psum + pr
    ocmp_ref[...] = jnp.concatenate(parts, axis=0).T
    hi, lo = _split2(psum)
    imp = _dot(ovt_ref[...], hi) + _dot(ovt_ref[...], lo)
    blk = _iota((nslc, 1), 0)
    cur = tpos // SLC_BLOCK
    work = jnp.where(blk == 0, FORCE, jnp.where(blk == cur, FORCE, jnp.where(blk == cur - 1, FORCE, imp)))
    work = jnp.where(blk <= cur, work, -FORCE)
    sel = jnp.zeros((nslc, tq), F32)
    for _ in range(min(SLC_TOPN, nslc)):
        mx = jnp.max(work, axis=0, keepdims=True)
        first = jnp.min(jnp.where(work == mx, blk, nslc), axis=0, keepdims=True)
        hit = blk == first
        sel = jnp.where(hit, 1.0, sel)
        work = jnp.where(hit, -jnp.inf, work)
    bias_ref[...] = jnp.where(sel > 0.5, 0.0, NEG)


def _nsa_cmp(nq_arr, kvcmp, vtcmp, batch, seq_len):
    tq = min(TQ_CMP, seq_len)
    nq = seq_len // tq
    ncmp = seq_len // CMP_STRIDE
    nslc = seq_len // SLC_BLOCK
    cstart = np.arange(ncmp)[None, :] * CMP_STRIDE
    sstart = np.arange(nslc)[:, None] * SLC_BLOCK
    ovt = (cstart < sstart + SLC_BLOCK) & (cstart + CMP_BLOCK - 1 >= sstart)
    ovt &= (np.arange(ncmp)[None, :] < (seq_len - CMP_BLOCK) // CMP_STRIDE + 1)
    ovt = jnp.asarray(ovt, BF16)
    return pl.pallas_call(
        _nsa_cmp_kernel,
        grid=(batch, nq),
        in_specs=[pl.BlockSpec((tq, GROUP_W), lambda b, i: (b * nq + i, 0)),
                  pl.BlockSpec((ncmp, LANE), lambda b, i: (b, 0)),
                  pl.BlockSpec((HEAD_DIM, ncmp), lambda b, i: (b, 0)),
                  pl.BlockSpec((nslc, ncmp), lambda b, i: (0, 0))],
        out_specs=[pl.BlockSpec((tq, GROUP_W), lambda b, i: (b * nq + i, 0)),
                   pl.BlockSpec((nslc, tq), lambda b, i: (0, b * nq + i))],
        out_shape=[jax.ShapeDtypeStruct((batch * seq_len, GROUP_W), F32),
                   jax.ShapeDtypeStruct((nslc, batch * seq_len), F32)],
        compiler_params=_cp(("parallel", "parallel")),
        name="nsa_cmp_select",
    )(nq_arr, kvcmp, vtcmp, ovt)


def _nsa_win_kernel(q_ref, k_ref, vt_ref, o_ref, *, window):
    i = pl.program_id(1)
    tq = q_ref.shape[0]
    span = tq + window
    base = pl.multiple_of(jnp.maximum(i * tq - window, 0), LANE)
    kv = k_ref[pl.ds(base, span), :]
    vh = jnp.concatenate([vt_ref[:, pl.ds(base, span)], jnp.ones((ONES_ROWS, span), BF16)], axis=0)
    qs = _stack_heads(q_ref[...])
    rel = (i * tq + _iota((1, tq), 1)) - (base + _iota((span, 1), 0))
    bias = jnp.where(rel >= 0, jnp.where(rel < window, 0.0, NEG), NEG)
    sts = [_dot_nt(kv, qs[h * tq:(h + 1) * tq]) for h in range(HEADS)]
    parts = []
    for h in range(HEADS):
        st = sts[h] + bias
        e = jnp.exp(st - jnp.max(st, axis=0, keepdims=True)).astype(BF16)
        r = _dot(vh, e)
        parts.append(r[:HEAD_DIM] / r[HEAD_DIM:HEAD_DIM + 1])
    o_ref[...] = jnp.concatenate(parts, axis=0).T


def _nsa_window(nq_arr, kw, ht, batch, seq_len):
    tq = min(TQ_WIN, seq_len)
    nq = seq_len // tq
    window = min(WINDOW, seq_len - tq)
    vw_row_blk = 2 * GROUP_W // HEAD_DIM + 1
    return pl.pallas_call(
        functools.partial(_nsa_win_kernel, window=window),
        grid=(batch, nq),
        in_specs=[pl.BlockSpec((tq, GROUP_W), lambda b, i: (b * nq + i, 0)),
                  pl.BlockSpec((seq_len, LANE), lambda b, i: (b, 0)),
                  pl.BlockSpec((HEAD_DIM, seq_len), lambda b, i: (vw_row_blk, b))],
        out_specs=pl.BlockSpec((tq, GROUP_W), lambda b, i: (b * nq + i, 0)),
        out_shape=jax.ShapeDtypeStruct((batch * seq_len, GROUP_W), F32),
        compiler_params=_cp(("parallel", "parallel")),
        name="nsa_window",
    )(nq_arr, kw, ht)


def _nsa_sel_kernel(qi_ref, kj_ref, q_ref, k_ref, vt_ref, bias_ref, ocmp_ref, owin_ref, gate_ref, o_ref,
                    qs_scr, m_scr, acc_scr):
    p = pl.program_id(1)
    i = qi_ref[p]
    j = kj_ref[p]
    tq, tk = q_ref.shape[0], k_ref.shape[0]

    @pl.when(j == 0)
    def _init():
        qs = _stack_heads(q_ref[...])
        for h in range(HEADS):
            qs_scr[h] = qs[h * tq:(h + 1) * tq]
        m_scr[...] = jnp.full(m_scr.shape, NEG, F32)
        acc_scr[...] = jnp.zeros(acc_scr.shape, F32)

    def step(masked):
        parts = [(0, tk // 2, 0, tq), (tk // 2, tk, tq // 2, tq)] if masked else [(0, tk, 0, tq)]
        chains = [(h, part) for part in parts for h in range(HEADS)]

        def scores(h, part):
            k0, k1, q0, q1 = part
            st = _dot_nt(k_ref[k0:k1, :], qs_scr[h, q0:q1, :])
            bias = bias_ref[k0 // SLC_BLOCK:k1 // SLC_BLOCK, q0:q1][:, None, :]
            st = (st.reshape((k1 - k0) // SLC_BLOCK, SLC_BLOCK, q1 - q0) + bias).reshape(k1 - k0, q1 - q0)
            if masked:
                shape = (k1 - k0, q1 - q0)
                st = jnp.where(_iota(shape, 0) + k0 <= _iota(shape, 1) + q0, st, NEG)
            return st

        pend = [scores(*chains[n]) for n in range(SKEW)]
        for n, (h, (k0, k1, q0, q1)) in enumerate(chains):
            st = pend.pop(0)
            if n + SKEW < len(chains):
                pend.append(scores(*chains[n + SKEW]))
            m_prev = m_scr[h:h + 1, q0:q1]
            m_new = jnp.maximum(m_prev, jnp.max(st, axis=0, keepdims=True))
            pt = jnp.exp(st - m_new).astype(BF16)
            alpha = jnp.exp(m_prev - m_new)
            m_scr[h:h + 1, q0:q1] = m_new
            vh = jnp.concatenate([vt_ref[:, k0:k1], jnp.ones((ONES_ROWS, k1 - k0), BF16)], axis=0)
            acc_scr[h, :, q0:q1] = acc_scr[h, :, q0:q1] * alpha + _dot(vh, pt)

    @pl.when(j < i)
    def _past():
        step(False)

    @pl.when(j == i)
    def _diag():
        step(True)

    @pl.when(j == i)
    def _fin():
        parts = []
        for h in range(HEADS):
            a = acc_scr[h]
            parts.append(a[:HEAD_DIM] / a[HEAD_DIM:HEAD_DIM + 1])
        osel = jnp.concatenate(parts, axis=0).T
        sig = jax.nn.sigmoid(gate_ref[...])
        grow = _iota((LANE, 1), 0)
        ghead = _iota((1, GROUP_W), 1) // HEAD_DIM
        out = None
        for br, o_br in enumerate((ocmp_ref[...], osel, owin_ref[...])):
            e_br = jnp.where(grow == 3 * ghead + br, 1.0, 0.0).astype(BF16)
            term = _dot_x3(sig, e_br) * o_br
            out = term if out is None else out + term
        o_ref[...] = out.astype(o_ref.dtype)


def _nsa_select(nq_arr, ks, ht, sel, ocmp, owin, gates, batch, seq_len):
    tq = tk = min(T_SEL, seq_len)
    nq = nk = seq_len // tq
    qi, kj = _tri_pairs(nq)
    qrow = lambda b, p, qi, kj: (b * nq + qi[p], 0)
    vs_row_blk = 2 * GROUP_W // HEAD_DIM
    grid_spec = pltpu.PrefetchScalarGridSpec(
        num_scalar_prefetch=2,
        grid=(batch, qi.shape[0]),
        in_specs=[pl.BlockSpec((tq, GROUP_W), qrow),
                  pl.BlockSpec((tk, LANE), lambda b, p, qi, kj: (b * nk + kj[p], 0)),
                  pl.BlockSpec((HEAD_DIM, tk), lambda b, p, qi, kj: (vs_row_blk, b * nk + kj[p])),
                  pl.BlockSpec((tk // SLC_BLOCK, tq), lambda b, p, qi, kj: (kj[p], b * nq + qi[p])),
                  pl.BlockSpec((tq, GROUP_W), qrow), pl.BlockSpec((tq, GROUP_W), qrow),
                  pl.BlockSpec((tq, LANE), qrow)],
        out_specs=pl.BlockSpec((tq, GROUP_W), qrow),
        scratch_shapes=[pltpu.VMEM((HEADS, tq, LANE), BF16), pltpu.VMEM((HEADS, tq), F32),
                        pltpu.VMEM((HEADS, HEAD_DIM + ONES_ROWS, tq), F32)])
    return pl.pallas_call(
        _nsa_sel_kernel,
        grid_spec=grid_spec,
        out_shape=jax.ShapeDtypeStruct((batch * seq_len, GROUP_W), BF16),
        compiler_params=_cp(("parallel", "arbitrary")),
        name="nsa_select_gate",
    )(qi, kj, nq_arr, ks, ht, sel, ocmp, owin, gates)


def _nsa(nq_arr, ks, kw, ht, kc, vc, gates, pe_k, pe_v, ck_w1, ck_w2, cv_w1, cv_w2, batch, seq_len):
    kvcmp, vtcmp = _nsa_compress(kc, vc, pe_k, pe_v, ck_w1, ck_w2, cv_w1, cv_w2, batch, seq_len)
    ocmp, sel = _nsa_cmp(nq_arr, kvcmp, vtcmp, batch, seq_len)
    owin = _nsa_window(nq_arr, kw, ht, batch, seq_len)
    return _nsa_select(nq_arr, ks, ht, sel, ocmp, owin, gates, batch, seq_len)


def _bd(mc, mask_bd):
    return jnp.where(mask_bd, jnp.concatenate([mc] * HEADS, axis=0), jnp.zeros((), mc.dtype))


def _mm_bd(x, mc, mask_bd):
    return _dot(x.astype(BF16), _bd(mc.astype(BF16), mask_bd))


def _gdn_prep_kernel(x_ref, ab_ref, cw_ref, alog_ref, dtb_ref,
                     u_ref, w_ref, qd_ref, in_ref, kdt_ref, gl_ref, xpad_scr, qkv_scr):
    n = pl.program_id(0)
    C = GDN_CHUNK
    W = GROUP_W
    nb, rows = x_ref.shape[0], x_ref.shape[1]
    inst = [(b, c) for b in range(nb) for c in range(rows // C)]

    @pl.when(n == 0)
    def _():
        xpad_scr[:, 0:8, :] = jnp.zeros((nb, 8, xpad_scr.shape[2]), F32)

    cw = cw_ref[...]
    for b in range(nb):
        x = x_ref[b]
        xpad_scr[b, 8:8 + rows, :] = x
        conv = (cw[0:1] * xpad_scr[b, 5:5 + rows, :] + cw[1:2] * xpad_scr[b, 6:6 + rows, :]
                + cw[2:3] * xpad_scr[b, 7:7 + rows, :] + cw[3:4] * x)
        xpad_scr[b, 0:8, :] = x[rows - 8:rows, :]
        qkv_scr[b] = conv * jax.nn.sigmoid(conv)

    r256 = _iota((W, W), 0)
    c256 = _iota((W, W), 1)
    mask_bd = (r256 // HEAD_DIM) == (c256 // HEAD_DIM)
    ones_bd = jnp.where(mask_bd, 1.0, 0.0).astype(BF16)
    eye256 = jnp.where(r256 == c256, 1.0, 0.0).astype(BF16)
    row = _iota((C, W), 0)
    jl = _iota((C, W), 1) % HEAD_DIM
    ltri = jnp.where(_iota((C, C), 1) <= _iota((C, C), 0), 1.0, 0.0).astype(BF16)
    erow = _iota((LANE, W), 0)
    ehead = _iota((LANE, W), 1) // HEAD_DIM
    e_g = jnp.where(erow == ehead, 1.0, 0.0).astype(BF16)
    e_b = jnp.where(erow == ehead + HEADS, 1.0, 0.0).astype(BF16)

    def each(f, *lists):
        return [f(*args) for args in zip(*lists)]

    sl = [slice(c * C, (c + 1) * C) for _, c in inst]
    q = [qkv_scr[b, sl[k], 0:W] for k, (b, _) in enumerate(inst)]
    kk = [qkv_scr[b, sl[k], W:2 * W] for k, (b, _) in enumerate(inst)]
    v = [qkv_scr[b, sl[k], 2 * W:3 * W] for k, (b, _) in enumerate(inst)]
    ab = [ab_ref[b, sl[k], :] for k, (b, _) in enumerate(inst)]

    qn = each(lambda t: t * lax.rsqrt(_dot_x2(t * t, ones_bd) + NORM_EPS) * (HEAD_DIM ** -0.5), q)
    kn = each(lambda t: t * lax.rsqrt(_dot_x2(t * t, ones_bd) + NORM_EPS), kk)

    def gate(a):
        z = a + dtb_ref[...]
        return -jnp.exp(alog_ref[...]) * (jnp.maximum(z, 0.0) + jnp.log1p(jnp.exp(-jnp.abs(z))))

    g_hl = each(lambda a: _dot_x3(gate(a), e_g), ab)
    beta = each(lambda a: _dot_x3(jax.nn.sigmoid(a), e_b), ab)
    gc = each(lambda g: _dot_x3_left(ltri, g), g_hl)
    glast = each(lambda g: g[C - 1:C, :], gc)
    exp_g = each(jnp.exp, gc)
    dmat = each(lambda g: _dot_x3_left(ltri, jnp.where(row > jl, g, 0.0)), g_hl)
    decay = each(lambda d: jnp.where(jl <= row, jnp.exp(d), 0.0), dmat)

    kt4 = each(lambda t: _dot_nt(eye256, jnp.concatenate([t.astype(BF16)] * HEADS, axis=0)), kn)
    kb_mat = each(lambda t: jnp.where(mask_bd, t, 0.0).astype(BF16), kt4)
    kbeta = each(lambda t, bb: t * bb, kn, beta)
    a_c = each(lambda t, m, d: jnp.where(jl < row, _dot(t.astype(BF16), m) * d, 0.0), kbeta, kb_mat, decay)
    intra = each(lambda t, m, d: _dot(t.astype(BF16), m) * d, qn, kb_mat, decay)

    t_c = each(lambda a: jnp.where(jl == row, 1.0, 0.0) - a, a_c)
    p_c = a_c
    for _ in range(5):
        p_c = each(lambda pc: _mm_bd(pc, pc, mask_bd), p_c)
        t_c = each(lambda tc, pc: tc + _mm_bd(tc, pc, mask_bd), t_c, p_c)

    u = each(lambda tc, t, bb: _mm_bd(tc, t * bb, mask_bd), t_c, v, beta)
    w = each(lambda tc, t, e: _mm_bd(tc, t * e, mask_bd), t_c, kbeta, exp_g)
    kdt = each(lambda t, gl, g: _dot_nt(eye256, (t * jnp.exp(gl - g)).astype(BF16)), kn, glast, gc)

    for k, (b, c) in enumerate(inst):
        u_ref[b, sl[k], :] = u[k]
        w_ref[b, sl[k], :] = w[k].astype(w_ref.dtype)
        qd_ref[b, sl[k], :] = (qn[k] * exp_g[k]).astype(qd_ref.dtype)
        in_ref[b, sl[k], :] = intra[k].astype(in_ref.dtype)
        kdt_ref[b, c] = kdt[k].astype(kdt_ref.dtype)
        gl_ref[b, c] = jnp.exp(glast[k])


def _dot_x3_left(w, x):
    hi = x.astype(BF16)
    r = x - hi.astype(F32)
    mid = r.astype(BF16)
    lo = (r - mid.astype(F32)).astype(BF16)
    return _dot(w, hi) + _dot(w, mid) + _dot(w, lo)


def _gdn_scan_kernel(u_ref, w_ref, qd_ref, in_ref, kdt_ref, gl_ref, z_ref, g_ref, o_ref, s_scr):
    n = pl.program_id(0)
    C = GDN_CHUNK
    W = GROUP_W
    nb, rows = u_ref.shape[0], u_ref.shape[1]

    @pl.when(n == 0)
    def _():
        s_scr[...] = jnp.zeros(s_scr.shape, F32)

    mask_bd = (_iota((W, W), 0) // HEAD_DIM) == (_iota((W, W), 1) // HEAD_DIM)
    ones_bd = jnp.where(mask_bd, 1.0, 0.0).astype(BF16)
    s = [s_scr[b] for b in range(nb)]
    for c in range(rows // C):
        sl = slice(c * C, (c + 1) * C)
        sb = [t.astype(BF16) for t in s]
        v_new = [u_ref[b, sl, :] - _dot(w_ref[b, sl, :], sb[b]) for b in range(nb)]
        vb = [t.astype(BF16) for t in v_new]
        s = [s[b] * gl_ref[b, c] + jnp.where(mask_bd, _dot(kdt_ref[b, c], vb[b]), 0.0) for b in range(nb)]
        o = [_dot(qd_ref[b, sl, :], sb[b]) + _dot(in_ref[b, sl, :], _bd(vb[b], mask_bd)) for b in range(nb)]
        for b in range(nb):
            ms = _dot_x2(o[b] * o[b], ones_bd) * (1.0 / HEAD_DIM)
            zz = z_ref[b, sl, :]
            y = o[b] * lax.rsqrt(ms + NORM_EPS) * g_ref[...] * (zz * jax.nn.sigmoid(zz))
            o_ref[b, sl, :] = y.astype(o_ref.dtype)
    for b in range(nb):
        s_scr[b] = s[b]


GDN_PREP_CHUNKS = 8
GDN_SCAN_CHUNKS = 8


def _gdn(hg, conv_w, a_log, dt_bias, norm_g, batch, seq_len):
    C = GDN_CHUNK
    nc = seq_len // C
    W = GROUP_W
    padl = lambda a: jnp.concatenate([a, jnp.zeros((LANE - a.shape[0],), F32)]).reshape(1, LANE)
    hg3 = hg.reshape(batch, seq_len, hg.shape[-1])
    cp = math.gcd(GDN_PREP_CHUNKS, nc)
    rp = cp * C
    blkp = pl.BlockSpec((batch, rp, W), lambda n: (0, n, 0))
    u, w, qd, intra, kdt, gl = pl.pallas_call(
        _gdn_prep_kernel,
        grid=(nc // cp,),
        in_specs=[pl.BlockSpec((batch, rp, 3 * W), lambda n: (0, n, 0)),
                  pl.BlockSpec((batch, rp, LANE), lambda n: (0, n, 8)),
                  pl.BlockSpec((4, 3 * W), lambda n: (0, 0)),
                  pl.BlockSpec((1, LANE), lambda n: (0, 0)),
                  pl.BlockSpec((1, LANE), lambda n: (0, 0))],
        out_specs=[blkp] * 4
        + [pl.BlockSpec((batch, cp, W, C), lambda n: (0, n, 0, 0)),
           pl.BlockSpec((batch, cp, 1, W), lambda n: (0, n, 0, 0))],
        out_shape=[jax.ShapeDtypeStruct((batch, seq_len, W), F32), jax.ShapeDtypeStruct((batch, seq_len, W), BF16),
                   jax.ShapeDtypeStruct((batch, seq_len, W), BF16), jax.ShapeDtypeStruct((batch, seq_len, W), BF16),
                   jax.ShapeDtypeStruct((batch, nc, W, C), BF16),
                   jax.ShapeDtypeStruct((batch, nc, 1, W), F32)],
        scratch_shapes=[pltpu.VMEM((batch, 8 + rp, 3 * W), F32), pltpu.VMEM((batch, rp, 3 * W), F32)],
        compiler_params=_cp(("arbitrary",)),
        name="gdn_prep",
    )(hg3, hg3, conv_w, padl(a_log), padl(dt_bias))

    cs = math.gcd(GDN_SCAN_CHUNKS, nc)
    blk = pl.BlockSpec((batch, cs * C, W), lambda n: (0, n, 0))
    out = pl.pallas_call(
        _gdn_scan_kernel,
        grid=(nc // cs,),
        in_specs=[blk, blk, blk, blk,
                  pl.BlockSpec((batch, cs, W, C), lambda n: (0, n, 0, 0)),
                  pl.BlockSpec((batch, cs, 1, W), lambda n: (0, n, 0, 0)),
                  pl.BlockSpec((batch, cs * C, W), lambda n: (0, n, 3)),
                  pl.BlockSpec((1, W), lambda n: (0, 0))],
        out_specs=blk,
        out_shape=jax.ShapeDtypeStruct((batch, seq_len, W), BF16),
        scratch_shapes=[pltpu.VMEM((batch, W, W), F32)],
        compiler_params=_cp(("arbitrary",)),
        name="gdn_scan",
    )(u, w, qd, intra, kdt, gl, hg3, jnp.tile(norm_g, HEADS).reshape(1, W))
    return out.reshape(batch * seq_len, W)


def kernel(x, w_in, w_out, ffn1_w_gu, ffn1_w_down, ffn2_w_gu, ffn2_w_down, ln1_g, ln1_b, ln2_g, ln2_b, ln3_g, ln3_b, diff_lam_q1, diff_lam_k1, diff_lam_q2, diff_lam_k2, diff_subln_g, gdn_conv_w, gdn_a_log, gdn_dt_bias, gdn_norm_g, nsa_pe_k, nsa_pe_v, nsa_cmp_k_w1, nsa_cmp_k_w2, nsa_cmp_v_w1, nsa_cmp_v_w2):
    B, S, D = x.shape
    depth = w_in.shape[0]
    alpha = (2 * depth) ** 0.25
    tab = _rope_table(S)
    w_out, ffn1_w_gu, ffn1_w_down, ffn2_w_gu, ffn2_w_down = (
        a.astype(BF16) for a in (w_out, ffn1_w_gu, ffn1_w_down, ffn2_w_gu, ffn2_w_down))
    xf = x.reshape(B * S, D)
    for l in range(depth):
        lam_init = 0.8 - 0.6 * math.exp(-0.3 * l)
        xf = _ffn_ln(xf, ffn1_w_gu, ffn1_w_down, l, ln1_g[l], ln1_b[l], alpha)
        wm, wt = _prep_w_in(w_in[l])
        dq, dk, nq_arr, ks, kw, sq, sk, sv, hg, kc, vc, gates, ht = _proj(xf, wm, wt, tab, S)
        lam_p = jnp.stack([diff_lam_q1[l], diff_lam_k1[l], diff_lam_q2[l], diff_lam_k2[l]])
        o_diff = _diff_attention(dq, dk, ht, lam_p, diff_subln_g[l], lam_init, B, S)
        o_gdn = _gdn(hg, gdn_conv_w[l], gdn_a_log[l], gdn_dt_bias[l], gdn_norm_g[l], B, S)
        o_nsa = _nsa(nq_arr, ks, kw, ht, kc, vc, gates, nsa_pe_k[l], nsa_pe_v[l], nsa_cmp_k_w1[l], nsa_cmp_k_w2[l],
                     nsa_cmp_v_w1[l], nsa_cmp_v_w2[l], B, S)
        o_sb = _sb_attention(sq, sk, sv, B, S)
        xf = _outproj_ffn_ln(xf, (o_diff, o_gdn, o_nsa, o_sb), w_out, ln2_g[l], ln2_b[l],
                             ffn2_w_gu, ffn2_w_down, l, ln3_g[l], ln3_b[l], alpha)
    return xf.reshape(B, S, D)
```

```python
import functools
import math

import numpy as np
import jax
import jax.numpy as jnp
from jax import lax
from jax.experimental import pallas as pl
from jax.experimental.pallas import tpu as pltpu

F32 = jnp.float32
BF16 = jnp.bfloat16

HEAD_DIM = 64
HEADS = 4
GROUP_W = HEADS * HEAD_DIM
DIFF_QK = HEAD_DIM // 2
GDN_CHUNK = 64
CMP_BLOCK, CMP_STRIDE = 32, 16
SLC_BLOCK, SLC_TOPN = 64, 16
WINDOW = 512
FORCE = 1e4
ROPE_THETA = 10000.0
LN_EPS = 1e-5
NORM_EPS = 1e-6
NEG = -1e30
LOG2E = 1.4426950408889634

LANE = 128
V7X_VMEM_BYTES = 64 * 1024 * 1024
VMEM_LIMIT = V7X_VMEM_BYTES - 8 * 1024 * 1024

TM_FFN = 512
FF_CHUNK = 256
TM_PROJ = 512
T_ATT = 1024
TQ_CMP = 512
TQ_WIN = 512
T_SEL = 1024


def _cp(sem):
    return pltpu.CompilerParams(dimension_semantics=sem, vmem_limit_bytes=VMEM_LIMIT)


def _iota(shape, dim):
    return lax.broadcasted_iota(jnp.int32, shape, dim)


def _dot(a, b):
    return jnp.dot(a, b, preferred_element_type=F32)


def _dot_nt(a, b):
    return lax.dot_general(a, b, (((1,), (1,)), ((), ())), preferred_element_type=F32)


def _split2(x):
    hi = x.astype(BF16)
    lo = (x - hi.astype(F32)).astype(BF16)
    return hi, lo


def _dot_x2(x, w):
    hi, lo = _split2(x)
    return _dot(hi, w) + _dot(lo, w)


def _dot_x3(x, w):
    hi = x.astype(BF16)
    r = x - hi.astype(F32)
    mid = r.astype(BF16)
    lo = (r - mid.astype(F32)).astype(BF16)
    return _dot(hi, w) + _dot(mid, w) + _dot(lo, w)


def _layer_norm(y, g, b):
    mu = jnp.mean(y, axis=-1, keepdims=True)
    d = y - mu
    var = jnp.mean(d * d, axis=-1, keepdims=True)
    return d * lax.rsqrt(var + LN_EPS) * g + b


def _const_spec(shape):
    nd = len(shape)
    return pl.BlockSpec(shape, lambda *_: (0,) * nd, pipeline_mode=pl.Buffered(1))


def _layer_spec(stacked, layer):
    return pl.BlockSpec((None,) + stacked.shape[1:], lambda *_: (layer, 0, 0), pipeline_mode=pl.Buffered(1))


def _ffn_ln_kernel(x_ref, wgu_ref, wd_ref, g_ref, b_ref, o_ref, *, alpha, d_ff, ff_chunk):
    x = x_ref[...]
    xb = x.astype(BF16)
    acc = None
    for c in range(d_ff // ff_chunk):
        lo = c * ff_chunk
        g = _dot(xb, wgu_ref[:, lo:lo + ff_chunk])
        u = _dot(xb, wgu_ref[:, d_ff + lo:d_ff + lo + ff_chunk])
        a = (g * jax.nn.sigmoid(g) * u).astype(BF16)
        part = _dot(a, wd_ref[lo:lo + ff_chunk, :])
        acc = part if acc is None else acc + part
    o_ref[...] = _layer_norm(alpha * x + 0.5 * acc, g_ref[...], b_ref[...])


def _ffn_ln(x, w_gu, w_down, layer, g, b, alpha):
    T, D = x.shape
    d_ff = w_down.shape[1]
    tm = min(TM_FFN, T)
    ff_chunk = FF_CHUNK if d_ff % FF_CHUNK == 0 else d_ff
    return pl.pallas_call(
        functools.partial(_ffn_ln_kernel, alpha=alpha, d_ff=d_ff, ff_chunk=ff_chunk),
        grid=(T // tm,),
        in_specs=[pl.BlockSpec((tm, D), lambda i: (i, 0)),
                  _layer_spec(w_gu, layer), _layer_spec(w_down, layer),
                  _const_spec((1, D)), _const_spec((1, D))],
        out_specs=pl.BlockSpec((tm, D), lambda i: (i, 0)),
        out_shape=jax.ShapeDtypeStruct((T, D), F32),
        compiler_params=_cp(("parallel",)),
        name="ffn_ln",
    )(x, w_gu, w_down, g.reshape(1, D), b.reshape(1, D))


def _outproj_ffn_kernel(x_ref, o0_ref, o1_ref, o2_ref, o3_ref, wo_ref, g2_ref, b2_ref,
                        wgu_ref, wd_ref, g3_ref, b3_ref, out_ref, *, alpha, d_ff, ff_chunk):
    gw = o0_ref.shape[1]
    mix = None
    for k, o_ref in enumerate((o0_ref, o1_ref, o2_ref, o3_ref)):
        part = _dot(o_ref[...], wo_ref[k * gw:(k + 1) * gw, :])
        mix = part if mix is None else mix + part
    y = _layer_norm(alpha * x_ref[...] + mix, g2_ref[...], b2_ref[...])
    yb = y.astype(BF16)
    acc = None
    for c in range(d_ff // ff_chunk):
        lo = c * ff_chunk
        g = _dot(yb, wgu_ref[:, lo:lo + ff_chunk])
        u = _dot(yb, wgu_ref[:, d_ff + lo:d_ff + lo + ff_chunk])
        a = (g * jax.nn.sigmoid(g) * u).astype(BF16)
        part = _dot(a, wd_ref[lo:lo + ff_chunk, :])
        acc = part if acc is None else acc + part
    out_ref[...] = _layer_norm(alpha * y + 0.5 * acc, g3_ref[...], b3_ref[...])


def _outproj_ffn_ln(x, outs, w_out, g2, b2, w_gu, w_down, layer, g3, b3, alpha):
    T, D = x.shape
    d_ff = w_down.shape[1]
    tm = min(TM_FFN, T)
    gw = outs[0].shape[1]
    ff_chunk = FF_CHUNK if d_ff % FF_CHUNK == 0 else d_ff
    vec = lambda a: a.reshape(1, D)
    return pl.pallas_call(
        functools.partial(_outproj_ffn_kernel, alpha=alpha, d_ff=d_ff, ff_chunk=ff_chunk),
        grid=(T // tm,),
        in_specs=[pl.BlockSpec((tm, D), lambda i: (i, 0))]
        + [pl.BlockSpec((tm, gw), lambda i: (i, 0))] * 4
        + [_layer_spec(w_out, layer), _const_spec((1, D)), _const_spec((1, D)),
           _layer_spec(w_gu, layer), _layer_spec(w_down, layer), _const_spec((1, D)), _const_spec((1, D))],
        out_specs=pl.BlockSpec((tm, D), lambda i: (i, 0)),
        out_shape=jax.ShapeDtypeStruct((T, D), F32),
        compiler_params=_cp(("parallel",)),
        name="outproj_ffn_ln",
    )(x, *outs, w_out, vec(g2), vec(b2), w_gu, w_down, vec(g3), vec(b3))


N_ROPE_BLK = 9


def _proj_kernel(x_ref, wm_ref, wt_ref, tab_ref,
                 dq_ref, dk_ref, nq_ref, ks_ref, kw_ref, sq_ref, sk_ref, sv_ref, hg_ref, kc_ref, vc_ref, gt_ref, ht_ref):
    xb = x_ref[...].astype(BF16)
    ht_ref[...] = _dot_nt(wt_ref[...], xb).astype(BF16)
    nr = N_ROPE_BLK * LANE
    h = _dot(xb, wm_ref[:, :nr])
    tab = tab_ref[...]
    lane = _iota((1, LANE), 1)
    for c in range(N_ROPE_BLK):
        t0 = 0 if c < 4 else 2
        half = (DIFF_QK if c < 4 else HEAD_DIM) // 2
        cs = tab[:, t0 * LANE:(t0 + 1) * LANE]
        sn = tab[:, (t0 + 1) * LANE:(t0 + 2) * LANE]
        if c >= 6:
            cs = jnp.where(lane < HEAD_DIM, cs, 1.0)
            sn = jnp.where(lane < HEAD_DIM, sn, 0.0)
        t = h[:, c * LANE:(c + 1) * LANE]
        rot = jnp.where(lane % (2 * half) < half, pltpu.roll(t, LANE - half, 1), pltpu.roll(t, half, 1))
        val = t * cs + rot * sn
        if c < 6:
            (dq_ref, dk_ref, nq_ref)[c // 2][:, (c % 2) * LANE:(c % 2 + 1) * LANE] = val.astype(BF16)
        elif c < 8:
            (ks_ref, kw_ref)[c - 6][...] = val.astype(BF16)
        else:
            kc_ref[...] = val[:, :HEAD_DIM]
            vc_ref[...] = val[:, HEAD_DIM:]
    hp = _dot(xb, wm_ref[:, 9 * LANE:15 * LANE]).astype(BF16)
    for n, ref in enumerate((sq_ref, sk_ref, sv_ref)):
        ref[...] = hp[:, n * GROUP_W:(n + 1) * GROUP_W]
    hf = _dot(xb, wm_ref[:, 15 * LANE:25 * LANE])
    hg_ref[...] = hf[:, :9 * LANE]
    gt_ref[...] = hf[:, 9 * LANE:10 * LANE]


def _proj(x, wm, wt, tab, seq_len):
    T, D = x.shape
    tm = min(TM_PROJ, seq_len)
    nst = seq_len // tm
    bf16_widths = (GROUP_W, GROUP_W, GROUP_W, LANE, LANE, GROUP_W, GROUP_W, GROUP_W)
    return pl.pallas_call(
        _proj_kernel,
        grid=(T // tm,),
        in_specs=[pl.BlockSpec((tm, D), lambda i: (i, 0)),
                  _const_spec(wm.shape), _const_spec(wt.shape),
                  pl.BlockSpec((tm, 4 * LANE), lambda i: (i % nst, 0))],
        out_specs=[pl.BlockSpec((tm, w), lambda i: (i, 0)) for w in bf16_widths]
        + [pl.BlockSpec((tm, 9 * LANE), lambda i: (i, 0)),
                   pl.BlockSpec((tm, HEAD_DIM), lambda i: (i, 0)),
                   pl.BlockSpec((tm, HEAD_DIM), lambda i: (i, 0)),
                   pl.BlockSpec((tm, LANE), lambda i: (i, 0)),
                   pl.BlockSpec((wt.shape[0], tm), lambda i: (0, i))],
        out_shape=[jax.ShapeDtypeStruct((T, w), BF16) for w in bf16_widths]
        + [jax.ShapeDtypeStruct((T, 9 * LANE), F32),
                   jax.ShapeDtypeStruct((T, HEAD_DIM), F32),
                   jax.ShapeDtypeStruct((T, HEAD_DIM), F32),
                   jax.ShapeDtypeStruct((T, LANE), F32),
                   jax.ShapeDtypeStruct((wt.shape[0], T), BF16)],
        compiler_params=_cp(("parallel",)),
        name="in_proj",
    )(x, wm, wt, tab)


def _prep_w_in(w):
    k = w.shape[0]
    w = w.astype(BF16)
    sizes = ((HEADS * DIFF_QK,) * 4 + (GROUP_W,) + (GROUP_W,) * 4 + (HEADS,) * 2
             + (GROUP_W,) + (HEAD_DIM,) * 6 + (3 * HEADS,) + (GROUP_W,) * 3)
    offs = np.concatenate([[0], np.cumsum(sizes)])
    (dq1, dq2, dk1, dk2, dv, gq, gk, gv, gz, ga, gb,
     nq, nkc, nvc, nks, nvs, nkw, nvw, ngate, sq, sk, sv) = [w[:, offs[i]:offs[i + 1]] for i in range(len(sizes))]
    scale = HEAD_DIM ** -0.5
    pad = lambda a: jnp.concatenate([a, jnp.zeros((k, LANE - a.shape[1]), w.dtype)], axis=1)
    seg = lambda first, last: w[:, offs[first]:offs[last + 1]]
    main = jnp.concatenate(
        [seg(0, 3), nq * scale, seg(14, 17), seg(12, 13),
         sq * scale, seg(20, 21), seg(5, 8), pad(seg(9, 10)), pad(ngate)], axis=1)
    wt = jnp.concatenate([dv, sv, nvs, nvw], axis=1).T
    return main, wt


def _rope_table(seq_len):
    def cs(dim):
        inv = ROPE_THETA ** (-jnp.arange(0, dim, 2, dtype=F32) / dim)
        ang = jnp.arange(seq_len, dtype=F32)[:, None] * inv[None, :]
        c, sgn = jnp.cos(ang), jnp.sin(ang)
        return jnp.concatenate([c, c], axis=1), jnp.concatenate([-sgn, sgn], axis=1)
    cd, sd, cn, sn = lax.optimization_barrier(cs(DIFF_QK) + cs(HEAD_DIM))
    return jnp.concatenate(
        [jnp.tile(cd, (1, 4)), jnp.tile(sd, (1, 4)), jnp.tile(cn, (1, 2)), jnp.tile(sn, (1, 2))], axis=1)


def _tri_pairs(n, descending=False):
    qi, kj = [], []
    for i in range(n):
        js = range(i, -1, -1) if descending else range(i + 1)
        for j in js:
            qi.append(i)
            kj.append(j)
    return jnp.asarray(qi, jnp.int32), jnp.asarray(kj, jnp.int32)


ONES_ROWS = 16
SKEW = 4


def _diff_kernel(qi_ref, kj_ref, q_ref, k_ref, vt_ref, lam_ref, g_ref, o_ref,
                 m_scr, acc_scr, *, lam_init):
    p = pl.program_id(1)
    i = qi_ref[p]
    j = kj_ref[p]
    tq, tk = q_ref.shape[0], k_ref.shape[0]
    c = (DIFF_QK ** -0.5) * LOG2E
    head32 = _iota((1, LANE), 1) // DIFF_QK

    @pl.when(j == 0)
    def _init():
        m_scr[...] = jnp.full(m_scr.shape, NEG, F32)
        acc_scr[...] = jnp.zeros(acc_scr.shape, F32)

    def step(masked):
        parts = [(0, tk // 2, 0, tq), (tk // 2, tk, tq // 2, tq)] if masked else [(0, tk, 0, tq)]
        chains = [(t, h, part) for part in parts for t in range(2) for h in range(HEADS)]

        def scores(t, h, part):
            k0, k1, q0, q1 = part
            kt = k_ref[k0:k1, t * LANE:(t + 1) * LANE]
            km = jnp.where(head32 == h, kt, jnp.zeros_like(kt))
            st = _dot_nt(km, q_ref[q0:q1, t * LANE:(t + 1) * LANE])
            if masked:
                shape = (k1 - k0, q1 - q0)
                st = jnp.where(_iota(shape, 0) + k0 <= _iota(shape, 1) + q0, st, NEG)
            return st

        pend = [scores(*chains[n]) for n in range(SKEW)]
        for n, (t, h, (k0, k1, q0, q1)) in enumerate(chains):
            st = pend.pop(0)
            if n + SKEW < len(chains):
                pend.append(scores(*chains[n + SKEW]))
            idx = t * HEADS + h
            m_prev = m_scr[idx:idx + 1, q0:q1]
            m_new = jnp.maximum(m_prev, jnp.max(st, axis=0, keepdims=True))
            pt = jnp.exp2((st - m_new) * c).astype(BF16)
            alpha = jnp.exp2((m_prev - m_new) * c)
            m_scr[idx:idx + 1, q0:q1] = m_new
            vh = jnp.concatenate([vt_ref[h * HEAD_DIM:(h + 1) * HEAD_DIM, k0:k1],
                                  jnp.ones((ONES_ROWS, k1 - k0), BF16)], axis=0)
            acc_scr[idx, :, q0:q1] = acc_scr[idx, :, q0:q1] * alpha + _dot(vh, pt)

    @pl.when(j < i)
    def _off():
        step(False)

    @pl.when(j == i)
    def _diag():
        step(True)
        lp = lam_ref[...]
        lam = (jnp.exp(jnp.sum(lp[0:1] * lp[1:2], axis=-1, keepdims=True))
               - jnp.exp(jnp.sum(lp[2:3] * lp[3:4], axis=-1, keepdims=True)) + lam_init)
        parts = []
        for h in range(HEADS):
            a0, a1 = acc_scr[h], acc_scr[HEADS + h]
            oh = (a0[:HEAD_DIM] / a0[HEAD_DIM:HEAD_DIM + 1] - lam * (a1[:HEAD_DIM] / a1[HEAD_DIM:HEAD_DIM + 1]))
            ms = jnp.sum(oh * oh, axis=0, keepdims=True) * (1.0 / HEAD_DIM)
            parts.append(oh * lax.rsqrt(ms + NORM_EPS))
        y = jnp.concatenate(parts, axis=0).T * g_ref[...] * (1.0 - lam_init)
        o_ref[...] = y.astype(o_ref.dtype)


def _diff_attention(dq, dk, ht, lam_p, subln_g, lam_init, batch, seq_len):
    t = min(T_ATT, seq_len)
    nq = seq_len // t
    qi, kj = _tri_pairs(nq)
    g_full = jnp.tile(subln_g, HEADS).reshape(1, GROUP_W)
    grid_spec = pltpu.PrefetchScalarGridSpec(
        num_scalar_prefetch=2,
        grid=(batch, qi.shape[0]),
        in_specs=[pl.BlockSpec((t, GROUP_W), lambda b, p, qi, kj: (b * nq + qi[p], 0)),
                  pl.BlockSpec((t, GROUP_W), lambda b, p, qi, kj: (b * nq + kj[p], 0)),
                  pl.BlockSpec((GROUP_W, t), lambda b, p, qi, kj: (0, b * nq + kj[p])),
                  pl.BlockSpec((4, DIFF_QK), lambda b, p, qi, kj: (0, 0)),
                  pl.BlockSpec((1, GROUP_W), lambda b, p, qi, kj: (0, 0))],
        out_specs=pl.BlockSpec((t, GROUP_W), lambda b, p, qi, kj: (b * nq + qi[p], 0)),
        scratch_shapes=[pltpu.VMEM((2 * HEADS, t), F32),
                        pltpu.VMEM((2 * HEADS, HEAD_DIM + ONES_ROWS, t), F32)])
    return pl.pallas_call(
        functools.partial(_diff_kernel, lam_init=lam_init),
        grid_spec=grid_spec,
        out_shape=jax.ShapeDtypeStruct((batch * seq_len, GROUP_W), BF16),
        compiler_params=_cp(("parallel", "arbitrary")),
        name="diff_attn",
    )(qi, kj, dq, dk, ht, lam_p, g_full)


SB_CUM = 256
SB_Z_MIN = -87.0


def _sb_kernel(qi_ref, kj_ref, q_ref, k_ref, v_ref, o_ref, carry_scr, acc_scr):
    p = pl.program_id(1)
    i = qi_ref[p]
    j = kj_ref[p]
    tq, tk = q_ref.shape[0], k_ref.shape[0]
    head64 = _iota((1, GROUP_W), 1) // HEAD_DIM
    cw = min(SB_CUM, tk)

    @pl.when(j == i)
    def _init():
        carry_scr[...] = jnp.zeros(carry_scr.shape, F32)
        acc_scr[...] = jnp.zeros(acc_scr.shape, F32)

    def step(masked):
        m_excl = jnp.where(_iota((cw, cw), 0) > _iota((cw, cw), 1), 1.0, 0.0).astype(BF16)
        parts = [(tq // 2, tq, tk // 2, tk), (0, tq, 0, tk // 2)] if masked else [(0, tq, 0, tk)]
        chains = [(h, part) for part in parts for h in range(HEADS)]
        zs, lbs, xs, cums, atts = {}, {}, {}, {}, {}
        pvs = {part: [] for part in parts}

        def before(part):
            q0, q1, k0, k1 = part
            shape = (q1 - q0, k1 - k0)
            return _iota(shape, 1) + k0 < _iota(shape, 0) + q0

        def st_a(n):
            h, (q0, q1, k0, k1) = chains[n]
            kt = k_ref[k0:k1, :]
            zs[n] = _dot_nt(q_ref[q0:q1, :], jnp.where(head64 == h, kt, jnp.zeros_like(kt)))

        def st_b(n):
            z = jnp.maximum(zs.pop(n), SB_Z_MIN)
            nlb = jnp.log(1.0 + jnp.exp2(z * (-LOG2E)))
            sp = nlb + z
            lbs[n] = nlb
            if masked:
                sp = jnp.where(before(chains[n][1]), sp, 0.0)
            xs[n] = sp.astype(BF16)

        def st_c(n):
            h, (q0, q1, k0, k1) = chains[n]
            x = xs.pop(n)
            blocks = []
            suffix = carry_scr[h, q0:q1, :]
            for blk in reversed(range((k1 - k0) // cw)):
                sl = slice(blk * cw, (blk + 1) * cw)
                cb = _dot(x[:, sl], m_excl) + suffix
                blocks.insert(0, cb)
                suffix = cb[:, 0:1] + x[:, blk * cw:blk * cw + 1].astype(F32)
            carry_scr[h, q0:q1, :] = suffix
            cums[n] = jnp.concatenate(blocks, axis=1)

        def st_d(n):
            att = jnp.exp2((lbs.pop(n) + cums.pop(n)) * (-LOG2E))
            if masked:
                att = jnp.where(before(chains[n][1]), att, 0.0)
            atts[n] = att.astype(BF16)

        def st_e(n):
            h, part = chains[n]
            vt = v_ref[part[2]:part[3], :]
            pvs[part].append(_dot(atts.pop(n), jnp.where(head64 == h, vt, jnp.zeros_like(vt))))

        nch = len(chains)
        st_a(0)
        st_a(1)
        for n in range(nch):
            st_b(n)
            st_c(n)
            if n >= 1:
                st_d(n - 1)
                st_e(n - 1)
                if n + 1 < nch:
                    st_a(n + 1)
        st_d(nch - 1)
        st_e(nch - 1)
        for (q0, q1, _, _), terms in pvs.items():
            acc_scr[q0:q1, :] = acc_scr[q0:q1, :] + sum(terms[1:], terms[0])

    @pl.when(j == i)
    def _diag():
        step(True)

    @pl.when(j < i)
    def _off():
        step(False)

    @pl.when(j == 0)
    def _fin():
        o_ref[...] = acc_scr[...].astype(o_ref.dtype)


def _sb_attention(sq, sk, sv, batch, seq_len):
    t = min(T_ATT, seq_len)
    nq = seq_len // t
    qi, kj = _tri_pairs(nq, descending=True)
    grid_spec = pltpu.PrefetchScalarGridSpec(
        num_scalar_prefetch=2,
        grid=(batch, qi.shape[0]),
        in_specs=[pl.BlockSpec((t, GROUP_W), lambda b, p, qi, kj: (b * nq + qi[p], 0)),
                  pl.BlockSpec((t, GROUP_W), lambda b, p, qi, kj: (b * nq + kj[p], 0)),
                  pl.BlockSpec((t, GROUP_W), lambda b, p, qi, kj: (b * nq + kj[p], 0))],
        out_specs=pl.BlockSpec((t, GROUP_W), lambda b, p, qi, kj: (b * nq + qi[p], 0)),
        scratch_shapes=[pltpu.VMEM((HEADS, t, 1), F32), pltpu.VMEM((t, GROUP_W), F32)])
    return pl.pallas_call(
        _sb_kernel,
        grid_spec=grid_spec,
        out_shape=jax.ShapeDtypeStruct((batch * seq_len, GROUP_W), BF16),
        compiler_params=_cp(("parallel", "arbitrary")),
        name="sb_attn",
    )(qi, kj, sq, sk, sv)


def _stack_heads(q):
    qf = q.astype(F32)
    lo = _iota((1, LANE), 1) < HEAD_DIM
    parts = []
    for blk in (qf[:, :LANE], qf[:, LANE:]):
        parts.append(jnp.where(lo, blk, 0.0))
        parts.append(jnp.where(lo, pltpu.roll(blk, HEAD_DIM, 1), 0.0))
    return jnp.concatenate(parts, axis=0).astype(BF16)


def _nsa_compress_kernel(rk_ref, rv_ref, pek_ref, pev_ref, wk1_ref, wv1_ref, w2_ref, w2vt_ref, o_ref, vt_ref):
    n = rk_ref.shape[0]
    half = rk_ref.shape[1]

    def hidden(r_ref, pe_ref, w1_ref):
        r = r_ref[...]
        y1 = _dot((r + pe_ref[0:1, :]).astype(BF16), w1_ref[0:half, :])
        y2 = _dot((r + pe_ref[1:2, :]).astype(BF16), w1_ref[half:2 * half, :])
        hid = y1 + pltpu.roll(y2, n - 1, 0)
        return (hid * jax.nn.sigmoid(hid)).astype(BF16)

    act = jnp.concatenate([hidden(rk_ref, pek_ref, wk1_ref), hidden(rv_ref, pev_ref, wv1_ref)], axis=1)
    o_ref[...] = _dot(act, w2_ref[...]).astype(o_ref.dtype)
    vt_ref[...] = _dot_nt(w2vt_ref[...], act).astype(vt_ref.dtype)


def _nsa_compress(kc, vc, pe_k, pe_v, ck_w1, ck_w2, cv_w1, cv_w2, batch, seq_len):
    nrow = seq_len // CMP_STRIDE
    width = CMP_STRIDE * HEAD_DIM
    hid = ck_w1.shape[1]
    zo = jnp.zeros((hid, HEAD_DIM), BF16)
    w2 = jnp.concatenate([jnp.concatenate([ck_w2.astype(BF16), zo], axis=1),
                          jnp.concatenate([zo, cv_w2.astype(BF16)], axis=1)], axis=0)
    row_blk = pl.BlockSpec((nrow, width), lambda b: (b, 0))
    return pl.pallas_call(
        _nsa_compress_kernel,
        grid=(batch,),
        in_specs=[row_blk, row_blk, _const_spec((2, width)), _const_spec((2, width)),
                  _const_spec((2 * width, hid)), _const_spec((2 * width, hid)),
                  _const_spec((2 * hid, LANE)), _const_spec((HEAD_DIM, 2 * hid))],
        out_specs=[pl.BlockSpec((nrow, LANE), lambda b: (b, 0)),
                   pl.BlockSpec((HEAD_DIM, nrow), lambda b: (b, 0))],
        out_shape=[jax.ShapeDtypeStruct((batch * nrow, LANE), BF16),
                   jax.ShapeDtypeStruct((batch * HEAD_DIM, nrow), BF16)],
        compiler_params=_cp(("parallel",)),
        name="nsa_compress",
    )(kc.reshape(batch * nrow, width), vc.reshape(batch * nrow, width),
      pe_k.reshape(2, width), pe_v.reshape(2, width), ck_w1.astype(BF16), cv_w1.astype(BF16),
      w2, w2[:, HEAD_DIM:].T)


def _nsa_cmp_kernel(q_ref, kv_ref, vt_ref, ovt_ref, ocmp_ref, bias_ref):
    i = pl.program_id(1)
    tq = q_ref.shape[0]
    ncmp = kv_ref.shape[0]
    nslc = ovt_ref.shape[0]
    qs = _stack_heads(q_ref[...])
    kv = kv_ref[...]
    vt = vt_ref[...]
    tpos = i * tq + _iota((1, tq), 1)
    cm = _iota((ncmp, 1), 0) * CMP_STRIDE + (CMP_BLOCK - 1) <= tpos
    sts = [_dot_nt(kv, qs[h * tq:(h + 1) * tq]) for h in range(HEADS)]
    psum = None
    parts = []
    for h in range(HEADS):
        st = jnp.where(cm, sts[h], NEG)
        e = jnp.exp(st - jnp.max(st, axis=0, keepdims=True))
        pr = jnp.where(cm, e * (1.0 / jnp.sum(e, axis=0, keepdims=True)), 0.0)
        parts.append(_dot(vt, pr.astype(BF16)))
        psum = pr if psum is None else psum + pr
    ocmp_ref[...] = jnp.concatenate(parts, axis=0).T
    hi, lo = _split2(psum)
    imp = _dot(ovt_ref[...], hi) + _dot(ovt_ref[...], lo)
    blk = _iota((nslc, 1), 0)
    cur = tpos // SLC_BLOCK
    work = jnp.where(blk == 0, FORCE, jnp.where(blk == cur, FORCE, jnp.where(blk == cur - 1, FORCE, imp)))
    work = jnp.where(blk <= cur, work, -FORCE)
    sel = jnp.zeros((nslc, tq), F32)
    for _ in range(min(SLC_TOPN, nslc)):
        mx = jnp.max(work, axis=0, keepdims=True)
        first = jnp.min(jnp.where(work == mx, blk, nslc), axis=0, keepdims=True)
        hit = blk == first
        sel = jnp.where(hit, 1.0, sel)
        work = jnp.where(hit, -jnp.inf, work)
    bias_ref[...] = jnp.where(sel > 0.5, 0.0, NEG)


def _nsa_cmp(nq_arr, kvcmp, vtcmp, batch, seq_len):
    tq = min(TQ_CMP, seq_len)
    nq = seq_len // tq
    ncmp = seq_len // CMP_STRIDE
    nslc = seq_len // SLC_BLOCK
    cstart = np.arange(ncmp)[None, :] * CMP_STRIDE
    sstart = np.arange(nslc)[:, None] * SLC_BLOCK
    ovt = (cstart < sstart + SLC_BLOCK) & (cstart + CMP_BLOCK - 1 >= sstart)
    ovt &= (np.arange(ncmp)[None, :] < (seq_len - CMP_BLOCK) // CMP_STRIDE + 1)
    ovt = jnp.asarray(ovt, BF16)
    return pl.pallas_call(
        _nsa_cmp_kernel,
        grid=(batch, nq),
        in_specs=[pl.BlockSpec((tq, GROUP_W), lambda b, i: (b * nq + i, 0)),
                  pl.BlockSpec((ncmp, LANE), lambda b, i: (b, 0)),
                  pl.BlockSpec((HEAD_DIM, ncmp), lambda b, i: (b, 0)),
                  pl.BlockSpec((nslc, ncmp), lambda b, i: (0, 0))],
        out_specs=[pl.BlockSpec((tq, GROUP_W), lambda b, i: (b * nq + i, 0)),
                   pl.BlockSpec((nslc, tq), lambda b, i: (0, b * nq + i))],
        out_shape=[jax.ShapeDtypeStruct((batch * seq_len, GROUP_W), F32),
                   jax.ShapeDtypeStruct((nslc, batch * seq_len), F32)],
        compiler_params=_cp(("parallel", "parallel")),
        name="nsa_cmp_select",
    )(nq_arr, kvcmp, vtcmp, ovt)


def _nsa_win_kernel(q_ref, k_ref, vt_ref, o_ref, *, window):
    i = pl.program_id(1)
    tq = q_ref.shape[0]
    span = tq + window
    base = pl.multiple_of(jnp.maximum(i * tq - window, 0), LANE)
    kv = k_ref[pl.ds(base, span), :]
    vh = jnp.concatenate([vt_ref[:, pl.ds(base, span)], jnp.ones((ONES_ROWS, span), BF16)], axis=0)
    qs = _stack_heads(q_ref[...])
    rel = (i * tq + _iota((1, tq), 1)) - (base + _iota((span, 1), 0))
    bias = jnp.where(rel >= 0, jnp.where(rel < window, 0.0, NEG), NEG)
    sts = [_dot_nt(kv, qs[h * tq:(h + 1) * tq]) for h in range(HEADS)]
    parts = []
    for h in range(HEADS):
        st = sts[h] + bias
        e = jnp.exp(st - jnp.max(st, axis=0, keepdims=True)).astype(BF16)
        r = _dot(vh, e)
        parts.append(r[:HEAD_DIM] / r[HEAD_DIM:HEAD_DIM + 1])
    o_ref[...] = jnp.concatenate(parts, axis=0).T


def _nsa_window(nq_arr, kw, ht, batch, seq_len):
    tq = min(TQ_WIN, seq_len)
    nq = seq_len // tq
    window = min(WINDOW, seq_len - tq)
    vw_row_blk = 2 * GROUP_W // HEAD_DIM + 1
    return pl.pallas_call(
        functools.partial(_nsa_win_kernel, window=window),
        grid=(batch, nq),
        in_specs=[pl.BlockSpec((tq, GROUP_W), lambda b, i: (b * nq + i, 0)),
                  pl.BlockSpec((seq_len, LANE), lambda b, i: (b, 0)),
                  pl.BlockSpec((HEAD_DIM, seq_len), lambda b, i: (vw_row_blk, b))],
        out_specs=pl.BlockSpec((tq, GROUP_W), lambda b, i: (b * nq + i, 0)),
        out_shape=jax.ShapeDtypeStruct((batch * seq_len, GROUP_W), F32),
        compiler_params=_cp(("parallel", "parallel")),
        name="nsa_window",
    )(nq_arr, kw, ht)


def _nsa_sel_kernel(qi_ref, kj_ref, q_ref, k_ref, vt_ref, bias_ref, ocmp_ref, owin_ref, gate_ref, o_ref,
                    qs_scr, m_scr, acc_scr):
    p = pl.program_id(1)
    i = qi_ref[p]
    j = kj_ref[p]
    tq, tk = q_ref.shape[0], k_ref.shape[0]

    @pl.when(j == 0)
    def _init():
        qs = _stack_heads(q_ref[...])
        for h in range(HEADS):
            qs_scr[h] = qs[h * tq:(h + 1) * tq]
        m_scr[...] = jnp.full(m_scr.shape, NEG, F32)
        acc_scr[...] = jnp.zeros(acc_scr.shape, F32)

    def step(masked):
        parts = [(0, tk // 2, 0, tq), (tk // 2, tk, tq // 2, tq)] if masked else [(0, tk, 0, tq)]
        chains = [(h, part) for part in parts for h in range(HEADS)]

        def scores(h, part):
            k0, k1, q0, q1 = part
            st = _dot_nt(k_ref[k0:k1, :], qs_scr[h, q0:q1, :])
            bias = bias_ref[k0 // SLC_BLOCK:k1 // SLC_BLOCK, q0:q1][:, None, :]
            st = (st.reshape((k1 - k0) // SLC_BLOCK, SLC_BLOCK, q1 - q0) + bias).reshape(k1 - k0, q1 - q0)
            if masked:
                shape = (k1 - k0, q1 - q0)
                st = jnp.where(_iota(shape, 0) + k0 <= _iota(shape, 1) + q0, st, NEG)
            return st

        pend = [scores(*chains[n]) for n in range(SKEW)]
        for n, (h, (k0, k1, q0, q1)) in enumerate(chains):
            st = pend.pop(0)
            if n + SKEW < len(chains):
                pend.append(scores(*chains[n + SKEW]))
            m_prev = m_scr[h:h + 1, q0:q1]
            m_new = jnp.maximum(m_prev, jnp.max(st, axis=0, keepdims=True))
            pt = jnp.exp(st - m_new).astype(BF16)
            alpha = jnp.exp(m_prev - m_new)
            m_scr[h:h + 1, q0:q1] = m_new
            vh = jnp.concatenate([vt_ref[:, k0:k1], jnp.ones((ONES_ROWS, k1 - k0), BF16)], axis=0)
            acc_scr[h, :, q0:q1] = acc_scr[h, :, q0:q1] * alpha + _dot(vh, pt)

    @pl.when(j < i)
    def _past():
        step(False)

    @pl.when(j == i)
    def _diag():
        step(True)

    @pl.when(j == i)
    def _fin():
        parts = []
        for h in range(HEADS):
            a = acc_scr[h]
            parts.append(a[:HEAD_DIM] / a[HEAD_DIM:HEAD_DIM + 1])
        osel = jnp.concatenate(parts, axis=0).T
        sig = jax.nn.sigmoid(gate_ref[...])
        grow = _iota((LANE, 1), 0)
        ghead = _iota((1, GROUP_W), 1) // HEAD_DIM
        out = None
        for br, o_br in enumerate((ocmp_ref[...], osel, owin_ref[...])):
            e_br = jnp.where(grow == 3 * ghead + br, 1.0, 0.0).astype(BF16)
            term = _dot_x3(sig, e_br) * o_br
            out = term if out is None else out + term
        o_ref[...] = out.astype(o_ref.dtype)


def _nsa_select(nq_arr, ks, ht, sel, ocmp, owin, gates, batch, seq_len):
    tq = tk = min(T_SEL, seq_len)
    nq = nk = seq_len // tq
    qi, kj = _tri_pairs(nq)
    qrow = lambda b, p, qi, kj: (b * nq + qi[p], 0)
    vs_row_blk = 2 * GROUP_W // HEAD_DIM
    grid_spec = pltpu.PrefetchScalarGridSpec(
        num_scalar_prefetch=2,
        grid=(batch, qi.shape[0]),
        in_specs=[pl.BlockSpec((tq, GROUP_W), qrow),
                  pl.BlockSpec((tk, LANE), lambda b, p, qi, kj: (b * nk + kj[p], 0)),
                  pl.BlockSpec((HEAD_DIM, tk), lambda b, p, qi, kj: (vs_row_blk, b * nk + kj[p])),
                  pl.BlockSpec((tk // SLC_BLOCK, tq), lambda b, p, qi, kj: (kj[p], b * nq + qi[p])),
                  pl.BlockSpec((tq, GROUP_W), qrow), pl.BlockSpec((tq, GROUP_W), qrow),
                  pl.BlockSpec((tq, LANE), qrow)],
        out_specs=pl.BlockSpec((tq, GROUP_W), qrow),
        scratch_shapes=[pltpu.VMEM((HEADS, tq, LANE), BF16), pltpu.VMEM((HEADS, tq), F32),
                        pltpu.VMEM((HEADS, HEAD_DIM + ONES_ROWS, tq), F32)])
    return pl.pallas_call(
        _nsa_sel_kernel,
        grid_spec=grid_spec,
        out_shape=jax.ShapeDtypeStruct((batch * seq_len, GROUP_W), BF16),
        compiler_params=_cp(("parallel", "arbitrary")),
        name="nsa_select_gate",
    )(qi, kj, nq_arr, ks, ht, sel, ocmp, owin, gates)


def _nsa(nq_arr, ks, kw, ht, kc, vc, gates, pe_k, pe_v, ck_w1, ck_w2, cv_w1, cv_w2, batch, seq_len):
    kvcmp, vtcmp = _nsa_compress(kc, vc, pe_k, pe_v, ck_w1, ck_w2, cv_w1, cv_w2, batch, seq_len)
    ocmp, sel = _nsa_cmp(nq_arr, kvcmp, vtcmp, batch, seq_len)
    owin = _nsa_window(nq_arr, kw, ht, batch, seq_len)
    return _nsa_select(nq_arr, ks, ht, sel, ocmp, owin, gates, batch, seq_len)


def _bd(mc, mask_bd):
    return jnp.where(mask_bd, jnp.concatenate([mc] * HEADS, axis=0), jnp.zeros((), mc.dtype))


def _mm_bd(x, mc, mask_bd):
    return _dot(x.astype(BF16), _bd(mc.astype(BF16), mask_bd))


def _gdn_prep_kernel(x_ref, ab_ref, cw_ref, alog_ref, dtb_ref,
                     u_ref, w_ref, qd_ref, in_ref, kdt_ref, gl_ref, xpad_scr, qkv_scr):
    n = pl.program_id(0)
    C = GDN_CHUNK
    W = GROUP_W
    nb, rows = x_ref.shape[0], x_ref.shape[1]
    inst = [(b, c) for b in range(nb) for c in range(rows // C)]

    @pl.when(n == 0)
    def _():
        xpad_scr[:, 0:8, :] = jnp.zeros((nb, 8, xpad_scr.shape[2]), F32)

    cw = cw_ref[...]
    for b in range(nb):
        x = x_ref[b]
        xpad_scr[b, 8:8 + rows, :] = x
        conv = (cw[0:1] * xpad_scr[b, 5:5 + rows, :] + cw[1:2] * xpad_scr[b, 6:6 + rows, :]
                + cw[2:3] * xpad_scr[b, 7:7 + rows, :] + cw[3:4] * x)
        xpad_scr[b, 0:8, :] = x[rows - 8:rows, :]
        qkv_scr[b] = conv * jax.nn.sigmoid(conv)

    r256 = _iota((W, W), 0)
    c256 = _iota((W, W), 1)
    mask_bd = (r256 // HEAD_DIM) == (c256 // HEAD_DIM)
    ones_bd = jnp.where(mask_bd, 1.0, 0.0).astype(BF16)
    eye256 = jnp.where(r256 == c256, 1.0, 0.0).astype(BF16)
    row = _iota((C, W), 0)
    jl = _iota((C, W), 1) % HEAD_DIM
    ltri = jnp.where(_iota((C, C), 1) <= _iota((C, C), 0), 1.0, 0.0).astype(BF16)
    erow = _iota((LANE, W), 0)
    ehead = _iota((LANE, W), 1) // HEAD_DIM
    e_g = jnp.where(erow == ehead, 1.0, 0.0).astype(BF16)
    e_b = jnp.where(erow == ehead + HEADS, 1.0, 0.0).astype(BF16)

    def each(f, *lists):
        return [f(*args) for args in zip(*lists)]

    sl = [slice(c * C, (c + 1) * C) for _, c in inst]
    q = [qkv_scr[b, sl[k], 0:W] for k, (b, _) in enumerate(inst)]
    kk = [qkv_scr[b, sl[k], W:2 * W] for k, (b, _) in enumerate(inst)]
    v = [qkv_scr[b, sl[k], 2 * W:3 * W] for k, (b, _) in enumerate(inst)]
    ab = [ab_ref[b, sl[k], :] for k, (b, _) in enumerate(inst)]

    qn = each(lambda t: t * lax.rsqrt(_dot_x2(t * t, ones_bd) + NORM_EPS) * (HEAD_DIM ** -0.5), q)
    kn = each(lambda t: t * lax.rsqrt(_dot_x2(t * t, ones_bd) + NORM_EPS), kk)

    def gate(a):
        z = a + dtb_ref[...]
        return -jnp.exp(alog_ref[...]) * (jnp.maximum(z, 0.0) + jnp.log1p(jnp.exp(-jnp.abs(z))))

    g_hl = each(lambda a: _dot_x3(gate(a), e_g), ab)
    beta = each(lambda a: _dot_x3(jax.nn.sigmoid(a), e_b), ab)
    gc = each(lambda g: _dot_x3_left(ltri, g), g_hl)
    glast = each(lambda g: g[C - 1:C, :], gc)
    exp_g = each(jnp.exp, gc)
    dmat = each(lambda g: _dot_x3_left(ltri, jnp.where(row > jl, g, 0.0)), g_hl)
    decay = each(lambda d: jnp.where(jl <= row, jnp.exp(d), 0.0), dmat)

    kt4 = each(lambda t: _dot_nt(eye256, jnp.concatenate([t.astype(BF16)] * HEADS, axis=0)), kn)
    kb_mat = each(lambda t: jnp.where(mask_bd, t, 0.0).astype(BF16), kt4)
    kbeta = each(lambda t, bb: t * bb, kn, beta)
    a_c = each(lambda t, m, d: jnp.where(jl < row, _dot(t.astype(BF16), m) * d, 0.0), kbeta, kb_mat, decay)
    intra = each(lambda t, m, d: _dot(t.astype(BF16), m) * d, qn, kb_mat, decay)

    t_c = each(lambda a: jnp.where(jl == row, 1.0, 0.0) - a, a_c)
    p_c = a_c
    for _ in range(5):
        p_c = each(lambda pc: _mm_bd(pc, pc, mask_bd), p_c)
        t_c = each(lambda tc, pc: tc + _mm_bd(tc, pc, mask_bd), t_c, p_c)

    u = each(lambda tc, t, bb: _mm_bd(tc, t * bb, mask_bd), t_c, v, beta)
    w = each(lambda tc, t, e: _mm_bd(tc, t * e, mask_bd), t_c, kbeta, exp_g)
    kdt = each(lambda t, gl, g: _dot_nt(eye256, (t * jnp.exp(gl - g)).astype(BF16)), kn, glast, gc)

    for k, (b, c) in enumerate(inst):
        u_ref[b, sl[k], :] = u[k]
        w_ref[b, sl[k], :] = w[k].astype(w_ref.dtype)
        qd_ref[b, sl[k], :] = (qn[k] * exp_g[k]).astype(qd_ref.dtype)
        in_ref[b, sl[k], :] = intra[k].astype(in_ref.dtype)
        kdt_ref[b, c] = kdt[k].astype(kdt_ref.dtype)
        gl_ref[b, c] = jnp.exp(glast[k])


def _dot_x3_left(w, x):
    hi = x.astype(BF16)
    r = x - hi.astype(F32)
    mid = r.astype(BF16)
    lo = (r - mid.astype(F32)).astype(BF16)
    return _dot(w, hi) + _dot(w, mid) + _dot(w, lo)


def _gdn_scan_kernel(u_ref, w_ref, qd_ref, in_ref, kdt_ref, gl_ref, z_ref, g_ref, o_ref, s_scr):
    n = pl.program_id(0)
    C = GDN_CHUNK
    W = GROUP_W
    nb, rows = u_ref.shape[0], u_ref.shape[1]

    @pl.when(n == 0)
    def _():
        s_scr[...] = jnp.zeros(s_scr.shape, F32)

    mask_bd = (_iota((W, W), 0) // HEAD_DIM) == (_iota((W, W), 1) // HEAD_DIM)
    ones_bd = jnp.where(mask_bd, 1.0, 0.0).astype(BF16)
    s = [s_scr[b] for b in range(nb)]
    for c in range(rows // C):
        sl = slice(c * C, (c + 1) * C)
        sb = [t.astype(BF16) for t in s]
        v_new = [u_ref[b, sl, :] - _dot(w_ref[b, sl, :], sb[b]) for b in range(nb)]
        vb = [t.astype(BF16) for t in v_new]
        s = [s[b] * gl_ref[b, c] + jnp.where(mask_bd, _dot(kdt_ref[b, c], vb[b]), 0.0) for b in range(nb)]
        o = [_dot(qd_ref[b, sl, :], sb[b]) + _dot(in_ref[b, sl, :], _bd(vb[b], mask_bd)) for b in range(nb)]
        for b in range(nb):
            ms = _dot_x2(o[b] * o[b], ones_bd) * (1.0 / HEAD_DIM)
            zz = z_ref[b, sl, :]
            y = o[b] * lax.rsqrt(ms + NORM_EPS) * g_ref[...] * (zz * jax.nn.sigmoid(zz))
            o_ref[b, sl, :] = y.astype(o_ref.dtype)
    for b in range(nb):
        s_scr[b] = s[b]


GDN_PREP_CHUNKS = 8
GDN_SCAN_CHUNKS = 8


def _gdn(hg, conv_w, a_log, dt_bias, norm_g, batch, seq_len):
    C = GDN_CHUNK
    nc = seq_len // C
    W = GROUP_W
    padl = lambda a: jnp.concatenate([a, jnp.zeros((LANE - a.shape[0],), F32)]).reshape(1, LANE)
    hg3 = hg.reshape(batch, seq_len, hg.shape[-1])
    cp = math.gcd(GDN_PREP_CHUNKS, nc)
    rp = cp * C
    blkp = pl.BlockSpec((batch, rp, W), lambda n: (0, n, 0))
    u, w, qd, intra, kdt, gl = pl.pallas_call(
        _gdn_prep_kernel,
        grid=(nc // cp,),
        in_specs=[pl.BlockSpec((batch, rp, 3 * W), lambda n: (0, n, 0)),
                  pl.BlockSpec((batch, rp, LANE), lambda n: (0, n, 8)),
                  pl.BlockSpec((4, 3 * W), lambda n: (0, 0)),
                  pl.BlockSpec((1, LANE), lambda n: (0, 0)),
                  pl.BlockSpec((1, LANE), lambda n: (0, 0))],
        out_specs=[blkp] * 4
        + [pl.BlockSpec((batch, cp, W, C), lambda n: (0, n, 0, 0)),
           pl.BlockSpec((batch, cp, 1, W), lambda n: (0, n, 0, 0))],
        out_shape=[jax.ShapeDtypeStruct((batch, seq_len, W), F32), jax.ShapeDtypeStruct((batch, seq_len, W), BF16),
                   jax.ShapeDtypeStruct((batch, seq_len, W), BF16), jax.ShapeDtypeStruct((batch, seq_len, W), BF16),
                   jax.ShapeDtypeStruct((batch, nc, W, C), BF16),
                   jax.ShapeDtypeStruct((batch, nc, 1, W), F32)],
        scratch_shapes=[pltpu.VMEM((batch, 8 + rp, 3 * W), F32), pltpu.VMEM((batch, rp, 3 * W), F32)],
        compiler_params=_cp(("arbitrary",)),
        name="gdn_prep",
    )(hg3, hg3, conv_w, padl(a_log), padl(dt_bias))

    cs = math.gcd(GDN_SCAN_CHUNKS, nc)
    blk = pl.BlockSpec((batch, cs * C, W), lambda n: (0, n, 0))
    out = pl.pallas_call(
        _gdn_scan_kernel,
        grid=(nc // cs,),
        in_specs=[blk, blk, blk, blk,
                  pl.BlockSpec((batch, cs, W, C), lambda n: (0, n, 0, 0)),
                  pl.BlockSpec((batch, cs, 1, W), lambda n: (0, n, 0, 0)),
                  pl.BlockSpec((batch, cs * C, W), lambda n: (0, n, 3)),
                  pl.BlockSpec((1, W), lambda n: (0, 0))],
        out_specs=blk,
        out_shape=jax.ShapeDtypeStruct((batch, seq_len, W), BF16),
        scratch_shapes=[pltpu.VMEM((batch, W, W), F32)],
        compiler_params=_cp(("arbitrary",)),
        name="gdn_scan",
    )(u, w, qd, intra, kdt, gl, hg3, jnp.tile(norm_g, HEADS).reshape(1, W))
    return out.reshape(batch * seq_len, W)


def kernel(x, w_in, w_out, ffn1_w_gu, ffn1_w_down, ffn2_w_gu, ffn2_w_down, ln1_g, ln1_b, ln2_g, ln2_b, ln3_g, ln3_b, diff_lam_q1, diff_lam_k1, diff_lam_q2, diff_lam_k2, diff_subln_g, gdn_conv_w, gdn_a_log, gdn_dt_bias, gdn_norm_g, nsa_pe_k, nsa_pe_v, nsa_cmp_k_w1, nsa_cmp_k_w2, nsa_cmp_v_w1, nsa_cmp_v_w2):
    B, S, D = x.shape
    depth = w_in.shape[0]
    alpha = (2 * depth) ** 0.25
    tab = _rope_table(S)
    w_out, ffn1_w_gu, ffn1_w_down, ffn2_w_gu, ffn2_w_down = (
        a.astype(BF16) for a in (w_out, ffn1_w_gu, ffn1_w_down, ffn2_w_gu, ffn2_w_down))
    xf = x.reshape(B * S, D)
    for l in range(depth):
        lam_init = 0.8 - 0.6 * math.exp(-0.3 * l)
        xf = _ffn_ln(xf, ffn1_w_gu, ffn1_w_down, l, ln1_g[l], ln1_b[l], alpha)
        wm, wt = _prep_w_in(w_in[l])
        dq, dk, nq_arr, ks, kw, sq, sk, sv, hg, kc, vc, gates, ht = _proj(xf, wm, wt, tab, S)
        lam_p = jnp.stack([diff_lam_q1[l], diff_lam_k1[l], diff_lam_q2[l], diff_lam_k2[l]])
        o_diff = _diff_attention(dq, dk, ht, lam_p, diff_subln_g[l], lam_init, B, S)
        o_gdn = _gdn(hg, gdn_conv_w[l], gdn_a_log[l], gdn_dt_bias[l], gdn_norm_g[l], B, S)
        o_nsa = _nsa(nq_arr, ks, kw, ht, kc, vc, gates, nsa_pe_k[l], nsa_pe_v[l], nsa_cmp_k_w1[l], nsa_cmp_k_w2[l],
                     nsa_cmp_v_w1[l], nsa_cmp_v_w2[l], B, S)
        o_sb = _sb_attention(sq, sk, sv, B, S)
        xf = _outproj_ffn_ln(xf, (o_diff, o_gdn, o_nsa, o_sb), w_out, ln2_g[l], ln2_b[l],
                             ffn2_w_gu, ffn2_w_down, l, ln3_g[l], ln3_b[l], alpha)
    return xf.reshape(B, S, D)
```

```python
import functools
import math

import numpy as np
import jax
import jax.numpy as jnp
from jax import lax
from jax.experimental import pallas as pl
from jax.experimental.pallas import tpu as pltpu

F32 = jnp.float32
BF16 = jnp.bfloat16

HEAD_DIM = 64
HEADS = 4
GROUP_W = HEADS * HEAD_DIM
DIFF_QK = HEAD_DIM // 2
GDN_CHUNK = 64
CMP_BLOCK, CMP_STRIDE = 32, 16
SLC_BLOCK, SLC_TOPN = 64, 16
WINDOW = 512
FORCE = 1e4
ROPE_THETA = 10000.0
LN_EPS = 1e-5
NORM_EPS = 1e-6
NEG = -1e30
LOG2E = 1.4426950408889634

LANE = 128
V7X_VMEM_BYTES = 64 * 1024 * 1024
VMEM_LIMIT = V7X_VMEM_BYTES - 8 * 1024 * 1024

TM_FFN = 512
FF_CHUNK = 256
TM_PROJ = 512
T_ATT = 1024
TQ_CMP = 512
TQ_WIN = 512
T_SEL = 1024


def _cp(sem):
    return pltpu.CompilerParams(dimension_semantics=sem, vmem_limit_bytes=VMEM_LIMIT)


def _iota(shape, dim):
    return lax.broadcasted_iota(jnp.int32, shape, dim)


def _dot(a, b):
    return jnp.dot(a, b, preferred_element_type=F32)


def _dot_nt(a, b):
    return lax.dot_general(a, b, (((1,), (1,)), ((), ())), preferred_element_type=F32)


def _split2(x):
    hi = x.astype(BF16)
    lo = (x - hi.astype(F32)).astype(BF16)
    return hi, lo


def _dot_x2(x, w):
    hi, lo = _split2(x)
    return _dot(hi, w) + _dot(lo, w)


def _dot_x3(x, w):
    hi = x.astype(BF16)
    r = x - hi.astype(F32)
    mid = r.astype(BF16)
    lo = (r - mid.astype(F32)).astype(BF16)
    return _dot(hi, w) + _dot(mid, w) + _dot(lo, w)


def _layer_norm(y, g, b):
    mu = jnp.mean(y, axis=-1, keepdims=True)
    d = y - mu
    var = jnp.mean(d * d, axis=-1, keepdims=True)
    return d * lax.rsqrt(var + LN_EPS) * g + b


def _const_spec(shape):
    nd = len(shape)
    return pl.BlockSpec(shape, lambda *_: (0,) * nd, pipeline_mode=pl.Buffered(1))


def _layer_spec(stacked, layer):
    return pl.BlockSpec((None,) + stacked.shape[1:], lambda *_: (layer, 0, 0), pipeline_mode=pl.Buffered(1))


def _ffn_ln_kernel(x_ref, wgu_ref, wd_ref, g_ref, b_ref, o_ref, *, alpha, d_ff, ff_chunk):
    x = x_ref[...]
    xb = x.astype(BF16)
    acc = None
    for c in range(d_ff // ff_chunk):
        lo = c * ff_chunk
        g = _dot(xb, wgu_ref[:, lo:lo + ff_chunk])
        u = _dot(xb, wgu_ref[:, d_ff + lo:d_ff + lo + ff_chunk])
        a = (g * jax.nn.sigmoid(g) * u).astype(BF16)
        part = _dot(a, wd_ref[lo:lo + ff_chunk, :])
        acc = part if acc is None else acc + part
    o_ref[...] = _layer_norm(alpha * x + 0.5 * acc, g_ref[...], b_ref[...])


def _ffn_ln(x, w_gu, w_down, layer, g, b, alpha):
    T, D = x.shape
    d_ff = w_down.shape[1]
    tm = min(TM_FFN, T)
    ff_chunk = FF_CHUNK if d_ff % FF_CHUNK == 0 else d_ff
    return pl.pallas_call(
        functools.partial(_ffn_ln_kernel, alpha=alpha, d_ff=d_ff, ff_chunk=ff_chunk),
        grid=(T // tm,),
        in_specs=[pl.BlockSpec((tm, D), lambda i: (i, 0)),
                  _layer_spec(w_gu, layer), _layer_spec(w_down, layer),
                  _const_spec((1, D)), _const_spec((1, D))],
        out_specs=pl.BlockSpec((tm, D), lambda i: (i, 0)),
        out_shape=jax.ShapeDtypeStruct((T, D), F32),
        compiler_params=_cp(("parallel",)),
        name="ffn_ln",
    )(x, w_gu, w_down, g.reshape(1, D), b.reshape(1, D))


def _outproj_ffn_kernel(x_ref, o0_ref, o1_ref, o2_ref, o3_ref, wo_ref, g2_ref, b2_ref,
                        wgu_ref, wd_ref, g3_ref, b3_ref, out_ref, *, alpha, d_ff, ff_chunk):
    gw = o0_ref.shape[1]
    mix = None
    for k, o_ref in enumerate((o0_ref, o1_ref, o2_ref, o3_ref)):
        part = _dot(o_ref[...], wo_ref[k * gw:(k + 1) * gw, :])
        mix = part if mix is None else mix + part
    y = _layer_norm(alpha * x_ref[...] + mix, g2_ref[...], b2_ref[...])
    yb = y.astype(BF16)
    acc = None
    for c in range(d_ff // ff_chunk):
        lo = c * ff_chunk
        g = _dot(yb, wgu_ref[:, lo:lo + ff_chunk])
        u = _dot(yb, wgu_ref[:, d_ff + lo:d_ff + lo + ff_chunk])
        a = (g * jax.nn.sigmoid(g) * u).astype(BF16)
        part = _dot(a, wd_ref[lo:lo + ff_chunk, :])
        acc = part if acc is None else acc + part
    out_ref[...] = _layer_norm(alpha * y + 0.5 * acc, g3_ref[...], b3_ref[...])


def _outproj_ffn_ln(x, outs, w_out, g2, b2, w_gu, w_down, layer, g3, b3, alpha):
    T, D = x.shape
    d_ff = w_down.shape[1]
    tm = min(TM_FFN, T)
    gw = outs[0].shape[1]
    ff_chunk = FF_CHUNK if d_ff % FF_CHUNK == 0 else d_ff
    vec = lambda a: a.reshape(1, D)
    return pl.pallas_call(
        functools.partial(_outproj_ffn_kernel, alpha=alpha, d_ff=d_ff, ff_chunk=ff_chunk),
        grid=(T // tm,),
        in_specs=[pl.BlockSpec((tm, D), lambda i: (i, 0))]
        + [pl.BlockSpec((tm, gw), lambda i: (i, 0))] * 4
        + [_layer_spec(w_out, layer), _const_spec((1, D)), _const_spec((1, D)),
           _layer_spec(w_gu, layer), _layer_spec(w_down, layer), _const_spec((1, D)), _const_spec((1, D))],
        out_specs=pl.BlockSpec((tm, D), lambda i: (i, 0)),
        out_shape=jax.ShapeDtypeStruct((T, D), F32),
        compiler_params=_cp(("parallel",)),
        name="outproj_ffn_ln",
    )(x, *outs, w_out, vec(g2), vec(b2), w_gu, w_down, vec(g3), vec(b3))


N_ROPE_BLK = 9


def _proj_kernel(x_ref, wm_ref, wt_ref, tab_ref,
                 dq_ref, dk_ref, nq_ref, ks_ref, kw_ref, sq_ref, sk_ref, sv_ref, hg_ref, kc_ref, vc_ref, gt_ref, ht_ref):
    xb = x_ref[...].astype(BF16)
    ht_ref[...] = _dot_nt(wt_ref[...], xb).astype(BF16)
    nr = N_ROPE_BLK * LANE
    h = _dot(xb, wm_ref[:, :nr])
    tab = tab_ref[...]
    lane = _iota((1, LANE), 1)
    tile = lambda t, reps: jnp.concatenate([t] * reps, axis=1)
    tab_d = (tile(tab[:, 0:32], 4), tile(tab[:, 32:64], 4))
    tab_n = (tile(tab[:, 64:128], 2), tile(tab[:, 128:192], 2))
    for c in range(N_ROPE_BLK):
        half = (DIFF_QK if c < 4 else HEAD_DIM) // 2
        cs, sn = (tab_d if c < 4 else tab_n)
        if c >= 6:
            cs = jnp.where(lane < HEAD_DIM, cs, 1.0)
            sn = jnp.where(lane < HEAD_DIM, sn, 0.0)
        t = h[:, c * LANE:(c + 1) * LANE]
        rot = jnp.where(lane % (2 * half) < half, pltpu.roll(t, LANE - half, 1), pltpu.roll(t, half, 1))
        val = t * cs + rot * sn
        if c < 6:
            (dq_ref, dk_ref, nq_ref)[c // 2][:, (c % 2) * LANE:(c % 2 + 1) * LANE] = val.astype(BF16)
        elif c < 8:
            (ks_ref, kw_ref)[c - 6][...] = val.astype(BF16)
        else:
            kc_ref[...] = val[:, :HEAD_DIM]
            vc_ref[...] = val[:, HEAD_DIM:]
    hp = _dot(xb, wm_ref[:, 9 * LANE:15 * LANE]).astype(BF16)
    for n, ref in enumerate((sq_ref, sk_ref, sv_ref)):
        ref[...] = hp[:, n * GROUP_W:(n + 1) * GROUP_W]
    hf = _dot(xb, wm_ref[:, 15 * LANE:25 * LANE])
    hg_ref[...] = hf[:, :9 * LANE]
    gt_ref[...] = hf[:, 9 * LANE:10 * LANE]


def _proj(x, wm, wt, tab, seq_len):
    T, D = x.shape
    tm = min(TM_PROJ, seq_len)
    nst = seq_len // tm
    bf16_widths = (GROUP_W, GROUP_W, GROUP_W, LANE, LANE, GROUP_W, GROUP_W, GROUP_W)
    return pl.pallas_call(
        _proj_kernel,
        grid=(T // tm,),
        in_specs=[pl.BlockSpec((tm, D), lambda i: (i, 0)),
                  _const_spec(wm.shape), _const_spec(wt.shape),
                  pl.BlockSpec((tm, 2 * LANE), lambda i: (i % nst, 0))],
        out_specs=[pl.BlockSpec((tm, w), lambda i: (i, 0)) for w in bf16_widths]
        + [pl.BlockSpec((tm, 9 * LANE), lambda i: (i, 0)),
                   pl.BlockSpec((tm, HEAD_DIM), lambda i: (i, 0)),
                   pl.BlockSpec((tm, HEAD_DIM), lambda i: (i, 0)),
                   pl.BlockSpec((tm, LANE), lambda i: (i, 0)),
                   pl.BlockSpec((wt.shape[0], tm), lambda i: (0, i))],
        out_shape=[jax.ShapeDtypeStruct((T, w), BF16) for w in bf16_widths]
        + [jax.ShapeDtypeStruct((T, 9 * LANE), F32),
                   jax.ShapeDtypeStruct((T, HEAD_DIM), F32),
                   jax.ShapeDtypeStruct((T, HEAD_DIM), F32),
                   jax.ShapeDtypeStruct((T, LANE), F32),
                   jax.ShapeDtypeStruct((wt.shape[0], T), BF16)],
        compiler_params=_cp(("parallel",)),
        name="in_proj",
    )(x, wm, wt, tab)


def _prep_w_in(w):
    k = w.shape[0]
    w = w.astype(BF16)
    sizes = ((HEADS * DIFF_QK,) * 4 + (GROUP_W,) + (GROUP_W,) * 4 + (HEADS,) * 2
             + (GROUP_W,) + (HEAD_DIM,) * 6 + (3 * HEADS,) + (GROUP_W,) * 3)
    offs = np.concatenate([[0], np.cumsum(sizes)])
    (dq1, dq2, dk1, dk2, dv, gq, gk, gv, gz, ga, gb,
     nq, nkc, nvc, nks, nvs, nkw, nvw, ngate, sq, sk, sv) = [w[:, offs[i]:offs[i + 1]] for i in range(len(sizes))]
    scale = HEAD_DIM ** -0.5
    pad = lambda a: jnp.concatenate([a, jnp.zeros((k, LANE - a.shape[1]), w.dtype)], axis=1)
    seg = lambda first, last: w[:, offs[first]:offs[last + 1]]
    main = jnp.concatenate(
        [seg(0, 3), nq * scale, seg(14, 17), seg(12, 13),
         sq * scale, seg(20, 21), seg(5, 8), pad(seg(9, 10)), pad(ngate)], axis=1)
    wt = jnp.concatenate([dv, sv, nvs, nvw], axis=1).T
    return main, wt


def _rope_table(seq_len):
    def cs(dim):
        inv = ROPE_THETA ** (-jnp.arange(0, dim, 2, dtype=F32) / dim)
        ang = jnp.arange(seq_len, dtype=F32)[:, None] * inv[None, :]
        c, sgn = jnp.cos(ang), jnp.sin(ang)
        return jnp.concatenate([c, c], axis=1), jnp.concatenate([-sgn, sgn], axis=1)
    cd, sd, cn, sn = lax.optimization_barrier(cs(DIFF_QK) + cs(HEAD_DIM))
    return jnp.concatenate([cd, sd, cn, sn, jnp.zeros((seq_len, HEAD_DIM), F32)], axis=1)


def _tri_pairs(n, descending=False):
    qi, kj = [], []
    for i in range(n):
        js = range(i, -1, -1) if descending else range(i + 1)
        for j in js:
            qi.append(i)
            kj.append(j)
    return jnp.asarray(qi, jnp.int32), jnp.asarray(kj, jnp.int32)


ONES_ROWS = 16
SKEW = 4


def _diff_kernel(qi_ref, kj_ref, q_ref, k_ref, vt_ref, lam_ref, g_ref, o_ref,
                 m_scr, acc_scr, *, lam_init):
    p = pl.program_id(1)
    i = qi_ref[p]
    j = kj_ref[p]
    tq, tk = q_ref.shape[0], k_ref.shape[0]
    c = (DIFF_QK ** -0.5) * LOG2E
    head32 = _iota((1, LANE), 1) // DIFF_QK

    @pl.when(j == 0)
    def _init():
        m_scr[...] = jnp.full(m_scr.shape, NEG, F32)
        acc_scr[...] = jnp.zeros(acc_scr.shape, F32)

    def step(masked):
        parts = [(0, tk // 2, 0, tq), (tk // 2, tk, tq // 2, tq)] if masked else [(0, tk, 0, tq)]
        chains = [(t, h, part) for part in parts for t in range(2) for h in range(HEADS)]

        def scores(t, h, part):
            k0, k1, q0, q1 = part
            kt = k_ref[k0:k1, t * LANE:(t + 1) * LANE]
            km = jnp.where(head32 == h, kt, jnp.zeros_like(kt))
            st = _dot_nt(km, q_ref[q0:q1, t * LANE:(t + 1) * LANE])
            if masked:
                shape = (k1 - k0, q1 - q0)
                st = jnp.where(_iota(shape, 0) + k0 <= _iota(shape, 1) + q0, st, NEG)
            return st

        pend = [scores(*chains[n]) for n in range(SKEW)]
        for n, (t, h, (k0, k1, q0, q1)) in enumerate(chains):
            st = pend.pop(0)
            if n + SKEW < len(chains):
                pend.append(scores(*chains[n + SKEW]))
            idx = t * HEADS + h
            m_prev = m_scr[idx:idx + 1, q0:q1]
            m_new = jnp.maximum(m_prev, jnp.max(st, axis=0, keepdims=True))
            pt = jnp.exp2((st - m_new) * c).astype(BF16)
            alpha = jnp.exp2((m_prev - m_new) * c)
            m_scr[idx:idx + 1, q0:q1] = m_new
            vh = jnp.concatenate([vt_ref[h * HEAD_DIM:(h + 1) * HEAD_DIM, k0:k1],
                                  jnp.ones((ONES_ROWS, k1 - k0), BF16)], axis=0)
            acc_scr[idx, :, q0:q1] = acc_scr[idx, :, q0:q1] * alpha + _dot(vh, pt)

    @pl.when(j < i)
    def _off():
        step(False)

    @pl.when(j == i)
    def _diag():
        step(True)
        lp = lam_ref[...]
        lam = (jnp.exp(jnp.sum(lp[0:1] * lp[1:2], axis=-1, keepdims=True))
               - jnp.exp(jnp.sum(lp[2:3] * lp[3:4], axis=-1, keepdims=True)) + lam_init)
        parts = []
        for h in range(HEADS):
            a0, a1 = acc_scr[h], acc_scr[HEADS + h]
            oh = (a0[:HEAD_DIM] / a0[HEAD_DIM:HEAD_DIM + 1] - lam * (a1[:HEAD_DIM] / a1[HEAD_DIM:HEAD_DIM + 1]))
            ms = jnp.sum(oh * oh, axis=0, keepdims=True) * (1.0 / HEAD_DIM)
            parts.append(oh * lax.rsqrt(ms + NORM_EPS))
        y = jnp.concatenate(parts, axis=0).T * g_ref[...] * (1.0 - lam_init)
        o_ref[...] = y.astype(o_ref.dtype)


def _diff_attention(dq, dk, ht, lam_p, subln_g, lam_init, batch, seq_len):
    t = min(T_ATT, seq_len)
    nq = seq_len // t
    qi, kj = _tri_pairs(nq)
    g_full = jnp.tile(subln_g, HEADS).reshape(1, GROUP_W)
    grid_spec = pltpu.PrefetchScalarGridSpec(
        num_scalar_prefetch=2,
        grid=(batch, qi.shape[0]),
        in_specs=[pl.BlockSpec((t, GROUP_W), lambda b, p, qi, kj: (b * nq + qi[p], 0)),
                  pl.BlockSpec((t, GROUP_W), lambda b, p, qi, kj: (b * nq + kj[p], 0)),
                  pl.BlockSpec((GROUP_W, t), lambda b, p, qi, kj: (0, b * nq + kj[p])),
                  pl.BlockSpec((4, DIFF_QK), lambda b, p, qi, kj: (0, 0)),
                  pl.BlockSpec((1, GROUP_W), lambda b, p, qi, kj: (0, 0))],
        out_specs=pl.BlockSpec((t, GROUP_W), lambda b, p, qi, kj: (b * nq + qi[p], 0)),
        scratch_shapes=[pltpu.VMEM((2 * HEADS, t), F32),
                        pltpu.VMEM((2 * HEADS, HEAD_DIM + ONES_ROWS, t), F32)])
    return pl.pallas_call(
        functools.partial(_diff_kernel, lam_init=lam_init),
        grid_spec=grid_spec,
        out_shape=jax.ShapeDtypeStruct((batch * seq_len, GROUP_W), BF16),
        compiler_params=_cp(("parallel", "arbitrary")),
        name="diff_attn",
    )(qi, kj, dq, dk, ht, lam_p, g_full)


SB_CUM = 256
SB_Z_MIN = -87.0


def _sb_kernel(qi_ref, kj_ref, q_ref, k_ref, v_ref, o_ref, carry_scr, acc_scr):
    p = pl.program_id(1)
    i = qi_ref[p]
    j = kj_ref[p]
    tq, tk = q_ref.shape[0], k_ref.shape[0]
    head64 = _iota((1, GROUP_W), 1) // HEAD_DIM
    cw = min(SB_CUM, tk)

    @pl.when(j == i)
    def _init():
        carry_scr[...] = jnp.zeros(carry_scr.shape, F32)
        acc_scr[...] = jnp.zeros(acc_scr.shape, F32)

    def step(masked):
        m_excl = jnp.where(_iota((cw, cw), 0) > _iota((cw, cw), 1), 1.0, 0.0).astype(BF16)
        parts = [(tq // 2, tq, tk // 2, tk), (0, tq, 0, tk // 2)] if masked else [(0, tq, 0, tk)]
        chains = [(h, part) for part in parts for h in range(HEADS)]
        zs, lbs, xs, cums, atts = {}, {}, {}, {}, {}
        pvs = {part: [] for part in parts}

        def before(part):
            q0, q1, k0, k1 = part
            shape = (q1 - q0, k1 - k0)
            return _iota(shape, 1) + k0 < _iota(shape, 0) + q0

        def st_a(n):
            h, (q0, q1, k0, k1) = chains[n]
            kt = k_ref[k0:k1, :]
            zs[n] = _dot_nt(q_ref[q0:q1, :], jnp.where(head64 == h, kt, jnp.zeros_like(kt)))

        def st_b(n):
            z = jnp.maximum(zs.pop(n), SB_Z_MIN)
            nlb = jnp.log(1.0 + jnp.exp2(z * (-LOG2E)))
            sp = nlb + z
            lbs[n] = nlb
            if masked:
                sp = jnp.where(before(chains[n][1]), sp, 0.0)
            xs[n] = sp.astype(BF16)

        def st_c(n):
            h, (q0, q1, k0, k1) = chains[n]
            x = xs.pop(n)
            blocks = []
            suffix = carry_scr[h, q0:q1, :]
            for blk in reversed(range((k1 - k0) // cw)):
                sl = slice(blk * cw, (blk + 1) * cw)
                cb = _dot(x[:, sl], m_excl) + suffix
                blocks.insert(0, cb)
                suffix = cb[:, 0:1] + x[:, blk * cw:blk * cw + 1].astype(F32)
            carry_scr[h, q0:q1, :] = suffix
            cums[n] = jnp.concatenate(blocks, axis=1)

        def st_d(n):
            att = jnp.exp2((lbs.pop(n) + cums.pop(n)) * (-LOG2E))
            if masked:
                att = jnp.where(before(chains[n][1]), att, 0.0)
            atts[n] = att.astype(BF16)

        def st_e(n):
            h, part = chains[n]
            vt = v_ref[part[2]:part[3], :]
            pvs[part].append(_dot(atts.pop(n), jnp.where(head64 == h, vt, jnp.zeros_like(vt))))

        nch = len(chains)
        st_a(0)
        st_a(1)
        for n in range(nch):
            st_b(n)
            st_c(n)
            if n >= 1:
                st_d(n - 1)
                st_e(n - 1)
                if n + 1 < nch:
                    st_a(n + 1)
        st_d(nch - 1)
        st_e(nch - 1)
        for (q0, q1, _, _), terms in pvs.items():
            acc_scr[q0:q1, :] = acc_scr[q0:q1, :] + sum(terms[1:], terms[0])

    @pl.when(j == i)
    def _diag():
        step(True)

    @pl.when(j < i)
    def _off():
        step(False)

    @pl.when(j == 0)
    def _fin():
        o_ref[...] = acc_scr[...].astype(o_ref.dtype)


def _sb_attention(sq, sk, sv, batch, seq_len):
    t = min(T_ATT, seq_len)
    nq = seq_len // t
    qi, kj = _tri_pairs(nq, descending=True)
    grid_spec = pltpu.PrefetchScalarGridSpec(
        num_scalar_prefetch=2,
        grid=(batch, qi.shape[0]),
        in_specs=[pl.BlockSpec((t, GROUP_W), lambda b, p, qi, kj: (b * nq + qi[p], 0)),
                  pl.BlockSpec((t, GROUP_W), lambda b, p, qi, kj: (b * nq + kj[p], 0)),
                  pl.BlockSpec((t, GROUP_W), lambda b, p, qi, kj: (b * nq + kj[p], 0))],
        out_specs=pl.BlockSpec((t, GROUP_W), lambda b, p, qi, kj: (b * nq + qi[p], 0)),
        scratch_shapes=[pltpu.VMEM((HEADS, t, 1), F32), pltpu.VMEM((t, GROUP_W), F32)])
    return pl.pallas_call(
        _sb_kernel,
        grid_spec=grid_spec,
        out_shape=jax.ShapeDtypeStruct((batch * seq_len, GROUP_W), BF16),
        compiler_params=_cp(("parallel", "arbitrary")),
        name="sb_attn",
    )(qi, kj, sq, sk, sv)


def _stack_heads(q):
    qf = q.astype(F32)
    lo = _iota((1, LANE), 1) < HEAD_DIM
    parts = []
    for blk in (qf[:, :LANE], qf[:, LANE:]):
        parts.append(jnp.where(lo, blk, 0.0))
        parts.append(jnp.where(lo, pltpu.roll(blk, HEAD_DIM, 1), 0.0))
    return jnp.concatenate(parts, axis=0).astype(BF16)


def _nsa_compress_kernel(rk_ref, rv_ref, pek_ref, pev_ref, wk1_ref, wv1_ref, w2_ref, w2vt_ref, o_ref, vt_ref):
    n = rk_ref.shape[0]
    half = rk_ref.shape[1]

    def hidden(r_ref, pe_ref, w1_ref):
        r = r_ref[...]
        y1 = _dot((r + pe_ref[0:1, :]).astype(BF16), w1_ref[0:half, :])
        y2 = _dot((r + pe_ref[1:2, :]).astype(BF16), w1_ref[half:2 * half, :])
        hid = y1 + pltpu.roll(y2, n - 1, 0)
        return (hid * jax.nn.sigmoid(hid)).astype(BF16)

    act = jnp.concatenate([hidden(rk_ref, pek_ref, wk1_ref), hidden(rv_ref, pev_ref, wv1_ref)], axis=1)
    o_ref[...] = _dot(act, w2_ref[...]).astype(o_ref.dtype)
    vt_ref[...] = _dot_nt(w2vt_ref[...], act).astype(vt_ref.dtype)


def _nsa_compress(kc, vc, pe_k, pe_v, ck_w1, ck_w2, cv_w1, cv_w2, batch, seq_len):
    nrow = seq_len // CMP_STRIDE
    width = CMP_STRIDE * HEAD_DIM
    hid = ck_w1.shape[1]
    zo = jnp.zeros((hid, HEAD_DIM), BF16)
    w2 = jnp.concatenate([jnp.concatenate([ck_w2.astype(BF16), zo], axis=1),
                          jnp.concatenate([zo, cv_w2.astype(BF16)], axis=1)], axis=0)
    row_blk = pl.BlockSpec((nrow, width), lambda b: (b, 0))
    return pl.pallas_call(
        _nsa_compress_kernel,
        grid=(batch,),
        in_specs=[row_blk, row_blk, _const_spec((2, width)), _const_spec((2, width)),
                  _const_spec((2 * width, hid)), _const_spec((2 * width, hid)),
                  _const_spec((2 * hid, LANE)), _const_spec((HEAD_DIM, 2 * hid))],
        out_specs=[pl.BlockSpec((nrow, LANE), lambda b: (b, 0)),
                   pl.BlockSpec((HEAD_DIM, nrow), lambda b: (b, 0))],
        out_shape=[jax.ShapeDtypeStruct((batch * nrow, LANE), BF16),
                   jax.ShapeDtypeStruct((batch * HEAD_DIM, nrow), BF16)],
        compiler_params=_cp(("parallel",)),
        name="nsa_compress",
    )(kc.reshape(batch * nrow, width), vc.reshape(batch * nrow, width),
      pe_k.reshape(2, width), pe_v.reshape(2, width), ck_w1.astype(BF16), cv_w1.astype(BF16),
      w2, w2[:, HEAD_DIM:].T)


def _nsa_cmp_kernel(q_ref, kv_ref, vt_ref, ovt_ref, ocmp_ref, bias_ref):
    i = pl.program_id(1)
    tq = q_ref.shape[0]
    ncmp = kv_ref.shape[0]
    nslc = ovt_ref.shape[0]
    qs = _stack_heads(q_ref[...])
    kv = kv_ref[...]
    vt = vt_ref[...]
    tpos = i * tq + _iota((1, tq), 1)
    cm = _iota((ncmp, 1), 0) * CMP_STRIDE + (CMP_BLOCK - 1) <= tpos
    sts = [_dot_nt(kv, qs[h * tq:(h + 1) * tq]) for h in range(HEADS)]
    psum = None
    parts = []
    for h in range(HEADS):
        st = jnp.where(cm, sts[h], NEG)
        e = jnp.exp(st - jnp.max(st, axis=0, keepdims=True))
        pr = jnp.where(cm, e * (1.0 / jnp.sum(e, axis=0, keepdims=True)), 0.0)
        parts.append(_dot(vt, pr.astype(BF16)))
        psum = pr if psum is None else psum + pr
    ocmp_ref[...] = jnp.concatenate(parts, axis=0).T
    hi, lo = _split2(psum)
    imp = _dot(ovt_ref[...], hi) + _dot(ovt_ref[...], lo)
    blk = _iota((nslc, 1), 0)
    cur = tpos // SLC_BLOCK
    work = jnp.where(blk == 0, FORCE, jnp.where(blk == cur, FORCE, jnp.where(blk == cur - 1, FORCE, imp)))
    work = jnp.where(blk <= cur, work, -FORCE)
    sel = jnp.zeros((nslc, tq), F32)
    for _ in range(min(SLC_TOPN, nslc)):
        mx = jnp.max(work, axis=0, keepdims=True)
        first = jnp.min(jnp.where(work == mx, blk, nslc), axis=0, keepdims=True)
        hit = blk == first
        sel = jnp.where(hit, 1.0, sel)
        work = jnp.where(hit, -jnp.inf, work)
    bias_ref[...] = jnp.where(sel > 0.5, 0.0, NEG)


def _nsa_cmp(nq_arr, kvcmp, vtcmp, batch, seq_len):
    tq = min(TQ_CMP, seq_len)
    nq = seq_len // tq
    ncmp = seq_len // CMP_STRIDE
    nslc = seq_len // SLC_BLOCK
    cstart = np.arange(ncmp)[None, :] * CMP_STRIDE
    sstart = np.arange(nslc)[:, None] * SLC_BLOCK
    ovt = (cstart < sstart + SLC_BLOCK) & (cstart + CMP_BLOCK - 1 >= sstart)
    ovt &= (np.arange(ncmp)[None, :] < (seq_len - CMP_BLOCK) // CMP_STRIDE + 1)
    ovt = jnp.asarray(ovt, BF16)
    return pl.pallas_call(
        _nsa_cmp_kernel,
        grid=(batch, nq),
        in_specs=[pl.BlockSpec((tq, GROUP_W), lambda b, i: (b * nq + i, 0)),
                  pl.BlockSpec((ncmp, LANE), lambda b, i: (b, 0)),
                  pl.BlockSpec((HEAD_DIM, ncmp), lambda b, i: (b, 0)),
                  pl.BlockSpec((nslc, ncmp), lambda b, i: (0, 0))],
        out_specs=[pl.BlockSpec((tq, GROUP_W), lambda b, i: (b * nq + i, 0)),
                   pl.BlockSpec((nslc, tq), lambda b, i: (0, b * nq + i))],
        out_shape=[jax.ShapeDtypeStruct((batch * seq_len, GROUP_W), F32),
                   jax.ShapeDtypeStruct((nslc, batch * seq_len), F32)],
        compiler_params=_cp(("parallel", "parallel")),
        name="nsa_cmp_select",
    )(nq_arr, kvcmp, vtcmp, ovt)


def _nsa_win_kernel(q_ref, k_ref, vt_ref, o_ref, *, window):
    i = pl.program_id(1)
    tq = q_ref.shape[0]
    span = tq + window
    base = pl.multiple_of(jnp.maximum(i * tq - window, 0), LANE)
    kv = k_ref[pl.ds(base, span), :]
    vh = jnp.concatenate([vt_ref[:, pl.ds(base, span)], jnp.ones((ONES_ROWS, span), BF16)], axis=0)
    qs = _stack_heads(q_ref[...])
    rel = (i * tq + _iota((1, tq), 1)) - (base + _iota((span, 1), 0))
    bias = jnp.where(rel >= 0, jnp.where(rel < window, 0.0, NEG), NEG)
    sts = [_dot_nt(kv, qs[h * tq:(h + 1) * tq]) for h in range(HEADS)]
    parts = []
    for h in range(HEADS):
        st = sts[h] + bias
        e = jnp.exp(st - jnp.max(st, axis=0, keepdims=True)).astype(BF16)
        r = _dot(vh, e)
        parts.append(r[:HEAD_DIM] / r[HEAD_DIM:HEAD_DIM + 1])
    o_ref[...] = jnp.concatenate(parts, axis=0).T


def _nsa_window(nq_arr, kw, ht, batch, seq_len):
    tq = min(TQ_WIN, seq_len)
    nq = seq_len // tq
    window = min(WINDOW, seq_len - tq)
    vw_row_blk = 2 * GROUP_W // HEAD_DIM + 1
    return pl.pallas_call(
        functools.partial(_nsa_win_kernel, window=window),
        grid=(batch, nq),
        in_specs=[pl.BlockSpec((tq, GROUP_W), lambda b, i: (b * nq + i, 0)),
                  pl.BlockSpec((seq_len, LANE), lambda b, i: (b, 0)),
                  pl.BlockSpec((HEAD_DIM, seq_len), lambda b, i: (vw_row_blk, b))],
        out_specs=pl.BlockSpec((tq, GROUP_W), lambda b, i: (b * nq + i, 0)),
        out_shape=jax.ShapeDtypeStruct((batch * seq_len, GROUP_W), F32),
        compiler_params=_cp(("parallel", "parallel")),
        name="nsa_window",
    )(nq_arr, kw, ht)


def _nsa_sel_kernel(qi_ref, kj_ref, q_ref, k_ref, vt_ref, bias_ref, ocmp_ref, owin_ref, gate_ref, o_ref,
                    qs_scr, m_scr, acc_scr):
    p = pl.program_id(1)
    i = qi_ref[p]
    j = kj_ref[p]
    tq, tk = q_ref.shape[0], k_ref.shape[0]

    @pl.when(j == 0)
    def _init():
        qs = _stack_heads(q_ref[...])
        for h in range(HEADS):
            qs_scr[h] = qs[h * tq:(h + 1) * tq]
        m_scr[...] = jnp.full(m_scr.shape, NEG, F32)
        acc_scr[...] = jnp.zeros(acc_scr.shape, F32)

    def step(masked):
        parts = [(0, tk // 2, 0, tq), (tk // 2, tk, tq // 2, tq)] if masked else [(0, tk, 0, tq)]
        chains = [(h, part) for part in parts for h in range(HEADS)]

        def scores(h, part):
            k0, k1, q0, q1 = part
            st = _dot_nt(k_ref[k0:k1, :], qs_scr[h, q0:q1, :])
            bias = bias_ref[k0 // SLC_BLOCK:k1 // SLC_BLOCK, q0:q1][:, None, :]
            st = (st.reshape((k1 - k0) // SLC_BLOCK, SLC_BLOCK, q1 - q0) + bias).reshape(k1 - k0, q1 - q0)
            if masked:
                shape = (k1 - k0, q1 - q0)
                st = jnp.where(_iota(shape, 0) + k0 <= _iota(shape, 1) + q0, st, NEG)
            return st

        pend = [scores(*chains[n]) for n in range(SKEW)]
        for n, (h, (k0, k1, q0, q1)) in enumerate(chains):
            st = pend.pop(0)
            if n + SKEW < len(chains):
                pend.append(scores(*chains[n + SKEW]))
            m_prev = m_scr[h:h + 1, q0:q1]
            m_new = jnp.maximum(m_prev, jnp.max(st, axis=0, keepdims=True))
            pt = jnp.exp(st - m_new).astype(BF16)
            alpha = jnp.exp(m_prev - m_new)
            m_scr[h:h + 1, q0:q1] = m_new
            vh = jnp.concatenate([vt_ref[:, k0:k1], jnp.ones((ONES_ROWS, k1 - k0), BF16)], axis=0)
            acc_scr[h, :, q0:q1] = acc_scr[h, :, q0:q1] * alpha + _dot(vh, pt)

    @pl.when(j < i)
    def _past():
        step(False)

    @pl.when(j == i)
    def _diag():
        step(True)

    @pl.when(j == i)
    def _fin():
        parts = []
        for h in range(HEADS):
            a = acc_scr[h]
            parts.append(a[:HEAD_DIM] / a[HEAD_DIM:HEAD_DIM + 1])
        osel = jnp.concatenate(parts, axis=0).T
        sig = jax.nn.sigmoid(gate_ref[...])
        grow = _iota((LANE, 1), 0)
        ghead = _iota((1, GROUP_W), 1) // HEAD_DIM
        out = None
        for br, o_br in enumerate((ocmp_ref[...], osel, owin_ref[...])):
            e_br = jnp.where(grow == 3 * ghead + br, 1.0, 0.0).astype(BF16)
            term = _dot_x3(sig, e_br) * o_br
            out = term if out is None else out + term
        o_ref[...] = out.astype(o_ref.dtype)


def _nsa_select(nq_arr, ks, ht, sel, ocmp, owin, gates, batch, seq_len):
    tq = tk = min(T_SEL, seq_len)
    nq = nk = seq_len // tq
    qi, kj = _tri_pairs(nq)
    qrow = lambda b, p, qi, kj: (b * nq + qi[p], 0)
    vs_row_blk = 2 * GROUP_W // HEAD_DIM
    grid_spec = pltpu.PrefetchScalarGridSpec(
        num_scalar_prefetch=2,
        grid=(batch, qi.shape[0]),
        in_specs=[pl.BlockSpec((tq, GROUP_W), qrow),
                  pl.BlockSpec((tk, LANE), lambda b, p, qi, kj: (b * nk + kj[p], 0)),
                  pl.BlockSpec((HEAD_DIM, tk), lambda b, p, qi, kj: (vs_row_blk, b * nk + kj[p])),
                  pl.BlockSpec((tk // SLC_BLOCK, tq), lambda b, p, qi, kj: (kj[p], b * nq + qi[p])),
                  pl.BlockSpec((tq, GROUP_W), qrow), pl.BlockSpec((tq, GROUP_W), qrow),
                  pl.BlockSpec((tq, LANE), qrow)],
        out_specs=pl.BlockSpec((tq, GROUP_W), qrow),
        scratch_shapes=[pltpu.VMEM((HEADS, tq, LANE), BF16), pltpu.VMEM((HEADS, tq), F32),
                        pltpu.VMEM((HEADS, HEAD_DIM + ONES_ROWS, tq), F32)])
    return pl.pallas_call(
        _nsa_sel_kernel,
        grid_spec=grid_spec,
        out_shape=jax.ShapeDtypeStruct((batch * seq_len, GROUP_W), BF16),
        compiler_params=_cp(("parallel", "arbitrary")),
        name="nsa_select_gate",
    )(qi, kj, nq_arr, ks, ht, sel, ocmp, owin, gates)


def _nsa(nq_arr, ks, kw, ht, kc, vc, gates, pe_k, pe_v, ck_w1, ck_w2, cv_w1, cv_w2, batch, seq_len):
    kvcmp, vtcmp = _nsa_compress(kc, vc, pe_k, pe_v, ck_w1, ck_w2, cv_w1, cv_w2, batch, seq_len)
    ocmp, sel = _nsa_cmp(nq_arr, kvcmp, vtcmp, batch, seq_len)
    owin = _nsa_window(nq_arr, kw, ht, batch, seq_len)
    return _nsa_select(nq_arr, ks, ht, sel, ocmp, owin, gates, batch, seq_len)


def _bd(mc, mask_bd):
    return jnp.where(mask_bd, jnp.concatenate([mc] * HEADS, axis=0), jnp.zeros((), mc.dtype))


def _mm_bd(x, mc, mask_bd):
    return _dot(x.astype(BF16), _bd(mc.astype(BF16), mask_bd))


def _gdn_prep_kernel(x_ref, ab_ref, cw_ref, alog_ref, dtb_ref,
                     u_ref, w_ref, qd_ref, in_ref, kdt_ref, gl_ref, xpad_scr, qkv_scr):
    n = pl.program_id(0)
    C = GDN_CHUNK
    W = GROUP_W
    nb, rows = x_ref.shape[0], x_ref.shape[1]
    inst = [(b, c) for b in range(nb) for c in range(rows // C)]

    @pl.when(n == 0)
    def _():
        xpad_scr[:, 0:8, :] = jnp.zeros((nb, 8, xpad_scr.shape[2]), F32)

    cw = cw_ref[...]
    for b in range(nb):
        x = x_ref[b]
        xpad_scr[b, 8:8 + rows, :] = x
        conv = (cw[0:1] * xpad_scr[b, 5:5 + rows, :] + cw[1:2] * xpad_scr[b, 6:6 + rows, :]
                + cw[2:3] * xpad_scr[b, 7:7 + rows, :] + cw[3:4] * x)
        xpad_scr[b, 0:8, :] = x[rows - 8:rows, :]
        qkv_scr[b] = conv * jax.nn.sigmoid(conv)

    r256 = _iota((W, W), 0)
    c256 = _iota((W, W), 1)
    mask_bd = (r256 // HEAD_DIM) == (c256 // HEAD_DIM)
    ones_bd = jnp.where(mask_bd, 1.0, 0.0).astype(BF16)
    eye256 = jnp.where(r256 == c256, 1.0, 0.0).astype(BF16)
    row = _iota((C, W), 0)
    jl = _iota((C, W), 1) % HEAD_DIM
    ltri = jnp.where(_iota((C, C), 1) <= _iota((C, C), 0), 1.0, 0.0).astype(BF16)
    erow = _iota((LANE, W), 0)
    ehead = _iota((LANE, W), 1) // HEAD_DIM
    e_g = jnp.where(erow == ehead, 1.0, 0.0).astype(BF16)
    e_b = jnp.where(erow == ehead + HEADS, 1.0, 0.0).astype(BF16)

    def each(f, *lists):
        return [f(*args) for args in zip(*lists)]

    sl = [slice(c * C, (c + 1) * C) for _, c in inst]
    q = [qkv_scr[b, sl[k], 0:W] for k, (b, _) in enumerate(inst)]
    kk = [qkv_scr[b, sl[k], W:2 * W] for k, (b, _) in enumerate(inst)]
    v = [qkv_scr[b, sl[k], 2 * W:3 * W] for k, (b, _) in enumerate(inst)]
    ab = [ab_ref[b, sl[k], :] for k, (b, _) in enumerate(inst)]

    qn = each(lambda t: t * lax.rsqrt(_dot_x2(t * t, ones_bd) + NORM_EPS) * (HEAD_DIM ** -0.5), q)
    kn = each(lambda t: t * lax.rsqrt(_dot_x2(t * t, ones_bd) + NORM_EPS), kk)

    def gate(a):
        z = a + dtb_ref[...]
        return -jnp.exp(alog_ref[...]) * (jnp.maximum(z, 0.0) + jnp.log1p(jnp.exp(-jnp.abs(z))))

    g_hl = each(lambda a: _dot_x3(gate(a), e_g), ab)
    beta = each(lambda a: _dot_x3(jax.nn.sigmoid(a), e_b), ab)
    gc = each(lambda g: _dot_x3_left(ltri, g), g_hl)
    glast = each(lambda g: g[C - 1:C, :], gc)
    exp_g = each(jnp.exp, gc)
    dmat = each(lambda g: _dot_x3_left(ltri, jnp.where(row > jl, g, 0.0)), g_hl)
    decay = each(lambda d: jnp.where(jl <= row, jnp.exp(d), 0.0), dmat)

    kt4 = each(lambda t: _dot_nt(eye256, jnp.concatenate([t.astype(BF16)] * HEADS, axis=0)), kn)
    kb_mat = each(lambda t: jnp.where(mask_bd, t, 0.0).astype(BF16), kt4)
    kbeta = each(lambda t, bb: t * bb, kn, beta)
    a_c = each(lambda t, m, d: jnp.where(jl < row, _dot(t.astype(BF16), m) * d, 0.0), kbeta, kb_mat, decay)
    intra = each(lambda t, m, d: _dot(t.astype(BF16), m) * d, qn, kb_mat, decay)

    t_c = each(lambda a: jnp.where(jl == row, 1.0, 0.0) - a, a_c)
    p_c = a_c
    for _ in range(5):
        p_c = each(lambda pc: _mm_bd(pc, pc, mask_bd), p_c)
        t_c = each(lambda tc, pc: tc + _mm_bd(tc, pc, mask_bd), t_c, p_c)

    u = each(lambda tc, t, bb: _mm_bd(tc, t * bb, mask_bd), t_c, v, beta)
    w = each(lambda tc, t, e: _mm_bd(tc, t * e, mask_bd), t_c, kbeta, exp_g)
    kdt = each(lambda t, gl, g: _dot_nt(eye256, (t * jnp.exp(gl - g)).astype(BF16)), kn, glast, gc)

    for k, (b, c) in enumerate(inst):
        u_ref[b, sl[k], :] = u[k]
        w_ref[b, sl[k], :] = w[k].astype(w_ref.dtype)
        qd_ref[b, sl[k], :] = (qn[k] * exp_g[k]).astype(qd_ref.dtype)
        in_ref[b, sl[k], :] = intra[k].astype(in_ref.dtype)
        kdt_ref[b, c] = kdt[k].astype(kdt_ref.dtype)
        gl_ref[b, c] = jnp.exp(glast[k])


def _dot_x3_left(w, x):
    hi = x.astype(BF16)
    r = x - hi.astype(F32)
    mid = r.astype(BF16)
    lo = (r - mid.astype(F32)).astype(BF16)
    return _dot(w, hi) + _dot(w, mid) + _dot(w, lo)


def _gdn_scan_kernel(u_ref, w_ref, qd_ref, in_ref, kdt_ref, gl_ref, z_ref, g_ref, o_ref, s_scr):
    n = pl.program_id(0)
    C = GDN_CHUNK
    W = GROUP_W
    nb, rows = u_ref.shape[0], u_ref.shape[1]

    @pl.when(n == 0)
    def _():
        s_scr[...] = jnp.zeros(s_scr.shape, F32)

    mask_bd = (_iota((W, W), 0) // HEAD_DIM) == (_iota((W, W), 1) // HEAD_DIM)
    ones_bd = jnp.where(mask_bd, 1.0, 0.0).astype(BF16)
    s = [s_scr[b] for b in range(nb)]
    for c in range(rows // C):
        sl = slice(c * C, (c + 1) * C)
        sb = [t.astype(BF16) for t in s]
        v_new = [u_ref[b, sl, :] - _dot(w_ref[b, sl, :], sb[b]) for b in range(nb)]
        vb = [t.astype(BF16) for t in v_new]
        s = [s[b] * gl_ref[b, c] + jnp.where(mask_bd, _dot(kdt_ref[b, c], vb[b]), 0.0) for b in range(nb)]
        o = [_dot(qd_ref[b, sl, :], sb[b]) + _dot(in_ref[b, sl, :], _bd(vb[b], mask_bd)) for b in range(nb)]
        for b in range(nb):
            ms = _dot_x2(o[b] * o[b], ones_bd) * (1.0 / HEAD_DIM)
            zz = z_ref[b, sl, :]
            y = o[b] * lax.rsqrt(ms + NORM_EPS) * g_ref[...] * (zz * jax.nn.sigmoid(zz))
            o_ref[b, sl, :] = y.astype(o_ref.dtype)
    for b in range(nb):
        s_scr[b] = s[b]


GDN_PREP_CHUNKS = 8
GDN_SCAN_CHUNKS = 8


def _gdn(hg, conv_w, a_log, dt_bias, norm_g, batch, seq_len):
    C = GDN_CHUNK
    nc = seq_len // C
    W = GROUP_W
    padl = lambda a: jnp.concatenate([a, jnp.zeros((LANE - a.shape[0],), F32)]).reshape(1, LANE)
    hg3 = hg.reshape(batch, seq_len, hg.shape[-1])
    cp = math.gcd(GDN_PREP_CHUNKS, nc)
    rp = cp * C
    blkp = pl.BlockSpec((batch, rp, W), lambda n: (0, n, 0))
    u, w, qd, intra, kdt, gl = pl.pallas_call(
        _gdn_prep_kernel,
        grid=(nc // cp,),
        in_specs=[pl.BlockSpec((batch, rp, 3 * W), lambda n: (0, n, 0)),
                  pl.BlockSpec((batch, rp, LANE), lambda n: (0, n, 8)),
                  pl.BlockSpec((4, 3 * W), lambda n: (0, 0)),
                  pl.BlockSpec((1, LANE), lambda n: (0, 0)),
                  pl.BlockSpec((1, LANE), lambda n: (0, 0))],
        out_specs=[blkp] * 4
        + [pl.BlockSpec((batch, cp, W, C), lambda n: (0, n, 0, 0)),
           pl.BlockSpec((batch, cp, 1, W), lambda n: (0, n, 0, 0))],
        out_shape=[jax.ShapeDtypeStruct((batch, seq_len, W), F32), jax.ShapeDtypeStruct((batch, seq_len, W), BF16),
                   jax.ShapeDtypeStruct((batch, seq_len, W), BF16), jax.ShapeDtypeStruct((batch, seq_len, W), BF16),
                   jax.ShapeDtypeStruct((batch, nc, W, C), BF16),
                   jax.ShapeDtypeStruct((batch, nc, 1, W), F32)],
        scratch_shapes=[pltpu.VMEM((batch, 8 + rp, 3 * W), F32), pltpu.VMEM((batch, rp, 3 * W), F32)],
        compiler_params=_cp(("arbitrary",)),
        name="gdn_prep",
    )(hg3, hg3, conv_w, padl(a_log), padl(dt_bias))

    cs = math.gcd(GDN_SCAN_CHUNKS, nc)
    blk = pl.BlockSpec((batch, cs * C, W), lambda n: (0, n, 0))
    out = pl.pallas_call(
        _gdn_scan_kernel,
        grid=(nc // cs,),
        in_specs=[blk, blk, blk, blk,
                  pl.BlockSpec((batch, cs, W, C), lambda n: (0, n, 0, 0)),
                  pl.BlockSpec((batch, cs, 1, W), lambda n: (0, n, 0, 0)),
                  pl.BlockSpec((batch, cs * C, W), lambda n: (0, n, 3)),
                  pl.BlockSpec((1, W), lambda n: (0, 0))],
        out_specs=blk,
        out_shape=jax.ShapeDtypeStruct((batch, seq_len, W), BF16),
        scratch_shapes=[pltpu.VMEM((batch, W, W), F32)],
        compiler_params=_cp(("arbitrary",)),
        name="gdn_scan",
    )(u, w, qd, intra, kdt, gl, hg3, jnp.tile(norm_g, HEADS).reshape(1, W))
    return out.reshape(batch * seq_len, W)


def kernel(x, w_in, w_out, ffn1_w_gu, ffn1_w_down, ffn2_w_gu, ffn2_w_down, ln1_g, ln1_b, ln2_g, ln2_b, ln3_g, ln3_b, diff_lam_q1, diff_lam_k1, diff_lam_q2, diff_lam_k2, diff_subln_g, gdn_conv_w, gdn_a_log, gdn_dt_bias, gdn_norm_g, nsa_pe_k, nsa_pe_v, nsa_cmp_k_w1, nsa_cmp_k_w2, nsa_cmp_v_w1, nsa_cmp_v_w2):
    B, S, D = x.shape
    depth = w_in.shape[0]
    alpha = (2 * depth) ** 0.25
    tab = _rope_table(S)
    w_out, ffn1_w_gu, ffn1_w_down, ffn2_w_gu, ffn2_w_down = (
        a.astype(BF16) for a in (w_out, ffn1_w_gu, ffn1_w_down, ffn2_w_gu, ffn2_w_down))
    xf = x.reshape(B * S, D)
    for l in range(depth):
        lam_init = 0.8 - 0.6 * math.exp(-0.3 * l)
        xf = _ffn_ln(xf, ffn1_w_gu, ffn1_w_down, l, ln1_g[l], ln1_b[l], alpha)
        wm, wt = _prep_w_in(w_in[l])
        dq, dk, nq_arr, ks, kw, sq, sk, sv, hg, kc, vc, gates, ht = _proj(xf, wm, wt, tab, S)
        lam_p = jnp.stack([diff_lam_q1[l], diff_lam_k1[l], diff_lam_q2[l], diff_lam_k2[l]])
        o_diff = _diff_attention(dq, dk, ht, lam_p, diff_subln_g[l], lam_init, B, S)
        o_gdn = _gdn(hg, gdn_conv_w[l], gdn_a_log[l], gdn_dt_bias[l], gdn_norm_g[l], B, S)
        o_nsa = _nsa(nq_arr, ks, kw, ht, kc, vc, gates, nsa_pe_k[l], nsa_pe_v[l], nsa_cmp_k_w1[l], nsa_cmp_k_w2[l],
                     nsa_cmp_v_w1[l], nsa_cmp_v_w2[l], B, S)
        o_sb = _sb_attention(sq, sk, sv, B, S)
        xf = _outproj_ffn_ln(xf, (o_diff, o_gdn, o_nsa, o_sb), w_out, ln2_g[l], ln2_b[l],
                             ffn2_w_gu, ffn2_w_down, l, ln3_g[l], ln3_b[l], alpha)
    return xf.reshape(B, S, D)
```
